```python
import numpy as np
import jax, jax.numpy as jnp
from jax import lax

D_MODEL = 1024
BATCH = 4
SEQ = 4096
DEPTH = 1
DEC_BATCH = 128
DEC_SEQ = 1
PAST_LEN = 2048
PAGE_SIZE = 128

HEAD_DIM = 64
H_FOX = 8
H_NSA = 8
H_KV = 2
GQA_GROUP = H_NSA // H_KV
W_FOX = H_FOX * HEAD_DIM
W_NSA = H_NSA * HEAD_DIM
W_KV = H_KV * HEAD_DIM
D_FF = 2816
CMP_STRIDE = 16
CMP_LEN = 32
CMP_RATIO = CMP_LEN // CMP_STRIDE
CMP_HIDDEN = HEAD_DIM
SLC_LEN = 64
SLC_TOPN = 16
WINDOW = 512
ROPE_THETA = 10000.0
Q_BLOCK = 128
EPS = 1e-6
NEG = -1e30
FORCE_SCORE = 1e4
N_PAGES = PAST_LEN // PAGE_SIZE
N_POOL = (5 * DEC_BATCH * N_PAGES + 3) // 4
WIN_BUF = min(WINDOW, PAST_LEN)
IN_SPLITS = (W_FOX, W_FOX, W_FOX, H_FOX, W_NSA, 6 * W_KV, 3 * H_NSA)
IN_COLS = W_FOX * 3 + H_FOX + W_NSA + 6 * W_KV + 3 * H_NSA

kernel_name = 'hymba_fox_nsa_macaron_decode_step'


def rmsnorm(x, g):
    xf = x.astype(jnp.float32)
    y = xf * lax.rsqrt(jnp.mean(xf * xf, axis=-1, keepdims=True) + EPS)
    return (y * g.astype(jnp.float32)).astype(x.dtype)


def swiglu(x, wg, wu, wd):
    return (jax.nn.silu(x @ wg) * (x @ wu)) @ wd


def half_ffn(x, g_pre, wg, wu, wd, g_post):
    return x + 0.5 * rmsnorm(swiglu(rmsnorm(x, g_pre), wg, wu, wd), g_post)


def rope(x, pos):
    half = HEAD_DIM // 2
    inv = ROPE_THETA ** (-jnp.arange(half, dtype=jnp.float32) / half)
    ang = pos.astype(jnp.float32)[:, None] * inv[None, :]
    cos = jnp.cos(ang)[None, :, None, :]
    sin = jnp.sin(ang)[None, :, None, :]
    xf = x.astype(jnp.float32)
    x1, x2 = xf[..., :half], xf[..., half:]
    return jnp.concatenate([x1 * cos - x2 * sin, x2 * cos + x1 * sin], axis=-1).astype(x.dtype)


def blocked_over_queries(fn, *xs):
    n_q = xs[0].shape[1]
    qb = min(Q_BLOCK, n_q)
    nb = -(-n_q // qb)
    pad = nb * qb - n_q

    def split(x):
        x = jnp.pad(x, [(0, 0), (0, pad)] + [(0, 0)] * (x.ndim - 2), mode='edge')
        x = x.reshape(x.shape[0], nb, qb, *x.shape[2:])
        return jnp.moveaxis(x, 1, 0)

    out = lax.map(lambda a: fn(*a), tuple(split(x) for x in xs))
    out = jnp.moveaxis(out, 0, 1)
    out = out.reshape(out.shape[0], nb * qb, *out.shape[3:])
    return out[:, :n_q]


def project(n, w_in, b_f, pos):
    B, T = n.shape[:2]
    cuts, acc = [], 0
    for s in IN_SPLITS[:-1]:
        acc += s
        cuts.append(acc)
    parts = jnp.split(n @ w_in, cuts, axis=-1)
    fq = parts[0].reshape(B, T, H_FOX, HEAD_DIM)
    fk = parts[1].reshape(B, T, H_FOX, HEAD_DIM)
    fv = parts[2].reshape(B, T, H_FOX, HEAD_DIM)
    logf = jax.nn.log_sigmoid(parts[3].astype(jnp.float32) + b_f.astype(jnp.float32))
    nq = rope(parts[4].reshape(B, T, H_NSA, HEAD_DIM), pos)
    kvraw = parts[5].reshape(B, T, 3, 2, H_KV, HEAD_DIM)
    ks = rope(kvraw[:, :, :, 0].reshape(B, T, 3 * H_KV, HEAD_DIM), pos).reshape(B, T, 3, H_KV, HEAD_DIM)
    nkv = jnp.stack([ks, kvraw[:, :, :, 1]], axis=3).reshape(B, T, 6, H_KV, HEAD_DIM)
    ng = jax.nn.sigmoid(parts[6].reshape(B, T, H_NSA, 3))
    return fq, fk, fv, logf, nq, nkv, ng


def fox_attention(q, k, v, cum, qpos):
    L = k.shape[1]
    kpos = jnp.arange(L, dtype=jnp.int32)
    cum_t = jnp.swapaxes(cum, 1, 2)
    scale = HEAD_DIM ** -0.5

    def blk(qb_, pb):
        p = pb[0]
        s = jnp.einsum('bqhd,bkhd->bhqk', qb_, k, preferred_element_type=jnp.float32) * scale
        s = s + jnp.take(cum_t, p, axis=2)[..., None] - cum_t[:, :, None, :]
        s = jnp.where(kpos[None, None, None, :] <= p[None, None, :, None], s, NEG)
        a = jax.nn.softmax(s, axis=-1).astype(v.dtype)
        return jnp.einsum('bhqk,bkhd->bqhd', a, v)

    return blocked_over_queries(blk, q, qpos[None, :])


def compress(x, w1, w2, pe):
    B, L = x.shape[:2]
    n_ch = L // CMP_STRIDE
    n_cmp = n_ch - CMP_RATIO + 1
    seg = CMP_STRIDE * HEAD_DIM
    ch = x[:, :n_ch * CMP_STRIDE].reshape(B, n_ch, CMP_STRIDE, H_KV, HEAD_DIM)
    ch = jnp.swapaxes(ch, 2, 3).reshape(B, n_ch, H_KV, seg)
    hid = pe.reshape(-1) @ w1
    for r in range(CMP_RATIO):
        hid = hid + ch[:, r:r + n_cmp] @ w1[r * seg:(r + 1) * seg]
    return jax.nn.gelu(hid) @ w2


def nsa_attention(q, g, kv, kw, pw0, qpos, wk1, wk2, pek, wv1, wv2, pev):
    B, L = kv.shape[:2]
    dtype = kv.dtype
    scale = HEAD_DIM ** -0.5
    kc = compress(kv[:, :, 0], wk1, wk2, pek)
    vc = compress(kv[:, :, 1], wv1, wv2, pev)
    n_cmp = kc.shape[1]
    cmp_start_np = np.arange(n_cmp) * CMP_STRIDE
    cmp_end_np = cmp_start_np + CMP_LEN - 1
    n_slc = -(-L // SLC_LEN)
    slc_start_np = np.arange(n_slc) * SLC_LEN
    overlap = jnp.asarray(((cmp_start_np[None, :] <= (slc_start_np + SLC_LEN - 1)[:, None])
                           & (cmp_end_np[None, :] >= slc_start_np[:, None])).astype(np.float32))
    cmp_end = jnp.asarray(cmp_end_np.astype(np.int32))
    slc_start = jnp.asarray(slc_start_np.astype(np.int32))
    pad = n_slc * SLC_LEN - L

    def to_blocks(x):
        x = jnp.pad(x, ((0, 0), (0, pad), (0, 0), (0, 0)))
        return x.reshape(B, n_slc, SLC_LEN, H_KV, HEAD_DIM).transpose(0, 3, 1, 2, 4)

    kb = to_blocks(kv[:, :, 2])
    vb = to_blocks(kv[:, :, 3])
    n_sel = min(SLC_TOPN, n_slc)
    Lw = kw.shape[1]
    gather = jax.vmap(jax.vmap(lambda blocks, ids: blocks[ids]))
    blk_ids = jnp.arange(n_slc, dtype=jnp.int32)
    in_blk = jnp.arange(SLC_LEN, dtype=jnp.int32)

    def blk(qb_, gb, pb):
        p = pb[0]
        nq = p.shape[0]
        qg = qb_.reshape(B, nq, H_KV, GQA_GROUP, HEAD_DIM)
        s = jnp.einsum('bqhgd,bnhd->bhgqn', qg, kc, preferred_element_type=jnp.float32) * scale
        ok = cmp_end[None, :] <= p[:, None]
        a_cmp = jnp.where(ok, jax.nn.softmax(jnp.where(ok, s, NEG), axis=-1), 0.0)
        o_cmp = jnp.einsum('bhgqn,bnhd->bqhgd', a_cmp.astype(dtype), vc)
        imp = jnp.einsum('bhgqn,jn->bhqj', a_cmp, overlap)
        qblk = p // SLC_LEN
        forced = (blk_ids[None, :] == 0) | (blk_ids[None, :] == qblk[:, None]) | (blk_ids[None, :] == qblk[:, None] - 1)
        valid = slc_start[None, :] <= p[:, None]
        score = jnp.where(valid, jnp.where(forced, FORCE_SCORE, imp), -1.0)
        _, idx = lax.top_k(score, n_sel)
        ksel = gather(kb, idx)
        vsel = gather(vb, idx)
        tpos = idx[..., None] * SLC_LEN + in_blk
        s = jnp.einsum('bqhgd,bhqnld->bhgqnl', qg, ksel, preferred_element_type=jnp.float32) * scale
        s = jnp.where((tpos <= p[:, None, None])[:, :, None], s, NEG)
        a = jax.nn.softmax(s.reshape(B, H_KV, GQA_GROUP, nq, n_sel * SLC_LEN), axis=-1)
        a = a.reshape(B, H_KV, GQA_GROUP, nq, n_sel, SLC_LEN).astype(dtype)
        o_slc = jnp.einsum('bhgqnl,bhqnld->bqhgd', a, vsel)
        sw = min(Lw, WINDOW - 1 + nq)
        start = jnp.clip(p[0] - (WINDOW - 1) - pw0, 0, Lw - sw)
        kwin = lax.dynamic_slice_in_dim(kw, start, sw, axis=1)
        wpos = pw0 + start + jnp.arange(sw, dtype=jnp.int32)
        s = jnp.einsum('bqhgd,bkhd->bhgqk', qg, kwin[:, :, 0], preferred_element_type=jnp.float32) * scale
        dist = p[:, None] - wpos[None, :]
        s = jnp.where((dist >= 0) & (dist < WINDOW), s, NEG)
        a = jax.nn.softmax(s, axis=-1).astype(dtype)
        o_win = jnp.einsum('bhgqk,bkhd->bqhgd', a, kwin[:, :, 1])
        o = jnp.stack([o_cmp, o_slc, o_win], axis=-1).reshape(B, nq, H_NSA, HEAD_DIM, 3)
        return jnp.einsum('bqhdc,bqhc->bqhd', o, gb.astype(o.dtype))

    return blocked_over_queries(blk, q, g, qpos[None, :])


def merge_groups(o_fox, o_nsa, g_fox, g_nsa, w_out):
    B, T = o_fox.shape[:2]
    of = rmsnorm(o_fox.reshape(B, T, W_FOX), g_fox)
    on = rmsnorm(o_nsa.reshape(B, T, W_NSA), g_nsa)
    return jnp.concatenate([of, on], axis=-1) @ w_out


def setup_inputs(seed: int = 0) -> dict:
    key = jax.random.key(seed)
    ks = iter(jax.random.split(key, 40))
    f32 = jnp.float32

    def nrm(shape, scale):
        return scale * jax.random.normal(next(ks), shape, f32)

    def gain(n):
        return 1.0 + nrm((DEPTH, n), 0.05)

    return {
        'x_prompt': nrm((BATCH, SEQ, D_MODEL), 1.0),
        'x_sample': nrm((DEC_BATCH, DEC_SEQ, D_MODEL), 1.0),
        'cache_fox_kv': nrm((DEPTH, N_POOL, PAGE_SIZE, 2, H_FOX, HEAD_DIM), 1.0),
        'cache_fox_logf': jax.nn.log_sigmoid(2.0 + nrm((DEPTH, N_POOL, PAGE_SIZE, H_FOX), 1.0)),
        'cache_nsa_kv': nrm((DEPTH, N_POOL, PAGE_SIZE, 4, H_KV, HEAD_DIM), 1.0),
        'state_nsa_win_kv': nrm((DEPTH, DEC_BATCH, WIN_BUF, 2, H_KV, HEAD_DIM), 1.0),
        'page_table': jax.random.permutation(next(ks), N_POOL)[:DEC_BATCH * N_PAGES].reshape(DEC_BATCH, N_PAGES).astype(jnp.int32),
        'g_ffn1_pre': gain(D_MODEL),
        'w_ffn1_gate': nrm((DEPTH, D_MODEL, D_FF), D_MODEL ** -0.5),
        'w_ffn1_up': nrm((DEPTH, D_MODEL, D_FF), D_MODEL ** -0.5),
        'w_ffn1_down': nrm((DEPTH, D_FF, D_MODEL), D_FF ** -0.5),
        'g_ffn1_post': gain(D_MODEL),
        'g_mix_pre': gain(D_MODEL),
        'w_in': nrm((DEPTH, D_MODEL, IN_COLS), D_MODEL ** -0.5),
        'b_fox_f': 2.0 + nrm((DEPTH, H_FOX), 0.1),
        'w_cmpk_1': nrm((DEPTH, CMP_LEN * HEAD_DIM, CMP_HIDDEN), (CMP_LEN * HEAD_DIM) ** -0.5),
        'w_cmpk_2': nrm((DEPTH, CMP_HIDDEN, HEAD_DIM), CMP_HIDDEN ** -0.5),
        'pe_cmpk': nrm((DEPTH, CMP_LEN, HEAD_DIM), 0.1),
        'w_cmpv_1': nrm((DEPTH, CMP_LEN * HEAD_DIM, CMP_HIDDEN), (CMP_LEN * HEAD_DIM) ** -0.5),
        'w_cmpv_2': nrm((DEPTH, CMP_HIDDEN, HEAD_DIM), CMP_HIDDEN ** -0.5),
        'pe_cmpv': nrm((DEPTH, CMP_LEN, HEAD_DIM), 0.1),
        'g_fox_out': gain(W_FOX),
        'g_nsa_out': gain(W_NSA),
        'w_out': nrm((DEPTH, W_FOX + W_NSA, D_MODEL), (W_FOX + W_NSA) ** -0.5),
        'g_mix_post': gain(D_MODEL),
        'g_ffn2_pre': gain(D_MODEL),
        'w_ffn2_gate': nrm((DEPTH, D_MODEL, D_FF), D_MODEL ** -0.5),
        'w_ffn2_up': nrm((DEPTH, D_MODEL, D_FF), D_MODEL ** -0.5),
        'w_ffn2_down': nrm((DEPTH, D_FF, D_MODEL), D_FF ** -0.5),
        'g_ffn2_post': gain(D_MODEL),
    }


def reference(x_prompt, x_sample, cache_fox_kv, cache_fox_logf, cache_nsa_kv, state_nsa_win_kv, page_table,
              g_ffn1_pre, w_ffn1_gate, w_ffn1_up, w_ffn1_down, g_ffn1_post, g_mix_pre, w_in, b_fox_f,
              w_cmpk_1, w_cmpk_2, pe_cmpk, w_cmpv_1, w_cmpv_2, pe_cmpv, g_fox_out, g_nsa_out, w_out,
              g_mix_post, g_ffn2_pre, w_ffn2_gate, w_ffn2_up, w_ffn2_down, g_ffn2_post):
    f32 = jnp.float32
    pos_p = jnp.arange(SEQ, dtype=jnp.int32)
    pos_s = PAST_LEN + jnp.arange(DEC_SEQ, dtype=jnp.int32)
    win_p_keep = min(WINDOW, SEQ)
    win_s_keep = min(WINDOW, PAST_LEN + DEC_SEQ)
    yp, ys = x_prompt, x_sample
    fkv_p, flf_p, nkv_p, win_p = [], [], [], []
    fkv_s, flf_s, nkv_s, win_s = [], [], [], []
    for l in range(DEPTH):
        ffn1 = (g_ffn1_pre[l], w_ffn1_gate[l], w_ffn1_up[l], w_ffn1_down[l], g_ffn1_post[l])
        ffn2 = (g_ffn2_pre[l], w_ffn2_gate[l], w_ffn2_up[l], w_ffn2_down[l], g_ffn2_post[l])
        cmp_w = (w_cmpk_1[l], w_cmpk_2[l], pe_cmpk[l], w_cmpv_1[l], w_cmpv_2[l], pe_cmpv[l])

        hp = half_ffn(yp, *ffn1)
        fq, fk, fv, logf, nq, nkv, ng = project(rmsnorm(hp, g_mix_pre[l]), w_in[l], b_fox_f[l], pos_p)
        o_fox = fox_attention(fq, fk, fv, jnp.cumsum(logf, axis=1), pos_p)
        o_nsa = nsa_attention(nq, ng, nkv[:, :, :4], nkv[:, :, 4:], 0, pos_p, *cmp_w)
        hp = hp + rmsnorm(merge_groups(o_fox, o_nsa, g_fox_out[l], g_nsa_out[l], w_out[l]), g_mix_post[l])
        yp = half_ffn(hp, *ffn2)
        fkv_p.append(jnp.stack([fk, fv], axis=2))
        flf_p.append(logf)
        nkv_p.append(nkv[:, :, :4])
        win_p.append(nkv[:, SEQ - win_p_keep:, 4:])

        hs = half_ffn(ys, *ffn1)
        fq, fk, fv, logf, nq, nkv, ng = project(rmsnorm(hs, g_mix_pre[l]), w_in[l], b_fox_f[l], pos_s)
        past_fkv = cache_fox_kv[l][page_table].reshape(DEC_BATCH, PAST_LEN, 2, H_FOX, HEAD_DIM)
        past_lf = cache_fox_logf[l][page_table].reshape(DEC_BATCH, PAST_LEN, H_FOX)
        k_all = jnp.concatenate([past_fkv[:, :, 0], fk], axis=1)
        v_all = jnp.concatenate([past_fkv[:, :, 1], fv], axis=1)
        cum = jnp.cumsum(jnp.concatenate([past_lf.astype(f32), logf], axis=1), axis=1)
        o_fox = fox_attention(fq, k_all, v_all, cum, pos_s)
        past_nkv = cache_nsa_kv[l][page_table].reshape(DEC_BATCH, PAST_LEN, 4, H_KV, HEAD_DIM)
        kv_all = jnp.concatenate([past_nkv, nkv[:, :, :4]], axis=1)
        kw_all = jnp.concatenate([state_nsa_win_kv[l], nkv[:, :, 4:]], axis=1)
        o_nsa = nsa_attention(nq, ng, kv_all, kw_all, PAST_LEN - state_nsa_win_kv.shape[2], pos_s, *cmp_w)
        hs = hs + rmsnorm(merge_groups(o_fox, o_nsa, g_fox_out[l], g_nsa_out[l], w_out[l]), g_mix_post[l])
        ys = half_ffn(hs, *ffn2)
        fkv_s.append(jnp.stack([fk, fv], axis=2))
        flf_s.append(logf)
        nkv_s.append(nkv[:, :, :4])
        win_s.append(kw_all[:, kw_all.shape[1] - win_s_keep:])

    new_fox_kv_prompt = jnp.stack(fkv_p, axis=0)
    new_fox_logf_prompt = jnp.stack(flf_p, axis=0)
    new_nsa_kv_prompt = jnp.stack(nkv_p, axis=0)
    new_win_kv_prompt = jnp.stack(win_p, axis=0)
    new_fox_kv_sample = jnp.stack(fkv_s, axis=0)
    new_fox_logf_sample = jnp.stack(flf_s, axis=0)
    new_nsa_kv_sample = jnp.stack(nkv_s, axis=0)
    new_win_kv_sample = jnp.stack(win_s, axis=0)
    return (yp, ys, new_fox_kv_prompt, new_fox_logf_prompt, new_nsa_kv_prompt, new_win_kv_prompt,
            new_fox_kv_sample, new_fox_logf_sample, new_nsa_kv_sample, new_win_kv_sample)
```

```python
import functools

import jax
import jax.numpy as jnp
from jax import lax
from jax.experimental import pallas as pl
from jax.experimental.pallas import tpu as pltpu

HEAD_DIM = 64
H_FOX = 8
H_NSA = 8
H_KV = 2
GQA_GROUP = H_NSA // H_KV
W_FOX = H_FOX * HEAD_DIM
W_NSA = H_NSA * HEAD_DIM
W_KV = H_KV * HEAD_DIM
CMP_STRIDE = 16
CMP_LEN = 32
SLC_LEN = 64
SLC_TOPN = 16
WINDOW = 512
ROPE_THETA = 10000.0
EPS = 1e-6
NEG = -1e30
FORCE_SCORE = 1e4
N_SMALL = H_FOX + 3 * H_NSA
GATE0 = H_FOX

LANES = 128
MXU_N = 256
VMEM_LIMIT = 56 * 1024 * 1024

F32 = jnp.float32
BF16 = jnp.bfloat16


def _dot(a, b):
    return jnp.dot(a, b, preferred_element_type=F32)


def _dot_nt(a, b):
    return lax.dot_general(a, b, (((1,), (1,)), ((), ())), preferred_element_type=F32)


def _div(x, n):
    assert n & (n - 1) == 0
    return lax.shift_right_logical(x, jnp.int32(n.bit_length() - 1))


def _split3(x):
    hi = x.astype(BF16)
    r1 = x - hi.astype(F32)
    mid = r1.astype(BF16)
    lo = (r1 - mid.astype(F32)).astype(BF16)
    return hi, mid, lo


def _rms(x, g):
    return x * lax.rsqrt(jnp.mean(x * x, axis=-1, keepdims=True) + EPS) * g


def _ff_chunks(d_ff):
    step = 6 * MXU_N
    return tuple((c, min(c + step, d_ff)) for c in range(0, d_ff, step))


def _ffn_core(x, gpre, wg_ref, wu_ref, wd_ref, gpost, chunks):
    xn = _rms(x, gpre).astype(BF16)
    acc = jnp.zeros(x.shape, F32)
    for c0, c1 in chunks:
        g = _dot(xn, wg_ref[:, c0:c1])
        u = _dot(xn, wu_ref[:, c0:c1])
        hm = (g * jax.nn.sigmoid(g) * u).astype(BF16)
        acc = acc + _dot(hm, wd_ref[c0:c1, :])
    return x + 0.5 * _rms(acc, gpost)


def _const_spec(shape):
    nd = len(shape)
    return pl.BlockSpec(shape, lambda *_: (0,) * nd, pipeline_mode=pl.Buffered(1))


def _params(sem):
    return pltpu.CompilerParams(dimension_semantics=sem, vmem_limit_bytes=VMEM_LIMIT)


def _ffn_kernel(x_ref, gpre_ref, wg_ref, wu_ref, wd_ref, gpost_ref, o_ref, *, chunks):
    o_ref[...] = _ffn_core(x_ref[...], gpre_ref[...], wg_ref, wu_ref, wd_ref, gpost_ref[...], chunks)


def _half_ffn(x, gpre, wg, wu, wd, gpost, tm):
    n, d = x.shape
    d_ff = wg.shape[1]
    row = pl.BlockSpec((tm, d), lambda i: (i, 0))
    return pl.pallas_call(
        functools.partial(_ffn_kernel, chunks=_ff_chunks(d_ff)),
        grid=(n // tm,),
        in_specs=[row, _const_spec((1, d)), _const_spec((d, d_ff)), _const_spec((d, d_ff)),
                  _const_spec((d_ff, d)), _const_spec((1, d))],
        out_specs=row,
        out_shape=jax.ShapeDtypeStruct((n, d), F32),
        compiler_params=_params(("parallel",)),
        name="half_ffn",
    )(x, gpre, wg, wu, wd, gpost)


_C_FQ, _C_FK, _C_FV, _C_NQ, _C_KV, _C_NQR, _C_KR, _C_END = 0, 512, 1024, 1536, 2048, 2816, 3328, 3712


def _proj_kernel(h_ref, g_ref, wb_ref, ws_ref, bf_ref, cos_ref, sin_ref,
                 fq_ref, fkv_ref, fk16_ref, fv16_ref, small_ref, nq_ref, nkv4_ref, win_ref, nkv16_ref):
    n = _rms(h_ref[...], g_ref[...]).astype(BF16)

    def mm(c0, c1):
        return _dot(n, wb_ref[:, c0:c1])

    scale = HEAD_DIM ** -0.5
    fq_ref[...] = mm(_C_FQ, _C_FK) * scale
    fk = mm(_C_FK, _C_FV)
    fv = mm(_C_FV, _C_NQ)
    fkv_ref[:, 0:W_FOX] = fk
    fkv_ref[:, W_FOX:2 * W_FOX] = fv
    fk16_ref[...] = fk.astype(BF16)
    fv16_ref[...] = fv.astype(BF16)

    cos = cos_ref[...]
    sin = sin_ref[...]
    nq = mm(_C_NQ, _C_KV)
    nqr = mm(_C_NQR, _C_KR)
    for c in range(W_NSA // LANES):
        sl = slice(c * LANES, (c + 1) * LANES)
        nq_ref[:, sl] = (nq[:, sl] * cos + nqr[:, sl] * sin) * scale

    kv = mm(_C_KV, _C_NQR)
    kr = mm(_C_KR, _C_END)
    for br in range(3):
        k = kv[:, br * 2 * W_KV: br * 2 * W_KV + W_KV] * cos + kr[:, br * W_KV:(br + 1) * W_KV] * sin
        v = kv[:, br * 2 * W_KV + W_KV:(br + 1) * 2 * W_KV]
        dst = nkv4_ref if br < 2 else win_ref
        off = (br % 2) * 2 * W_KV if br < 2 else 0
        dst[:, off:off + W_KV] = k
        dst[:, off + W_KV:off + 2 * W_KV] = v
        nkv16_ref[:, br * 2 * W_KV: br * 2 * W_KV + W_KV] = k.astype(BF16)
        nkv16_ref[:, br * 2 * W_KV + W_KV:(br + 1) * 2 * W_KV] = v.astype(BF16)

    sm = _dot(n, ws_ref[...]) + bf_ref[...]
    lane = lax.broadcasted_iota(jnp.int32, sm.shape, 1)
    log_sig = jnp.minimum(sm, 0.0) - jnp.log(1.0 + jnp.exp(-jnp.abs(sm)))
    small_ref[...] = jnp.where(lane < H_FOX, log_sig, jax.nn.sigmoid(sm))


def _project(h, g, w_big, w_small, b_small, cos, sin, tm, n_pos_tiles):
    n, d = h.shape
    row = lambda w: pl.BlockSpec((tm, w), lambda i: (i, 0))
    pos = pl.BlockSpec((tm, LANES), lambda i: (i % n_pos_tiles, 0))
    widths = (W_FOX, 2 * W_FOX, W_FOX, W_FOX, LANES, W_NSA, 4 * W_KV, 2 * W_KV, 6 * W_KV)
    dtypes = (F32, F32, BF16, BF16, F32, F32, F32, F32, BF16)
    return pl.pallas_call(
        _proj_kernel,
        grid=(n // tm,),
        in_specs=[row(d), _const_spec((1, d)), _const_spec(w_big.shape), _const_spec(w_small.shape),
                  _const_spec((1, LANES)), pos, pos],
        out_specs=[row(w) for w in widths],
        out_shape=[jax.ShapeDtypeStruct((n, w), dt) for w, dt in zip(widths, dtypes)],
        compiler_params=_params(("parallel",)),
        name="project",
    )(h, g, w_big, w_small, b_small, cos, sin)


def _merge_ffn_kernel(h_ref, of_ref, on_ref, gf_ref, gn_ref, wo_ref, gmix_ref,
                      gpre_ref, wg_ref, wu_ref, wd_ref, gpost_ref, y_ref, *, chunks):
    of = _rms(of_ref[...], gf_ref[...]).astype(BF16)
    on = _rms(on_ref[...], gn_ref[...]).astype(BF16)
    mrg = _dot(of, wo_ref[0:W_FOX, :]) + _dot(on, wo_ref[W_FOX:W_FOX + W_NSA, :])
    h2 = h_ref[...] + _rms(mrg, gmix_ref[...])
    y_ref[...] = _ffn_core(h2, gpre_ref[...], wg_ref, wu_ref, wd_ref, gpost_ref[...], chunks)


def _merge_ffn(h, o_fox, o_nsa, gf, gn, w_out, gmix, gpre, wg, wu, wd, gpost, tm):
    n, d = h.shape
    d_ff = wg.shape[1]
    row = lambda w: pl.BlockSpec((tm, w), lambda i: (i, 0))
    return pl.pallas_call(
        functools.partial(_merge_ffn_kernel, chunks=_ff_chunks(d_ff)),
        grid=(n // tm,),
        in_specs=[row(d), row(W_FOX), row(W_NSA), _const_spec((1, W_FOX)), _const_spec((1, W_NSA)),
                  _const_spec(w_out.shape), _const_spec((1, d)), _const_spec((1, d)),
                  _const_spec((d, d_ff)), _const_spec((d, d_ff)), _const_spec((d_ff, d)), _const_spec((1, d))],
        out_specs=row(d),
        out_shape=jax.ShapeDtypeStruct((n, d), F32),
        compiler_params=_params(("parallel",)),
        name="merge_ffn",
    )(h, o_fox, o_nsa, gf, gn, w_out, gmix, gpre, wg, wu, wd, gpost)


def _cumsum_kernel(x_ref, o_ref, carry_ref):
    @pl.when(pl.program_id(1) == 0)
    def _():
        carry_ref[...] = jnp.zeros(carry_ref.shape, F32)

    x = x_ref[...]
    tc = x.shape[0]
    r = lax.broadcasted_iota(jnp.int32, (tc, tc), 0)
    c = lax.broadcasted_iota(jnp.int32, (tc, tc), 1)
    tri = (c <= r).astype(BF16)
    hi, mid, lo = _split3(x)
    cs = _dot(tri, hi) + _dot(tri, mid) + _dot(tri, lo) + carry_ref[...]
    o_ref[...] = cs
    carry_ref[...] = cs[tc - 1:tc, :]


def _cumsum(x, tc):
    b, t, w = x.shape
    blk = pl.BlockSpec((None, tc, w), lambda i, j: (i, j, 0))
    return pl.pallas_call(
        _cumsum_kernel,
        grid=(b, t // tc),
        in_specs=[blk],
        out_specs=blk,
        out_shape=jax.ShapeDtypeStruct(x.shape, F32),
        scratch_shapes=[pltpu.VMEM((1, w), F32)],
        compiler_params=_params(("parallel", "arbitrary")),
        name="logf_cumsum",
    )(x)


def _fox_kernel(q_ref, k_ref, v_ref, cq_ref, ck_ref, o_ref, m_ref, l_ref, acc_ref, *, tq):
    hp = pl.program_id(1)
    qi = pl.program_id(2)
    lane = lax.broadcasted_iota(jnp.int32, (tq, LANES), 1)
    lo = lane < HEAD_DIM
    q2 = q_ref[...]
    qz = (jnp.where(lo, q2, 0.0).astype(BF16), jnp.where(lo, 0.0, q2).astype(BF16))
    cq_all = cq_ref[...]
    cq = [jnp.sum(jnp.where(lane == 2 * hp + e, cq_all, 0.0), axis=1, keepdims=True) for e in (0, 1)]
    row8 = lax.broadcasted_iota(jnp.int32, (8, tq), 0)
    qpos = lax.broadcasted_iota(jnp.int32, (tq, tq), 0)
    kpos = lax.broadcasted_iota(jnp.int32, (tq, tq), 1)

    m_ref[...] = jnp.full(m_ref.shape, NEG, F32)
    l_ref[...] = jnp.zeros(l_ref.shape, F32)
    acc_ref[...] = jnp.zeros(acc_ref.shape, F32)

    def step(kt, masked):
        k0 = pl.multiple_of(kt * tq, tq)
        k2 = k_ref[pl.ds(k0, tq), :]
        v2 = v_ref[pl.ds(k0, tq), :]
        ck_all = ck_ref[:, pl.ds(k0, tq)]
        for e in (0, 1):
            ck = jnp.sum(jnp.where(row8 == 2 * hp + e, ck_all, 0.0), axis=0, keepdims=True)
            s = _dot_nt(qz[e], k2) + (cq[e] - ck)
            if masked:
                s = jnp.where(kpos <= qpos, s, NEG)
            m_old = m_ref[e]
            m_new = jnp.maximum(m_old, jnp.max(s, axis=1, keepdims=True))
            alpha = jnp.exp(m_old - m_new)
            p = jnp.exp(s - m_new)
            l_ref[e] = alpha * l_ref[e] + jnp.sum(p, axis=1, keepdims=True)
            acc_ref[e] = alpha * acc_ref[e] + _dot(p.astype(BF16), v2)
            m_ref[e] = m_new

    def body(kt, carry):
        step(kt, False)
        return carry

    lax.fori_loop(0, qi, body, 0)
    step(qi, True)
    o0 = acc_ref[0] * (1.0 / l_ref[0])
    o1 = acc_ref[1] * (1.0 / l_ref[1])
    o_ref[...] = jnp.where(lo, o0, o1)


def _fox_attention(fq, fk16, fv16, cum, cum_t, tq):
    b, t, _ = fq.shape
    qblk = pl.BlockSpec((None, tq, LANES), lambda i, hp, qi: (i, qi, hp))
    kvblk = pl.BlockSpec((None, t, LANES), lambda i, hp, qi: (i, 0, hp))
    return pl.pallas_call(
        functools.partial(_fox_kernel, tq=tq),
        grid=(b, W_FOX // LANES, t // tq),
        in_specs=[qblk, kvblk, kvblk,
                  pl.BlockSpec((None, tq, LANES), lambda i, hp, qi: (i, qi, 0)),
                  pl.BlockSpec((None, H_FOX, t), lambda i, hp, qi: (i, 0, 0))],
        out_specs=qblk,
        out_shape=jax.ShapeDtypeStruct((b, t, W_FOX), F32),
        scratch_shapes=[pltpu.VMEM((2, tq, 1), F32), pltpu.VMEM((2, tq, 1), F32),
                        pltpu.VMEM((2, tq, LANES), F32)],
        compiler_params=_params(("parallel", "parallel", "arbitrary")),
        name="fox_prompt",
    )(fq, fk16, fv16, cum, cum_t)


def _compress_core(load_rows, n_ch, wj_ref, pe_ref, wpe_ref, w2_ref):
    acc = jnp.zeros((n_ch, 2 * 4 * HEAD_DIM), F32)
    for j in range(CMP_STRIDE):
        xk, xv = load_rows(j)
        acc = acc + _dot(xk.astype(BF16), wj_ref[j, 0:W_KV, :]) + _dot(xv.astype(BF16), wj_ref[j, W_KV:2 * W_KV, :])
    first = acc[:, 0:4 * HEAD_DIM]
    second = pltpu.roll(acc[:, 4 * HEAD_DIM:], n_ch - 1, axis=0)
    pe_term = _dot(pe_ref[...], wpe_ref[...])[0:1, :]
    hid = jax.nn.gelu(first + second + pe_term)
    return _dot(hid.astype(BF16), w2_ref[...])


def _compress_kernel(xk_ref, xv_ref, wj_ref, pe_ref, wpe_ref, w2_ref, kc_ref, vc_ref, *, n_ch):
    rows = lambda j: (xk_ref[pl.ds(j, n_ch, stride=CMP_STRIDE), :], xv_ref[pl.ds(j, n_ch, stride=CMP_STRIDE), :])
    out = _compress_core(rows, n_ch, wj_ref, pe_ref, wpe_ref, w2_ref)
    kc_ref[...] = out[:, 0:W_KV].astype(BF16)
    vc_ref[...] = out[:, W_KV:2 * W_KV].astype(BF16)


def _compress(nkv4, cw):
    b, t, _ = nkv4.shape
    n_ch = t // CMP_STRIDE
    out = pl.BlockSpec((None, n_ch, W_KV), lambda i: (i, 0, 0))
    return pl.pallas_call(
        functools.partial(_compress_kernel, n_ch=n_ch),
        grid=(b,),
        in_specs=[pl.BlockSpec((None, t, W_KV), lambda i: (i, 0, 0)),
                  pl.BlockSpec((None, t, W_KV), lambda i: (i, 0, 1))] + [_const_spec(w.shape) for w in cw],
        out_specs=[out, out],
        out_shape=[jax.ShapeDtypeStruct((b, n_ch, W_KV), BF16)] * 2,
        compiler_params=_params(("parallel",)),
        name="compress_prompt",
    )(nkv4, nkv4, *cw)


def _overlap(n_ch, n_slc):
    n = lax.broadcasted_iota(jnp.int32, (n_ch, LANES), 0)
    j = lax.broadcasted_iota(jnp.int32, (n_ch, LANES), 1)
    hit = ((n * CMP_STRIDE <= j * SLC_LEN + SLC_LEN - 1) & (n * CMP_STRIDE + CMP_LEN - 1 >= j * SLC_LEN)
           & (n < n_ch - 1) & (j < n_slc))
    return hit.astype(BF16)


def _select_blocks(imp, pos, n_slc, n_sel):
    lane = lax.broadcasted_iota(jnp.int32, imp.shape, 1)
    qblk = _div(pos, SLC_LEN)
    forced = (lane == 0) | (lane == qblk) | (lane == qblk - 1)
    valid = lane * SLC_LEN <= pos
    score = jnp.where(valid, jnp.where(forced, FORCE_SCORE, imp), -1.0)
    score = jnp.where(lane < n_slc, score, -2.0)
    cnt = jnp.zeros(imp.shape, F32)
    for i in range(n_slc):
        ci = score[:, i:i + 1]
        beats = (ci > score) | ((ci == score) & (lane > i))
        cnt = cnt + jnp.where(beats, 1.0, 0.0)
    return jnp.where((cnt < n_sel) & (lane < n_slc), 1.0, 0.0)


def _nsa_kernel(q_ref, gate_ref, kc_ref, vc_ref, sk_ref, sv_ref, wk_ref, wv_ref, o_ref,
                q4_ref, m_ref, l_ref, acc_ref, *, tq, tk, n_ch, n_slc, wlen):
    h = pl.program_id(1)
    qi = pl.program_id(2)
    q0 = qi * tq
    g = GQA_GROUP
    lane = lax.broadcasted_iota(jnp.int32, (tq, LANES), 1)
    lane_half = _div(lane, HEAD_DIM)
    pos = q0 + lax.broadcasted_iota(jnp.int32, (tq, 1), 0)

    q = q_ref[...]
    for i in range(g):
        piece = q[:, (i // 2) * LANES:(i // 2 + 1) * LANES]
        src = jnp.where(h == (i % 2), piece, pltpu.roll(piece, HEAD_DIM, axis=1))
        q4_ref[i * tq:(i + 1) * tq, :] = jnp.where(lane_half == h, src, 0.0).astype(BF16)
    q4 = q4_ref[...]

    sc = _dot_nt(q4, kc_ref[...]).reshape(g, tq, n_ch)
    n_idx = lax.broadcasted_iota(jnp.int32, (tq, n_ch), 1)
    ok = ((n_idx * CMP_STRIDE + CMP_LEN - 1 <= pos) & (n_idx < n_ch - 1))[None]
    sc = jnp.where(ok, sc, NEG)
    e = jnp.where(ok, jnp.exp(sc - jnp.max(sc, axis=-1, keepdims=True)), 0.0)
    lc = jnp.sum(e, axis=-1, keepdims=True)
    a = (e * (1.0 / jnp.where(lc > 0.0, lc, 1.0))).reshape(g * tq, n_ch)
    a_hi = a.astype(BF16)
    a_lo = (a - a_hi.astype(F32)).astype(BF16)
    o_cmp = _dot(a_hi, vc_ref[...]).reshape(g, tq, LANES)
    ov = _overlap(n_ch, n_slc)
    imp = jnp.sum((_dot(a_hi, ov) + _dot(a_lo, ov)).reshape(g, tq, LANES), axis=0)
    sel = _select_blocks(imp, pos, n_slc, min(SLC_TOPN, n_slc)).astype(BF16)

    m_ref[...] = jnp.full(m_ref.shape, NEG, F32)
    l_ref[...] = jnp.zeros(l_ref.shape, F32)
    acc_ref[...] = jnp.zeros(acc_ref.shape, F32)
    blk_of_key = (lax.broadcasted_iota(jnp.int32, (LANES, tk), 0)
                  - _div(lax.broadcasted_iota(jnp.int32, (LANES, tk), 1), SLC_LEN))
    kcol = lax.broadcasted_iota(jnp.int32, (tq, tk), 1)

    def body(kt, carry):
        k0 = pl.multiple_of(kt * tk, tk)
        k2 = sk_ref[pl.ds(k0, tk), :]
        v2 = sv_ref[pl.ds(k0, tk), :]
        s = _dot_nt(q4, k2).reshape(g, tq, tk)
        expand = (blk_of_key == kt * (tk // SLC_LEN)).astype(BF16)
        allow = ((_dot(sel, expand) > 0.5) & (k0 + kcol <= pos))[None]
        s = jnp.where(allow, s, NEG)
        m_old = m_ref[...]
        m_new = jnp.maximum(m_old, jnp.max(s, axis=-1, keepdims=True))
        alpha = jnp.exp(m_old - m_new)
        p = jnp.where(allow, jnp.exp(s - m_new), 0.0)
        l_ref[...] = alpha * l_ref[...] + jnp.sum(p, axis=-1, keepdims=True)
        pv = _dot(p.reshape(g * tq, tk).astype(BF16), v2).reshape(g, tq, LANES)
        acc_ref[...] = alpha * acc_ref[...] + pv
        m_ref[...] = m_new
        return carry

    lax.fori_loop(0, _div(q0 + tq - 1, tk) + 1, body, 0)
    o_slc = acc_ref[...] * (1.0 / l_ref[...])

    ws = pl.multiple_of(jnp.maximum(q0 + tq - wlen, 0), tq)
    sw = _dot_nt(q4, wk_ref[pl.ds(ws, wlen), :]).reshape(g, tq, wlen)
    dist = pos - (ws + lax.broadcasted_iota(jnp.int32, (tq, wlen), 1))
    okw = ((dist >= 0) & (dist < WINDOW))[None]
    sw = jnp.where(okw, sw, NEG)
    pw = jnp.where(okw, jnp.exp(sw - jnp.max(sw, axis=-1, keepdims=True)), 0.0)
    lw = jnp.sum(pw, axis=-1, keepdims=True)
    o_win = _dot(pw.reshape(g * tq, wlen).astype(BF16), wv_ref[pl.ds(ws, wlen), :]).reshape(g, tq, LANES)
    o_win = o_win * (1.0 / lw)

    gates = gate_ref[...]
    vals = []
    for i in range(g):
        col0 = GATE0 + 3 * (g * h + i)
        gc = [jnp.sum(jnp.where(lane == col0 + c, gates, 0.0), axis=1, keepdims=True) for c in range(3)]
        val = gc[0] * o_cmp[i] + gc[1] * o_slc[i] + gc[2] * o_win[i]
        vals.append(jnp.where(h == (i % 2), val, pltpu.roll(val, HEAD_DIM, axis=1)))
    for pr in range(g // 2):
        o_ref[:, pr * LANES:(pr + 1) * LANES] = jnp.where(lane_half == 0, vals[2 * pr], vals[2 * pr + 1])


def _nsa_attention(nq, small, kc, vc, nkv16, tq, tk):
    b, t, _ = nq.shape
    n_ch = kc.shape[1]
    n_slc = -(-t // SLC_LEN)
    wlen = min(t, WINDOW + tq)
    g = GQA_GROUP
    qblk = pl.BlockSpec((None, tq, g * HEAD_DIM), lambda i, h, qi: (i, qi, h))
    cblk = pl.BlockSpec((None, n_ch, W_KV), lambda i, h, qi: (i, 0, 0))
    lane_blk = lambda c: pl.BlockSpec((None, t, W_KV), lambda i, h, qi: (i, 0, c))
    return pl.pallas_call(
        functools.partial(_nsa_kernel, tq=tq, tk=tk, n_ch=n_ch, n_slc=n_slc, wlen=wlen),
        grid=(b, H_KV, t // tq),
        in_specs=[qblk, pl.BlockSpec((None, tq, LANES), lambda i, h, qi: (i, qi, 0)), cblk, cblk,
                  lane_blk(2), lane_blk(3), lane_blk(4), lane_blk(5)],
        out_specs=qblk,
        out_shape=jax.ShapeDtypeStruct((b, t, W_NSA), F32),
        scratch_shapes=[pltpu.VMEM((g * tq, LANES), BF16), pltpu.VMEM((g, tq, 1), F32),
                        pltpu.VMEM((g, tq, 1), F32), pltpu.VMEM((g, tq, LANES), F32)],
        compiler_params=_params(("parallel", "parallel", "arbitrary")),
        name="nsa_prompt",
    )(nq, small, kc, vc, nkv16, nkv16, nkv16, nkv16)


def _fox_dec_kernel(pt_ref, q_ref, knew_ref, vnew_ref, lfnew_ref, kv_ref, lft_ref, o_ref,
                    m_ref, l_ref, acc_ref, carry_ref):
    del pt_ref
    j = pl.program_id(1)
    row = lax.broadcasted_iota(jnp.int32, (H_FOX, W_FOX), 0)
    lane = lax.broadcasted_iota(jnp.int32, (H_FOX, W_FOX), 1)
    diag = _div(lane, HEAD_DIM) == row
    qbd = jnp.where(diag, jnp.broadcast_to(q_ref[...], (H_FOX, W_FOX)), 0.0).astype(BF16)

    @pl.when(j == 0)
    def _():
        kn = knew_ref[...].astype(BF16).astype(F32)
        m_ref[...] = jnp.sum(qbd.astype(F32) * kn, axis=1, keepdims=True)
        l_ref[...] = jnp.ones(l_ref.shape, F32)
        acc_ref[...] = jnp.broadcast_to(vnew_ref[...].astype(BF16).astype(F32), (H_FOX, W_FOX))
        carry_ref[...] = lfnew_ref[...]

    kv = kv_ref[...]
    page = kv.shape[0]
    k = kv[:, 0:W_FOX].astype(BF16)
    v = kv[:, W_FOX:2 * W_FOX].astype(BF16)
    lft = lft_ref[...]
    later = (lax.broadcasted_iota(jnp.int32, (page, page), 0)
             > lax.broadcasted_iota(jnp.int32, (page, page), 1)).astype(BF16)
    hi, mid, lo = _split3(lft)
    suffix = _dot(hi, later) + _dot(mid, later) + _dot(lo, later)
    s = _dot_nt(qbd, k) + (carry_ref[...] + suffix)
    m_old = m_ref[...]
    m_new = jnp.maximum(m_old, jnp.max(s, axis=1, keepdims=True))
    alpha = jnp.exp(m_old - m_new)
    p = jnp.exp(s - m_new)
    l_ref[...] = alpha * l_ref[...] + jnp.sum(p, axis=1, keepdims=True)
    acc_ref[...] = alpha * acc_ref[...] + _dot(p.astype(BF16), v)
    m_ref[...] = m_new
    carry_ref[...] = carry_ref[...] + jnp.sum(lft, axis=1, keepdims=True)

    @pl.when(j == pl.num_programs(1) - 1)
    def _():
        o = acc_ref[...] * (1.0 / l_ref[...])
        o_ref[...] = jnp.sum(jnp.where(diag, o, 0.0), axis=0, keepdims=True)


def _fox_decode(page_table, fq, fk, fv, lf_col, cache_kv, cache_lft):
    s, n_pages = page_table.shape
    page = cache_kv.shape[1]
    tok = pl.BlockSpec((None, 1, W_FOX), lambda i, j, pt: (i, 0, 0))
    pg = lambda i, j, pt: (pt[i * n_pages + n_pages - 1 - j], 0, 0)
    return pl.pallas_call(
        _fox_dec_kernel,
        grid_spec=pltpu.PrefetchScalarGridSpec(
            num_scalar_prefetch=1,
            grid=(s, n_pages),
            in_specs=[tok, tok, tok, pl.BlockSpec((None, H_FOX, 1), lambda i, j, pt: (i, 0, 0)),
                      pl.BlockSpec((None, page, 2 * W_FOX), pg),
                      pl.BlockSpec((None, H_FOX, page), pg)],
            out_specs=tok,
            scratch_shapes=[pltpu.VMEM((H_FOX, 1), F32), pltpu.VMEM((H_FOX, 1), F32),
                            pltpu.VMEM((H_FOX, W_FOX), F32), pltpu.VMEM((H_FOX, 1), F32)],
        ),
        out_shape=jax.ShapeDtypeStruct((s, 1, W_FOX), F32),
        compiler_params=_params(("parallel", "arbitrary")),
        name="fox_decode",
    )(page_table.reshape(-1), fq, fk, fv, lf_col, cache_kv, cache_lft)


def _nsa_dec_kernel(pt_ref, q_ref, new_ref, gate_ref, win_ref, page_ref, wj_ref, pe_ref, wpe_ref, w2_ref,
                    o_ref, xk_buf, xv_buf, sel_ref, ocmp_ref, m_ref, l_ref, acc_ref,
                    *, n_pages, past_len):
    del pt_ref
    j = pl.program_id(1)
    page = page_ref.shape[0]
    n_ch = past_len // CMP_STRIDE
    n_slc = past_len // SLC_LEN + 1
    blocks_per_page = page // SLC_LEN
    win_buf = win_ref.shape[0]
    row = lax.broadcasted_iota(jnp.int32, (H_NSA, LANES), 0)
    lane = lax.broadcasted_iota(jnp.int32, (H_NSA, LANES), 1)
    own_half = _div(lane, HEAD_DIM) == _div(row, GQA_GROUP)

    q = q_ref[...]
    qbd = jnp.zeros((H_NSA, LANES), F32)
    for i in range(H_NSA):
        piece = q[:, (i // 2) * LANES:(i // 2 + 1) * LANES]
        if (i % 2) != (i // GQA_GROUP):
            piece = pltpu.roll(piece, HEAD_DIM, axis=1)
        qbd = jnp.where(row == i, jnp.broadcast_to(piece, (H_NSA, LANES)), qbd)
    qbd = jnp.where(own_half, qbd, 0.0)
    qbd16 = qbd.astype(BF16)
    new = new_ref[...]

    @pl.when(j < n_pages)
    def _():
        r0 = pl.multiple_of(j * page, page)
        xk_buf[pl.ds(r0, page), :] = page_ref[:, 0:W_KV]
        xv_buf[pl.ds(r0, page), :] = page_ref[:, W_KV:2 * W_KV]

    @pl.when(j == n_pages - 1)
    def _():
        rows = lambda jj: (xk_buf[pl.ds(jj, n_ch, stride=CMP_STRIDE), :],
                           xv_buf[pl.ds(jj, n_ch, stride=CMP_STRIDE), :])
        out = _compress_core(rows, n_ch, wj_ref, pe_ref, wpe_ref, w2_ref)
        kc = out[:, 0:W_KV].astype(BF16)
        vc = out[:, W_KV:2 * W_KV].astype(BF16)
        n_idx = lax.broadcasted_iota(jnp.int32, (H_NSA, n_ch), 1)
        ok = (n_idx * CMP_STRIDE + CMP_LEN - 1 <= past_len) & (n_idx < n_ch - 1)
        sc = jnp.where(ok, _dot_nt(qbd16, kc), NEG)
        e = jnp.where(ok, jnp.exp(sc - jnp.max(sc, axis=1, keepdims=True)), 0.0)
        lc = jnp.sum(e, axis=1, keepdims=True)
        a = e * (1.0 / jnp.where(lc > 0.0, lc, 1.0))
        a_hi = a.astype(BF16)
        a_lo = (a - a_hi.astype(F32)).astype(BF16)
        ocmp_ref[...] = _dot(a_hi, vc)
        ov = _overlap(n_ch, n_slc)
        imp8 = _dot(a_hi, ov) + _dot(a_lo, ov)
        imp = jnp.zeros((H_NSA, LANES), F32)
        for h in range(H_KV):
            tot = jnp.sum(imp8[h * GQA_GROUP:(h + 1) * GQA_GROUP], axis=0, keepdims=True)
            imp = jnp.where(_div(row, GQA_GROUP) == h, jnp.broadcast_to(tot, (H_NSA, LANES)), imp)
        pos = jnp.full((H_NSA, 1), past_len, jnp.int32)
        sel_ref[...] = _select_blocks(imp, pos, n_slc, min(SLC_TOPN, n_slc))

    @pl.when(j == n_pages)
    def _():
        kn = new[:, 2 * W_KV:3 * W_KV].astype(BF16).astype(F32)
        m_ref[...] = jnp.sum(qbd16.astype(F32) * kn, axis=1, keepdims=True)
        l_ref[...] = jnp.ones(l_ref.shape, F32)
        acc_ref[...] = jnp.broadcast_to(new[:, 3 * W_KV:4 * W_KV].astype(BF16).astype(F32), (H_NSA, LANES))

    @pl.when(j >= n_pages)
    def _():
        pg = page_ref[...]
        k2 = pg[:, 0:W_KV].astype(BF16)
        v2 = pg[:, W_KV:2 * W_KV].astype(BF16)
        blk = (lax.broadcasted_iota(jnp.int32, (LANES, page), 0)
               - _div(lax.broadcasted_iota(jnp.int32, (LANES, page), 1), SLC_LEN))
        expand = (blk == (j - n_pages) * blocks_per_page).astype(BF16)
        allow = _dot(sel_ref[...].astype(BF16), expand) > 0.5
        s = jnp.where(allow, _dot_nt(qbd16, k2), NEG)
        m_old = m_ref[...]
        m_new = jnp.maximum(m_old, jnp.max(s, axis=1, keepdims=True))
        alpha = jnp.exp(m_old - m_new)
        p = jnp.where(allow, jnp.exp(s - m_new), 0.0)
        l_ref[...] = alpha * l_ref[...] + jnp.sum(p, axis=1, keepdims=True)
        acc_ref[...] = alpha * acc_ref[...] + _dot(p.astype(BF16), v2)
        m_ref[...] = m_new

    @pl.when(j == 2 * n_pages - 1)
    def _():
        o_slc = acc_ref[...] * (1.0 / l_ref[...])
        wkv = win_ref[...]
        kw = wkv[:, 0:W_KV].astype(BF16)
        vw = wkv[:, W_KV:2 * W_KV].astype(BF16)
        r = lax.broadcasted_iota(jnp.int32, (H_NSA, win_buf), 1)
        okw = (win_buf - r) < WINDOW
        sw = jnp.where(okw, _dot_nt(qbd16, kw), NEG)
        swn = jnp.sum(qbd16.astype(F32) * new[:, 4 * W_KV:5 * W_KV].astype(BF16).astype(F32),
                      axis=1, keepdims=True)
        mw = jnp.maximum(jnp.max(sw, axis=1, keepdims=True), swn)
        pw = jnp.where(okw, jnp.exp(sw - mw), 0.0)
        pn = jnp.exp(swn - mw)
        lw = jnp.sum(pw, axis=1, keepdims=True) + pn
        vn = new[:, 5 * W_KV:6 * W_KV].astype(BF16).astype(F32)
        o_win = (_dot(pw.astype(BF16), vw) + pn.astype(BF16).astype(F32) * vn) * (1.0 / lw)

        gates = gate_ref[...]
        o_cmp = ocmp_ref[...]
        pieces = []
        for i in range(H_NSA):
            c0 = GATE0 + 3 * i
            val = (gates[:, c0:c0 + 1] * o_cmp[i:i + 1] + gates[:, c0 + 1:c0 + 2] * o_slc[i:i + 1]
                   + gates[:, c0 + 2:c0 + 3] * o_win[i:i + 1])
            if (i % 2) != (i // GQA_GROUP):
                val = pltpu.roll(val, HEAD_DIM, axis=1)
            pieces.append(val)
        lane1 = lax.broadcasted_iota(jnp.int32, (1, LANES), 1)
        for pr in range(H_NSA // 2):
            o_ref[:, pr * LANES:(pr + 1) * LANES] = jnp.where(lane1 < HEAD_DIM, pieces[2 * pr], pieces[2 * pr + 1])


def _nsa_decode(page_table, nq, new_row, small, win_state, cache, cw):
    s, n_pages = page_table.shape
    page = cache.shape[1]
    past_len = n_pages * page
    n_ch = past_len // CMP_STRIDE
    win_buf = win_state.shape[1]
    tok = lambda w: pl.BlockSpec((None, 1, w), lambda i, j, pt: (i, 0, 0))
    pg = lambda i, j, pt: (pt[i * n_pages + j % n_pages], 0, j // n_pages)
    const = lambda shape: pl.BlockSpec(shape, lambda i, j, pt: (0,) * len(shape), pipeline_mode=pl.Buffered(1))
    return pl.pallas_call(
        functools.partial(_nsa_dec_kernel, n_pages=n_pages, past_len=past_len),
        grid_spec=pltpu.PrefetchScalarGridSpec(
            num_scalar_prefetch=1,
            grid=(s, 2 * n_pages),
            in_specs=[tok(W_NSA), tok(6 * W_KV), tok(LANES),
                      pl.BlockSpec((None, win_buf, 2 * W_KV), lambda i, j, pt: (i, 0, 0)),
                      pl.BlockSpec((None, page, 2 * W_KV), pg)] + [const(w.shape) for w in cw],
            out_specs=tok(W_NSA),
            scratch_shapes=[pltpu.VMEM((past_len, W_KV), F32), pltpu.VMEM((past_len, W_KV), F32),
                            pltpu.VMEM((H_NSA, LANES), F32), pltpu.VMEM((H_NSA, LANES), F32),
                            pltpu.VMEM((H_NSA, 1), F32), pltpu.VMEM((H_NSA, 1), F32),
                            pltpu.VMEM((H_NSA, LANES), F32)],
        ),
        out_shape=jax.ShapeDtypeStruct((s, 1, W_NSA), F32),
        compiler_params=_params(("parallel", "arbitrary")),
        name="nsa_decode",
    )(page_table.reshape(-1), nq, new_row, small, win_state, cache, *cw)


def _rot_cols(w):
    d, n = w.shape
    w = w.reshape(d, n // HEAD_DIM, 2, HEAD_DIM // 2)
    return jnp.stack([-w[:, :, 1], w[:, :, 0]], axis=2).reshape(d, n)


def _prep_projection(w_in, b_f):
    c = [0, W_FOX, 2 * W_FOX, 3 * W_FOX, 3 * W_FOX + H_FOX, 3 * W_FOX + H_FOX + W_NSA,
         3 * W_FOX + H_FOX + W_NSA + 6 * W_KV]
    fq, fk, fv, ff, nq, kv = (w_in[:, c[i]:c[i + 1]] for i in range(6))
    gt = w_in[:, c[6]:]
    ks = jnp.concatenate([kv[:, br * 2 * W_KV: br * 2 * W_KV + W_KV] for br in range(3)], axis=1)
    w_big = jnp.concatenate([fq, fk, fv, nq, kv, _rot_cols(nq), _rot_cols(ks)], axis=1).astype(BF16)
    d = w_in.shape[0]
    w_small = jnp.concatenate([ff, gt, jnp.zeros((d, LANES - N_SMALL), w_in.dtype)], axis=1).astype(BF16)
    b_small = jnp.concatenate([b_f.astype(F32), jnp.zeros((LANES - H_FOX,), F32)])[None, :]
    return w_big, w_small, b_small


def _prep_compress(wk1, wk2, pek, wv1, wv2, pev):
    seg = CMP_STRIDE * HEAD_DIM
    ratio = CMP_LEN // CMP_STRIDE
    eye = jnp.eye(4, dtype=F32)

    def blocks(w):
        return w.reshape(ratio, CMP_STRIDE, HEAD_DIM, wk1.shape[1])

    per_group = jnp.stack([blocks(wk1), blocks(wk1), blocks(wv1), blocks(wv1)], axis=0)
    wj = jnp.einsum("grjde,gh->jgdrhe", per_group, eye)
    wj = wj.reshape(CMP_STRIDE, 4 * HEAD_DIM, ratio * 4 * HEAD_DIM).astype(BF16)
    pe = jnp.concatenate([pek.reshape(-1), pev.reshape(-1)])
    pe = jnp.broadcast_to(pe[None, :], (8, pe.shape[0])).astype(BF16)
    zero = jnp.zeros_like(wk1)
    wpe = jnp.concatenate([jnp.concatenate([wk1, wk1, zero, zero], axis=1),
                           jnp.concatenate([zero, zero, wv1, wv1], axis=1)], axis=0).astype(BF16)
    w2 = jnp.einsum("gde,gh->gdhe", jnp.stack([wk2, wk2, wv2, wv2]), eye)
    w2 = w2.reshape(4 * HEAD_DIM, 4 * HEAD_DIM).astype(BF16)
    del seg
    return wj, pe, wpe, w2


def _rope_tables(pos):
    half = HEAD_DIM // 2
    inv = ROPE_THETA ** (-jnp.arange(half, dtype=F32) / half)
    ang = pos.astype(F32)[:, None] * inv[None, :]
    reps = LANES // half
    return jnp.tile(jnp.cos(ang), (1, reps)), jnp.tile(jnp.sin(ang), (1, reps))


def _row_tile(n, cap):
    t = min(n, cap)
    while n % t:
        t //= 2
    return t


def kernel(x_prompt, x_sample, cache_fox_kv, cache_fox_logf, cache_nsa_kv, state_nsa_win_kv, page_table,
           g_ffn1_pre, w_ffn1_gate, w_ffn1_up, w_ffn1_down, g_ffn1_post, g_mix_pre, w_in, b_fox_f,
           w_cmpk_1, w_cmpk_2, pe_cmpk, w_cmpv_1, w_cmpv_2, pe_cmpv, g_fox_out, g_nsa_out, w_out,
           g_mix_post, g_ffn2_pre, w_ffn2_gate, w_ffn2_up, w_ffn2_down, g_ffn2_post):
    depth = w_in.shape[0]
    b, t, d = x_prompt.shape
    s, dec_seq, _ = x_sample.shape
    assert dec_seq == 1, "the sample group decodes one token per sequence"
    n_pool, page = cache_fox_kv.shape[1], cache_fox_kv.shape[2]
    n_pages = page_table.shape[1]
    past_len = n_pages * page
    win_buf = state_nsa_win_kv.shape[2]
    assert t % LANES == 0 and page % SLC_LEN == 0
    page_table = page_table.astype(jnp.int32)

    tm_p = _row_tile(t, 512)
    cos_p, sin_p = _rope_tables(jnp.arange(t, dtype=jnp.int32))
    cos_s, sin_s = _rope_tables(jnp.full((s,), past_len, jnp.int32))
    row = lambda v: v.astype(F32)[None, :]

    yp = x_prompt.reshape(b * t, d)
    ys = x_sample.reshape(s, d)
    outs = [[] for _ in range(8)]
    for l in range(depth):
        ffn1 = (row(g_ffn1_pre[l]), w_ffn1_gate[l].astype(BF16), w_ffn1_up[l].astype(BF16),
                w_ffn1_down[l].astype(BF16), row(g_ffn1_post[l]))
        ffn2 = (row(g_ffn2_pre[l]), w_ffn2_gate[l].astype(BF16), w_ffn2_up[l].astype(BF16),
                w_ffn2_down[l].astype(BF16), row(g_ffn2_post[l]))
        w_big, w_small, b_small = _prep_projection(w_in[l], b_fox_f[l])
        cw = _prep_compress(w_cmpk_1[l], w_cmpk_2[l], pe_cmpk[l], w_cmpv_1[l], w_cmpv_2[l], pe_cmpv[l])
        merge = (row(g_fox_out[l]), row(g_nsa_out[l]), w_out[l].astype(BF16), row(g_mix_post[l]))

        hp = _half_ffn(yp, *ffn1, tm_p)
        fq, fkv, fk16, fv16, small, nq, nkv4, win, nkv16 = _project(
            hp, row(g_mix_pre[l]), w_big, w_small, b_small, cos_p, sin_p, tm_p, t // tm_p)
        small3 = small.reshape(b, t, LANES)
        cum = _cumsum(small3, _row_tile(t, 512))
        cum_t = jnp.swapaxes(cum[:, :, :H_FOX], 1, 2)
        o_fox = _fox_attention(fq.reshape(b, t, W_FOX), fk16.reshape(b, t, W_FOX), fv16.reshape(b, t, W_FOX),
                               cum, cum_t, _row_tile(t, 256))
        nkv4_3 = nkv4.reshape(b, t, 4 * W_KV)
        kc, vc = _compress(nkv4_3, cw)
        o_nsa = _nsa_attention(nq.reshape(b, t, W_NSA), small3, kc, vc, nkv16.reshape(b, t, 6 * W_KV),
                               LANES, _row_tile(t, 512))
        yp = _merge_ffn(hp, o_fox.reshape(b * t, W_FOX), o_nsa.reshape(b * t, W_NSA), *merge, *ffn2, tm_p)
        keep = min(WINDOW, t)
        outs[0].append(fkv.reshape(b, t, 2, H_FOX, HEAD_DIM))
        outs[1].append(small3[:, :, :H_FOX])
        outs[2].append(nkv4_3.reshape(b, t, 4, H_KV, HEAD_DIM))
        outs[3].append(win.reshape(b, t, 2, H_KV, HEAD_DIM)[:, t - keep:])

        hs = _half_ffn(ys, *ffn1, s)
        fq, fkv, _, _, small, nq, nkv4, win, _ = _project(
            hs, row(g_mix_pre[l]), w_big, w_small, b_small, cos_s, sin_s, s, 1)
        lf_col = small[:, :H_FOX].reshape(s, H_FOX, 1)
        o_fox = _fox_decode(page_table, fq.reshape(s, 1, W_FOX), fkv[:, :W_FOX].reshape(s, 1, W_FOX),
                            fkv[:, W_FOX:].reshape(s, 1, W_FOX), lf_col,
                            cache_fox_kv[l].reshape(n_pool, page, 2 * W_FOX),
                            jnp.swapaxes(cache_fox_logf[l], 1, 2))
        new_row = jnp.concatenate([nkv4, win], axis=1).reshape(s, 1, 6 * W_KV)
        win_state = state_nsa_win_kv[l].reshape(s, win_buf, 2 * W_KV)
        o_nsa = _nsa_decode(page_table, nq.reshape(s, 1, W_NSA), new_row, small.reshape(s, 1, LANES), win_state,
                            cache_nsa_kv[l].reshape(n_pool, page, 4 * W_KV), cw)
        ys = _merge_ffn(hs, o_fox.reshape(s, W_FOX), o_nsa.reshape(s, W_NSA), *merge, *ffn2, s)
        keep = min(WINDOW, past_len + 1)
        kw_all = jnp.concatenate([state_nsa_win_kv[l], win.reshape(s, 1, 2, H_KV, HEAD_DIM)], axis=1)
        outs[4].append(fkv.reshape(s, 1, 2, H_FOX, HEAD_DIM))
        outs[5].append(small[:, :H_FOX].reshape(s, 1, H_FOX))
        outs[6].append(nkv4.reshape(s, 1, 4, H_KV, HEAD_DIM))
        outs[7].append(kw_all[:, kw_all.shape[1] - keep:])

    stacked = [jnp.stack(o, axis=0) for o in outs]
    return (yp.reshape(b, t, d), ys.reshape(s, 1, d), *stacked)
```

```python
import functools

import jax
import jax.numpy as jnp
from jax import lax
from jax.experimental import pallas as pl
from jax.experimental.pallas import tpu as pltpu

HEAD_DIM = 64
H_FOX = 8
H_NSA = 8
H_KV = 2
GQA_GROUP = H_NSA // H_KV
W_FOX = H_FOX * HEAD_DIM
W_NSA = H_NSA * HEAD_DIM
W_KV = H_KV * HEAD_DIM
CMP_STRIDE = 16
CMP_LEN = 32
SLC_LEN = 64
SLC_TOPN = 16
WINDOW = 512
ROPE_THETA = 10000.0
EPS = 1e-6
NEG = -1e30
FORCE_SCORE = 1e4
N_SMALL = H_FOX + 3 * H_NSA
GATE0 = H_FOX

LANES = 128
MXU_N = 256
VMEM_LIMIT = 56 * 1024 * 1024

F32 = jnp.float32
BF16 = jnp.bfloat16


def _dot(a, b):
    return jnp.dot(a, b, preferred_element_type=F32)


def _dot_nt(a, b):
    return lax.dot_general(a, b, (((1,), (1,)), ((), ())), preferred_element_type=F32)


def _div(x, n):
    assert n & (n - 1) == 0
    return lax.shift_right_logical(x, jnp.int32(n.bit_length() - 1))


def _split3(x):
    hi = x.astype(BF16)
    r1 = x - hi.astype(F32)
    mid = r1.astype(BF16)
    lo = (r1 - mid.astype(F32)).astype(BF16)
    return hi, mid, lo


def _rms(x, g):
    return x * lax.rsqrt(jnp.mean(x * x, axis=-1, keepdims=True) + EPS) * g


def _ff_chunks(d_ff):
    step = 6 * MXU_N
    return tuple((c, min(c + step, d_ff)) for c in range(0, d_ff, step))


def _ffn_core(x, gpre, wg_ref, wu_ref, wd_ref, gpost, chunks):
    xn = _rms(x, gpre).astype(BF16)
    acc = jnp.zeros(x.shape, F32)
    for c0, c1 in chunks:
        g = _dot(xn, wg_ref[:, c0:c1])
        u = _dot(xn, wu_ref[:, c0:c1])
        hm = (g * jax.nn.sigmoid(g) * u).astype(BF16)
        acc = acc + _dot(hm, wd_ref[c0:c1, :])
    return x + 0.5 * _rms(acc, gpost)


def _const_spec(shape):
    nd = len(shape)
    return pl.BlockSpec(shape, lambda *_: (0,) * nd, pipeline_mode=pl.Buffered(1))


def _params(sem):
    return pltpu.CompilerParams(dimension_semantics=sem, vmem_limit_bytes=VMEM_LIMIT)


def _ffn_kernel(x_ref, gpre_ref, wg_ref, wu_ref, wd_ref, gpost_ref, o_ref, *, chunks):
    o_ref[...] = _ffn_core(x_ref[...], gpre_ref[...], wg_ref, wu_ref, wd_ref, gpost_ref[...], chunks)


def _half_ffn(x, gpre, wg, wu, wd, gpost, tm):
    n, d = x.shape
    d_ff = wg.shape[1]
    row = pl.BlockSpec((tm, d), lambda i: (i, 0))
    return pl.pallas_call(
        functools.partial(_ffn_kernel, chunks=_ff_chunks(d_ff)),
        grid=(n // tm,),
        in_specs=[row, _const_spec((1, d)), _const_spec((d, d_ff)), _const_spec((d, d_ff)),
                  _const_spec((d_ff, d)), _const_spec((1, d))],
        out_specs=row,
        out_shape=jax.ShapeDtypeStruct((n, d), F32),
        compiler_params=_params(("parallel",)),
        name="half_ffn",
    )(x, gpre, wg, wu, wd, gpost)


_C_FQ, _C_FK, _C_FV, _C_NQ, _C_KV, _C_NQR, _C_KR, _C_END = 0, 512, 1024, 1536, 2048, 2816, 3328, 3712


def _proj_kernel(h_ref, g_ref, wb_ref, ws_ref, bf_ref, cos_ref, sin_ref,
                 fq_ref, fkv_ref, fk16_ref, fv16_ref, small_ref, nq_ref, nkv4_ref, win_ref, nkv16_ref):
    n = _rms(h_ref[...], g_ref[...]).astype(BF16)

    def mm(c0, c1):
        return _dot(n, wb_ref[:, c0:c1])

    scale = HEAD_DIM ** -0.5
    fq_ref[...] = mm(_C_FQ, _C_FK) * scale
    fk = mm(_C_FK, _C_FV)
    fv = mm(_C_FV, _C_NQ)
    fkv_ref[:, 0:W_FOX] = fk
    fkv_ref[:, W_FOX:2 * W_FOX] = fv
    fk16_ref[...] = fk.astype(BF16)
    fv16_ref[...] = fv.astype(BF16)

    cos = cos_ref[...]
    sin = sin_ref[...]
    nq = mm(_C_NQ, _C_KV)
    nqr = mm(_C_NQR, _C_KR)
    for c in range(W_NSA // LANES):
        sl = slice(c * LANES, (c + 1) * LANES)
        nq_ref[:, sl] = (nq[:, sl] * cos + nqr[:, sl] * sin) * scale

    kv = mm(_C_KV, _C_NQR)
    kr = mm(_C_KR, _C_END)
    for br in range(3):
        k = kv[:, br * 2 * W_KV: br * 2 * W_KV + W_KV] * cos + kr[:, br * W_KV:(br + 1) * W_KV] * sin
        v = kv[:, br * 2 * W_KV + W_KV:(br + 1) * 2 * W_KV]
        dst = nkv4_ref if br < 2 else win_ref
        off = (br % 2) * 2 * W_KV if br < 2 else 0
        dst[:, off:off + W_KV] = k
        dst[:, off + W_KV:off + 2 * W_KV] = v
        nkv16_ref[:, br * 2 * W_KV: br * 2 * W_KV + W_KV] = k.astype(BF16)
        nkv16_ref[:, br * 2 * W_KV + W_KV:(br + 1) * 2 * W_KV] = v.astype(BF16)

    sm = _dot(n, ws_ref[...]) + bf_ref[...]
    lane = lax.broadcasted_iota(jnp.int32, sm.shape, 1)
    log_sig = jnp.minimum(sm, 0.0) - jnp.log(1.0 + jnp.exp(-jnp.abs(sm)))
    small_ref[...] = jnp.where(lane < H_FOX, log_sig, jax.nn.sigmoid(sm))


def _project(h, g, w_big, w_small, b_small, cos, sin, tm, n_pos_tiles):
    n, d = h.shape
    row = lambda w: pl.BlockSpec((tm, w), lambda i: (i, 0))
    pos = pl.BlockSpec((tm, LANES), lambda i: (i % n_pos_tiles, 0))
    widths = (W_FOX, 2 * W_FOX, W_FOX, W_FOX, LANES, W_NSA, 4 * W_KV, 2 * W_KV, 6 * W_KV)
    dtypes = (F32, F32, BF16, BF16, F32, F32, F32, F32, BF16)
    return pl.pallas_call(
        _proj_kernel,
        grid=(n // tm,),
        in_specs=[row(d), _const_spec((1, d)), _const_spec(w_big.shape), _const_spec(w_small.shape),
                  _const_spec((1, LANES)), pos, pos],
        out_specs=[row(w) for w in widths],
        out_shape=[jax.ShapeDtypeStruct((n, w), dt) for w, dt in zip(widths, dtypes)],
        compiler_params=_params(("parallel",)),
        name="project",
    )(h, g, w_big, w_small, b_small, cos, sin)


def _merge_ffn_kernel(h_ref, of_ref, on_ref, gf_ref, gn_ref, wo_ref, gmix_ref,
                      gpre_ref, wg_ref, wu_ref, wd_ref, gpost_ref, y_ref, *, chunks):
    of = _rms(of_ref[...], gf_ref[...]).astype(BF16)
    on = _rms(on_ref[...], gn_ref[...]).astype(BF16)
    mrg = _dot(of, wo_ref[0:W_FOX, :]) + _dot(on, wo_ref[W_FOX:W_FOX + W_NSA, :])
    h2 = h_ref[...] + _rms(mrg, gmix_ref[...])
    y_ref[...] = _ffn_core(h2, gpre_ref[...], wg_ref, wu_ref, wd_ref, gpost_ref[...], chunks)


def _merge_ffn(h, o_fox, o_nsa, gf, gn, w_out, gmix, gpre, wg, wu, wd, gpost, tm):
    n, d = h.shape
    d_ff = wg.shape[1]
    row = lambda w: pl.BlockSpec((tm, w), lambda i: (i, 0))
    return pl.pallas_call(
        functools.partial(_merge_ffn_kernel, chunks=_ff_chunks(d_ff)),
        grid=(n // tm,),
        in_specs=[row(d), row(W_FOX), row(W_NSA), _const_spec((1, W_FOX)), _const_spec((1, W_NSA)),
                  _const_spec(w_out.shape), _const_spec((1, d)), _const_spec((1, d)),
                  _const_spec((d, d_ff)), _const_spec((d, d_ff)), _const_spec((d_ff, d)), _const_spec((1, d))],
        out_specs=row(d),
        out_shape=jax.ShapeDtypeStruct((n, d), F32),
        compiler_params=_params(("parallel",)),
        name="merge_ffn",
    )(h, o_fox, o_nsa, gf, gn, w_out, gmix, gpre, wg, wu, wd, gpost)


def _cumsum_kernel(x_ref, o_ref, carry_ref):
    @pl.when(pl.program_id(1) == 0)
    def _():
        carry_ref[...] = jnp.zeros(carry_ref.shape, F32)

    x = x_ref[...]
    tc = x.shape[0]
    r = lax.broadcasted_iota(jnp.int32, (tc, tc), 0)
    c = lax.broadcasted_iota(jnp.int32, (tc, tc), 1)
    tri = (c <= r).astype(BF16)
    hi, mid, lo = _split3(x)
    cs = _dot(tri, hi) + _dot(tri, mid) + _dot(tri, lo) + carry_ref[...]
    o_ref[...] = cs
    carry_ref[...] = cs[tc - 1:tc, :]


def _cumsum(x, tc):
    b, t, w = x.shape
    blk = pl.BlockSpec((None, tc, w), lambda i, j: (i, j, 0))
    return pl.pallas_call(
        _cumsum_kernel,
        grid=(b, t // tc),
        in_specs=[blk],
        out_specs=blk,
        out_shape=jax.ShapeDtypeStruct(x.shape, F32),
        scratch_shapes=[pltpu.VMEM((1, w), F32)],
        compiler_params=_params(("parallel", "arbitrary")),
        name="logf_cumsum",
    )(x)


def _fox_kernel(q_ref, k_ref, v_ref, cq_ref, ck_ref, o_ref, m_ref, l_ref, acc_ref, *, tq):
    hp = pl.program_id(1)
    qi = pl.program_id(2)
    lane = lax.broadcasted_iota(jnp.int32, (tq, LANES), 1)
    lo = lane < HEAD_DIM
    q2 = q_ref[...]
    qz = (jnp.where(lo, q2, 0.0).astype(BF16), jnp.where(lo, 0.0, q2).astype(BF16))
    cq_all = cq_ref[...]
    cq = [jnp.sum(jnp.where(lane == 2 * hp + e, cq_all, 0.0), axis=1, keepdims=True) for e in (0, 1)]
    row8 = lax.broadcasted_iota(jnp.int32, (8, tq), 0)
    qpos = lax.broadcasted_iota(jnp.int32, (tq, tq), 0)
    kpos = lax.broadcasted_iota(jnp.int32, (tq, tq), 1)

    m_ref[...] = jnp.full(m_ref.shape, NEG, F32)
    l_ref[...] = jnp.zeros(l_ref.shape, F32)
    acc_ref[...] = jnp.zeros(acc_ref.shape, F32)

    def step(kt, masked):
        k0 = pl.multiple_of(kt * tq, tq)
        k2 = k_ref[pl.ds(k0, tq), :]
        v2 = v_ref[pl.ds(k0, tq), :]
        ck_all = ck_ref[:, pl.ds(k0, tq)]
        for e in (0, 1):
            ck = jnp.sum(jnp.where(row8 == 2 * hp + e, ck_all, 0.0), axis=0, keepdims=True)
            s = _dot_nt(qz[e], k2) + (cq[e] - ck)
            if masked:
                s = jnp.where(kpos <= qpos, s, NEG)
            m_old = m_ref[e]
            m_new = jnp.maximum(m_old, jnp.max(s, axis=1, keepdims=True))
            alpha = jnp.exp(m_old - m_new)
            p = jnp.exp(s - m_new)
            l_ref[e] = alpha * l_ref[e] + jnp.sum(p, axis=1, keepdims=True)
            acc_ref[e] = alpha * acc_ref[e] + _dot(p.astype(BF16), v2)
            m_ref[e] = m_new

    def body(kt, carry):
        step(kt, False)
        return carry

    lax.fori_loop(0, qi, body, 0)
    step(qi, True)
    o0 = acc_ref[0] * (1.0 / l_ref[0])
    o1 = acc_ref[1] * (1.0 / l_ref[1])
    o_ref[...] = jnp.where(lo, o0, o1)


def _fox_attention(fq, fk16, fv16, cum, cum_t, tq):
    b, t, _ = fq.shape
    qblk = pl.BlockSpec((None, tq, LANES), lambda i, hp, qi: (i, qi, hp))
    kvblk = pl.BlockSpec((None, t, LANES), lambda i, hp, qi: (i, 0, hp))
    return pl.pallas_call(
        functools.partial(_fox_kernel, tq=tq),
        grid=(b, W_FOX // LANES, t // tq),
        in_specs=[qblk, kvblk, kvblk,
                  pl.BlockSpec((None, tq, LANES), lambda i, hp, qi: (i, qi, 0)),
                  pl.BlockSpec((None, H_FOX, t), lambda i, hp, qi: (i, 0, 0))],
        out_specs=qblk,
        out_shape=jax.ShapeDtypeStruct((b, t, W_FOX), F32),
        scratch_shapes=[pltpu.VMEM((2, tq, 1), F32), pltpu.VMEM((2, tq, 1), F32),
                        pltpu.VMEM((2, tq, LANES), F32)],
        compiler_params=_params(("parallel", "parallel", "arbitrary")),
        name="fox_prompt",
    )(fq, fk16, fv16, cum, cum_t)


def _compress_hidden(load_rows, n_ch, wj_ref, pe_ref, wpe_ref):
    acc = jnp.zeros((n_ch, 2 * 4 * HEAD_DIM), F32)
    for j in range(CMP_STRIDE):
        xk, xv = load_rows(j)
        acc = acc + _dot(xk.astype(BF16), wj_ref[j, 0:W_KV, :]) + _dot(xv.astype(BF16), wj_ref[j, W_KV:2 * W_KV, :])
    first = acc[:, 0:4 * HEAD_DIM]
    second = pltpu.roll(acc[:, 4 * HEAD_DIM:], n_ch - 1, axis=0)
    pe_term = _dot(pe_ref[...], wpe_ref[...])[0:1, :]
    return jax.nn.gelu(first + second + pe_term)


def _compress_core(load_rows, n_ch, wj_ref, pe_ref, wpe_ref, w2_ref):
    hid = _compress_hidden(load_rows, n_ch, wj_ref, pe_ref, wpe_ref)
    return _dot(hid.astype(BF16), w2_ref[...])


def _compress_kernel(xk_ref, xv_ref, wj_ref, pe_ref, wpe_ref, w2_ref, kc_ref, vc_ref, *, n_ch):
    rows = lambda j: (xk_ref[pl.ds(j, n_ch, stride=CMP_STRIDE), :], xv_ref[pl.ds(j, n_ch, stride=CMP_STRIDE), :])
    out = _compress_core(rows, n_ch, wj_ref, pe_ref, wpe_ref, w2_ref)
    kc_ref[...] = out[:, 0:W_KV].astype(BF16)
    vc_ref[...] = out[:, W_KV:2 * W_KV].astype(BF16)


def _compress(nkv4, cw):
    b, t, _ = nkv4.shape
    n_ch = t // CMP_STRIDE
    out = pl.BlockSpec((None, n_ch, W_KV), lambda i: (i, 0, 0))
    return pl.pallas_call(
        functools.partial(_compress_kernel, n_ch=n_ch),
        grid=(b,),
        in_specs=[pl.BlockSpec((None, t, W_KV), lambda i: (i, 0, 0)),
                  pl.BlockSpec((None, t, W_KV), lambda i: (i, 0, 1))] + [_const_spec(w.shape) for w in cw],
        out_specs=[out, out],
        out_shape=[jax.ShapeDtypeStruct((b, n_ch, W_KV), BF16)] * 2,
        compiler_params=_params(("parallel",)),
        name="compress_prompt",
    )(nkv4, nkv4, *cw)


def _overlap(n_ch, n_slc):
    n = lax.broadcasted_iota(jnp.int32, (n_ch, LANES), 0)
    j = lax.broadcasted_iota(jnp.int32, (n_ch, LANES), 1)
    hit = ((n * CMP_STRIDE <= j * SLC_LEN + SLC_LEN - 1) & (n * CMP_STRIDE + CMP_LEN - 1 >= j * SLC_LEN)
           & (n < n_ch - 1) & (j < n_slc))
    return hit.astype(BF16)


def _select_blocks(imp, pos, n_slc, n_sel):
    lane = lax.broadcasted_iota(jnp.int32, imp.shape, 1)
    qblk = _div(pos, SLC_LEN)
    forced = (lane == 0) | (lane == qblk) | (lane == qblk - 1)
    valid = lane * SLC_LEN <= pos
    score = jnp.where(valid, jnp.where(forced, FORCE_SCORE, imp), -1.0)
    score = jnp.where(lane < n_slc, score, -2.0)
    cnt = jnp.zeros(imp.shape, F32)
    for i in range(n_slc):
        ci = score[:, i:i + 1]
        beats = (ci > score) | ((ci == score) & (lane > i))
        cnt = cnt + jnp.where(beats, 1.0, 0.0)
    return jnp.where((cnt < n_sel) & (lane < n_slc), 1.0, 0.0)


def _nsa_kernel(q_ref, gate_ref, kc_ref, vc_ref, sk_ref, sv_ref, wk_ref, wv_ref, o_ref,
                q4_ref, m_ref, l_ref, acc_ref, *, tq, tk, n_ch, n_slc, wlen):
    h = pl.program_id(1)
    qi = pl.program_id(2)
    q0 = qi * tq
    g = GQA_GROUP
    lane = lax.broadcasted_iota(jnp.int32, (tq, LANES), 1)
    lane_half = _div(lane, HEAD_DIM)
    pos = q0 + lax.broadcasted_iota(jnp.int32, (tq, 1), 0)

    q = q_ref[...]
    for i in range(g):
        piece = q[:, (i // 2) * LANES:(i // 2 + 1) * LANES]
        src = jnp.where(h == (i % 2), piece, pltpu.roll(piece, HEAD_DIM, axis=1))
        q4_ref[i * tq:(i + 1) * tq, :] = jnp.where(lane_half == h, src, 0.0).astype(BF16)
    q4 = q4_ref[...]

    sc = _dot_nt(q4, kc_ref[...]).reshape(g, tq, n_ch)
    n_idx = lax.broadcasted_iota(jnp.int32, (tq, n_ch), 1)
    ok = ((n_idx * CMP_STRIDE + CMP_LEN - 1 <= pos) & (n_idx < n_ch - 1))[None]
    sc = jnp.where(ok, sc, NEG)
    e = jnp.where(ok, jnp.exp(sc - jnp.max(sc, axis=-1, keepdims=True)), 0.0)
    lc = jnp.sum(e, axis=-1, keepdims=True)
    a = (e * (1.0 / jnp.where(lc > 0.0, lc, 1.0))).reshape(g * tq, n_ch)
    a_hi = a.astype(BF16)
    a_lo = (a - a_hi.astype(F32)).astype(BF16)
    o_cmp = _dot(a_hi, vc_ref[...]).reshape(g, tq, LANES)
    ov = _overlap(n_ch, n_slc)
    imp = jnp.sum((_dot(a_hi, ov) + _dot(a_lo, ov)).reshape(g, tq, LANES), axis=0)
    sel = _select_blocks(imp, pos, n_slc, min(SLC_TOPN, n_slc)).astype(BF16)

    m_ref[...] = jnp.full(m_ref.shape, NEG, F32)
    l_ref[...] = jnp.zeros(l_ref.shape, F32)
    acc_ref[...] = jnp.zeros(acc_ref.shape, F32)
    blk_of_key = (lax.broadcasted_iota(jnp.int32, (LANES, tk), 0)
                  - _div(lax.broadcasted_iota(jnp.int32, (LANES, tk), 1), SLC_LEN))
    kcol = lax.broadcasted_iota(jnp.int32, (tq, tk), 1)

    def body(kt, carry):
        k0 = pl.multiple_of(kt * tk, tk)
        k2 = sk_ref[pl.ds(k0, tk), :]
        v2 = sv_ref[pl.ds(k0, tk), :]
        s = _dot_nt(q4, k2).reshape(g, tq, tk)
        expand = (blk_of_key == kt * (tk // SLC_LEN)).astype(BF16)
        allow = ((_dot(sel, expand) > 0.5) & (k0 + kcol <= pos))[None]
        s = jnp.where(allow, s, NEG)
        m_old = m_ref[...]
        m_new = jnp.maximum(m_old, jnp.max(s, axis=-1, keepdims=True))
        alpha = jnp.exp(m_old - m_new)
        p = jnp.where(allow, jnp.exp(s - m_new), 0.0)
        l_ref[...] = alpha * l_ref[...] + jnp.sum(p, axis=-1, keepdims=True)
        pv = _dot(p.reshape(g * tq, tk).astype(BF16), v2).reshape(g, tq, LANES)
        acc_ref[...] = alpha * acc_ref[...] + pv
        m_ref[...] = m_new
        return carry

    lax.fori_loop(0, _div(q0 + tq - 1, tk) + 1, body, 0)
    o_slc = acc_ref[...] * (1.0 / l_ref[...])

    ws = pl.multiple_of(jnp.maximum(q0 + tq - wlen, 0), tq)
    sw = _dot_nt(q4, wk_ref[pl.ds(ws, wlen), :]).reshape(g, tq, wlen)
    dist = pos - (ws + lax.broadcasted_iota(jnp.int32, (tq, wlen), 1))
    okw = ((dist >= 0) & (dist < WINDOW))[None]
    sw = jnp.where(okw, sw, NEG)
    pw = jnp.where(okw, jnp.exp(sw - jnp.max(sw, axis=-1, keepdims=True)), 0.0)
    lw = jnp.sum(pw, axis=-1, keepdims=True)
    o_win = _dot(pw.reshape(g * tq, wlen).astype(BF16), wv_ref[pl.ds(ws, wlen), :]).reshape(g, tq, LANES)
    o_win = o_win * (1.0 / lw)

    gates = gate_ref[...]
    vals = []
    for i in range(g):
        col0 = GATE0 + 3 * (g * h + i)
        gc = [jnp.sum(jnp.where(lane == col0 + c, gates, 0.0), axis=1, keepdims=True) for c in range(3)]
        val = gc[0] * o_cmp[i] + gc[1] * o_slc[i] + gc[2] * o_win[i]
        vals.append(jnp.where(h == (i % 2), val, pltpu.roll(val, HEAD_DIM, axis=1)))
    for pr in range(g // 2):
        o_ref[:, pr * LANES:(pr + 1) * LANES] = jnp.where(lane_half == 0, vals[2 * pr], vals[2 * pr + 1])


def _nsa_attention(nq, small, kc, vc, nkv16, tq, tk):
    b, t, _ = nq.shape
    n_ch = kc.shape[1]
    n_slc = -(-t // SLC_LEN)
    wlen = min(t, WINDOW + tq)
    g = GQA_GROUP
    qblk = pl.BlockSpec((None, tq, g * HEAD_DIM), lambda i, h, qi: (i, qi, h))
    cblk = pl.BlockSpec((None, n_ch, W_KV), lambda i, h, qi: (i, 0, 0))
    lane_blk = lambda c: pl.BlockSpec((None, t, W_KV), lambda i, h, qi: (i, 0, c))
    return pl.pallas_call(
        functools.partial(_nsa_kernel, tq=tq, tk=tk, n_ch=n_ch, n_slc=n_slc, wlen=wlen),
        grid=(b, H_KV, t // tq),
        in_specs=[qblk, pl.BlockSpec((None, tq, LANES), lambda i, h, qi: (i, qi, 0)), cblk, cblk,
                  lane_blk(2), lane_blk(3), lane_blk(4), lane_blk(5)],
        out_specs=qblk,
        out_shape=jax.ShapeDtypeStruct((b, t, W_NSA), F32),
        scratch_shapes=[pltpu.VMEM((g * tq, LANES), BF16), pltpu.VMEM((g, tq, 1), F32),
                        pltpu.VMEM((g, tq, 1), F32), pltpu.VMEM((g, tq, LANES), F32)],
        compiler_params=_params(("parallel", "parallel", "arbitrary")),
        name="nsa_prompt",
    )(nq, small, kc, vc, nkv16, nkv16, nkv16, nkv16)


def _fox_dec_kernel(pt_ref, q_ref, knew_ref, vnew_ref, lfnew_ref, kv_ref, lft_ref, o_ref,
                    m_ref, l_ref, acc_ref, carry_ref):
    del pt_ref
    j = pl.program_id(1)
    row = lax.broadcasted_iota(jnp.int32, (H_FOX, W_FOX), 0)
    lane = lax.broadcasted_iota(jnp.int32, (H_FOX, W_FOX), 1)
    diag = _div(lane, HEAD_DIM) == row
    qbd = jnp.where(diag, jnp.broadcast_to(q_ref[...], (H_FOX, W_FOX)), 0.0).astype(BF16)

    @pl.when(j == 0)
    def _():
        kn = knew_ref[...].astype(BF16).astype(F32)
        m_ref[...] = jnp.sum(qbd.astype(F32) * kn, axis=1, keepdims=True)
        l_ref[...] = jnp.ones(l_ref.shape, F32)
        acc_ref[...] = jnp.broadcast_to(vnew_ref[...].astype(BF16).astype(F32), (H_FOX, W_FOX))
        carry_ref[...] = lfnew_ref[...]

    kv = kv_ref[...]
    page = kv.shape[0]
    k = kv[:, 0:W_FOX].astype(BF16)
    v = kv[:, W_FOX:2 * W_FOX].astype(BF16)
    lft = lft_ref[...]
    later = (lax.broadcasted_iota(jnp.int32, (page, page), 0)
             > lax.broadcasted_iota(jnp.int32, (page, page), 1)).astype(BF16)
    hi, mid, lo = _split3(lft)
    suffix = _dot(hi, later) + _dot(mid, later) + _dot(lo, later)
    s = _dot_nt(qbd, k) + (carry_ref[...] + suffix)
    m_old = m_ref[...]
    m_new = jnp.maximum(m_old, jnp.max(s, axis=1, keepdims=True))
    alpha = jnp.exp(m_old - m_new)
    p = jnp.exp(s - m_new)
    l_ref[...] = alpha * l_ref[...] + jnp.sum(p, axis=1, keepdims=True)
    acc_ref[...] = alpha * acc_ref[...] + _dot(p.astype(BF16), v)
    m_ref[...] = m_new
    carry_ref[...] = carry_ref[...] + jnp.sum(lft, axis=1, keepdims=True)

    @pl.when(j == pl.num_programs(1) - 1)
    def _():
        o = acc_ref[...] * (1.0 / l_ref[...])
        o_ref[...] = jnp.sum(jnp.where(diag, o, 0.0), axis=0, keepdims=True)


def _fox_decode(page_table, fq, fk, fv, lf_col, cache_kv, cache_lft):
    s, n_pages = page_table.shape
    page = cache_kv.shape[1]
    tok = pl.BlockSpec((None, 1, W_FOX), lambda i, j, pt: (i, 0, 0))
    pg = lambda i, j, pt: (pt[i * n_pages + n_pages - 1 - j], 0, 0)
    return pl.pallas_call(
        _fox_dec_kernel,
        grid_spec=pltpu.PrefetchScalarGridSpec(
            num_scalar_prefetch=1,
            grid=(s, n_pages),
            in_specs=[tok, tok, tok, pl.BlockSpec((None, H_FOX, 1), lambda i, j, pt: (i, 0, 0)),
                      pl.BlockSpec((None, page, 2 * W_FOX), pg),
                      pl.BlockSpec((None, H_FOX, page), pg)],
            out_specs=tok,
            scratch_shapes=[pltpu.VMEM((H_FOX, 1), F32), pltpu.VMEM((H_FOX, 1), F32),
                            pltpu.VMEM((H_FOX, W_FOX), F32), pltpu.VMEM((H_FOX, 1), F32)],
        ),
        out_shape=jax.ShapeDtypeStruct((s, 1, W_FOX), F32),
        compiler_params=_params(("parallel", "arbitrary")),
        name="fox_decode",
    )(page_table.reshape(-1), fq, fk, fv, lf_col, cache_kv, cache_lft)


def _nsa_dec_kernel(pt_ref, q_ref, new_ref, gate_ref, win_ref, page_ref, wj_ref, pe_ref, wpe_ref, w2_ref,
                    o_ref, xk_buf, xv_buf, sel_ref, ocmp_ref, m_ref, l_ref, acc_ref,
                    *, n_pages, past_len):
    del pt_ref
    j = pl.program_id(1)
    page = page_ref.shape[0]
    n_ch = past_len // CMP_STRIDE
    n_slc = past_len // SLC_LEN + 1
    blocks_per_page = page // SLC_LEN
    win_buf = win_ref.shape[0]
    row = lax.broadcasted_iota(jnp.int32, (H_NSA, LANES), 0)
    lane = lax.broadcasted_iota(jnp.int32, (H_NSA, LANES), 1)
    own_half = _div(lane, HEAD_DIM) == _div(row, GQA_GROUP)

    q = q_ref[...]
    qbd = jnp.zeros((H_NSA, LANES), F32)
    for i in range(H_NSA):
        piece = q[:, (i // 2) * LANES:(i // 2 + 1) * LANES]
        if (i % 2) != (i // GQA_GROUP):
            piece = pltpu.roll(piece, HEAD_DIM, axis=1)
        qbd = jnp.where(row == i, jnp.broadcast_to(piece, (H_NSA, LANES)), qbd)
    qbd = jnp.where(own_half, qbd, 0.0)
    qbd16 = qbd.astype(BF16)
    new = new_ref[...]

    @pl.when(j < n_pages)
    def _():
        r0 = pl.multiple_of(j * page, page)
        xk_buf[pl.ds(r0, page), :] = page_ref[:, 0:W_KV]
        xv_buf[pl.ds(r0, page), :] = page_ref[:, W_KV:2 * W_KV]

    @pl.when(j == n_pages - 1)
    def _():
        rows = lambda jj: (xk_buf[pl.ds(jj, n_ch, stride=CMP_STRIDE), :],
                           xv_buf[pl.ds(jj, n_ch, stride=CMP_STRIDE), :])
        out = _compress_core(rows, n_ch, wj_ref, pe_ref, wpe_ref, w2_ref)
        kc = out[:, 0:W_KV].astype(BF16)
        vc = out[:, W_KV:2 * W_KV].astype(BF16)
        n_idx = lax.broadcasted_iota(jnp.int32, (H_NSA, n_ch), 1)
        ok = (n_idx * CMP_STRIDE + CMP_LEN - 1 <= past_len) & (n_idx < n_ch - 1)
        sc = jnp.where(ok, _dot_nt(qbd16, kc), NEG)
        e = jnp.where(ok, jnp.exp(sc - jnp.max(sc, axis=1, keepdims=True)), 0.0)
        lc = jnp.sum(e, axis=1, keepdims=True)
        a = e * (1.0 / jnp.where(lc > 0.0, lc, 1.0))
        a_hi = a.astype(BF16)
        a_lo = (a - a_hi.astype(F32)).astype(BF16)
        ocmp_ref[...] = _dot(a_hi, vc)
        ov = _overlap(n_ch, n_slc)
        imp8 = _dot(a_hi, ov) + _dot(a_lo, ov)
        imp = jnp.zeros((H_NSA, LANES), F32)
        for h in range(H_KV):
            tot = jnp.sum(imp8[h * GQA_GROUP:(h + 1) * GQA_GROUP], axis=0, keepdims=True)
            imp = jnp.where(_div(row, GQA_GROUP) == h, jnp.broadcast_to(tot, (H_NSA, LANES)), imp)
        pos = jnp.full((H_NSA, 1), past_len, jnp.int32)
        sel_ref[...] = _select_blocks(imp, pos, n_slc, min(SLC_TOPN, n_slc))

    @pl.when(j == n_pages)
    def _():
        kn = new[:, 2 * W_KV:3 * W_KV].astype(BF16).astype(F32)
        m_ref[...] = jnp.sum(qbd16.astype(F32) * kn, axis=1, keepdims=True)
        l_ref[...] = jnp.ones(l_ref.shape, F32)
        acc_ref[...] = jnp.broadcast_to(new[:, 3 * W_KV:4 * W_KV].astype(BF16).astype(F32), (H_NSA, LANES))

    @pl.when(j >= n_pages)
    def _():
        pg = page_ref[...]
        k2 = pg[:, 0:W_KV].astype(BF16)
        v2 = pg[:, W_KV:2 * W_KV].astype(BF16)
        blk = (lax.broadcasted_iota(jnp.int32, (LANES, page), 0)
               - _div(lax.broadcasted_iota(jnp.int32, (LANES, page), 1), SLC_LEN))
        expand = (blk == (j - n_pages) * blocks_per_page).astype(BF16)
        allow = _dot(sel_ref[...].astype(BF16), expand) > 0.5
        s = jnp.where(allow, _dot_nt(qbd16, k2), NEG)
        m_old = m_ref[...]
        m_new = jnp.maximum(m_old, jnp.max(s, axis=1, keepdims=True))
        alpha = jnp.exp(m_old - m_new)
        p = jnp.where(allow, jnp.exp(s - m_new), 0.0)
        l_ref[...] = alpha * l_ref[...] + jnp.sum(p, axis=1, keepdims=True)
        acc_ref[...] = alpha * acc_ref[...] + _dot(p.astype(BF16), v2)
        m_ref[...] = m_new

    @pl.when(j == 2 * n_pages - 1)
    def _():
        o_slc = acc_ref[...] * (1.0 / l_ref[...])
        wkv = win_ref[...]
        kw = wkv[:, 0:W_KV].astype(BF16)
        vw = wkv[:, W_KV:2 * W_KV].astype(BF16)
        r = lax.broadcasted_iota(jnp.int32, (H_NSA, win_buf), 1)
        okw = (win_buf - r) < WINDOW
        sw = jnp.where(okw, _dot_nt(qbd16, kw), NEG)
        swn = jnp.sum(qbd16.astype(F32) * new[:, 4 * W_KV:5 * W_KV].astype(BF16).astype(F32),
                      axis=1, keepdims=True)
        mw = jnp.maximum(jnp.max(sw, axis=1, keepdims=True), swn)
        pw = jnp.where(okw, jnp.exp(sw - mw), 0.0)
        pn = jnp.exp(swn - mw)
        lw = jnp.sum(pw, axis=1, keepdims=True) + pn
        vn = new[:, 5 * W_KV:6 * W_KV].astype(BF16).astype(F32)
        o_win = (_dot(pw.astype(BF16), vw) + pn.astype(BF16).astype(F32) * vn) * (1.0 / lw)

        gates = gate_ref[...]
        o_cmp = ocmp_ref[...]
        pieces = []
        for i in range(H_NSA):
            c0 = GATE0 + 3 * i
            val = (gates[:, c0:c0 + 1] * o_cmp[i:i + 1] + gates[:, c0 + 1:c0 + 2] * o_slc[i:i + 1]
                   + gates[:, c0 + 2:c0 + 3] * o_win[i:i + 1])
            if (i % 2) != (i // GQA_GROUP):
                val = pltpu.roll(val, HEAD_DIM, axis=1)
            pieces.append(val)
        lane1 = lax.broadcasted_iota(jnp.int32, (1, LANES), 1)
        for pr in range(H_NSA // 2):
            o_ref[:, pr * LANES:(pr + 1) * LANES] = jnp.where(lane1 < HEAD_DIM, pieces[2 * pr], pieces[2 * pr + 1])


def _nsa_decode(page_table, nq, new_row, small, win_state, cache, cw):
    s, n_pages = page_table.shape
    page = cache.shape[1]
    past_len = n_pages * page
    n_ch = past_len // CMP_STRIDE
    win_buf = win_state.shape[1]
    tok = lambda w: pl.BlockSpec((None, 1, w), lambda i, j, pt: (i, 0, 0))
    pg = lambda i, j, pt: (pt[i * n_pages + j % n_pages], 0, j // n_pages)
    const = lambda shape: pl.BlockSpec(shape, lambda i, j, pt: (0,) * len(shape), pipeline_mode=pl.Buffered(1))
    return pl.pallas_call(
        functools.partial(_nsa_dec_kernel, n_pages=n_pages, past_len=past_len),
        grid_spec=pltpu.PrefetchScalarGridSpec(
            num_scalar_prefetch=1,
            grid=(s, 2 * n_pages),
            in_specs=[tok(W_NSA), tok(6 * W_KV), tok(LANES),
                      pl.BlockSpec((None, win_buf, 2 * W_KV), lambda i, j, pt: (i, 0, 0)),
                      pl.BlockSpec((None, page, 2 * W_KV), pg)] + [const(w.shape) for w in cw],
            out_specs=tok(W_NSA),
            scratch_shapes=[pltpu.VMEM((past_len, W_KV), F32), pltpu.VMEM((past_len, W_KV), F32),
                            pltpu.VMEM((H_NSA, LANES), F32), pltpu.VMEM((H_NSA, LANES), F32),
                            pltpu.VMEM((H_NSA, 1), F32), pltpu.VMEM((H_NSA, 1), F32),
                            pltpu.VMEM((H_NSA, LANES), F32)],
        ),
        out_shape=jax.ShapeDtypeStruct((s, 1, W_NSA), F32),
        compiler_params=_params(("parallel", "arbitrary")),
        name="nsa_decode",
    )(page_table.reshape(-1), nq, new_row, small, win_state, cache, *cw)


def _page_copies(pt_ref, sample, slot, n_pages, page, streams):
    out = []
    for pg in range(n_pages):
        idx = pt_ref[sample * n_pages + pg]
        for hbm, buf, sem, on_lanes in streams:
            dst = buf.at[slot, :, pl.ds(pg * page, page)] if on_lanes else buf.at[slot, pg]
            out.append(pltpu.make_async_copy(hbm.at[idx], dst, sem.at[slot]))
    return out


def _gather_pages(pt_ref, n_pages, page, streams):
    i = pl.program_id(0)
    slot = lax.rem(i, 2)

    @pl.when(i == 0)
    def _():
        for c in _page_copies(pt_ref, 0, 0, n_pages, page, streams):
            c.start()

    @pl.when(i + 1 < pl.num_programs(0))
    def _():
        for c in _page_copies(pt_ref, i + 1, 1 - slot, n_pages, page, streams):
            c.start()

    for c in _page_copies(pt_ref, i, slot, n_pages, page, streams):
        c.wait()
    return slot


def _head_scores(q_col, k_view, s_ref, head, rows0, n_tiles, tile):
    qb = jnp.broadcast_to(q_col, (HEAD_DIM, tile))
    for pg in range(n_tiles):
        kt = k_view[rows0:rows0 + HEAD_DIM, pg * tile:(pg + 1) * tile]
        s_ref[pg, head:head + 1, :] = jnp.sum(kt * qb, axis=0, keepdims=True)


def _head_values(p_ref, v_view, head, rows0, n_tiles, tile):
    acc = jnp.zeros((HEAD_DIM, tile), F32)
    for pg in range(n_tiles):
        vt = v_view[rows0:rows0 + HEAD_DIM, pg * tile:(pg + 1) * tile]
        acc = acc + vt * jnp.broadcast_to(p_ref[pg, head:head + 1, :], (HEAD_DIM, tile))
    return jnp.sum(acc, axis=1, keepdims=True)


def _softmax_tiles(s, s_new, allow=None):
    if allow is not None:
        s = jnp.where(allow, s, NEG)
    m = jnp.maximum(jnp.max(jnp.max(s, axis=0), axis=1, keepdims=True), s_new)
    p = jnp.exp(s - m[None])
    if allow is not None:
        p = jnp.where(allow, p, 0.0)
    p_new = jnp.exp(s_new - m)
    l = jnp.sum(jnp.sum(p, axis=0), axis=1, keepdims=True) + p_new
    return p, p_new, 1.0 / l


def _col_dot(a_col, b_col, n_heads):
    return jnp.sum((a_col * b_col).reshape(n_heads, HEAD_DIM, 1), axis=1)


def _fox_dec2_kernel(pt_ref, q_ref, knew_ref, vnew_ref, lfnew_ref, kv_hbm, lf_hbm, o_ref,
                     kvbuf, lfbuf, s_ref, p_ref, sem_kv, sem_lf, *, n_pages, page):
    slot = _gather_pages(pt_ref, n_pages, page,
                         [(kv_hbm, kvbuf, sem_kv, True), (lf_hbm, lfbuf, sem_lf, False)])
    kv = kvbuf.at[slot]
    q = q_ref[...]
    for h in range(H_FOX):
        _head_scores(q[h * HEAD_DIM:(h + 1) * HEAD_DIM], kv, s_ref, h, h * HEAD_DIM, n_pages, page)

    rows = n_pages * H_FOX
    lf = lfbuf[slot].reshape(rows, page)
    r = lax.broadcasted_iota(jnp.int32, (page, page), 0)
    c = lax.broadcasted_iota(jnp.int32, (page, page), 1)
    later = (r > c).astype(BF16)
    hi, mid, lo = _split3(lf)
    within = _dot(hi, later) + _dot(mid, later) + _dot(lo, later)
    r = lax.broadcasted_iota(jnp.int32, (rows, rows), 0)
    c = lax.broadcasted_iota(jnp.int32, (rows, rows), 1)
    later_pages = ((c > r) & (((c - r) & (H_FOX - 1)) == 0)).astype(BF16)
    tot = jnp.broadcast_to(jnp.sum(lf, axis=1, keepdims=True), (rows, page))
    hi, mid, lo = _split3(tot)
    beyond = _dot(later_pages, hi) + _dot(later_pages, mid) + _dot(later_pages, lo)
    bias = (within + beyond + lfnew_ref[...]).reshape(n_pages, H_FOX, page)

    s_new = _col_dot(q, knew_ref[...], H_FOX)
    p, p_new, inv_l = _softmax_tiles(s_ref[...] + bias, s_new)
    p_ref[...] = p
    v_new = vnew_ref[...]
    for h in range(H_FOX):
        hs = slice(h * HEAD_DIM, (h + 1) * HEAD_DIM)
        o = _head_values(p_ref, kv, h, W_FOX + h * HEAD_DIM, n_pages, page)
        o_ref[hs, :] = (o + p_new[h:h + 1] * v_new[hs]) * inv_l[h:h + 1]


def _fox_decode2(page_table, q_col, k_col, v_col, lf_rows, cache_kvt, cache_lft):
    s, n_pages = page_table.shape
    rows, page = cache_kvt.shape[1], cache_kvt.shape[2]
    col = lambda n: pl.BlockSpec((None, n, 1), lambda i, pt: (i, 0, 0))
    anyspec = pl.BlockSpec(memory_space=pl.ANY)
    return pl.pallas_call(
        functools.partial(_fox_dec2_kernel, n_pages=n_pages, page=page),
        grid_spec=pltpu.PrefetchScalarGridSpec(
            num_scalar_prefetch=1,
            grid=(s,),
            in_specs=[col(W_FOX), col(W_FOX), col(W_FOX), col(n_pages * H_FOX), anyspec, anyspec],
            out_specs=col(W_FOX),
            scratch_shapes=[pltpu.VMEM((2, rows, n_pages * page), F32),
                            pltpu.VMEM((2, n_pages, H_FOX, page), F32),
                            pltpu.VMEM((n_pages, H_FOX, page), F32), pltpu.VMEM((n_pages, H_FOX, page), F32),
                            pltpu.SemaphoreType.DMA((2,)), pltpu.SemaphoreType.DMA((2,))],
        ),
        out_shape=jax.ShapeDtypeStruct((s, W_FOX, 1), F32),
        compiler_params=_params(("arbitrary",)),
        name="fox_decode",
    )(page_table.reshape(-1), q_col, k_col, v_col, lf_rows, cache_kvt, cache_lft)


def _nsa_dec2_kernel(pt_ref, qrow_ref, qcol_ref, new_ref, gate_ref, win_ref, cache_hbm,
                     wj_ref, pe_ref, wpe_ref, w2_ref, w2t_ref, o_ref,
                     xbuf, xk_buf, xv_buf, s_ref, p_ref, sw_ref, pw_ref, sem, *, n_pages, page):
    slot = _gather_pages(pt_ref, n_pages, page, [(cache_hbm, xbuf, sem, True)])
    x = xbuf.at[slot]
    past_len = n_pages * page
    n_ch = past_len // CMP_STRIDE
    n_slc = past_len // SLC_LEN + 1
    win_buf = win_ref.shape[1]
    g = GQA_GROUP
    row = lax.broadcasted_iota(jnp.int32, (H_NSA, LANES), 0)
    lane = lax.broadcasted_iota(jnp.int32, (H_NSA, LANES), 1)
    q_col = qcol_ref[...]
    new = new_ref[...]

    ident = (lax.broadcasted_iota(jnp.int32, (page, page), 0)
             == lax.broadcasted_iota(jnp.int32, (page, page), 1)).astype(BF16)
    for pg in range(n_pages):
        xt = _dot_nt(ident, x[0:2 * W_KV, pg * page:(pg + 1) * page].astype(BF16))
        xk_buf[pg * page:(pg + 1) * page, :] = xt[:, 0:W_KV]
        xv_buf[pg * page:(pg + 1) * page, :] = xt[:, W_KV:2 * W_KV]
    rows_of = lambda jj: (xk_buf[pl.ds(jj, n_ch, stride=CMP_STRIDE), :], xv_buf[pl.ds(jj, n_ch, stride=CMP_STRIDE), :])
    hid = _compress_hidden(rows_of, n_ch, wj_ref, pe_ref, wpe_ref).astype(BF16)
    kc = _dot(hid, w2_ref[:, 0:W_KV]).astype(BF16)
    vct = _dot_nt(w2t_ref[W_KV:2 * W_KV, :], hid).astype(BF16)

    q_row = qrow_ref[...]
    qbd = jnp.zeros((H_NSA, LANES), F32)
    for i in range(H_NSA):
        piece = q_row[:, (i // 2) * LANES:(i // 2 + 1) * LANES]
        if (i % 2) != (i // g):
            piece = pltpu.roll(piece, HEAD_DIM, axis=1)
        qbd = jnp.where(row == i, jnp.broadcast_to(piece, (H_NSA, LANES)), qbd)
    qbd = jnp.where(_div(lane, HEAD_DIM) == _div(row, g), qbd, 0.0).astype(BF16)
    n_idx = lax.broadcasted_iota(jnp.int32, (H_NSA, n_ch), 1)
    ok = (n_idx * CMP_STRIDE + CMP_LEN - 1 <= past_len) & (n_idx < n_ch - 1)
    sc = jnp.where(ok, _dot_nt(qbd, kc), NEG)
    e = jnp.where(ok, jnp.exp(sc - jnp.max(sc, axis=1, keepdims=True)), 0.0)
    lc = jnp.sum(e, axis=1, keepdims=True)
    a = e * (1.0 / jnp.where(lc > 0.0, lc, 1.0))
    a_hi = a.astype(BF16)
    a_lo = (a - a_hi.astype(F32)).astype(BF16)
    o_cmp_t = _dot_nt(vct, a_hi)
    ov = _overlap(n_ch, n_slc)
    imp8 = _dot(a_hi, ov) + _dot(a_lo, ov)
    imp = jnp.zeros((H_NSA, LANES), F32)
    for h in range(H_KV):
        tot = jnp.sum(imp8[h * g:(h + 1) * g], axis=0, keepdims=True)
        imp = jnp.where(_div(row, g) == h, jnp.broadcast_to(tot, (H_NSA, LANES)), imp)
    sel = _select_blocks(imp, jnp.full((H_NSA, 1), past_len, jnp.int32), n_slc, min(SLC_TOPN, n_slc))

    per_page = page // SLC_LEN
    allow = []
    for pg in range(n_pages):
        m = jnp.zeros((H_NSA, page), F32)
        for b in range(per_page):
            blk = pg * per_page + b
            lanes_b = _div(lax.broadcasted_iota(jnp.int32, (H_NSA, page), 1), SLC_LEN) == b
            m = jnp.where(lanes_b, jnp.broadcast_to(sel[:, blk:blk + 1], (H_NSA, page)), m)
        allow.append(m > 0.5)
    allow = jnp.stack(allow, axis=0)
    for i in range(H_NSA):
        _head_scores(q_col[i * HEAD_DIM:(i + 1) * HEAD_DIM], x, s_ref, i, 2 * W_KV + (i // g) * HEAD_DIM,
                     n_pages, page)
    k_new = jnp.concatenate([new[2 * W_KV + (i // g) * HEAD_DIM: 2 * W_KV + (i // g + 1) * HEAD_DIM]
                             for i in range(H_NSA)], axis=0)
    p, p_new, inv_l = _softmax_tiles(s_ref[...], _col_dot(q_col, k_new, H_NSA), allow)
    p_ref[...] = p

    wtile = min(win_buf, 4 * LANES)
    n_wt = win_buf // wtile
    wv = win_ref
    for i in range(H_NSA):
        _head_scores(q_col[i * HEAD_DIM:(i + 1) * HEAD_DIM], wv, sw_ref, i, (i // g) * HEAD_DIM, n_wt, wtile)
    slot_idx = (lax.broadcasted_iota(jnp.int32, (n_wt, H_NSA, wtile), 0) * wtile
                + lax.broadcasted_iota(jnp.int32, (n_wt, H_NSA, wtile), 2))
    kw_new = jnp.concatenate([new[4 * W_KV + (i // g) * HEAD_DIM: 4 * W_KV + (i // g + 1) * HEAD_DIM]
                              for i in range(H_NSA)], axis=0)
    pw, pw_new, inv_lw = _softmax_tiles(sw_ref[...], _col_dot(q_col, kw_new, H_NSA), (win_buf - slot_idx) < WINDOW)
    pw_ref[...] = pw

    gates = gate_ref[...]
    for i in range(H_NSA):
        h = i // g
        hs = slice(h * HEAD_DIM, (h + 1) * HEAD_DIM)
        o_slc = _head_values(p_ref, x, i, 3 * W_KV + h * HEAD_DIM, n_pages, page)
        o_slc = (o_slc + p_new[i:i + 1] * new[3 * W_KV + h * HEAD_DIM: 3 * W_KV + (h + 1) * HEAD_DIM]) * inv_l[i:i + 1]
        o_win = _head_values(pw_ref, wv, i, W_KV + h * HEAD_DIM, n_wt, wtile)
        o_win = (o_win + pw_new[i:i + 1] * new[5 * W_KV + h * HEAD_DIM: 5 * W_KV + (h + 1) * HEAD_DIM]) * inv_lw[i:i + 1]
        c0 = GATE0 + 3 * i
        o_ref[i * HEAD_DIM:(i + 1) * HEAD_DIM, :] = (gates[:, c0:c0 + 1] * o_cmp_t[hs, i:i + 1]
                                                     + gates[:, c0 + 1:c0 + 2] * o_slc
                                                     + gates[:, c0 + 2:c0 + 3] * o_win)


def _nsa_decode2(page_table, q_row, q_col, new_col, small, win_t, cache_t, cw):
    s, n_pages = page_table.shape
    rows, page = cache_t.shape[1], cache_t.shape[2]
    past_len = n_pages * page
    win_buf = win_t.shape[2]
    wtile = min(win_buf, 4 * LANES)
    col = lambda n: pl.BlockSpec((None, n, 1), lambda i, pt: (i, 0, 0))
    rowspec = lambda n: pl.BlockSpec((None, 1, n), lambda i, pt: (i, 0, 0))
    const = lambda shape: pl.BlockSpec(shape, lambda i, pt: (0,) * len(shape), pipeline_mode=pl.Buffered(1))
    return pl.pallas_call(
        functools.partial(_nsa_dec2_kernel, n_pages=n_pages, page=page),
        grid_spec=pltpu.PrefetchScalarGridSpec(
            num_scalar_prefetch=1,
            grid=(s,),
            in_specs=[rowspec(W_NSA), col(W_NSA), col(6 * W_KV), rowspec(LANES),
                      pl.BlockSpec((None, 2 * W_KV, win_buf), lambda i, pt: (i, 0, 0)),
                      pl.BlockSpec(memory_space=pl.ANY)] + [const(w.shape) for w in cw],
            out_specs=col(W_NSA),
            scratch_shapes=[pltpu.VMEM((2, rows, past_len), F32),
                            pltpu.VMEM((past_len, W_KV), F32), pltpu.VMEM((past_len, W_KV), F32),
                            pltpu.VMEM((n_pages, H_NSA, page), F32), pltpu.VMEM((n_pages, H_NSA, page), F32),
                            pltpu.VMEM((win_buf // wtile, H_NSA, wtile), F32),
                            pltpu.VMEM((win_buf // wtile, H_NSA, wtile), F32),
                            pltpu.SemaphoreType.DMA((2,))],
        ),
        out_shape=jax.ShapeDtypeStruct((s, W_NSA, 1), F32),
        compiler_params=_params(("arbitrary",)),
        name="nsa_decode",
    )(page_table.reshape(-1), q_row, q_col, new_col, small, win_t, cache_t, *cw)


def _rot_cols(w):
    d, n = w.shape
    w = w.reshape(d, n // HEAD_DIM, 2, HEAD_DIM // 2)
    return jnp.stack([-w[:, :, 1], w[:, :, 0]], axis=2).reshape(d, n)


def _prep_projection(w_in, b_f):
    c = [0, W_FOX, 2 * W_FOX, 3 * W_FOX, 3 * W_FOX + H_FOX, 3 * W_FOX + H_FOX + W_NSA,
         3 * W_FOX + H_FOX + W_NSA + 6 * W_KV]
    fq, fk, fv, ff, nq, kv = (w_in[:, c[i]:c[i + 1]] for i in range(6))
    gt = w_in[:, c[6]:]
    ks = jnp.concatenate([kv[:, br * 2 * W_KV: br * 2 * W_KV + W_KV] for br in range(3)], axis=1)
    w_big = jnp.concatenate([fq, fk, fv, nq, kv, _rot_cols(nq), _rot_cols(ks)], axis=1).astype(BF16)
    d = w_in.shape[0]
    w_small = jnp.concatenate([ff, gt, jnp.zeros((d, LANES - N_SMALL), w_in.dtype)], axis=1).astype(BF16)
    b_small = jnp.concatenate([b_f.astype(F32), jnp.zeros((LANES - H_FOX,), F32)])[None, :]
    return w_big, w_small, b_small


def _prep_compress(wk1, wk2, pek, wv1, wv2, pev):
    seg = CMP_STRIDE * HEAD_DIM
    ratio = CMP_LEN // CMP_STRIDE
    eye = jnp.eye(4, dtype=F32)

    def blocks(w):
        return w.reshape(ratio, CMP_STRIDE, HEAD_DIM, wk1.shape[1])

    per_group = jnp.stack([blocks(wk1), blocks(wk1), blocks(wv1), blocks(wv1)], axis=0)
    wj = jnp.einsum("grjde,gh->jgdrhe", per_group, eye)
    wj = wj.reshape(CMP_STRIDE, 4 * HEAD_DIM, ratio * 4 * HEAD_DIM).astype(BF16)
    pe = jnp.concatenate([pek.reshape(-1), pev.reshape(-1)])
    pe = jnp.broadcast_to(pe[None, :], (8, pe.shape[0])).astype(BF16)
    zero = jnp.zeros_like(wk1)
    wpe = jnp.concatenate([jnp.concatenate([wk1, wk1, zero, zero], axis=1),
                           jnp.concatenate([zero, zero, wv1, wv1], axis=1)], axis=0).astype(BF16)
    w2 = jnp.einsum("gde,gh->gdhe", jnp.stack([wk2, wk2, wv2, wv2]), eye)
    w2 = w2.reshape(4 * HEAD_DIM, 4 * HEAD_DIM).astype(BF16)
    del seg
    return wj, pe, wpe, w2, w2.T


def _rope_tables(pos):
    half = HEAD_DIM // 2
    inv = ROPE_THETA ** (-jnp.arange(half, dtype=F32) / half)
    ang = pos.astype(F32)[:, None] * inv[None, :]
    reps = LANES // half
    return jnp.tile(jnp.cos(ang), (1, reps)), jnp.tile(jnp.sin(ang), (1, reps))


def _row_tile(n, cap):
    t = min(n, cap)
    while n % t:
        t //= 2
    return t


def kernel(x_prompt, x_sample, cache_fox_kv, cache_fox_logf, cache_nsa_kv, state_nsa_win_kv, page_table,
           g_ffn1_pre, w_ffn1_gate, w_ffn1_up, w_ffn1_down, g_ffn1_post, g_mix_pre, w_in, b_fox_f,
           w_cmpk_1, w_cmpk_2, pe_cmpk, w_cmpv_1, w_cmpv_2, pe_cmpv, g_fox_out, g_nsa_out, w_out,
           g_mix_post, g_ffn2_pre, w_ffn2_gate, w_ffn2_up, w_ffn2_down, g_ffn2_post):
    depth = w_in.shape[0]
    b, t, d = x_prompt.shape
    s, dec_seq, _ = x_sample.shape
    assert dec_seq == 1, "the sample group decodes one token per sequence"
    n_pool, page = cache_fox_kv.shape[1], cache_fox_kv.shape[2]
    n_pages = page_table.shape[1]
    past_len = n_pages * page
    win_buf = state_nsa_win_kv.shape[2]
    assert t % LANES == 0 and page % SLC_LEN == 0
    page_table = page_table.astype(jnp.int32)

    tm_p = _row_tile(t, 512)
    cos_p, sin_p = _rope_tables(jnp.arange(t, dtype=jnp.int32))
    cos_s, sin_s = _rope_tables(jnp.full((s,), past_len, jnp.int32))
    row = lambda v: v.astype(F32)[None, :]

    yp = x_prompt.reshape(b * t, d)
    ys = x_sample.reshape(s, d)
    outs = [[] for _ in range(8)]
    for l in range(depth):
        ffn1 = (row(g_ffn1_pre[l]), w_ffn1_gate[l].astype(BF16), w_ffn1_up[l].astype(BF16),
                w_ffn1_down[l].astype(BF16), row(g_ffn1_post[l]))
        ffn2 = (row(g_ffn2_pre[l]), w_ffn2_gate[l].astype(BF16), w_ffn2_up[l].astype(BF16),
                w_ffn2_down[l].astype(BF16), row(g_ffn2_post[l]))
        w_big, w_small, b_small = _prep_projection(w_in[l], b_fox_f[l])
        cw = _prep_compress(w_cmpk_1[l], w_cmpk_2[l], pe_cmpk[l], w_cmpv_1[l], w_cmpv_2[l], pe_cmpv[l])
        merge = (row(g_fox_out[l]), row(g_nsa_out[l]), w_out[l].astype(BF16), row(g_mix_post[l]))

        hp = _half_ffn(yp, *ffn1, tm_p)
        fq, fkv, fk16, fv16, small, nq, nkv4, win, nkv16 = _project(
            hp, row(g_mix_pre[l]), w_big, w_small, b_small, cos_p, sin_p, tm_p, t // tm_p)
        small3 = small.reshape(b, t, LANES)
        cum = _cumsum(small3, _row_tile(t, 512))
        cum_t = jnp.swapaxes(cum[:, :, :H_FOX], 1, 2)
        o_fox = _fox_attention(fq.reshape(b, t, W_FOX), fk16.reshape(b, t, W_FOX), fv16.reshape(b, t, W_FOX),
                               cum, cum_t, _row_tile(t, 256))
        nkv4_3 = nkv4.reshape(b, t, 4 * W_KV)
        kc, vc = _compress(nkv4_3, cw[:4])
        o_nsa = _nsa_attention(nq.reshape(b, t, W_NSA), small3, kc, vc, nkv16.reshape(b, t, 6 * W_KV),
                               LANES, _row_tile(t, 512))
        yp = _merge_ffn(hp, o_fox.reshape(b * t, W_FOX), o_nsa.reshape(b * t, W_NSA), *merge, *ffn2, tm_p)
        keep = min(WINDOW, t)
        outs[0].append(fkv.reshape(b, t, 2, H_FOX, HEAD_DIM))
        outs[1].append(small3[:, :, :H_FOX])
        outs[2].append(nkv4_3.reshape(b, t, 4, H_KV, HEAD_DIM))
        outs[3].append(win.reshape(b, t, 2, H_KV, HEAD_DIM)[:, t - keep:])

        hs = _half_ffn(ys, *ffn1, s)
        fq, fkv, _, _, small, nq, nkv4, win, _ = _project(
            hs, row(g_mix_pre[l]), w_big, w_small, b_small, cos_s, sin_s, s, 1)
        to_rows = lambda c: jnp.transpose(c, (0, 2, 3, 4, 1)).reshape(c.shape[0], -1, c.shape[1])
        lf_rows = jnp.tile(small[:, :H_FOX], (1, n_pages)).reshape(s, n_pages * H_FOX, 1)
        o_fox = _fox_decode2(page_table, fq.reshape(s, W_FOX, 1), fkv[:, :W_FOX].reshape(s, W_FOX, 1),
                             fkv[:, W_FOX:].reshape(s, W_FOX, 1), lf_rows,
                             to_rows(cache_fox_kv[l]), jnp.swapaxes(cache_fox_logf[l], 1, 2))
        new_col = jnp.concatenate([nkv4, win], axis=1).reshape(s, 6 * W_KV, 1)
        o_nsa = _nsa_decode2(page_table, nq.reshape(s, 1, W_NSA), nq.reshape(s, W_NSA, 1), new_col,
                             small.reshape(s, 1, LANES), to_rows(state_nsa_win_kv[l]),
                             to_rows(cache_nsa_kv[l]), cw)
        ys = _merge_ffn(hs, o_fox.reshape(s, W_FOX), o_nsa.reshape(s, W_NSA), *merge, *ffn2, s)
        keep = min(WINDOW, past_len + 1)
        kw_all = jnp.concatenate([state_nsa_win_kv[l], win.reshape(s, 1, 2, H_KV, HEAD_DIM)], axis=1)
        outs[4].append(fkv.reshape(s, 1, 2, H_FOX, HEAD_DIM))
        outs[5].append(small[:, :H_FOX].reshape(s, 1, H_FOX))
        outs[6].append(nkv4.reshape(s, 1, 4, H_KV, HEAD_DIM))
        outs[7].append(kw_all[:, kw_all.shape[1] - keep:])

    stacked = [jnp.stack(o, axis=0) for o in outs]
    return (yp.reshape(b, t, d), ys.reshape(s, 1, d), *stacked)
```

```python
import functools

import jax
import jax.numpy as jnp
from jax import lax
from jax.experimental import pallas as pl
from jax.experimental.pallas import tpu as pltpu

HEAD_DIM = 64
H_FOX = 8
H_NSA = 8
H_KV = 2
GQA_GROUP = H_NSA // H_KV
W_FOX = H_FOX * HEAD_DIM
W_NSA = H_NSA * HEAD_DIM
W_KV = H_KV * HEAD_DIM
CMP_STRIDE = 16
CMP_LEN = 32
SLC_LEN = 64
SLC_TOPN = 16
WINDOW = 512
ROPE_THETA = 10000.0
EPS = 1e-6
NEG = -1e30
FORCE_SCORE = 1e4
N_SMALL = H_FOX + 3 * H_NSA
GATE0 = H_FOX

LANES = 128
MXU_N = 256
VMEM_LIMIT = 56 * 1024 * 1024
FOX_HEADS_PER_STEP = 2

F32 = jnp.float32
BF16 = jnp.bfloat16


def _dot(a, b):
    return jnp.dot(a, b, preferred_element_type=F32)


def _dot_nt(a, b):
    return lax.dot_general(a, b, (((1,), (1,)), ((), ())), preferred_element_type=F32)


def _div(x, n):
    assert n & (n - 1) == 0
    return lax.shift_right_logical(x, jnp.int32(n.bit_length() - 1))


def _split3(x):
    hi = x.astype(BF16)
    r1 = x - hi.astype(F32)
    mid = r1.astype(BF16)
    lo = (r1 - mid.astype(F32)).astype(BF16)
    return hi, mid, lo


def _rms(x, g):
    return x * lax.rsqrt(jnp.mean(x * x, axis=-1, keepdims=True) + EPS) * g


def _ff_chunks(d_ff):
    step = 6 * MXU_N
    return tuple((c, min(c + step, d_ff)) for c in range(0, d_ff, step))


def _ffn_core(x, gpre, wg_ref, wu_ref, wd_ref, gpost, chunks):
    xn = _rms(x, gpre).astype(BF16)
    acc = jnp.zeros(x.shape, F32)
    for c0, c1 in chunks:
        g = _dot(xn, wg_ref[:, c0:c1])
        u = _dot(xn, wu_ref[:, c0:c1])
        hm = (g * jax.nn.sigmoid(g) * u).astype(BF16)
        acc = acc + _dot(hm, wd_ref[c0:c1, :])
    return x + 0.5 * _rms(acc, gpost)


def _const_spec(shape):
    nd = len(shape)
    return pl.BlockSpec(shape, lambda *_: (0,) * nd, pipeline_mode=pl.Buffered(1))


def _params(sem):
    return pltpu.CompilerParams(dimension_semantics=sem, vmem_limit_bytes=VMEM_LIMIT)


def _lo_half(rows):
    return lax.broadcasted_iota(jnp.int32, (rows, LANES), 1) < HEAD_DIM


def _pad_heads(x, n_heads, fill):
    lo = _lo_half(x.shape[0])
    out = []
    for h in range(n_heads):
        piece = x[:, (h // 2) * LANES:(h // 2 + 1) * LANES]
        if h % 2:
            piece = pltpu.roll(piece, HEAD_DIM, axis=1)
        out.append(jnp.where(lo, piece, fill))
    return out


def _normalise(acc):
    den = jnp.where(_lo_half(acc.shape[0]), pltpu.roll(acc, HEAD_DIM, axis=1), 1.0)
    return acc * (1.0 / den)


def _pair_up(even, odd):
    return jnp.where(_lo_half(even.shape[0]), even, pltpu.roll(odd, HEAD_DIM, axis=1))


def _ffn_kernel(x_ref, gpre_ref, wg_ref, wu_ref, wd_ref, gpost_ref, o_ref, *, chunks):
    o_ref[...] = _ffn_core(x_ref[...], gpre_ref[...], wg_ref, wu_ref, wd_ref, gpost_ref[...], chunks)


def _half_ffn(x, gpre, wg, wu, wd, gpost, tm):
    n, d = x.shape
    d_ff = wg.shape[1]
    row = pl.BlockSpec((tm, d), lambda i: (i, 0))
    return pl.pallas_call(
        functools.partial(_ffn_kernel, chunks=_ff_chunks(d_ff)),
        grid=(n // tm,),
        in_specs=[row, _const_spec((1, d)), _const_spec((d, d_ff)), _const_spec((d, d_ff)),
                  _const_spec((d_ff, d)), _const_spec((1, d))],
        out_specs=row,
        out_shape=jax.ShapeDtypeStruct((n, d), F32),
        compiler_params=_params(("parallel",)),
        name="half_ffn",
    )(x, gpre, wg, wu, wd, gpost)


_C_FQ, _C_FK, _C_FV, _C_NQ, _C_KV, _C_NQR, _C_KR, _C_END = 0, 512, 1024, 1536, 2048, 2816, 3328, 3712


def _proj_kernel(h_ref, g_ref, wb_ref, ws_ref, bf_ref, cos_ref, sin_ref, small_ref, *rest, packed, n_pos_tiles):
    n = _rms(h_ref[...], g_ref[...]).astype(BF16)
    tm = n.shape[0]

    def mm(c0, c1):
        return _dot(n, wb_ref[:, c0:c1])

    def put(ref, tiles):
        for i, t in enumerate(tiles):
            ref[:, i * LANES:(i + 1) * LANES] = t.astype(BF16)

    def put_t(ref, tiles):
        for i, t in enumerate(tiles):
            for c in range(t.shape[1] // LANES):
                r0 = i * t.shape[1] + c * LANES
                ref[r0:r0 + LANES, :] = t[:, c * LANES:(c + 1) * LANES].T

    scale = HEAD_DIM ** -0.5
    lane = lax.broadcasted_iota(jnp.int32, (tm, LANES), 1)
    fq = mm(_C_FQ, _C_FK) * scale
    fk = mm(_C_FK, _C_FV)
    fv = mm(_C_FV, _C_NQ)

    cos = cos_ref[...]
    sin = sin_ref[...]
    nq = mm(_C_NQ, _C_KV)
    nqr = mm(_C_NQR, _C_KR)
    nq = jnp.concatenate([(nq[:, c * LANES:(c + 1) * LANES] * cos + nqr[:, c * LANES:(c + 1) * LANES] * sin) * scale
                          for c in range(W_NSA // LANES)], axis=1)

    kv = mm(_C_KV, _C_NQR)
    kr = mm(_C_KR, _C_END)
    ks, vs = [], []
    for br in range(3):
        k = kv[:, br * 2 * W_KV: br * 2 * W_KV + W_KV] * cos + kr[:, br * W_KV:(br + 1) * W_KV] * sin
        v = kv[:, br * 2 * W_KV + W_KV:(br + 1) * 2 * W_KV]
        ks.append(k)
        vs.append(v)

    sm = _dot(n, ws_ref[...]) + bf_ref[...]
    log_sig = jnp.minimum(sm, 0.0) - jnp.log(1.0 + jnp.exp(-jnp.abs(sm)))
    small_ref[...] = jnp.where(lane < H_FOX, log_sig, jax.nn.sigmoid(sm))

    if not packed:
        fkv_ref, nkv4_ref, win_ref, fq_ref, nq_ref = rest
        fkv_ref[...] = jnp.concatenate([fk, fv], axis=1)
        nkv4_ref[...] = jnp.concatenate([ks[0], vs[0], ks[1], vs[1]], axis=1)
        win_ref[...] = jnp.concatenate([ks[2], vs[2]], axis=1)
        fq_ref[...] = fq
        nq_ref[...] = nq
        return
    fkvt_ref, nkv4t_ref, wint_ref, cmp_ref, fqp_ref, fkp_ref, fvp_ref, nqp_ref, nkvp_ref = rest
    put_t(fkvt_ref, [fk, fv])
    put_t(nkv4t_ref, [ks[0], vs[0], ks[1], vs[1]])
    put_t(wint_ref, [ks[2], vs[2]])
    cmp_ref[...] = jnp.concatenate([ks[0], vs[0]], axis=1)
    ones3 = jnp.where((lane >= HEAD_DIM) & (lane < HEAD_DIM + 3), 1.0, 0.0)
    put(fqp_ref, _pad_heads(fq, H_FOX, 0.0))
    put(fkp_ref, _pad_heads(fk, H_FOX, ones3))
    put(fvp_ref, _pad_heads(fv, H_FOX, 1.0))
    put(nqp_ref, _pad_heads(nq, H_NSA, 0.0))
    pos = (lax.rem(pl.program_id(0), n_pos_tiles) * tm + lax.broadcasted_iota(jnp.int32, (tm, 1), 0))
    onehot = jnp.where(lane - HEAD_DIM == _div(pos, SLC_LEN), 1.0, 0.0)
    put(nkvp_ref, _pad_heads(ks[1], H_KV, onehot) + _pad_heads(vs[1], H_KV, 1.0)
        + _pad_heads(ks[2], H_KV, 0.0) + _pad_heads(vs[2], H_KV, 1.0))


def _project(h, g, w_big, w_small, b_small, cos, sin, tm, n_pos_tiles, packed):
    n, d = h.shape
    row = lambda w: pl.BlockSpec((tm, w), lambda i: (i, 0))
    pos = pl.BlockSpec((tm, LANES), lambda i: (i % n_pos_tiles, 0))
    if packed:
        outs = [(LANES, F32), (2 * W_KV, F32)] + [(H_FOX * LANES, BF16)] * 3 + [(H_NSA * LANES, BF16),
                                                                                (4 * H_KV * LANES, BF16)]
        t_rows = (2 * W_FOX, 4 * W_KV, 2 * W_KV)
    else:
        outs = [(LANES, F32), (2 * W_FOX, F32), (4 * W_KV, F32), (2 * W_KV, F32), (W_FOX, F32), (W_NSA, F32)]
        t_rows = ()
    out_specs = [row(w) for w, _ in outs]
    out_shape = [jax.ShapeDtypeStruct((n, w), dt) for w, dt in outs]
    batch = n // (tm * n_pos_tiles)
    for k, r in enumerate(t_rows):
        out_specs.insert(1 + k, pl.BlockSpec((None, r, tm), lambda i: (i // n_pos_tiles, 0, i % n_pos_tiles)))
        out_shape.insert(1 + k, jax.ShapeDtypeStruct((batch, r, tm * n_pos_tiles), F32))
    return pl.pallas_call(
        functools.partial(_proj_kernel, packed=packed, n_pos_tiles=n_pos_tiles),
        grid=(n // tm,),
        in_specs=[row(d), _const_spec((1, d)), _const_spec(w_big.shape), _const_spec(w_small.shape),
                  _const_spec((1, LANES)), pos, pos],
        out_specs=out_specs,
        out_shape=out_shape,
        compiler_params=_params(("parallel",)),
        name="project",
    )(h, g, w_big, w_small, b_small, cos, sin)


def _merge_ffn_kernel(h_ref, of_ref, on_ref, gf_ref, gn_ref, wo_ref, gmix_ref,
                      gpre_ref, wg_ref, wu_ref, wd_ref, gpost_ref, y_ref, *, chunks):
    of = _rms(of_ref[...], gf_ref[...]).astype(BF16)
    on = _rms(on_ref[...], gn_ref[...]).astype(BF16)
    mrg = _dot(of, wo_ref[0:W_FOX, :]) + _dot(on, wo_ref[W_FOX:W_FOX + W_NSA, :])
    h2 = h_ref[...] + _rms(mrg, gmix_ref[...])
    y_ref[...] = _ffn_core(h2, gpre_ref[...], wg_ref, wu_ref, wd_ref, gpost_ref[...], chunks)


def _merge_ffn(h, o_fox, o_nsa, gf, gn, w_out, gmix, gpre, wg, wu, wd, gpost, tm):
    n, d = h.shape
    d_ff = wg.shape[1]
    row = lambda w: pl.BlockSpec((tm, w), lambda i: (i, 0))
    return pl.pallas_call(
        functools.partial(_merge_ffn_kernel, chunks=_ff_chunks(d_ff)),
        grid=(n // tm,),
        in_specs=[row(d), row(W_FOX), row(W_NSA), _const_spec((1, W_FOX)), _const_spec((1, W_NSA)),
                  _const_spec(w_out.shape), _const_spec((1, d)), _const_spec((1, d)),
                  _const_spec((d, d_ff)), _const_spec((d, d_ff)), _const_spec((d_ff, d)), _const_spec((1, d))],
        out_specs=row(d),
        out_shape=jax.ShapeDtypeStruct((n, d), F32),
        compiler_params=_params(("parallel",)),
        name="merge_ffn",
    )(h, o_fox, o_nsa, gf, gn, w_out, gmix, gpre, wg, wu, wd, gpost)


def _cumsum_kernel(x_ref, o_ref, carry_ref):
    @pl.when(pl.program_id(1) == 0)
    def _():
        carry_ref[...] = jnp.zeros(carry_ref.shape, F32)

    x = x_ref[...]
    tc = x.shape[0]
    r = lax.broadcasted_iota(jnp.int32, (tc, tc), 0)
    c = lax.broadcasted_iota(jnp.int32, (tc, tc), 1)
    tri = (c <= r).astype(BF16)
    hi, mid, lo = _split3(x)
    cs = _dot(tri, hi) + _dot(tri, mid) + _dot(tri, lo) + carry_ref[...]
    o_ref[...] = cs
    carry_ref[...] = cs[tc - 1:tc, :]


def _cumsum(x, tc):
    b, t, w = x.shape
    blk = pl.BlockSpec((None, tc, w), lambda i, j: (i, j, 0))
    return pl.pallas_call(
        _cumsum_kernel,
        grid=(b, t // tc),
        in_specs=[blk],
        out_specs=blk,
        out_shape=jax.ShapeDtypeStruct(x.shape, F32),
        scratch_shapes=[pltpu.VMEM((1, w), F32)],
        compiler_params=_params(("parallel", "arbitrary")),
        name="logf_cumsum",
    )(x)


def _flash_update(s, v, m_ref, acc_ref, idx, sub=None, allow=None):
    if sub is not None:
        s = s - sub
    if allow is not None:
        s = jnp.where(allow, s, NEG)
    m_old = m_ref[idx]
    m_new = jnp.maximum(m_old, jnp.max(s, axis=1, keepdims=True))
    p = jnp.exp(s - m_new)
    acc_ref[idx] = jnp.exp(m_old - m_new) * acc_ref[idx] + _dot(p.astype(BF16), v)
    m_ref[idx] = m_new


def _fox_kernel(q_ref, k_ref, v_ref, cq_ref, ck_ref, o_ref, qp_ref, m_ref, acc_ref, *, tq, tk, heads):
    hg = pl.program_id(1)
    q0 = pl.program_id(2) * tq
    lane = lax.broadcasted_iota(jnp.int32, (tq, LANES), 1)
    cq_all = cq_ref[...]
    for g in range(heads):
        cq = jnp.sum(jnp.where(lane == hg * heads + g, cq_all, 0.0), axis=1, keepdims=True)
        hi, mid, lo = _split3(cq)
        q = q_ref[:, g * LANES:(g + 1) * LANES].astype(F32)
        q = jnp.where(lane == HEAD_DIM, hi.astype(F32), q)
        q = jnp.where(lane == HEAD_DIM + 1, mid.astype(F32), q)
        q = jnp.where(lane == HEAD_DIM + 2, lo.astype(F32), q)
        qp_ref[g] = q.astype(BF16)
    m_ref[...] = jnp.full(m_ref.shape, NEG, F32)
    acc_ref[...] = jnp.zeros(acc_ref.shape, F32)
    row8 = lax.broadcasted_iota(jnp.int32, (H_FOX, tk), 0)
    qpos = q0 + lax.broadcasted_iota(jnp.int32, (tq, tk), 0)
    kcol = lax.broadcasted_iota(jnp.int32, (tq, tk), 1)

    def scores(kt):
        k0 = pl.multiple_of(kt * tk, tk)
        return tuple(_dot_nt(qp_ref[g], k_ref[pl.ds(k0, tk), g * LANES:(g + 1) * LANES]) for g in range(heads))

    def update(kt, s, masked):
        k0 = pl.multiple_of(kt * tk, tk)
        ck_all = ck_ref[:, pl.ds(k0, tk)]
        allow = (k0 + kcol <= qpos) if masked else None
        for g in range(heads):
            ck = jnp.sum(jnp.where(row8 == hg * heads + g, ck_all, 0.0), axis=0, keepdims=True)
            _flash_update(s[g], v_ref[pl.ds(k0, tk), g * LANES:(g + 1) * LANES], m_ref, acc_ref, g,
                          sub=ck, allow=allow)

    def body(kt, s):
        s_next = scores(kt + 1)
        update(kt, s, False)
        return s_next

    n_full = _div(q0, tk)
    update(n_full, lax.fori_loop(0, n_full, body, scores(0)), True)
    for pr in range(heads // 2):
        o_ref[:, pr * LANES:(pr + 1) * LANES] = _pair_up(_normalise(acc_ref[2 * pr]), _normalise(acc_ref[2 * pr + 1]))


def _fox_attention(fq_p, fk_p, fv_p, cum, cum_t, tq, tk):
    b, t, _ = fq_p.shape
    g = FOX_HEADS_PER_STEP
    qblk = pl.BlockSpec((None, tq, g * LANES), lambda i, hg, qi: (i, qi, hg))
    kvblk = pl.BlockSpec((None, t, g * LANES), lambda i, hg, qi: (i, 0, hg))
    oblk = pl.BlockSpec((None, tq, g * HEAD_DIM), lambda i, hg, qi: (i, qi, hg))
    return pl.pallas_call(
        functools.partial(_fox_kernel, tq=tq, tk=tk, heads=g),
        grid=(b, H_FOX // g, t // tq),
        in_specs=[qblk, kvblk, kvblk,
                  pl.BlockSpec((None, tq, LANES), lambda i, hg, qi: (i, qi, 0)),
                  pl.BlockSpec((None, H_FOX, t), lambda i, hg, qi: (i, 0, 0))],
        out_specs=oblk,
        out_shape=jax.ShapeDtypeStruct((b, t, W_FOX), F32),
        scratch_shapes=[pltpu.VMEM((g, tq, LANES), BF16), pltpu.VMEM((g, tq, 1), F32),
                        pltpu.VMEM((g, tq, LANES), F32)],
        compiler_params=_params(("parallel", "parallel", "arbitrary")),
        name="fox_prompt",
    )(fq_p, fk_p, fv_p, cum, cum_t)


def _compress_hidden(load_rows, n_ch, wj_ref, pe_ref, wpe_ref):
    acc = jnp.zeros((n_ch, 2 * 4 * HEAD_DIM), F32)
    for j in range(CMP_STRIDE):
        xk, xv = load_rows(j)
        acc = acc + _dot(xk.astype(BF16), wj_ref[j, 0:W_KV, :]) + _dot(xv.astype(BF16), wj_ref[j, W_KV:2 * W_KV, :])
    first = acc[:, 0:4 * HEAD_DIM]
    second = pltpu.roll(acc[:, 4 * HEAD_DIM:], n_ch - 1, axis=0)
    pe_term = _dot(pe_ref[...], wpe_ref[...])[0:1, :]
    return jax.nn.gelu(first + second + pe_term)


def _compress_kernel(xk_ref, xv_ref, wj_ref, pe_ref, wpe_ref, w2p_ref, kc_ref, vc_ref, *, n_ch):
    rows = lambda j: (xk_ref[pl.ds(j, n_ch, stride=CMP_STRIDE), :], xv_ref[pl.ds(j, n_ch, stride=CMP_STRIDE), :])
    hid = _compress_hidden(rows, n_ch, wj_ref, pe_ref, wpe_ref).astype(BF16)
    out = _dot(hid, w2p_ref[...])
    kc_ref[...] = out[:, 0:H_KV * LANES].astype(BF16)
    for h in range(H_KV):
        vc = out[:, (H_KV + h) * LANES:(H_KV + h + 1) * LANES]
        vc_ref[:, h * LANES:(h + 1) * LANES] = jnp.where(_lo_half(n_ch), vc, 1.0).astype(BF16)


def _compress(nkv4, cw):
    b, t, _ = nkv4.shape
    n_ch = t // CMP_STRIDE
    out = pl.BlockSpec((None, n_ch, H_KV * LANES), lambda i: (i, 0, 0))
    return pl.pallas_call(
        functools.partial(_compress_kernel, n_ch=n_ch),
        grid=(b,),
        in_specs=[pl.BlockSpec((None, t, W_KV), lambda i: (i, 0, 0)),
                  pl.BlockSpec((None, t, W_KV), lambda i: (i, 0, 1))] + [_const_spec(w.shape) for w in cw],
        out_specs=[out, out],
        out_shape=[jax.ShapeDtypeStruct((b, n_ch, H_KV * LANES), BF16)] * 2,
        compiler_params=_params(("parallel",)),
        name="compress_prompt",
    )(nkv4, nkv4, *cw)


def _overlap(n_ch, n_slc):
    n = lax.broadcasted_iota(jnp.int32, (n_ch, LANES), 0)
    j = lax.broadcasted_iota(jnp.int32, (n_ch, LANES), 1)
    hit = ((n * CMP_STRIDE <= j * SLC_LEN + SLC_LEN - 1) & (n * CMP_STRIDE + CMP_LEN - 1 >= j * SLC_LEN)
           & (n < n_ch - 1) & (j < n_slc))
    return hit.astype(BF16)


def _overlap_t(n_ch, n_slc):
    j = lax.broadcasted_iota(jnp.int32, (LANES, n_ch), 0)
    n = lax.broadcasted_iota(jnp.int32, (LANES, n_ch), 1)
    hit = ((n * CMP_STRIDE <= j * SLC_LEN + SLC_LEN - 1) & (n * CMP_STRIDE + CMP_LEN - 1 >= j * SLC_LEN)
           & (n < n_ch - 1) & (j < n_slc))
    return hit.astype(BF16)


def _block_keys(imp, pos, n_slc, axis):
    blk = lax.broadcasted_iota(jnp.int32, imp.shape, axis)
    qblk = _div(pos, SLC_LEN)
    forced = (blk == 0) | (blk == qblk) | (blk == qblk - 1)
    valid = blk * SLC_LEN <= pos
    score = jnp.where(valid, jnp.where(forced, FORCE_SCORE, imp + 0.0), -1.0)
    score = jnp.where(blk < n_slc, score, -2.0)
    bits = lax.bitcast_convert_type(score, jnp.int32)
    return bits ^ (lax.shift_right_arithmetic(bits, jnp.int32(31)) & jnp.int32(0x7FFFFFFF))


def _count_beats(key, lo, hi, axis):
    blk = lax.broadcasted_iota(jnp.int32, key.shape, axis)
    km1 = key - 1
    cnt = jnp.zeros(key.shape, jnp.int32)
    for i in range(lo, hi):
        ki = key[i:i + 1, :] if axis == 0 else key[:, i:i + 1]
        cnt = cnt + jnp.where(ki > jnp.where(blk > i, km1, key), 1, 0)
    return cnt


def _nsa_kernel(q_ref, gate_ref, kc_ref, vc_ref, sk_ref, sv_ref, wk_ref, wv_ref, o_ref,
                q4_ref, q4s_ref, cnt_ref, m_ref, acc_ref, *, tq, tk, n_ch, n_slc, wlen):
    h = pl.program_id(1)
    q0 = pl.program_id(2) * tq
    g = GQA_GROUP
    n_sel = min(SLC_TOPN, n_slc)
    lane = lax.broadcasted_iota(jnp.int32, (tq, LANES), 1)
    lo = lane < HEAD_DIM
    pos = q0 + lax.broadcasted_iota(jnp.int32, (tq, 1), 0)
    for i in range(g):
        q4_ref[i * tq:(i + 1) * tq, :] = q_ref[:, i * LANES:(i + 1) * LANES]
    q4 = q4_ref[...]

    sc = _dot_nt(q4, kc_ref[...]).reshape(g, tq, n_ch)
    n_idx = lax.broadcasted_iota(jnp.int32, (tq, n_ch), 1)
    ok = ((n_idx * CMP_STRIDE + CMP_LEN - 1 <= pos) & (n_idx < n_ch - 1))[None]
    sc = jnp.where(ok, sc, NEG)
    e = jnp.where(ok, jnp.exp(sc - jnp.max(sc, axis=-1, keepdims=True)), 0.0)
    lc = jnp.sum(e, axis=-1, keepdims=True)
    a = (e * (1.0 / jnp.where(lc > 0.0, lc, 1.0))).reshape(g * tq, n_ch)
    a_hi = a.astype(BF16)
    a_lo = (a - a_hi.astype(F32)).astype(BF16)
    o_cmp = _dot(a_hi, vc_ref[...]).reshape(g, tq, LANES)
    ov_t = _overlap_t(n_ch, n_slc)
    imp4 = _dot_nt(ov_t, a_hi) + _dot_nt(ov_t, a_lo)
    imp_t = sum(imp4[:, i * tq:(i + 1) * tq] for i in range(g))

    key = _block_keys(imp_t, q0 + lax.broadcasted_iota(jnp.int32, (1, tq), 1), n_slc, 0)
    n_valid = _div(q0 + tq - 1, SLC_LEN) + 1
    cnt_ref[...] = jnp.zeros(cnt_ref.shape, jnp.int32)
    for b0 in range(0, n_slc, 8):
        @pl.when((b0 < n_valid) & (n_valid > n_sel))
        def _():
            cnt_ref[...] += _count_beats(key, b0, min(b0 + 8, n_slc), 0)
    sel_t = (cnt_ref[...] < n_sel) & (lax.broadcasted_iota(jnp.int32, (LANES, tq), 0) < n_slc)
    sel_bias = pltpu.roll(jnp.where(sel_t, 0.0, NEG).T, HEAD_DIM, axis=1)
    for i in range(g):
        qi = q_ref[:, i * LANES:(i + 1) * LANES].astype(F32)
        q4s_ref[i * tq:(i + 1) * tq, :] = jnp.where(lo, qi, sel_bias).astype(BF16)

    m_ref[...] = jnp.full(m_ref.shape, NEG, F32)
    acc_ref[...] = jnp.zeros(acc_ref.shape, F32)

    pos4 = q0 + (lax.broadcasted_iota(jnp.int32, (g * tq, 1), 0) & (tq - 1))

    def scores(kt):
        return _dot_nt(q4s_ref[...], sk_ref[pl.ds(pl.multiple_of(kt * tk, tk), tk), :])

    def update(kt, s, masked):
        k0 = pl.multiple_of(kt * tk, tk)
        allow = (k0 + lax.broadcasted_iota(jnp.int32, (g * tq, tk), 1) <= pos4) if masked else None
        _flash_update(s, sv_ref[pl.ds(k0, tk), :], m_ref, acc_ref, 0, allow=allow)

    def body(kt, s):
        s_next = scores(kt + 1)
        update(kt, s, False)
        return s_next

    n_full = _div(q0, tk)
    update(n_full, lax.fori_loop(0, n_full, body, scores(0)), True)
    o_slc = _normalise(acc_ref[0]).reshape(g, tq, LANES)

    ws = pl.multiple_of(jnp.maximum(q0 + tq - wlen, 0), tq)
    dist = pos4 - (ws + lax.broadcasted_iota(jnp.int32, (g * tq, wlen), 1))
    okw = (dist >= 0) & (dist < WINDOW)
    sw = jnp.where(okw, _dot_nt(q4, wk_ref[pl.ds(ws, wlen), :]), NEG)
    pw = jnp.exp(sw - jnp.max(sw, axis=-1, keepdims=True))
    o_win = _normalise(_dot(pw.astype(BF16), wv_ref[pl.ds(ws, wlen), :])).reshape(g, tq, LANES)

    gates = gate_ref[...]
    vals = []
    for i in range(g):
        col0 = GATE0 + 3 * (g * h + i)
        gc = [jnp.sum(jnp.where(lane == col0 + c, gates, 0.0), axis=1, keepdims=True) for c in range(3)]
        vals.append(gc[0] * o_cmp[i] + gc[1] * o_slc[i] + gc[2] * o_win[i])
    for pr in range(g // 2):
        o_ref[:, pr * LANES:(pr + 1) * LANES] = _pair_up(vals[2 * pr], vals[2 * pr + 1])


def _nsa_attention(nq_p, small, kc_p, vc_p, nkv_p, tq, tk):
    b, t, _ = nq_p.shape
    n_ch = kc_p.shape[1]
    n_slc = -(-t // SLC_LEN)
    assert n_slc <= LANES - HEAD_DIM, "the block mask rides in the 64 spare query lanes"
    wlen = min(t, WINDOW + tq)
    g = GQA_GROUP
    cblk = pl.BlockSpec((None, n_ch, LANES), lambda i, h, qi: (i, 0, h))
    lane_blk = lambda c: pl.BlockSpec((None, t, LANES), lambda i, h, qi: (i, 0, c * H_KV + h))
    return pl.pallas_call(
        functools.partial(_nsa_kernel, tq=tq, tk=tk, n_ch=n_ch, n_slc=n_slc, wlen=wlen),
        grid=(b, H_KV, t // tq),
        in_specs=[pl.BlockSpec((None, tq, g * LANES), lambda i, h, qi: (i, qi, h)),
                  pl.BlockSpec((None, tq, LANES), lambda i, h, qi: (i, qi, 0)), cblk, cblk,
                  lane_blk(0), lane_blk(1), lane_blk(2), lane_blk(3)],
        out_specs=pl.BlockSpec((None, tq, g * HEAD_DIM), lambda i, h, qi: (i, qi, h)),
        out_shape=jax.ShapeDtypeStruct((b, t, W_NSA), F32),
        scratch_shapes=[pltpu.VMEM((g * tq, LANES), BF16), pltpu.VMEM((g * tq, LANES), BF16),
                        pltpu.VMEM((LANES, tq), jnp.int32),
                        pltpu.VMEM((1, g * tq, 1), F32), pltpu.VMEM((1, g * tq, LANES), F32)],
        compiler_params=_params(("parallel", "parallel", "arbitrary")),
        name="nsa_prompt",
    )(nq_p, small, kc_p, vc_p, nkv_p, nkv_p, nkv_p, nkv_p)


def _page_copies(pt_ref, sample, slot, n_pages, page, streams):
    out = []
    for pg in range(n_pages):
        idx = pt_ref[sample * n_pages + pg]
        for hbm, buf, sem, on_lanes in streams:
            dst = buf.at[slot, :, pl.ds(pg * page, page)] if on_lanes else buf.at[slot, pg]
            out.append(pltpu.make_async_copy(hbm.at[idx], dst, sem.at[slot]))
    return out


def _gather_pages(pt_ref, n_pages, page, streams):
    i = pl.program_id(0)
    slot = lax.rem(i, 2)

    @pl.when(i == 0)
    def _():
        for c in _page_copies(pt_ref, 0, 0, n_pages, page, streams):
            c.start()

    @pl.when(i + 1 < pl.num_programs(0))
    def _():
        for c in _page_copies(pt_ref, i + 1, 1 - slot, n_pages, page, streams):
            c.start()

    for c in _page_copies(pt_ref, i, slot, n_pages, page, streams):
        c.wait()
    return slot


def _head_scores(q_col, k_view, s_ref, head, rows0, n_tiles, tile):
    qb = jnp.broadcast_to(q_col, (HEAD_DIM, tile))
    for pg in range(n_tiles):
        kt = k_view[rows0:rows0 + HEAD_DIM, pg * tile:(pg + 1) * tile]
        s_ref[pg, head:head + 1, :] = jnp.sum(kt * qb, axis=0, keepdims=True)


def _head_values(p_ref, v_view, head, rows0, n_tiles, tile):
    acc = jnp.zeros((HEAD_DIM, tile), F32)
    for pg in range(n_tiles):
        vt = v_view[rows0:rows0 + HEAD_DIM, pg * tile:(pg + 1) * tile]
        acc = acc + vt * jnp.broadcast_to(p_ref[pg, head:head + 1, :], (HEAD_DIM, tile))
    return jnp.sum(acc, axis=1, keepdims=True)


def _softmax_tiles(s, s_new, allow=None):
    if allow is not None:
        s = jnp.where(allow, s, NEG)
    m = jnp.maximum(jnp.max(jnp.max(s, axis=0), axis=1, keepdims=True), s_new)
    p = jnp.exp(s - m[None])
    if allow is not None:
        p = jnp.where(allow, p, 0.0)
    p_new = jnp.exp(s_new - m)
    l = jnp.sum(jnp.sum(p, axis=0), axis=1, keepdims=True) + p_new
    return p, p_new, 1.0 / l


def _col_dot(a_col, b_col, n_heads):
    return jnp.sum((a_col * b_col).reshape(n_heads, HEAD_DIM, 1), axis=1)


def _fox_dec_kernel(pt_ref, q_ref, knew_ref, vnew_ref, lfnew_ref, kv_hbm, lf_hbm, o_ref,
                    kvbuf, lfbuf, s_ref, p_ref, sem_kv, sem_lf, *, n_pages, page):
    slot = _gather_pages(pt_ref, n_pages, page,
                         [(kv_hbm, kvbuf, sem_kv, True), (lf_hbm, lfbuf, sem_lf, False)])
    kv = kvbuf.at[slot]
    q = q_ref[...]
    for h in range(H_FOX):
        _head_scores(q[h * HEAD_DIM:(h + 1) * HEAD_DIM], kv, s_ref, h, h * HEAD_DIM, n_pages, page)

    rows = n_pages * H_FOX
    lf = lfbuf[slot].reshape(rows, page)
    r = lax.broadcasted_iota(jnp.int32, (page, page), 0)
    c = lax.broadcasted_iota(jnp.int32, (page, page), 1)
    later = (r > c).astype(BF16)
    hi, mid, lo = _split3(lf)
    within = _dot(hi, later) + _dot(mid, later) + _dot(lo, later)
    r = lax.broadcasted_iota(jnp.int32, (rows, rows), 0)
    c = lax.broadcasted_iota(jnp.int32, (rows, rows), 1)
    later_pages = ((c > r) & (((c - r) & (H_FOX - 1)) == 0)).astype(BF16)
    tot = jnp.broadcast_to(jnp.sum(lf, axis=1, keepdims=True), (rows, page))
    hi, mid, lo = _split3(tot)
    beyond = _dot(later_pages, hi) + _dot(later_pages, mid) + _dot(later_pages, lo)
    bias = (within + beyond + lfnew_ref[...]).reshape(n_pages, H_FOX, page)

    s_new = _col_dot(q, knew_ref[...], H_FOX)
    p, p_new, inv_l = _softmax_tiles(s_ref[...] + bias, s_new)
    p_ref[...] = p
    v_new = vnew_ref[...]
    for h in range(H_FOX):
        hs = slice(h * HEAD_DIM, (h + 1) * HEAD_DIM)
        o = _head_values(p_ref, kv, h, W_FOX + h * HEAD_DIM, n_pages, page)
        o_ref[hs, :] = (o + p_new[h:h + 1] * v_new[hs]) * inv_l[h:h + 1]


def _fox_decode(page_table, q_col, k_col, v_col, lf_rows, cache_kvt, cache_lft):
    s, n_pages = page_table.shape
    rows, page = cache_kvt.shape[1], cache_kvt.shape[2]
    col = lambda n: pl.BlockSpec((None, n, 1), lambda i, pt: (i, 0, 0))
    anyspec = pl.BlockSpec(memory_space=pl.ANY)
    return pl.pallas_call(
        functools.partial(_fox_dec_kernel, n_pages=n_pages, page=page),
        grid_spec=pltpu.PrefetchScalarGridSpec(
            num_scalar_prefetch=1,
            grid=(s,),
            in_specs=[col(W_FOX), col(W_FOX), col(W_FOX), col(n_pages * H_FOX), anyspec, anyspec],
            out_specs=col(W_FOX),
            scratch_shapes=[pltpu.VMEM((2, rows, n_pages * page), F32),
                            pltpu.VMEM((2, n_pages, H_FOX, page), F32),
                            pltpu.VMEM((n_pages, H_FOX, page), F32), pltpu.VMEM((n_pages, H_FOX, page), F32),
                            pltpu.SemaphoreType.DMA((2,)), pltpu.SemaphoreType.DMA((2,))],
        ),
        out_shape=jax.ShapeDtypeStruct((s, W_FOX, 1), F32),
        compiler_params=_params(("arbitrary",)),
        name="fox_decode",
    )(page_table.reshape(-1), q_col, k_col, v_col, lf_rows, cache_kvt, cache_lft)


def _nsa_dec_kernel(pt_ref, qrow_ref, qcol_ref, new_ref, gate_ref, win_ref, cache_hbm,
                    wj_ref, pe_ref, wpe_ref, w2_ref, w2t_ref, o_ref,
                    xbuf, xk_buf, xv_buf, s_ref, p_ref, sw_ref, pw_ref, sem, *, n_pages, page):
    slot = _gather_pages(pt_ref, n_pages, page, [(cache_hbm, xbuf, sem, True)])
    x = xbuf.at[slot]
    past_len = n_pages * page
    n_ch = past_len // CMP_STRIDE
    n_slc = past_len // SLC_LEN + 1
    n_sel = min(SLC_TOPN, n_slc)
    win_buf = win_ref.shape[1]
    g = GQA_GROUP
    row = lax.broadcasted_iota(jnp.int32, (H_NSA, LANES), 0)
    lane = lax.broadcasted_iota(jnp.int32, (H_NSA, LANES), 1)
    q_col = qcol_ref[...]
    new = new_ref[...]

    ident = (lax.broadcasted_iota(jnp.int32, (page, page), 0)
             == lax.broadcasted_iota(jnp.int32, (page, page), 1)).astype(BF16)
    for pg in range(n_pages):
        xt = _dot_nt(ident, x[0:2 * W_KV, pg * page:(pg + 1) * page].astype(BF16))
        xk_buf[pg * page:(pg + 1) * page, :] = xt[:, 0:W_KV]
        xv_buf[pg * page:(pg + 1) * page, :] = xt[:, W_KV:2 * W_KV]
    rows_of = lambda jj: (xk_buf[pl.ds(jj, n_ch, stride=CMP_STRIDE), :], xv_buf[pl.ds(jj, n_ch, stride=CMP_STRIDE), :])
    hid = _compress_hidden(rows_of, n_ch, wj_ref, pe_ref, wpe_ref).astype(BF16)
    kc = _dot(hid, w2_ref[:, 0:W_KV]).astype(BF16)
    vct = _dot_nt(w2t_ref[W_KV:2 * W_KV, :], hid).astype(BF16)

    q_row = qrow_ref[...]
    qbd = jnp.zeros((H_NSA, LANES), F32)
    for i in range(H_NSA):
        piece = q_row[:, (i // 2) * LANES:(i // 2 + 1) * LANES]
        if (i % 2) != (i // g):
            piece = pltpu.roll(piece, HEAD_DIM, axis=1)
        qbd = jnp.where(row == i, jnp.broadcast_to(piece, (H_NSA, LANES)), qbd)
    qbd = jnp.where(_div(lane, HEAD_DIM) == _div(row, g), qbd, 0.0).astype(BF16)
    n_idx = lax.broadcasted_iota(jnp.int32, (H_NSA, n_ch), 1)
    ok = (n_idx * CMP_STRIDE + CMP_LEN - 1 <= past_len) & (n_idx < n_ch - 1)
    sc = jnp.where(ok, _dot_nt(qbd, kc), NEG)
    e = jnp.where(ok, jnp.exp(sc - jnp.max(sc, axis=1, keepdims=True)), 0.0)
    lc = jnp.sum(e, axis=1, keepdims=True)
    a = e * (1.0 / jnp.where(lc > 0.0, lc, 1.0))
    a_hi = a.astype(BF16)
    a_lo = (a - a_hi.astype(F32)).astype(BF16)
    o_cmp_t = _dot_nt(vct, a_hi)
    ov = _overlap(n_ch, n_slc)
    imp8 = _dot(a_hi, ov) + _dot(a_lo, ov)
    imp = jnp.zeros((H_NSA, LANES), F32)
    for h in range(H_KV):
        tot = jnp.sum(imp8[h * g:(h + 1) * g], axis=0, keepdims=True)
        imp = jnp.where(_div(row, g) == h, jnp.broadcast_to(tot, (H_NSA, LANES)), imp)
    key = _block_keys(imp, jnp.full((H_NSA, 1), past_len, jnp.int32), n_slc, 1)
    sel = jnp.where((_count_beats(key, 0, n_slc, 1) < n_sel) & (lane < n_slc), 1.0, 0.0)

    per_page = page // SLC_LEN
    allow = []
    for pg in range(n_pages):
        m = jnp.zeros((H_NSA, page), F32)
        for b in range(per_page):
            blk = pg * per_page + b
            lanes_b = _div(lax.broadcasted_iota(jnp.int32, (H_NSA, page), 1), SLC_LEN) == b
            m = jnp.where(lanes_b, jnp.broadcast_to(sel[:, blk:blk + 1], (H_NSA, page)), m)
        allow.append(m > 0.5)
    allow = jnp.stack(allow, axis=0)
    for i in range(H_NSA):
        _head_scores(q_col[i * HEAD_DIM:(i + 1) * HEAD_DIM], x, s_ref, i, 2 * W_KV + (i // g) * HEAD_DIM,
                     n_pages, page)
    k_new = jnp.concatenate([new[2 * W_KV + (i // g) * HEAD_DIM: 2 * W_KV + (i // g + 1) * HEAD_DIM]
                             for i in range(H_NSA)], axis=0)
    p, p_new, inv_l = _softmax_tiles(s_ref[...], _col_dot(q_col, k_new, H_NSA), allow)
    p_ref[...] = p

    wtile = min(win_buf, 4 * LANES)
    n_wt = win_buf // wtile
    wv = win_ref
    for i in range(H_NSA):
        _head_scores(q_col[i * HEAD_DIM:(i + 1) * HEAD_DIM], wv, sw_ref, i, (i // g) * HEAD_DIM, n_wt, wtile)
    slot_idx = (lax.broadcasted_iota(jnp.int32, (n_wt, H_NSA, wtile), 0) * wtile
                + lax.broadcasted_iota(jnp.int32, (n_wt, H_NSA, wtile), 2))
    kw_new = jnp.concatenate([new[4 * W_KV + (i // g) * HEAD_DIM: 4 * W_KV + (i // g + 1) * HEAD_DIM]
                              for i in range(H_NSA)], axis=0)
    pw, pw_new, inv_lw = _softmax_tiles(sw_ref[...], _col_dot(q_col, kw_new, H_NSA), (win_buf - slot_idx) < WINDOW)
    pw_ref[...] = pw

    gates = gate_ref[...]
    for i in range(H_NSA):
        h = i // g
        hs = slice(h * HEAD_DIM, (h + 1) * HEAD_DIM)
        o_slc = _head_values(p_ref, x, i, 3 * W_KV + h * HEAD_DIM, n_pages, page)
        o_slc = (o_slc + p_new[i:i + 1] * new[3 * W_KV + h * HEAD_DIM: 3 * W_KV + (h + 1) * HEAD_DIM]) * inv_l[i:i + 1]
        o_win = _head_values(pw_ref, wv, i, W_KV + h * HEAD_DIM, n_wt, wtile)
        o_win = (o_win + pw_new[i:i + 1] * new[5 * W_KV + h * HEAD_DIM: 5 * W_KV + (h + 1) * HEAD_DIM]) * inv_lw[i:i + 1]
        c0 = GATE0 + 3 * i
        o_ref[i * HEAD_DIM:(i + 1) * HEAD_DIM, :] = (gates[:, c0:c0 + 1] * o_cmp_t[hs, i:i + 1]
                                                     + gates[:, c0 + 1:c0 + 2] * o_slc
                                                     + gates[:, c0 + 2:c0 + 3] * o_win)


def _nsa_decode(page_table, q_row, q_col, new_col, small, win_t, cache_t, cw):
    s, n_pages = page_table.shape
    rows, page = cache_t.shape[1], cache_t.shape[2]
    past_len = n_pages * page
    win_buf = win_t.shape[2]
    wtile = min(win_buf, 4 * LANES)
    col = lambda n: pl.BlockSpec((None, n, 1), lambda i, pt: (i, 0, 0))
    rowspec = lambda n: pl.BlockSpec((None, 1, n), lambda i, pt: (i, 0, 0))
    const = lambda shape: pl.BlockSpec(shape, lambda i, pt: (0,) * len(shape), pipeline_mode=pl.Buffered(1))
    return pl.pallas_call(
        functools.partial(_nsa_dec_kernel, n_pages=n_pages, page=page),
        grid_spec=pltpu.PrefetchScalarGridSpec(
            num_scalar_prefetch=1,
            grid=(s,),
            in_specs=[rowspec(W_NSA), col(W_NSA), col(6 * W_KV), rowspec(LANES),
                      pl.BlockSpec((None, 2 * W_KV, win_buf), lambda i, pt: (i, 0, 0)),
                      pl.BlockSpec(memory_space=pl.ANY)] + [const(w.shape) for w in cw],
            out_specs=col(W_NSA),
            scratch_shapes=[pltpu.VMEM((2, rows, past_len), F32),
                            pltpu.VMEM((past_len, W_KV), F32), pltpu.VMEM((past_len, W_KV), F32),
                            pltpu.VMEM((n_pages, H_NSA, page), F32), pltpu.VMEM((n_pages, H_NSA, page), F32),
                            pltpu.VMEM((win_buf // wtile, H_NSA, wtile), F32),
                            pltpu.VMEM((win_buf // wtile, H_NSA, wtile), F32),
                            pltpu.SemaphoreType.DMA((2,))],
        ),
        out_shape=jax.ShapeDtypeStruct((s, W_NSA, 1), F32),
        compiler_params=_params(("arbitrary",)),
        name="nsa_decode",
    )(page_table.reshape(-1), q_row, q_col, new_col, small, win_t, cache_t, *cw)


def _rot_cols(w):
    d, n = w.shape
    w = w.reshape(d, n // HEAD_DIM, 2, HEAD_DIM // 2)
    return jnp.stack([-w[:, :, 1], w[:, :, 0]], axis=2).reshape(d, n)


def _prep_projection(w_in, b_f):
    c = [0, W_FOX, 2 * W_FOX, 3 * W_FOX, 3 * W_FOX + H_FOX, 3 * W_FOX + H_FOX + W_NSA,
         3 * W_FOX + H_FOX + W_NSA + 6 * W_KV]
    fq, fk, fv, ff, nq, kv = (w_in[:, c[i]:c[i + 1]] for i in range(6))
    gt = w_in[:, c[6]:]
    ks = jnp.concatenate([kv[:, br * 2 * W_KV: br * 2 * W_KV + W_KV] for br in range(3)], axis=1)
    w_big = jnp.concatenate([fq, fk, fv, nq, kv, _rot_cols(nq), _rot_cols(ks)], axis=1).astype(BF16)
    d = w_in.shape[0]
    w_small = jnp.concatenate([ff, gt, jnp.zeros((d, LANES - N_SMALL), w_in.dtype)], axis=1).astype(BF16)
    b_small = jnp.concatenate([b_f.astype(F32), jnp.zeros((LANES - H_FOX,), F32)])[None, :]
    return w_big, w_small, b_small


def _prep_compress(wk1, wk2, pek, wv1, wv2, pev):
    ratio = CMP_LEN // CMP_STRIDE
    eye = jnp.eye(4, dtype=F32)

    def blocks(w):
        return w.reshape(ratio, CMP_STRIDE, HEAD_DIM, wk1.shape[1])

    per_group = jnp.stack([blocks(wk1), blocks(wk1), blocks(wv1), blocks(wv1)], axis=0)
    wj = jnp.einsum("grjde,gh->jgdrhe", per_group, eye)
    wj = wj.reshape(CMP_STRIDE, 4 * HEAD_DIM, ratio * 4 * HEAD_DIM).astype(BF16)
    pe = jnp.concatenate([pek.reshape(-1), pev.reshape(-1)])
    pe = jnp.broadcast_to(pe[None, :], (8, pe.shape[0])).astype(BF16)
    zero = jnp.zeros_like(wk1)
    wpe = jnp.concatenate([jnp.concatenate([wk1, wk1, zero, zero], axis=1),
                           jnp.concatenate([zero, zero, wv1, wv1], axis=1)], axis=0).astype(BF16)
    w2 = jnp.einsum("gde,gh->gdhe", jnp.stack([wk2, wk2, wv2, wv2]), eye)
    w2_pad = jnp.concatenate([w2, jnp.zeros_like(w2)], axis=3)
    w2 = w2.reshape(4 * HEAD_DIM, 4 * HEAD_DIM).astype(BF16)
    w2_pad = w2_pad.reshape(4 * HEAD_DIM, 4 * LANES).astype(BF16)
    return (wj, pe, wpe), w2, w2_pad


def _rope_tables(pos):
    half = HEAD_DIM // 2
    inv = ROPE_THETA ** (-jnp.arange(half, dtype=F32) / half)
    ang = pos.astype(F32)[:, None] * inv[None, :]
    reps = LANES // half
    return jnp.tile(jnp.cos(ang), (1, reps)), jnp.tile(jnp.sin(ang), (1, reps))


def _row_tile(n, cap):
    t = min(n, cap)
    while n % t:
        t //= 2
    return t


def kernel(x_prompt, x_sample, cache_fox_kv, cache_fox_logf, cache_nsa_kv, state_nsa_win_kv, page_table,
           g_ffn1_pre, w_ffn1_gate, w_ffn1_up, w_ffn1_down, g_ffn1_post, g_mix_pre, w_in, b_fox_f,
           w_cmpk_1, w_cmpk_2, pe_cmpk, w_cmpv_1, w_cmpv_2, pe_cmpv, g_fox_out, g_nsa_out, w_out,
           g_mix_post, g_ffn2_pre, w_ffn2_gate, w_ffn2_up, w_ffn2_down, g_ffn2_post):
    depth = w_in.shape[0]
    b, t, d = x_prompt.shape
    s, dec_seq, _ = x_sample.shape
    assert dec_seq == 1, "the sample group decodes one token per sequence"
    page = cache_fox_kv.shape[2]
    n_pages = page_table.shape[1]
    past_len = n_pages * page
    assert t % LANES == 0 and page % SLC_LEN == 0
    page_table = page_table.astype(jnp.int32)

    tm_p = _row_tile(t, 512)
    cos_p, sin_p = _rope_tables(jnp.arange(t, dtype=jnp.int32))
    cos_s, sin_s = _rope_tables(jnp.full((s,), past_len, jnp.int32))
    row = lambda v: v.astype(F32)[None, :]
    to_rows = lambda c: jnp.transpose(c, (0, 2, 3, 4, 1)).reshape(c.shape[0], -1, c.shape[1])

    yp = x_prompt.reshape(b * t, d)
    ys = x_sample.reshape(s, d)
    outs = [[] for _ in range(8)]
    for l in range(depth):
        ffn1 = (row(g_ffn1_pre[l]), w_ffn1_gate[l].astype(BF16), w_ffn1_up[l].astype(BF16),
                w_ffn1_down[l].astype(BF16), row(g_ffn1_post[l]))
        ffn2 = (row(g_ffn2_pre[l]), w_ffn2_gate[l].astype(BF16), w_ffn2_up[l].astype(BF16),
                w_ffn2_down[l].astype(BF16), row(g_ffn2_post[l]))
        w_big, w_small, b_small = _prep_projection(w_in[l], b_fox_f[l])
        cw, w2, w2_pad = _prep_compress(w_cmpk_1[l], w_cmpk_2[l], pe_cmpk[l], w_cmpv_1[l], w_cmpv_2[l], pe_cmpv[l])
        merge = (row(g_fox_out[l]), row(g_nsa_out[l]), w_out[l].astype(BF16), row(g_mix_post[l]))

        hp = _half_ffn(yp, *ffn1, tm_p)
        small, fkv_t, nkv4_t, win_t, cmp, fq_p, fk_p, fv_p, nq_p, nkv_p = _project(
            hp, row(g_mix_pre[l]), w_big, w_small, b_small, cos_p, sin_p, tm_p, t // tm_p, True)
        small3 = small.reshape(b, t, LANES)
        cum = _cumsum(small3, _row_tile(t, 512))
        cum_t = jnp.swapaxes(cum[:, :, :H_FOX], 1, 2)
        per_head = lambda x: x.reshape(b, t, x.shape[1])
        o_fox = _fox_attention(per_head(fq_p), per_head(fk_p), per_head(fv_p), cum, cum_t,
                               _row_tile(t, 512), _row_tile(t, 512))
        kc_p, vc_p = _compress(cmp.reshape(b, t, 2 * W_KV), cw + (w2_pad,))
        o_nsa = _nsa_attention(per_head(nq_p), small3, kc_p, vc_p, per_head(nkv_p), LANES, _row_tile(t, 512))
        yp = _merge_ffn(hp, o_fox.reshape(b * t, W_FOX), o_nsa.reshape(b * t, W_NSA), *merge, *ffn2, tm_p)
        keep = min(WINDOW, t)
        tokens_first = lambda x, *dims: jnp.transpose(x.reshape(b, *dims, x.shape[2]), (0, 4, 1, 2, 3))
        outs[0].append(tokens_first(fkv_t, 2, H_FOX, HEAD_DIM))
        outs[1].append(small3[:, :, :H_FOX])
        outs[2].append(tokens_first(nkv4_t, 4, H_KV, HEAD_DIM))
        outs[3].append(tokens_first(win_t[:, :, t - keep:], 2, H_KV, HEAD_DIM))

        hs = _half_ffn(ys, *ffn1, s)
        small, fkv, nkv4, win, fq, nq = _project(
            hs, row(g_mix_pre[l]), w_big, w_small, b_small, cos_s, sin_s, s, 1, False)
        lf_rows = jnp.tile(small[:, :H_FOX], (1, n_pages)).reshape(s, n_pages * H_FOX, 1)
        o_fox = _fox_decode(page_table, fq.reshape(s, W_FOX, 1), fkv[:, :W_FOX].reshape(s, W_FOX, 1),
                            fkv[:, W_FOX:].reshape(s, W_FOX, 1), lf_rows,
                            to_rows(cache_fox_kv[l]), jnp.swapaxes(cache_fox_logf[l], 1, 2))
        new_col = jnp.concatenate([nkv4, win], axis=1).reshape(s, 6 * W_KV, 1)
        o_nsa = _nsa_decode(page_table, nq.reshape(s, 1, W_NSA), nq.reshape(s, W_NSA, 1), new_col,
                            small.reshape(s, 1, LANES), to_rows(state_nsa_win_kv[l]),
                            to_rows(cache_nsa_kv[l]), cw + (w2, w2.T))
        ys = _merge_ffn(hs, o_fox.reshape(s, W_FOX), o_nsa.reshape(s, W_NSA), *merge, *ffn2, s)
        keep = min(WINDOW, past_len + 1)
        kw_all = jnp.concatenate([state_nsa_win_kv[l], win.reshape(s, 1, 2, H_KV, HEAD_DIM)], axis=1)
        outs[4].append(fkv.reshape(s, 1, 2, H_FOX, HEAD_DIM))
        outs[5].append(small[:, :H_FOX].reshape(s, 1, H_FOX))
        outs[6].append(nkv4.reshape(s, 1, 4, H_KV, HEAD_DIM))
        outs[7].append(kw_all[:, kw_all.shape[1] - keep:])

    stacked = [jnp.stack(o, axis=0) for o in outs]
    return (yp.reshape(b, t, d), ys.reshape(s, 1, d), *stacked)
```

```python
import functools

import jax
import jax.numpy as jnp
from jax import lax
from jax.experimental import pallas as pl
from jax.experimental.pallas import tpu as pltpu

HEAD_DIM = 64
H_FOX = 8
H_NSA = 8
H_KV = 2
GQA_GROUP = H_NSA // H_KV
W_FOX = H_FOX * HEAD_DIM
W_NSA = H_NSA * HEAD_DIM
W_KV = H_KV * HEAD_DIM
CMP_STRIDE = 16
CMP_LEN = 32
SLC_LEN = 64
SLC_TOPN = 16
WINDOW = 512
ROPE_THETA = 10000.0
EPS = 1e-6
NEG = -1e30
FORCE_SCORE = 1e4
N_SMALL = H_FOX + 3 * H_NSA
GATE0 = H_FOX

LANES = 128
MXU_N = 256
VMEM_LIMIT = 56 * 1024 * 1024
FOX_HEADS_PER_STEP = 2

F32 = jnp.float32
BF16 = jnp.bfloat16


def _dot(a, b):
    return jnp.dot(a, b, preferred_element_type=F32)


def _dot_nt(a, b):
    return lax.dot_general(a, b, (((1,), (1,)), ((), ())), preferred_element_type=F32)


def _div(x, n):
    assert n & (n - 1) == 0
    return lax.shift_right_logical(x, jnp.int32(n.bit_length() - 1))


def _split3(x):
    hi = x.astype(BF16)
    r1 = x - hi.astype(F32)
    mid = r1.astype(BF16)
    lo = (r1 - mid.astype(F32)).astype(BF16)
    return hi, mid, lo


def _rms(x, g):
    return x * lax.rsqrt(jnp.mean(x * x, axis=-1, keepdims=True) + EPS) * g


def _ff_chunks(d_ff):
    step = 6 * MXU_N
    return tuple((c, min(c + step, d_ff)) for c in range(0, d_ff, step))


def _ffn_core(x, gpre, wg_ref, wu_ref, wd_ref, gpost, chunks):
    xn = _rms(x, gpre).astype(BF16)
    acc = jnp.zeros(x.shape, F32)
    for c0, c1 in chunks:
        g = _dot(xn, wg_ref[:, c0:c1])
        u = _dot(xn, wu_ref[:, c0:c1])
        hm = (g * jax.nn.sigmoid(g) * u).astype(BF16)
        acc = acc + _dot(hm, wd_ref[c0:c1, :])
    return x + 0.5 * _rms(acc, gpost)


def _const_spec(shape):
    nd = len(shape)
    return pl.BlockSpec(shape, lambda *_: (0,) * nd, pipeline_mode=pl.Buffered(1))


def _params(sem):
    return pltpu.CompilerParams(dimension_semantics=sem, vmem_limit_bytes=VMEM_LIMIT)


def _lo_half(rows):
    return lax.broadcasted_iota(jnp.int32, (rows, LANES), 1) < HEAD_DIM


def _pad_heads(x, n_heads, fill):
    lo = _lo_half(x.shape[0])
    out = []
    for h in range(n_heads):
        piece = x[:, (h // 2) * LANES:(h // 2 + 1) * LANES]
        if h % 2:
            piece = pltpu.roll(piece, HEAD_DIM, axis=1)
        out.append(jnp.where(lo, piece, fill))
    return out


def _normalise(acc):
    den = jnp.where(_lo_half(acc.shape[0]), pltpu.roll(acc, HEAD_DIM, axis=1), 1.0)
    return acc * (1.0 / den)


def _pair_up(even, odd):
    return jnp.where(_lo_half(even.shape[0]), even, pltpu.roll(odd, HEAD_DIM, axis=1))


def _ffn_kernel(x_ref, gpre_ref, wg_ref, wu_ref, wd_ref, gpost_ref, o_ref, *, chunks):
    o_ref[...] = _ffn_core(x_ref[...], gpre_ref[...], wg_ref, wu_ref, wd_ref, gpost_ref[...], chunks)


def _half_ffn(x, gpre, wg, wu, wd, gpost, tm):
    n, d = x.shape
    d_ff = wg.shape[1]
    row = pl.BlockSpec((tm, d), lambda i: (i, 0))
    return pl.pallas_call(
        functools.partial(_ffn_kernel, chunks=_ff_chunks(d_ff)),
        grid=(n // tm,),
        in_specs=[row, _const_spec((1, d)), _const_spec((d, d_ff)), _const_spec((d, d_ff)),
                  _const_spec((d_ff, d)), _const_spec((1, d))],
        out_specs=row,
        out_shape=jax.ShapeDtypeStruct((n, d), F32),
        compiler_params=_params(("parallel",)),
        name="half_ffn",
    )(x, gpre, wg, wu, wd, gpost)


_C_FQ, _C_FK, _C_FV, _C_NQ, _C_KV, _C_NQR, _C_KR, _C_END = 0, 512, 1024, 1536, 2048, 2816, 3328, 3712


def _proj_kernel(h_ref, g_ref, wb_ref, ws_ref, bf_ref, cos_ref, sin_ref, small_ref, *rest, packed, n_pos_tiles):
    n = _rms(h_ref[...], g_ref[...]).astype(BF16)
    tm = n.shape[0]

    def mm(c0, c1):
        return _dot(n, wb_ref[:, c0:c1])

    def put(ref, tiles):
        for i, t in enumerate(tiles):
            ref[:, i * LANES:(i + 1) * LANES] = t.astype(BF16)

    def put_t(ref, tiles):
        for i, t in enumerate(tiles):
            for c in range(t.shape[1] // LANES):
                r0 = i * t.shape[1] + c * LANES
                ref[r0:r0 + LANES, :] = t[:, c * LANES:(c + 1) * LANES].T

    scale = HEAD_DIM ** -0.5
    lane = lax.broadcasted_iota(jnp.int32, (tm, LANES), 1)
    fq = mm(_C_FQ, _C_FK) * scale
    fk = mm(_C_FK, _C_FV)
    fv = mm(_C_FV, _C_NQ)

    cos = cos_ref[...]
    sin = sin_ref[...]
    nq = mm(_C_NQ, _C_KV)
    nqr = mm(_C_NQR, _C_KR)
    nq = jnp.concatenate([(nq[:, c * LANES:(c + 1) * LANES] * cos + nqr[:, c * LANES:(c + 1) * LANES] * sin) * scale
                          for c in range(W_NSA // LANES)], axis=1)

    kv = mm(_C_KV, _C_NQR)
    kr = mm(_C_KR, _C_END)
    ks, vs = [], []
    for br in range(3):
        k = kv[:, br * 2 * W_KV: br * 2 * W_KV + W_KV] * cos + kr[:, br * W_KV:(br + 1) * W_KV] * sin
        v = kv[:, br * 2 * W_KV + W_KV:(br + 1) * 2 * W_KV]
        ks.append(k)
        vs.append(v)

    sm = _dot(n, ws_ref[...]) + bf_ref[...]
    log_sig = jnp.minimum(sm, 0.0) - jnp.log(1.0 + jnp.exp(-jnp.abs(sm)))
    small_ref[...] = jnp.where(lane < H_FOX, log_sig, jax.nn.sigmoid(sm))

    if not packed:
        fkv_ref, nkv4_ref, win_ref, fq_ref, nq_ref = rest
        fkv_ref[...] = jnp.concatenate([fk, fv], axis=1)
        nkv4_ref[...] = jnp.concatenate([ks[0], vs[0], ks[1], vs[1]], axis=1)
        win_ref[...] = jnp.concatenate([ks[2], vs[2]], axis=1)
        fq_ref[...] = fq
        nq_ref[...] = nq
        return
    fkvt_ref, nkv4t_ref, wint_ref, cmp_ref, fqp_ref, fkp_ref, fvp_ref, nqp_ref, nkvp_ref = rest
    put_t(fkvt_ref, [fk, fv])
    put_t(nkv4t_ref, [ks[0], vs[0], ks[1], vs[1]])
    put_t(wint_ref, [ks[2], vs[2]])
    cmp_ref[...] = jnp.concatenate([ks[0], vs[0]], axis=1)
    ones3 = jnp.where((lane >= HEAD_DIM) & (lane < HEAD_DIM + 3), 1.0, 0.0)
    put(fqp_ref, _pad_heads(fq, H_FOX, 0.0))
    put(fkp_ref, _pad_heads(fk, H_FOX, ones3))
    put(fvp_ref, _pad_heads(fv, H_FOX, 1.0))
    put(nqp_ref, _pad_heads(nq, H_NSA, 0.0))
    pos = (lax.rem(pl.program_id(0), n_pos_tiles) * tm + lax.broadcasted_iota(jnp.int32, (tm, 1), 0))
    onehot = jnp.where(lane - HEAD_DIM == _div(pos, SLC_LEN), 1.0, 0.0)
    put(nkvp_ref, _pad_heads(ks[1], H_KV, onehot) + _pad_heads(vs[1], H_KV, 1.0)
        + _pad_heads(ks[2], H_KV, 0.0) + _pad_heads(vs[2], H_KV, 1.0))


def _project(h, g, w_big, w_small, b_small, cos, sin, tm, n_pos_tiles, packed):
    n, d = h.shape
    row = lambda w: pl.BlockSpec((tm, w), lambda i: (i, 0))
    pos = pl.BlockSpec((tm, LANES), lambda i: (i % n_pos_tiles, 0))
    if packed:
        outs = [(LANES, F32), (2 * W_KV, F32)] + [(H_FOX * LANES, BF16)] * 3 + [(H_NSA * LANES, BF16),
                                                                                (4 * H_KV * LANES, BF16)]
        t_rows = (2 * W_FOX, 4 * W_KV, 2 * W_KV)
    else:
        outs = [(LANES, F32), (2 * W_FOX, F32), (4 * W_KV, F32), (2 * W_KV, F32), (W_FOX, F32), (W_NSA, F32)]
        t_rows = ()
    out_specs = [row(w) for w, _ in outs]
    out_shape = [jax.ShapeDtypeStruct((n, w), dt) for w, dt in outs]
    batch = n // (tm * n_pos_tiles)
    for k, r in enumerate(t_rows):
        out_specs.insert(1 + k, pl.BlockSpec((None, r, tm), lambda i: (i // n_pos_tiles, 0, i % n_pos_tiles)))
        out_shape.insert(1 + k, jax.ShapeDtypeStruct((batch, r, tm * n_pos_tiles), F32))
    return pl.pallas_call(
        functools.partial(_proj_kernel, packed=packed, n_pos_tiles=n_pos_tiles),
        grid=(n // tm,),
        in_specs=[row(d), _const_spec((1, d)), _const_spec(w_big.shape), _const_spec(w_small.shape),
                  _const_spec((1, LANES)), pos, pos],
        out_specs=out_specs,
        out_shape=out_shape,
        compiler_params=_params(("parallel",)),
        name="project",
    )(h, g, w_big, w_small, b_small, cos, sin)


def _merge_ffn_kernel(h_ref, of_ref, on_ref, gf_ref, gn_ref, wo_ref, gmix_ref,
                      gpre_ref, wg_ref, wu_ref, wd_ref, gpost_ref, y_ref, *, chunks):
    of = _rms(of_ref[...], gf_ref[...]).astype(BF16)
    on = _rms(on_ref[...], gn_ref[...]).astype(BF16)
    mrg = _dot(of, wo_ref[0:W_FOX, :]) + _dot(on, wo_ref[W_FOX:W_FOX + W_NSA, :])
    h2 = h_ref[...] + _rms(mrg, gmix_ref[...])
    y_ref[...] = _ffn_core(h2, gpre_ref[...], wg_ref, wu_ref, wd_ref, gpost_ref[...], chunks)


def _merge_ffn(h, o_fox, o_nsa, gf, gn, w_out, gmix, gpre, wg, wu, wd, gpost, tm):
    n, d = h.shape
    d_ff = wg.shape[1]
    row = lambda w: pl.BlockSpec((tm, w), lambda i: (i, 0))
    return pl.pallas_call(
        functools.partial(_merge_ffn_kernel, chunks=_ff_chunks(d_ff)),
        grid=(n // tm,),
        in_specs=[row(d), row(W_FOX), row(W_NSA), _const_spec((1, W_FOX)), _const_spec((1, W_NSA)),
                  _const_spec(w_out.shape), _const_spec((1, d)), _const_spec((1, d)),
                  _const_spec((d, d_ff)), _const_spec((d, d_ff)), _const_spec((d_ff, d)), _const_spec((1, d))],
        out_specs=row(d),
        out_shape=jax.ShapeDtypeStruct((n, d), F32),
        compiler_params=_params(("parallel",)),
        name="merge_ffn",
    )(h, o_fox, o_nsa, gf, gn, w_out, gmix, gpre, wg, wu, wd, gpost)


def _cumsum_kernel(x_ref, o_ref, carry_ref):
    @pl.when(pl.program_id(1) == 0)
    def _():
        carry_ref[...] = jnp.zeros(carry_ref.shape, F32)

    x = x_ref[...]
    tc = x.shape[0]
    r = lax.broadcasted_iota(jnp.int32, (tc, tc), 0)
    c = lax.broadcasted_iota(jnp.int32, (tc, tc), 1)
    tri = (c <= r).astype(BF16)
    hi, mid, lo = _split3(x)
    cs = _dot(tri, hi) + _dot(tri, mid) + _dot(tri, lo) + carry_ref[...]
    o_ref[...] = cs
    carry_ref[...] = cs[tc - 1:tc, :]


def _cumsum(x, tc):
    b, t, w = x.shape
    blk = pl.BlockSpec((None, tc, w), lambda i, j: (i, j, 0))
    return pl.pallas_call(
        _cumsum_kernel,
        grid=(b, t // tc),
        in_specs=[blk],
        out_specs=blk,
        out_shape=jax.ShapeDtypeStruct(x.shape, F32),
        scratch_shapes=[pltpu.VMEM((1, w), F32)],
        compiler_params=_params(("parallel", "arbitrary")),
        name="logf_cumsum",
    )(x)


def _flash_update(s, v, m_ref, acc_ref, idx, sub=None, allow=None):
    if sub is not None:
        s = s - sub
    if allow is not None:
        s = jnp.where(allow, s, NEG)
    m_old = m_ref[idx]
    m_new = jnp.maximum(m_old, jnp.max(s, axis=1, keepdims=True))
    p = jnp.exp(s - m_new)
    acc_ref[idx] = jnp.exp(m_old - m_new) * acc_ref[idx] + _dot(p.astype(BF16), v)
    m_ref[idx] = m_new


def _fox_kernel(q_ref, k_ref, v_ref, cq_ref, ck_ref, o_ref, qp_ref, m_ref, acc_ref, *, tq, tk, heads):
    hg = pl.program_id(1)
    q0 = pl.program_id(2) * tq
    lane = lax.broadcasted_iota(jnp.int32, (tq, LANES), 1)
    cq_all = cq_ref[...]
    for g in range(heads):
        cq = jnp.sum(jnp.where(lane == hg * heads + g, cq_all, 0.0), axis=1, keepdims=True)
        hi, mid, lo = _split3(cq)
        q = q_ref[:, g * LANES:(g + 1) * LANES].astype(F32)
        q = jnp.where(lane == HEAD_DIM, hi.astype(F32), q)
        q = jnp.where(lane == HEAD_DIM + 1, mid.astype(F32), q)
        q = jnp.where(lane == HEAD_DIM + 2, lo.astype(F32), q)
        qp_ref[g] = q.astype(BF16)
    m_ref[...] = jnp.full(m_ref.shape, NEG, F32)
    acc_ref[...] = jnp.zeros(acc_ref.shape, F32)
    row8 = lax.broadcasted_iota(jnp.int32, (H_FOX, tk), 0)
    qpos = q0 + lax.broadcasted_iota(jnp.int32, (tq, tk), 0)
    kcol = lax.broadcasted_iota(jnp.int32, (tq, tk), 1)

    def scores(kt):
        k0 = pl.multiple_of(kt * tk, tk)
        return tuple(_dot_nt(qp_ref[g], k_ref[pl.ds(k0, tk), g * LANES:(g + 1) * LANES]) for g in range(heads))

    def update(kt, s, masked):
        k0 = pl.multiple_of(kt * tk, tk)
        ck_all = ck_ref[:, pl.ds(k0, tk)]
        allow = (k0 + kcol <= qpos) if masked else None
        for g in range(heads):
            ck = jnp.sum(jnp.where(row8 == hg * heads + g, ck_all, 0.0), axis=0, keepdims=True)
            _flash_update(s[g], v_ref[pl.ds(k0, tk), g * LANES:(g + 1) * LANES], m_ref, acc_ref, g,
                          sub=ck, allow=allow)

    def body(kt, s):
        s_next = scores(kt + 1)
        update(kt, s, False)
        return s_next

    n_full = _div(q0, tk)
    update(n_full, lax.fori_loop(0, n_full, body, scores(0)), True)
    for pr in range(heads // 2):
        o_ref[:, pr * LANES:(pr + 1) * LANES] = _pair_up(_normalise(acc_ref[2 * pr]), _normalise(acc_ref[2 * pr + 1]))


def _fox_attention(fq_p, fk_p, fv_p, cum, cum_t, tq, tk):
    b, t, _ = fq_p.shape
    g = FOX_HEADS_PER_STEP
    qblk = pl.BlockSpec((None, tq, g * LANES), lambda i, hg, qi: (i, qi, hg))
    kvblk = pl.BlockSpec((None, t, g * LANES), lambda i, hg, qi: (i, 0, hg))
    oblk = pl.BlockSpec((None, tq, g * HEAD_DIM), lambda i, hg, qi: (i, qi, hg))
    return pl.pallas_call(
        functools.partial(_fox_kernel, tq=tq, tk=tk, heads=g),
        grid=(b, H_FOX // g, t // tq),
        in_specs=[qblk, kvblk, kvblk,
                  pl.BlockSpec((None, tq, LANES), lambda i, hg, qi: (i, qi, 0)),
                  pl.BlockSpec((None, H_FOX, t), lambda i, hg, qi: (i, 0, 0))],
        out_specs=oblk,
        out_shape=jax.ShapeDtypeStruct((b, t, W_FOX), F32),
        scratch_shapes=[pltpu.VMEM((g, tq, LANES), BF16), pltpu.VMEM((g, tq, 1), F32),
                        pltpu.VMEM((g, tq, LANES), F32)],
        compiler_params=_params(("parallel", "parallel", "arbitrary")),
        name="fox_prompt",
    )(fq_p, fk_p, fv_p, cum, cum_t)


def _compress_hidden(load_rows, n_ch, wj_ref, pe_ref, wpe_ref):
    acc_k = jnp.zeros((n_ch, 2 * W_KV), F32)
    acc_v = jnp.zeros((n_ch, 2 * W_KV), F32)
    for j in range(CMP_STRIDE):
        xk, xv = load_rows(j)
        acc_k = acc_k + _dot(xk.astype(BF16), wj_ref[j, 0:W_KV, :])
        acc_v = acc_v + _dot(xv.astype(BF16), wj_ref[j, W_KV:2 * W_KV, :])
    first = jnp.concatenate([acc_k[:, 0:W_KV], acc_v[:, 0:W_KV]], axis=1)
    second = jnp.concatenate([acc_k[:, W_KV:], acc_v[:, W_KV:]], axis=1)
    second = pltpu.roll(second, n_ch - 1, axis=0)
    pe_term = _dot(pe_ref[...], wpe_ref[...])[0:1, :]
    return jax.nn.gelu(first + second + pe_term)


def _compress_kernel(xk_ref, xv_ref, wj_ref, pe_ref, wpe_ref, w2p_ref, kc_ref, vc_ref, *, n_ch):
    rows = lambda j: (xk_ref[pl.ds(j, n_ch, stride=CMP_STRIDE), :], xv_ref[pl.ds(j, n_ch, stride=CMP_STRIDE), :])
    hid = _compress_hidden(rows, n_ch, wj_ref, pe_ref, wpe_ref).astype(BF16)
    out = _dot(hid, w2p_ref[...])
    kc_ref[...] = out[:, 0:H_KV * LANES].astype(BF16)
    for h in range(H_KV):
        vc = out[:, (H_KV + h) * LANES:(H_KV + h + 1) * LANES]
        vc_ref[:, h * LANES:(h + 1) * LANES] = jnp.where(_lo_half(n_ch), vc, 1.0).astype(BF16)


def _compress(nkv4, cw):
    b, t, _ = nkv4.shape
    n_ch = t // CMP_STRIDE
    out = pl.BlockSpec((None, n_ch, H_KV * LANES), lambda i: (i, 0, 0))
    return pl.pallas_call(
        functools.partial(_compress_kernel, n_ch=n_ch),
        grid=(b,),
        in_specs=[pl.BlockSpec((None, t, W_KV), lambda i: (i, 0, 0)),
                  pl.BlockSpec((None, t, W_KV), lambda i: (i, 0, 1))] + [_const_spec(w.shape) for w in cw],
        out_specs=[out, out],
        out_shape=[jax.ShapeDtypeStruct((b, n_ch, H_KV * LANES), BF16)] * 2,
        compiler_params=_params(("parallel",)),
        name="compress_prompt",
    )(nkv4, nkv4, *cw)


def _overlap(n_ch, n_slc):
    n = lax.broadcasted_iota(jnp.int32, (n_ch, LANES), 0)
    j = lax.broadcasted_iota(jnp.int32, (n_ch, LANES), 1)
    hit = ((n * CMP_STRIDE <= j * SLC_LEN + SLC_LEN - 1) & (n * CMP_STRIDE + CMP_LEN - 1 >= j * SLC_LEN)
           & (n < n_ch - 1) & (j < n_slc))
    return hit.astype(BF16)


def _overlap_t(n_ch, n_slc):
    j = lax.broadcasted_iota(jnp.int32, (LANES, n_ch), 0)
    n = lax.broadcasted_iota(jnp.int32, (LANES, n_ch), 1)
    hit = ((n * CMP_STRIDE <= j * SLC_LEN + SLC_LEN - 1) & (n * CMP_STRIDE + CMP_LEN - 1 >= j * SLC_LEN)
           & (n < n_ch - 1) & (j < n_slc))
    return hit.astype(BF16)


def _block_scores(imp, pos, n_slc, axis):
    blk = lax.broadcasted_iota(jnp.int32, imp.shape, axis)
    qblk = _div(pos, SLC_LEN)
    forced = (blk == 0) | (blk == qblk) | (blk == qblk - 1)
    valid = blk * SLC_LEN <= pos
    score = jnp.where(valid, jnp.where(forced, FORCE_SCORE, imp), -1.0)
    return jnp.where(blk < n_slc, score, -2.0)


def _count_beats_lanes(score, n_slc):
    blk = lax.broadcasted_iota(jnp.int32, score.shape, 1)
    cnt = jnp.zeros(score.shape, jnp.int32)
    for i in range(n_slc):
        si = score[:, i:i + 1]
        cnt = cnt + jnp.where((si > score) | ((si == score) & (blk > i)), 1, 0)
    return cnt


def _count_beats_rows(score, lo, hi, n_rows):
    out = []
    for v in range(n_rows // 8):
        s_v = score[8 * v:8 * v + 8, :]
        blk = 8 * v + lax.broadcasted_iota(jnp.int32, s_v.shape, 0)
        cnt = jnp.zeros(s_v.shape, jnp.int32)
        for i in range(lo, hi):
            si = score[i:i + 1, :]
            if 8 * v + 7 < i:
                beats = si > s_v
            elif 8 * v > i:
                beats = si >= s_v
            else:
                beats = (si > s_v) | ((si == s_v) & (blk > i))
            cnt = cnt + jnp.where(beats, 1, 0)
        out.append(cnt)
    return jnp.concatenate(out, axis=0)


def _nsa_kernel(q_ref, gate_ref, kc_ref, vc_ref, sk_ref, sv_ref, wk_ref, wv_ref, o_ref,
                q4_ref, q4s_ref, cnt_ref, m_ref, acc_ref, *, tq, tk, n_ch, n_slc, wlen):
    h = pl.program_id(1)
    q0 = pl.program_id(2) * tq
    g = GQA_GROUP
    n_sel = min(SLC_TOPN, n_slc)
    lane = lax.broadcasted_iota(jnp.int32, (tq, LANES), 1)
    lo = lane < HEAD_DIM
    pos = q0 + lax.broadcasted_iota(jnp.int32, (tq, 1), 0)
    for i in range(g):
        q4_ref[i * tq:(i + 1) * tq, :] = q_ref[:, i * LANES:(i + 1) * LANES]
    q4 = q4_ref[...]

    sc = _dot_nt(q4, kc_ref[...]).reshape(g, tq, n_ch)
    n_idx = lax.broadcasted_iota(jnp.int32, (tq, n_ch), 1)
    ok = ((n_idx * CMP_STRIDE + CMP_LEN - 1 <= pos) & (n_idx < n_ch - 1))[None]
    sc = jnp.where(ok, sc, NEG)
    e = jnp.where(ok, jnp.exp(sc - jnp.max(sc, axis=-1, keepdims=True)), 0.0)
    lc = jnp.sum(e, axis=-1, keepdims=True)
    a = (e * (1.0 / jnp.where(lc > 0.0, lc, 1.0))).reshape(g * tq, n_ch)
    a_hi = a.astype(BF16)
    a_lo = (a - a_hi.astype(F32)).astype(BF16)
    o_cmp = _dot(a_hi, vc_ref[...]).reshape(g, tq, LANES)
    ov_t = _overlap_t(n_ch, n_slc)
    imp4 = _dot_nt(ov_t, a_hi) + _dot_nt(ov_t, a_lo)
    imp_t = sum(imp4[:, i * tq:(i + 1) * tq] for i in range(g))

    score = _block_scores(imp_t, q0 + lax.broadcasted_iota(jnp.int32, (1, tq), 1), n_slc, 0)
    n_valid = _div(q0 + tq - 1, SLC_LEN) + 1
    n_rows = cnt_ref.shape[0]
    cnt_ref[...] = jnp.zeros(cnt_ref.shape, jnp.int32)
    for b0 in range(0, n_slc, 8):
        @pl.when((b0 < n_valid) & (n_valid > n_sel))
        def _():
            cnt_ref[...] += _count_beats_rows(score, b0, min(b0 + 8, n_slc), n_rows)
    sel_t = (cnt_ref[...] < n_sel) & (lax.broadcasted_iota(jnp.int32, (n_rows, tq), 0) < n_slc)
    bias_t = jnp.concatenate([jnp.where(sel_t, 0.0, NEG), jnp.full((LANES - n_rows, tq), NEG, F32)], axis=0)
    sel_bias = pltpu.roll(bias_t.T, HEAD_DIM, axis=1)
    for i in range(g):
        qi = q_ref[:, i * LANES:(i + 1) * LANES].astype(F32)
        q4s_ref[i * tq:(i + 1) * tq, :] = jnp.where(lo, qi, sel_bias).astype(BF16)

    m_ref[...] = jnp.full(m_ref.shape, NEG, F32)
    acc_ref[...] = jnp.zeros(acc_ref.shape, F32)

    pos4 = q0 + (lax.broadcasted_iota(jnp.int32, (g * tq, 1), 0) & (tq - 1))

    def scores(kt):
        return _dot_nt(q4s_ref[...], sk_ref[pl.ds(pl.multiple_of(kt * tk, tk), tk), :])

    def update(kt, s, masked):
        k0 = pl.multiple_of(kt * tk, tk)
        allow = (k0 + lax.broadcasted_iota(jnp.int32, (g * tq, tk), 1) <= pos4) if masked else None
        _flash_update(s, sv_ref[pl.ds(k0, tk), :], m_ref, acc_ref, 0, allow=allow)

    def body(kt, s):
        s_next = scores(kt + 1)
        update(kt, s, False)
        return s_next

    n_full = _div(q0, tk)
    update(n_full, lax.fori_loop(0, n_full, body, scores(0)), True)
    o_slc = _normalise(acc_ref[0]).reshape(g, tq, LANES)

    ws = pl.multiple_of(jnp.maximum(q0 + tq - wlen, 0), tq)
    dist = pos4 - (ws + lax.broadcasted_iota(jnp.int32, (g * tq, wlen), 1))
    okw = (dist >= 0) & (dist < WINDOW)
    sw = jnp.where(okw, _dot_nt(q4, wk_ref[pl.ds(ws, wlen), :]), NEG)
    pw = jnp.exp(sw - jnp.max(sw, axis=-1, keepdims=True))
    o_win = _normalise(_dot(pw.astype(BF16), wv_ref[pl.ds(ws, wlen), :])).reshape(g, tq, LANES)

    gates = gate_ref[...]
    vals = []
    for i in range(g):
        col0 = GATE0 + 3 * (g * h + i)
        gc = [jnp.sum(jnp.where(lane == col0 + c, gates, 0.0), axis=1, keepdims=True) for c in range(3)]
        vals.append(gc[0] * o_cmp[i] + gc[1] * o_slc[i] + gc[2] * o_win[i])
    for pr in range(g // 2):
        o_ref[:, pr * LANES:(pr + 1) * LANES] = _pair_up(vals[2 * pr], vals[2 * pr + 1])


def _nsa_attention(nq_p, small, kc_p, vc_p, nkv_p, tq, tk):
    b, t, _ = nq_p.shape
    n_ch = kc_p.shape[1]
    n_slc = -(-t // SLC_LEN)
    assert n_slc <= LANES - HEAD_DIM, "the block mask rides in the 64 spare query lanes"
    wlen = min(t, WINDOW + tq)
    g = GQA_GROUP
    cblk = pl.BlockSpec((None, n_ch, LANES), lambda i, h, qi: (i, 0, h))
    lane_blk = lambda c: pl.BlockSpec((None, t, LANES), lambda i, h, qi: (i, 0, c * H_KV + h))
    return pl.pallas_call(
        functools.partial(_nsa_kernel, tq=tq, tk=tk, n_ch=n_ch, n_slc=n_slc, wlen=wlen),
        grid=(b, H_KV, t // tq),
        in_specs=[pl.BlockSpec((None, tq, g * LANES), lambda i, h, qi: (i, qi, h)),
                  pl.BlockSpec((None, tq, LANES), lambda i, h, qi: (i, qi, 0)), cblk, cblk,
                  lane_blk(0), lane_blk(1), lane_blk(2), lane_blk(3)],
        out_specs=pl.BlockSpec((None, tq, g * HEAD_DIM), lambda i, h, qi: (i, qi, h)),
        out_shape=jax.ShapeDtypeStruct((b, t, W_NSA), F32),
        scratch_shapes=[pltpu.VMEM((g * tq, LANES), BF16), pltpu.VMEM((g * tq, LANES), BF16),
                        pltpu.VMEM((-(-n_slc // 8) * 8, tq), jnp.int32),
                        pltpu.VMEM((1, g * tq, 1), F32), pltpu.VMEM((1, g * tq, LANES), F32)],
        compiler_params=_params(("parallel", "parallel", "arbitrary")),
        name="nsa_prompt",
    )(nq_p, small, kc_p, vc_p, nkv_p, nkv_p, nkv_p, nkv_p)


def _page_copies(pt_ref, step, slot, n_pages, page, streams, group):
    out = []
    for u in range(group):
        for pg in range(n_pages):
            idx = pt_ref[(step * group + u) * n_pages + pg]
            for hbm, buf, sem, on_lanes in streams:
                dst = buf.at[slot, u, :, pl.ds(pg * page, page)] if on_lanes else buf.at[slot, u, pg]
                out.append(pltpu.make_async_copy(hbm.at[idx], dst, sem.at[slot]))
    return out


def _gather_pages(pt_ref, n_pages, page, streams, group=1):
    i = pl.program_id(0)
    slot = lax.rem(i, 2)

    @pl.when(i == 0)
    def _():
        for c in _page_copies(pt_ref, 0, 0, n_pages, page, streams, group):
            c.start()

    @pl.when(i + 1 < pl.num_programs(0))
    def _():
        for c in _page_copies(pt_ref, i + 1, 1 - slot, n_pages, page, streams, group):
            c.start()

    for c in _page_copies(pt_ref, i, slot, n_pages, page, streams, group):
        c.wait()
    return slot


def _head_scores(q_col, k_view, s_ref, head, rows0, n_tiles, tile):
    qb = jnp.broadcast_to(q_col, (HEAD_DIM, tile))
    for pg in range(n_tiles):
        kt = k_view[rows0:rows0 + HEAD_DIM, pg * tile:(pg + 1) * tile]
        s_ref[pg, head:head + 1, :] = jnp.sum(kt * qb, axis=0, keepdims=True)


def _head_values(p_tile, v_view, rows0, n_tiles, tile):
    acc = jnp.zeros((HEAD_DIM, tile), F32)
    for pg in range(n_tiles):
        vt = v_view[rows0:rows0 + HEAD_DIM, pg * tile:(pg + 1) * tile]
        acc = acc + vt * jnp.broadcast_to(p_tile(pg), (HEAD_DIM, tile))
    return jnp.sum(acc, axis=1, keepdims=True)


def _softmax_tiles(s, s_new):
    m = jnp.maximum(jnp.max(jnp.max(s, axis=0), axis=1, keepdims=True), s_new)
    p = jnp.exp(s - m[None])
    p_new = jnp.exp(s_new - m)
    l = jnp.sum(jnp.sum(p, axis=0), axis=1, keepdims=True) + p_new
    return p, p_new, 1.0 / l


def _softmax_rows(s, s_new, allow):
    s = jnp.where(allow, s, NEG)
    m = jnp.maximum(jnp.max(s, axis=1, keepdims=True), s_new)
    p = jnp.where(allow, jnp.exp(s - m), 0.0)
    p_new = jnp.exp(s_new - m)
    return p, p_new, 1.0 / (jnp.sum(p, axis=1, keepdims=True) + p_new)


def _col_dot(a_col, b_col, n_heads):
    return jnp.sum((a_col * b_col).reshape(n_heads, HEAD_DIM, 1), axis=1)


def _as_column(row):
    return jnp.broadcast_to(row, (LANES, row.shape[1])).T[:, 0:1]


def _as_row(col):
    return jnp.broadcast_to(col, (col.shape[0], LANES)).T[0:1, :]


def _fox_dec_kernel(pt_ref, q_ref, kvnew_ref, small_ref, kv_hbm, lf_hbm, o_ref,
                    kvbuf, lfbuf, s_ref, p_ref, sem_kv, sem_lf, *, n_pages, page):
    slot = _gather_pages(pt_ref, n_pages, page,
                         [(kv_hbm, kvbuf, sem_kv, True), (lf_hbm, lfbuf, sem_lf, False)])
    kv = kvbuf.at[slot, 0]
    q = _as_column(q_ref[...])
    kv_new = _as_column(kvnew_ref[...])
    lf_new = jnp.concatenate([_as_column(small_ref[...])[0:H_FOX]] * n_pages, axis=0)
    for h in range(H_FOX):
        _head_scores(q[h * HEAD_DIM:(h + 1) * HEAD_DIM], kv, s_ref, h, h * HEAD_DIM, n_pages, page)

    rows = n_pages * H_FOX
    lf = lfbuf[slot, 0].reshape(rows, page)
    r = lax.broadcasted_iota(jnp.int32, (page, page), 0)
    c = lax.broadcasted_iota(jnp.int32, (page, page), 1)
    later = (r > c).astype(BF16)
    hi, mid, lo = _split3(lf)
    within = _dot(hi, later) + _dot(mid, later) + _dot(lo, later)
    r = lax.broadcasted_iota(jnp.int32, (rows, rows), 0)
    c = lax.broadcasted_iota(jnp.int32, (rows, rows), 1)
    later_pages = ((c > r) & (((c - r) & (H_FOX - 1)) == 0)).astype(BF16)
    tot = jnp.broadcast_to(jnp.sum(lf, axis=1, keepdims=True), (rows, page))
    hi, mid, lo = _split3(tot)
    beyond = _dot(later_pages, hi) + _dot(later_pages, mid) + _dot(later_pages, lo)
    bias = (within + beyond + lf_new).reshape(n_pages, H_FOX, page)

    s_new = _col_dot(q, kv_new[0:W_FOX], H_FOX)
    p, p_new, inv_l = _softmax_tiles(s_ref[...] + bias, s_new)
    p_ref[...] = p
    outs = []
    for h in range(H_FOX):
        o = _head_values(lambda pg, h=h: p_ref[pg, h:h + 1, :], kv, W_FOX + h * HEAD_DIM, n_pages, page)
        v_new = kv_new[W_FOX + h * HEAD_DIM: W_FOX + (h + 1) * HEAD_DIM]
        outs.append((o + p_new[h:h + 1] * v_new) * inv_l[h:h + 1])
    o_ref[...] = _as_row(jnp.concatenate(outs, axis=0))


def _fox_decode(page_table, q_row, kv_row, small_row, cache_kvt, cache_lft):
    s, n_pages = page_table.shape
    rows, page = cache_kvt.shape[1], cache_kvt.shape[2]
    col = lambda n: pl.BlockSpec((None, 1, n), lambda i, pt: (i, 0, 0))
    anyspec = pl.BlockSpec(memory_space=pl.ANY)
    return pl.pallas_call(
        functools.partial(_fox_dec_kernel, n_pages=n_pages, page=page),
        grid_spec=pltpu.PrefetchScalarGridSpec(
            num_scalar_prefetch=1,
            grid=(s,),
            in_specs=[col(W_FOX), col(2 * W_FOX), col(LANES), anyspec, anyspec],
            out_specs=col(W_FOX),
            scratch_shapes=[pltpu.VMEM((2, 1, rows, n_pages * page), F32),
                            pltpu.VMEM((2, 1, n_pages, H_FOX, page), F32),
                            pltpu.VMEM((n_pages, H_FOX, page), F32), pltpu.VMEM((n_pages, H_FOX, page), F32),
                            pltpu.SemaphoreType.DMA((2,)), pltpu.SemaphoreType.DMA((2,))],
        ),
        out_shape=jax.ShapeDtypeStruct((s, 1, W_FOX), F32),
        compiler_params=_params(("arbitrary",)),
        name="fox_decode",
    )(page_table.reshape(-1), q_row, kv_row, small_row, cache_kvt, cache_lft)


def _nsa_dec_kernel(pt_ref, qrow_ref, nkv4_ref, wnew_ref, gate_ref, win_ref, cache_hbm,
                    wj_ref, pe_ref, wpe_ref, w2_ref, w2t_ref, o_ref,
                    xbuf, xk_buf, xv_buf, p_ref, pw_ref, sem, *, n_pages, page, group):
    slot = _gather_pages(pt_ref, n_pages, page, [(cache_hbm, xbuf, sem, True)], group)
    n_ch = n_pages * page // CMP_STRIDE

    r = lax.broadcasted_iota(jnp.int32, (page, page), 0)
    t = lax.broadcasted_iota(jnp.int32, (page, page), 1)
    per = page // CMP_STRIDE
    perm = (t == CMP_STRIDE * (r & (per - 1)) + _div(r, per)).astype(BF16)
    for u in range(group):
        for pg in range(n_pages):
            xt = _dot_nt(perm, xbuf[slot, u, 0:2 * W_KV, pg * page:(pg + 1) * page].astype(BF16))
            c0 = u * n_ch + pg * per
            for j in range(CMP_STRIDE):
                xk_buf[j, c0:c0 + per, :] = xt[j * per:(j + 1) * per, 0:W_KV]
                xv_buf[j, c0:c0 + per, :] = xt[j * per:(j + 1) * per, W_KV:2 * W_KV]
    hid = _compress_hidden(lambda jj: (xk_buf[jj], xv_buf[jj]), group * n_ch, wj_ref, pe_ref, wpe_ref).astype(BF16)
    for u in range(group):
        o_ref[u] = _nsa_dec_one(hid[u * n_ch:(u + 1) * n_ch], xbuf.at[slot, u], qrow_ref[u], nkv4_ref[u],
                                wnew_ref[u], gate_ref[u], win_ref.at[u], w2_ref, w2t_ref, p_ref.at[u], pw_ref.at[u],
                                n_pages, page)


def _nsa_dec_one(hid, x, q_row, nkv4_new, win_new, gates, wv, w2_ref, w2t_ref, p_ref, pw_ref, n_pages, page):
    past_len = n_pages * page
    n_ch = past_len // CMP_STRIDE
    n_slc = past_len // SLC_LEN + 1
    n_sel = min(SLC_TOPN, n_slc)
    win_buf = wv.shape[1]
    g = GQA_GROUP
    row = lax.broadcasted_iota(jnp.int32, (H_NSA, LANES), 0)
    lane = lax.broadcasted_iota(jnp.int32, (H_NSA, LANES), 1)
    v_new = _as_column(nkv4_new)[3 * W_KV:4 * W_KV]
    vw_new = _as_column(win_new)[W_KV:2 * W_KV]
    kc = _dot(hid, w2_ref[:, 0:W_KV]).astype(BF16)
    vct = _dot_nt(w2t_ref[W_KV:2 * W_KV, :], hid).astype(BF16)

    qbd = jnp.zeros((H_NSA, LANES), F32)
    for i in range(H_NSA):
        piece = q_row[:, (i // 2) * LANES:(i // 2 + 1) * LANES]
        if (i % 2) != (i // g):
            piece = pltpu.roll(piece, HEAD_DIM, axis=1)
        qbd = jnp.where(row == i, jnp.broadcast_to(piece, (H_NSA, LANES)), qbd)
    qbd = jnp.where(_div(lane, HEAD_DIM) == _div(row, g), qbd, 0.0).astype(BF16)
    n_idx = lax.broadcasted_iota(jnp.int32, (H_NSA, n_ch), 1)
    ok = (n_idx * CMP_STRIDE + CMP_LEN - 1 <= past_len) & (n_idx < n_ch - 1)
    sc = jnp.where(ok, _dot_nt(qbd, kc), NEG)
    e = jnp.where(ok, jnp.exp(sc - jnp.max(sc, axis=1, keepdims=True)), 0.0)
    lc = jnp.sum(e, axis=1, keepdims=True)
    a = e * (1.0 / jnp.where(lc > 0.0, lc, 1.0))
    a_hi = a.astype(BF16)
    a_lo = (a - a_hi.astype(F32)).astype(BF16)
    o_cmp_t = _dot_nt(vct, a_hi)
    ov = _overlap(n_ch, n_slc)
    imp8 = _dot(a_hi, ov) + _dot(a_lo, ov)
    imp = jnp.zeros((H_NSA, LANES), F32)
    for h in range(H_KV):
        tot = jnp.sum(imp8[h * g:(h + 1) * g], axis=0, keepdims=True)
        imp = jnp.where(_div(row, g) == h, jnp.broadcast_to(tot, (H_NSA, LANES)), imp)
    score = _block_scores(imp, jnp.full((H_NSA, 1), past_len, jnp.int32), n_slc, 1)
    sel = jnp.where((_count_beats_lanes(score, n_slc) < n_sel) & (lane < n_slc), 1.0, 0.0)

    per_page = page // SLC_LEN
    allow = []
    for pg in range(n_pages):
        m = jnp.zeros((H_NSA, page), F32)
        for b in range(per_page):
            blk = pg * per_page + b
            lanes_b = _div(lax.broadcasted_iota(jnp.int32, (H_NSA, page), 1), SLC_LEN) == b
            m = jnp.where(lanes_b, jnp.broadcast_to(sel[:, blk:blk + 1], (H_NSA, page)), m)
        allow.append(m > 0.5)
    allow = jnp.concatenate(allow, axis=1)
    qf = qbd.astype(F32)
    s_new = jnp.sum(qf * nkv4_new[:, 2 * W_KV:3 * W_KV], axis=1, keepdims=True)
    p, p_new, inv_l = _softmax_rows(_dot(qbd, x[2 * W_KV:3 * W_KV, :].astype(BF16)), s_new, allow)
    p_ref[...] = p

    slot_idx = lax.broadcasted_iota(jnp.int32, (H_NSA, win_buf), 1)
    sw_new = jnp.sum(qf * win_new[:, 0:W_KV], axis=1, keepdims=True)
    pw, pw_new, inv_lw = _softmax_rows(_dot(qbd, wv[0:W_KV, :].astype(BF16)), sw_new, (win_buf - slot_idx) < WINDOW)
    pw_ref[...] = pw

    wtile = min(win_buf, 4 * LANES)
    outs = []
    for i in range(H_NSA):
        h = i // g
        hs = slice(h * HEAD_DIM, (h + 1) * HEAD_DIM)
        o_slc = _head_values(lambda pg, i=i: p_ref[i:i + 1, pg * page:(pg + 1) * page], x,
                             3 * W_KV + h * HEAD_DIM, n_pages, page)
        o_slc = (o_slc + p_new[i:i + 1] * v_new[hs]) * inv_l[i:i + 1]
        o_win = _head_values(lambda wt, i=i: pw_ref[i:i + 1, wt * wtile:(wt + 1) * wtile], wv,
                             W_KV + h * HEAD_DIM, win_buf // wtile, wtile)
        o_win = (o_win + pw_new[i:i + 1] * vw_new[hs]) * inv_lw[i:i + 1]
        c0 = GATE0 + 3 * i
        outs.append(gates[:, c0:c0 + 1] * o_cmp_t[hs, i:i + 1] + gates[:, c0 + 1:c0 + 2] * o_slc
                    + gates[:, c0 + 2:c0 + 3] * o_win)
    return _as_row(jnp.concatenate(outs, axis=0))


def _nsa_decode(page_table, q_row, nkv4_row, win_row, small, win_t, cache_t, cw):
    s, n_pages = page_table.shape
    rows, page = cache_t.shape[1], cache_t.shape[2]
    past_len = n_pages * page
    n_ch = past_len // CMP_STRIDE
    win_buf = win_t.shape[2]
    group = next(g for g in (4, 2, 1) if s % g == 0)
    rowspec = lambda n: pl.BlockSpec((group, 1, n), lambda i, pt: (i, 0, 0))
    const = lambda shape: pl.BlockSpec(shape, lambda i, pt: (0,) * len(shape), pipeline_mode=pl.Buffered(1))
    return pl.pallas_call(
        functools.partial(_nsa_dec_kernel, n_pages=n_pages, page=page, group=group),
        grid_spec=pltpu.PrefetchScalarGridSpec(
            num_scalar_prefetch=1,
            grid=(s // group,),
            in_specs=[rowspec(W_NSA), rowspec(4 * W_KV), rowspec(2 * W_KV), rowspec(LANES),
                      pl.BlockSpec((group, 2 * W_KV, win_buf), lambda i, pt: (i, 0, 0)),
                      pl.BlockSpec(memory_space=pl.ANY)] + [const(w.shape) for w in cw],
            out_specs=rowspec(W_NSA),
            scratch_shapes=[pltpu.VMEM((2, group, rows, past_len), F32),
                            pltpu.VMEM((CMP_STRIDE, group * n_ch, W_KV), F32),
                            pltpu.VMEM((CMP_STRIDE, group * n_ch, W_KV), F32),
                            pltpu.VMEM((group, H_NSA, past_len), F32), pltpu.VMEM((group, H_NSA, win_buf), F32),
                            pltpu.SemaphoreType.DMA((2,))],
        ),
        out_shape=jax.ShapeDtypeStruct((s, 1, W_NSA), F32),
        compiler_params=_params(("arbitrary",)),
        name="nsa_decode",
    )(page_table.reshape(-1), q_row, nkv4_row, win_row, small, win_t, cache_t, *cw)


def _rot_cols(w):
    d, n = w.shape
    w = w.reshape(d, n // HEAD_DIM, 2, HEAD_DIM // 2)
    return jnp.stack([-w[:, :, 1], w[:, :, 0]], axis=2).reshape(d, n)


def _prep_projection(w_in, b_f):
    c = [0, W_FOX, 2 * W_FOX, 3 * W_FOX, 3 * W_FOX + H_FOX, 3 * W_FOX + H_FOX + W_NSA,
         3 * W_FOX + H_FOX + W_NSA + 6 * W_KV]
    fq, fk, fv, ff, nq, kv = (w_in[:, c[i]:c[i + 1]] for i in range(6))
    gt = w_in[:, c[6]:]
    ks = jnp.concatenate([kv[:, br * 2 * W_KV: br * 2 * W_KV + W_KV] for br in range(3)], axis=1)
    w_big = jnp.concatenate([fq, fk, fv, nq, kv, _rot_cols(nq), _rot_cols(ks)], axis=1).astype(BF16)
    d = w_in.shape[0]
    w_small = jnp.concatenate([ff, gt, jnp.zeros((d, LANES - N_SMALL), w_in.dtype)], axis=1).astype(BF16)
    b_small = jnp.concatenate([b_f.astype(F32), jnp.zeros((LANES - H_FOX,), F32)])[None, :]
    return w_big, w_small, b_small


def _prep_compress(wk1, wk2, pek, wv1, wv2, pev):
    ratio = CMP_LEN // CMP_STRIDE
    eye = jnp.eye(4, dtype=F32)

    def blocks(w):
        return w.reshape(ratio, CMP_STRIDE, HEAD_DIM, wk1.shape[1])

    per_head = lambda w: jnp.einsum("rjde,hk->jhdrke", blocks(w), jnp.eye(H_KV, dtype=F32)).reshape(
        CMP_STRIDE, W_KV, ratio * W_KV)
    wj = jnp.concatenate([per_head(wk1), per_head(wv1)], axis=1).astype(BF16)
    pe = jnp.concatenate([pek.reshape(-1), pev.reshape(-1)])
    pe = jnp.broadcast_to(pe[None, :], (8, pe.shape[0])).astype(BF16)
    zero = jnp.zeros_like(wk1)
    wpe = jnp.concatenate([jnp.concatenate([wk1, wk1, zero, zero], axis=1),
                           jnp.concatenate([zero, zero, wv1, wv1], axis=1)], axis=0).astype(BF16)
    w2 = jnp.einsum("gde,gh->gdhe", jnp.stack([wk2, wk2, wv2, wv2]), eye)
    w2_pad = jnp.concatenate([w2, jnp.zeros_like(w2)], axis=3)
    w2 = w2.reshape(4 * HEAD_DIM, 4 * HEAD_DIM).astype(BF16)
    w2_pad = w2_pad.reshape(4 * HEAD_DIM, 4 * LANES).astype(BF16)
    return (wj, pe, wpe), w2, w2_pad


def _rope_tables(pos):
    half = HEAD_DIM // 2
    inv = ROPE_THETA ** (-jnp.arange(half, dtype=F32) / half)
    ang = pos.astype(F32)[:, None] * inv[None, :]
    reps = LANES // half
    return jnp.tile(jnp.cos(ang), (1, reps)), jnp.tile(jnp.sin(ang), (1, reps))


def _row_tile(n, cap):
    t = min(n, cap)
    while n % t:
        t //= 2
    return t


def kernel(x_prompt, x_sample, cache_fox_kv, cache_fox_logf, cache_nsa_kv, state_nsa_win_kv, page_table,
           g_ffn1_pre, w_ffn1_gate, w_ffn1_up, w_ffn1_down, g_ffn1_post, g_mix_pre, w_in, b_fox_f,
           w_cmpk_1, w_cmpk_2, pe_cmpk, w_cmpv_1, w_cmpv_2, pe_cmpv, g_fox_out, g_nsa_out, w_out,
           g_mix_post, g_ffn2_pre, w_ffn2_gate, w_ffn2_up, w_ffn2_down, g_ffn2_post):
    depth = w_in.shape[0]
    b, t, d = x_prompt.shape
    s, dec_seq, _ = x_sample.shape
    assert dec_seq == 1, "the sample group decodes one token per sequence"
    page = cache_fox_kv.shape[2]
    n_pages = page_table.shape[1]
    past_len = n_pages * page
    assert t % LANES == 0 and page % SLC_LEN == 0
    page_table = page_table.astype(jnp.int32)

    tm_p = _row_tile(t, 512)
    cos_p, sin_p = _rope_tables(jnp.arange(t, dtype=jnp.int32))
    cos_s, sin_s = _rope_tables(jnp.full((s,), past_len, jnp.int32))
    row = lambda v: v.astype(F32)[None, :]
    to_rows = lambda c: jnp.transpose(c, (0, 2, 3, 4, 1)).reshape(c.shape[0], -1, c.shape[1])

    yp = x_prompt.reshape(b * t, d)
    ys = x_sample.reshape(s, d)
    outs = [[] for _ in range(8)]
    for l in range(depth):
        ffn1 = (row(g_ffn1_pre[l]), w_ffn1_gate[l].astype(BF16), w_ffn1_up[l].astype(BF16),
                w_ffn1_down[l].astype(BF16), row(g_ffn1_post[l]))
        ffn2 = (row(g_ffn2_pre[l]), w_ffn2_gate[l].astype(BF16), w_ffn2_up[l].astype(BF16),
                w_ffn2_down[l].astype(BF16), row(g_ffn2_post[l]))
        w_big, w_small, b_small = _prep_projection(w_in[l], b_fox_f[l])
        cw, w2, w2_pad = _prep_compress(w_cmpk_1[l], w_cmpk_2[l], pe_cmpk[l], w_cmpv_1[l], w_cmpv_2[l], pe_cmpv[l])
        merge = (row(g_fox_out[l]), row(g_nsa_out[l]), w_out[l].astype(BF16), row(g_mix_post[l]))

        hp = _half_ffn(yp, *ffn1, tm_p)
        small, fkv_t, nkv4_t, win_t, cmp, fq_p, fk_p, fv_p, nq_p, nkv_p = _project(
            hp, row(g_mix_pre[l]), w_big, w_small, b_small, cos_p, sin_p, tm_p, t // tm_p, True)
        small3 = small.reshape(b, t, LANES)
        cum = _cumsum(small3, _row_tile(t, 512))
        cum_t = jnp.swapaxes(cum[:, :, :H_FOX], 1, 2)
        per_head = lambda x: x.reshape(b, t, x.shape[1])
        o_fox = _fox_attention(per_head(fq_p), per_head(fk_p), per_head(fv_p), cum, cum_t,
                               _row_tile(t, 512), _row_tile(t, 512))
        kc_p, vc_p = _compress(cmp.reshape(b, t, 2 * W_KV), cw + (w2_pad,))
        o_nsa = _nsa_attention(per_head(nq_p), small3, kc_p, vc_p, per_head(nkv_p), LANES, _row_tile(t, 512))
        yp = _merge_ffn(hp, o_fox.reshape(b * t, W_FOX), o_nsa.reshape(b * t, W_NSA), *merge, *ffn2, tm_p)
        keep = min(WINDOW, t)
        tokens_first = lambda x, *dims: jnp.transpose(x.reshape(b, *dims, x.shape[2]), (0, 4, 1, 2, 3))
        outs[0].append(tokens_first(fkv_t, 2, H_FOX, HEAD_DIM))
        outs[1].append(small3[:, :, :H_FOX])
        outs[2].append(tokens_first(nkv4_t, 4, H_KV, HEAD_DIM))
        outs[3].append(tokens_first(win_t[:, :, t - keep:], 2, H_KV, HEAD_DIM))

        hs = _half_ffn(ys, *ffn1, s)
        small, fkv, nkv4, win, fq, nq = _project(
            hs, row(g_mix_pre[l]), w_big, w_small, b_small, cos_s, sin_s, s, 1, False)
        as_rows = lambda x: x.reshape(s, 1, x.shape[1])
        o_fox = _fox_decode(page_table, as_rows(fq), as_rows(fkv), as_rows(small),
                            to_rows(cache_fox_kv[l]), jnp.swapaxes(cache_fox_logf[l], 1, 2))
        o_nsa = _nsa_decode(page_table, as_rows(nq), as_rows(nkv4), as_rows(win), as_rows(small),
                            to_rows(state_nsa_win_kv[l]), to_rows(cache_nsa_kv[l]), cw + (w2, w2.T))
        ys = _merge_ffn(hs, o_fox.reshape(s, W_FOX), o_nsa.reshape(s, W_NSA), *merge, *ffn2, s)
        keep = min(WINDOW, past_len + 1)
        kw_all = jnp.concatenate([state_nsa_win_kv[l], win.reshape(s, 1, 2, H_KV, HEAD_DIM)], axis=1)
        outs[4].append(fkv.reshape(s, 1, 2, H_FOX, HEAD_DIM))
        outs[5].append(small[:, :H_FOX].reshape(s, 1, H_FOX))
        outs[6].append(nkv4.reshape(s, 1, 4, H_KV, HEAD_DIM))
        outs[7].append(kw_all[:, kw_all.shape[1] - keep:])

    stacked = [jnp.stack(o, axis=0) for o in outs]
    return (yp.reshape(b, t, d), ys.reshape(s, 1, d), *stacked)
```

```python
import functools

import jax
import jax.numpy as jnp
from jax import lax
from jax.experimental import pallas as pl
from jax.experimental.pallas import tpu as pltpu

HEAD_DIM = 64
H_FOX = 8
H_NSA = 8
H_KV = 2
GQA_GROUP = H_NSA // H_KV
W_FOX = H_FOX * HEAD_DIM
W_NSA = H_NSA * HEAD_DIM
W_KV = H_KV * HEAD_DIM
CMP_STRIDE = 16
CMP_LEN = 32
SLC_LEN = 64
SLC_TOPN = 16
WINDOW = 512
ROPE_THETA = 10000.0
EPS = 1e-6
NEG = -1e30
FORCE_SCORE = 1e4
N_SMALL = H_FOX + 3 * H_NSA
GATE0 = H_FOX

LANES = 128
MXU_N = 256
VMEM_LIMIT = 56 * 1024 * 1024
FOX_HEADS_PER_STEP = 2

F32 = jnp.float32
BF16 = jnp.bfloat16


def _dot(a, b):
    return jnp.dot(a, b, preferred_element_type=F32)


def _dot_nt(a, b):
    return lax.dot_general(a, b, (((1,), (1,)), ((), ())), preferred_element_type=F32)


def _div(x, n):
    assert n & (n - 1) == 0
    return lax.shift_right_logical(x, jnp.int32(n.bit_length() - 1))


def _split3(x):
    hi = x.astype(BF16)
    r1 = x - hi.astype(F32)
    mid = r1.astype(BF16)
    lo = (r1 - mid.astype(F32)).astype(BF16)
    return hi, mid, lo


def _rms(x, g):
    return x * lax.rsqrt(jnp.mean(x * x, axis=-1, keepdims=True) + EPS) * g


def _ff_chunks(d_ff):
    step = 6 * MXU_N
    return tuple((c, min(c + step, d_ff)) for c in range(0, d_ff, step))


def _ffn_core(x, gpre, wg_ref, wu_ref, wd_ref, gpost, chunks):
    xn = _rms(x, gpre).astype(BF16)
    acc = jnp.zeros(x.shape, F32)
    for c0, c1 in chunks:
        g = _dot(xn, wg_ref[:, c0:c1])
        u = _dot(xn, wu_ref[:, c0:c1])
        hm = (g * jax.nn.sigmoid(g) * u).astype(BF16)
        acc = acc + _dot(hm, wd_ref[c0:c1, :])
    return x + 0.5 * _rms(acc, gpost)


def _const_spec(shape):
    nd = len(shape)
    return pl.BlockSpec(shape, lambda *_: (0,) * nd, pipeline_mode=pl.Buffered(1))


def _params(sem):
    return pltpu.CompilerParams(dimension_semantics=sem, vmem_limit_bytes=VMEM_LIMIT)


def _lo_half(rows):
    return lax.broadcasted_iota(jnp.int32, (rows, LANES), 1) < HEAD_DIM


def _pad_heads(x, n_heads, fill):
    lo = _lo_half(x.shape[0])
    out = []
    for h in range(n_heads):
        piece = x[:, (h // 2) * LANES:(h // 2 + 1) * LANES]
        if h % 2:
            piece = pltpu.roll(piece, HEAD_DIM, axis=1)
        out.append(jnp.where(lo, piece, fill))
    return out


def _normalise(acc):
    den = jnp.where(_lo_half(acc.shape[0]), pltpu.roll(acc, HEAD_DIM, axis=1), 1.0)
    return acc * (1.0 / den)


def _pair_up(even, odd):
    return jnp.where(_lo_half(even.shape[0]), even, pltpu.roll(odd, HEAD_DIM, axis=1))


def _ffn_kernel(x_ref, gpre_ref, wg_ref, wu_ref, wd_ref, gpost_ref, o_ref, *, chunks):
    o_ref[...] = _ffn_core(x_ref[...], gpre_ref[...], wg_ref, wu_ref, wd_ref, gpost_ref[...], chunks)


def _half_ffn(x, gpre, wg, wu, wd, gpost, tm):
    n, d = x.shape
    d_ff = wg.shape[1]
    row = pl.BlockSpec((tm, d), lambda i: (i, 0))
    return pl.pallas_call(
        functools.partial(_ffn_kernel, chunks=_ff_chunks(d_ff)),
        grid=(n // tm,),
        in_specs=[row, _const_spec((1, d)), _const_spec((d, d_ff)), _const_spec((d, d_ff)),
                  _const_spec((d_ff, d)), _const_spec((1, d))],
        out_specs=row,
        out_shape=jax.ShapeDtypeStruct((n, d), F32),
        compiler_params=_params(("parallel",)),
        name="half_ffn",
    )(x, gpre, wg, wu, wd, gpost)


_C_FQ, _C_FK, _C_FV, _C_NQ, _C_KV, _C_NQR, _C_KR, _C_END = 0, 512, 1024, 1536, 2048, 2816, 3328, 3712


def _proj_kernel(h_ref, g_ref, wb_ref, ws_ref, bf_ref, cos_ref, sin_ref, small_ref, *rest, packed, n_pos_tiles):
    n = _rms(h_ref[...], g_ref[...]).astype(BF16)
    tm = n.shape[0]

    def mm(c0, c1):
        return _dot(n, wb_ref[:, c0:c1])

    def put(ref, tiles):
        for i, t in enumerate(tiles):
            ref[:, i * LANES:(i + 1) * LANES] = t.astype(BF16)

    def put_t(ref, tiles):
        for i, t in enumerate(tiles):
            for c in range(t.shape[1] // LANES):
                r0 = i * t.shape[1] + c * LANES
                ref[r0:r0 + LANES, :] = t[:, c * LANES:(c + 1) * LANES].T

    scale = HEAD_DIM ** -0.5
    lane = lax.broadcasted_iota(jnp.int32, (tm, LANES), 1)
    fq = mm(_C_FQ, _C_FK) * scale
    fk = mm(_C_FK, _C_FV)
    fv = mm(_C_FV, _C_NQ)

    cos = cos_ref[...]
    sin = sin_ref[...]
    nq = mm(_C_NQ, _C_KV)
    nqr = mm(_C_NQR, _C_KR)
    nq = jnp.concatenate([(nq[:, c * LANES:(c + 1) * LANES] * cos + nqr[:, c * LANES:(c + 1) * LANES] * sin) * scale
                          for c in range(W_NSA // LANES)], axis=1)

    kv = mm(_C_KV, _C_NQR)
    kr = mm(_C_KR, _C_END)
    ks, vs = [], []
    for br in range(3):
        k = kv[:, br * 2 * W_KV: br * 2 * W_KV + W_KV] * cos + kr[:, br * W_KV:(br + 1) * W_KV] * sin
        v = kv[:, br * 2 * W_KV + W_KV:(br + 1) * 2 * W_KV]
        ks.append(k)
        vs.append(v)

    sm = _dot(n, ws_ref[...]) + bf_ref[...]
    log_sig = jnp.minimum(sm, 0.0) - jnp.log(1.0 + jnp.exp(-jnp.abs(sm)))
    small_ref[...] = jnp.where(lane < H_FOX, log_sig, jax.nn.sigmoid(sm))

    if not packed:
        fkv_ref, nkv4_ref, win_ref, fq_ref, nq_ref = rest
        fkv_ref[...] = jnp.concatenate([fk, fv], axis=1)
        nkv4_ref[...] = jnp.concatenate([ks[0], vs[0], ks[1], vs[1]], axis=1)
        win_ref[...] = jnp.concatenate([ks[2], vs[2]], axis=1)
        fq_ref[...] = fq
        nq_ref[...] = nq
        return
    fkvt_ref, nkv4t_ref, wint_ref, cmp_ref, fqp_ref, fkp_ref, fvp_ref, nqp_ref, nkvp_ref = rest
    put_t(fkvt_ref, [fk, fv])
    put_t(nkv4t_ref, [ks[0], vs[0], ks[1], vs[1]])
    put_t(wint_ref, [ks[2], vs[2]])
    cmp_ref[...] = jnp.concatenate([ks[0], vs[0]], axis=1)
    ones3 = jnp.where((lane >= HEAD_DIM) & (lane < HEAD_DIM + 3), 1.0, 0.0)
    put(fqp_ref, _pad_heads(fq, H_FOX, 0.0))
    put(fkp_ref, _pad_heads(fk, H_FOX, ones3))
    put(fvp_ref, _pad_heads(fv, H_FOX, 1.0))
    put(nqp_ref, _pad_heads(nq, H_NSA, 0.0))
    pos = (lax.rem(pl.program_id(0), n_pos_tiles) * tm + lax.broadcasted_iota(jnp.int32, (tm, 1), 0))
    onehot = jnp.where(lane - HEAD_DIM == _div(pos, SLC_LEN), 1.0, 0.0)
    put(nkvp_ref, _pad_heads(ks[1], H_KV, onehot) + _pad_heads(vs[1], H_KV, 1.0)
        + _pad_heads(ks[2], H_KV, 0.0) + _pad_heads(vs[2], H_KV, 1.0))


def _project(h, g, w_big, w_small, b_small, cos, sin, tm, n_pos_tiles, packed):
    n, d = h.shape
    row = lambda w: pl.BlockSpec((tm, w), lambda i: (i, 0))
    pos = pl.BlockSpec((tm, LANES), lambda i: (i % n_pos_tiles, 0))
    if packed:
        outs = [(LANES, F32), (2 * W_KV, F32)] + [(H_FOX * LANES, BF16)] * 3 + [(H_NSA * LANES, BF16),
                                                                                (4 * H_KV * LANES, BF16)]
        t_rows = (2 * W_FOX, 4 * W_KV, 2 * W_KV)
    else:
        outs = [(LANES, F32), (2 * W_FOX, F32), (4 * W_KV, F32), (2 * W_KV, F32), (W_FOX, F32), (W_NSA, F32)]
        t_rows = ()
    out_specs = [row(w) for w, _ in outs]
    out_shape = [jax.ShapeDtypeStruct((n, w), dt) for w, dt in outs]
    batch = n // (tm * n_pos_tiles)
    for k, r in enumerate(t_rows):
        out_specs.insert(1 + k, pl.BlockSpec((None, r, tm), lambda i: (i // n_pos_tiles, 0, i % n_pos_tiles)))
        out_shape.insert(1 + k, jax.ShapeDtypeStruct((batch, r, tm * n_pos_tiles), F32))
    return pl.pallas_call(
        functools.partial(_proj_kernel, packed=packed, n_pos_tiles=n_pos_tiles),
        grid=(n // tm,),
        in_specs=[row(d), _const_spec((1, d)), _const_spec(w_big.shape), _const_spec(w_small.shape),
                  _const_spec((1, LANES)), pos, pos],
        out_specs=out_specs,
        out_shape=out_shape,
        compiler_params=_params(("parallel",)),
        name="project",
    )(h, g, w_big, w_small, b_small, cos, sin)


def _merge_ffn_kernel(h_ref, of_ref, on_ref, gf_ref, gn_ref, wo_ref, gmix_ref,
                      gpre_ref, wg_ref, wu_ref, wd_ref, gpost_ref, y_ref, *, chunks):
    of = _rms(of_ref[...], gf_ref[...]).astype(BF16)
    on = _rms(on_ref[...], gn_ref[...]).astype(BF16)
    mrg = _dot(of, wo_ref[0:W_FOX, :]) + _dot(on, wo_ref[W_FOX:W_FOX + W_NSA, :])
    h2 = h_ref[...] + _rms(mrg, gmix_ref[...])
    y_ref[...] = _ffn_core(h2, gpre_ref[...], wg_ref, wu_ref, wd_ref, gpost_ref[...], chunks)


def _merge_ffn(h, o_fox, o_nsa, gf, gn, w_out, gmix, gpre, wg, wu, wd, gpost, tm):
    n, d = h.shape
    d_ff = wg.shape[1]
    row = lambda w: pl.BlockSpec((tm, w), lambda i: (i, 0))
    return pl.pallas_call(
        functools.partial(_merge_ffn_kernel, chunks=_ff_chunks(d_ff)),
        grid=(n // tm,),
        in_specs=[row(d), row(W_FOX), row(W_NSA), _const_spec((1, W_FOX)), _const_spec((1, W_NSA)),
                  _const_spec(w_out.shape), _const_spec((1, d)), _const_spec((1, d)),
                  _const_spec((d, d_ff)), _const_spec((d, d_ff)), _const_spec((d_ff, d)), _const_spec((1, d))],
        out_specs=row(d),
        out_shape=jax.ShapeDtypeStruct((n, d), F32),
        compiler_params=_params(("parallel",)),
        name="merge_ffn",
    )(h, o_fox, o_nsa, gf, gn, w_out, gmix, gpre, wg, wu, wd, gpost)


def _forget_bias_kernel(x_ref, q_ref, k_ref, v_ref, q2_ref, k2_ref, v2_ref, carry_ref):
    @pl.when(pl.program_id(1) == 0)
    def _():
        carry_ref[...] = jnp.zeros(carry_ref.shape, F32)

    x = x_ref[...]
    tc = x.shape[0]
    r = lax.broadcasted_iota(jnp.int32, (tc, tc), 0)
    c = lax.broadcasted_iota(jnp.int32, (tc, tc), 1)
    tri = (c <= r).astype(BF16)
    hi, mid, lo = _split3(x)
    cs = _dot(tri, hi) + _dot(tri, mid) + _dot(tri, lo) + carry_ref[...]
    carry_ref[...] = cs[tc - 1:tc, :]

    lane = lax.broadcasted_iota(jnp.int32, (tc, LANES), 1)
    for h in range(H_FOX):
        sl = slice(h * LANES, (h + 1) * LANES)
        pos = [p.astype(F32) for p in _split3(cs[:, h:h + 1])]
        neg = [p.astype(F32) for p in _split3(-cs[:, h:h + 1])]
        q = q_ref[:, sl].astype(F32)
        k = k_ref[:, sl].astype(F32)
        for i in range(3):
            q = jnp.where(lane == HEAD_DIM + i, pos[i], q)
            k = jnp.where(lane == HEAD_DIM + 3 + i, neg[i], k)
        q = jnp.where((lane >= HEAD_DIM + 3) & (lane < HEAD_DIM + 6), 1.0, q)
        q2_ref[sl, :] = q.T.astype(BF16)
        k2_ref[:, sl] = k.astype(BF16)
        v2_ref[sl, :] = v_ref[:, sl].astype(F32).T.astype(BF16)


def _forget_bias(small, fq_p, fk_p, fv_p, tc):
    b, t, w = small.shape
    blk = lambda n: pl.BlockSpec((None, tc, n), lambda i, j: (i, j, 0))
    wide = H_FOX * LANES
    blk_t = pl.BlockSpec((None, wide, tc), lambda i, j: (i, 0, j))
    return pl.pallas_call(
        _forget_bias_kernel,
        grid=(b, t // tc),
        in_specs=[blk(w), blk(wide), blk(wide), blk(wide)],
        out_specs=[blk_t, blk(wide), blk_t],
        out_shape=[jax.ShapeDtypeStruct((b, wide, t), BF16), jax.ShapeDtypeStruct((b, t, wide), BF16),
                   jax.ShapeDtypeStruct((b, wide, t), BF16)],
        scratch_shapes=[pltpu.VMEM((1, w), F32)],
        compiler_params=_params(("parallel", "arbitrary")),
        name="forget_bias",
    )(small, fq_p, fk_p, fv_p)


def _scores(q, k):
    return _dot_nt(q, k)


def _flash_update(s, v, m_ref, acc_ref, idx, allow=None):
    if allow is not None:
        s = jnp.where(allow, s, NEG)
    m_old = m_ref[idx]
    m_new = jnp.maximum(m_old, jnp.max(s, axis=1, keepdims=True))
    p = jnp.exp(s - m_new)
    acc_ref[idx] = jnp.exp(m_old - m_new) * acc_ref[idx] + _dot(p.astype(BF16), v)
    m_ref[idx] = m_new


def _flash_update_t(s, vt, m_ref, acc_ref, idx, allow=None):
    if allow is not None:
        s = jnp.where(allow, s, NEG)
    m_old = m_ref[idx]
    m_new = jnp.maximum(m_old, jnp.max(s, axis=0, keepdims=True))
    p = jnp.exp(s - m_new)
    acc_ref[idx] = jnp.exp(m_old - m_new) * acc_ref[idx] + _dot(vt, p.astype(BF16))
    m_ref[idx] = m_new


def _normalise_t(acc):
    return acc[0:HEAD_DIM] * (1.0 / acc[HEAD_DIM:HEAD_DIM + 1])


def _fox_kernel(qt_ref, k_ref, vt_ref, o_ref, m_ref, acc_ref, *, tq, tk, heads):
    q0 = pl.program_id(2) * tq
    m_ref[...] = jnp.full(m_ref.shape, NEG, F32)
    acc_ref[...] = jnp.zeros(acc_ref.shape, F32)
    krow = lax.broadcasted_iota(jnp.int32, (tk, tq), 0)
    qpos = q0 + lax.broadcasted_iota(jnp.int32, (tk, tq), 1)
    head = lambda g: slice(g * LANES, (g + 1) * LANES)

    def scores(kt):
        k0 = pl.multiple_of(kt * tk, tk)
        return tuple(_dot(k_ref[pl.ds(k0, tk), head(g)], qt_ref[head(g), :]) for g in range(heads))

    def update(kt, s, masked):
        k0 = pl.multiple_of(kt * tk, tk)
        allow = (k0 + krow <= qpos) if masked else None
        for g in range(heads):
            _flash_update_t(s[g], vt_ref[head(g), pl.ds(k0, tk)], m_ref, acc_ref, g, allow=allow)

    def body(kt, s):
        s_next = scores(kt + 1)
        update(kt, s, False)
        return s_next

    n_full = _div(q0, tk)
    update(n_full, lax.fori_loop(0, n_full, body, scores(0)), True)
    for pr in range(heads // 2):
        pair = jnp.concatenate([_normalise_t(acc_ref[2 * pr]), _normalise_t(acc_ref[2 * pr + 1])], axis=0)
        o_ref[:, pr * LANES:(pr + 1) * LANES] = pair.T


def _fox_attention(fq_t, fk_p, fv_t, tq, tk):
    b, t, _ = fk_p.shape
    g = FOX_HEADS_PER_STEP
    qblk = pl.BlockSpec((None, g * LANES, tq), lambda i, hg, qi: (i, hg, qi))
    kblk = pl.BlockSpec((None, t, g * LANES), lambda i, hg, qi: (i, 0, hg))
    vblk = pl.BlockSpec((None, g * LANES, t), lambda i, hg, qi: (i, hg, 0))
    oblk = pl.BlockSpec((None, tq, g * HEAD_DIM), lambda i, hg, qi: (i, qi, hg))
    return pl.pallas_call(
        functools.partial(_fox_kernel, tq=tq, tk=tk, heads=g),
        grid=(b, H_FOX // g, t // tq),
        in_specs=[qblk, kblk, vblk],
        out_specs=oblk,
        out_shape=jax.ShapeDtypeStruct((b, t, W_FOX), F32),
        scratch_shapes=[pltpu.VMEM((g, 1, tq), F32), pltpu.VMEM((g, LANES, tq), F32)],
        compiler_params=_params(("parallel", "parallel", "arbitrary")),
        name="fox_prompt",
    )(fq_t, fk_p, fv_t)


def _compress_hidden(load_rows, n_ch, wj_ref, pe_ref, wpe_ref):
    acc_k = jnp.zeros((n_ch, 2 * W_KV), F32)
    acc_v = jnp.zeros((n_ch, 2 * W_KV), F32)
    for j in range(CMP_STRIDE):
        xk, xv = load_rows(j)
        acc_k = acc_k + _dot(xk.astype(BF16), wj_ref[j, 0:W_KV, :])
        acc_v = acc_v + _dot(xv.astype(BF16), wj_ref[j, W_KV:2 * W_KV, :])
    first = jnp.concatenate([acc_k[:, 0:W_KV], acc_v[:, 0:W_KV]], axis=1)
    second = jnp.concatenate([acc_k[:, W_KV:], acc_v[:, W_KV:]], axis=1)
    second = pltpu.roll(second, n_ch - 1, axis=0)
    pe_term = _dot(pe_ref[...], wpe_ref[...])[0:1, :]
    return jax.nn.gelu(first + second + pe_term)


def _compress_kernel(xk_ref, xv_ref, wj_ref, pe_ref, wpe_ref, w2p_ref, kc_ref, vc_ref, *, n_ch):
    rows = lambda j: (xk_ref[pl.ds(j, n_ch, stride=CMP_STRIDE), :], xv_ref[pl.ds(j, n_ch, stride=CMP_STRIDE), :])
    hid = _compress_hidden(rows, n_ch, wj_ref, pe_ref, wpe_ref).astype(BF16)
    out = _dot(hid, w2p_ref[...])
    kc_ref[...] = out[:, 0:H_KV * LANES].astype(BF16)
    for h in range(H_KV):
        vc = out[:, (H_KV + h) * LANES:(H_KV + h + 1) * LANES]
        vc_ref[:, h * LANES:(h + 1) * LANES] = jnp.where(_lo_half(n_ch), vc, 1.0).astype(BF16)


def _compress(nkv4, cw):
    b, t, _ = nkv4.shape
    n_ch = t // CMP_STRIDE
    out = pl.BlockSpec((None, n_ch, H_KV * LANES), lambda i: (i, 0, 0))
    return pl.pallas_call(
        functools.partial(_compress_kernel, n_ch=n_ch),
        grid=(b,),
        in_specs=[pl.BlockSpec((None, t, W_KV), lambda i: (i, 0, 0)),
                  pl.BlockSpec((None, t, W_KV), lambda i: (i, 0, 1))] + [_const_spec(w.shape) for w in cw],
        out_specs=[out, out],
        out_shape=[jax.ShapeDtypeStruct((b, n_ch, H_KV * LANES), BF16)] * 2,
        compiler_params=_params(("parallel",)),
        name="compress_prompt",
    )(nkv4, nkv4, *cw)


def _overlap(n_ch, n_slc):
    n = lax.broadcasted_iota(jnp.int32, (n_ch, LANES), 0)
    j = lax.broadcasted_iota(jnp.int32, (n_ch, LANES), 1)
    hit = ((n * CMP_STRIDE <= j * SLC_LEN + SLC_LEN - 1) & (n * CMP_STRIDE + CMP_LEN - 1 >= j * SLC_LEN)
           & (n < n_ch - 1) & (j < n_slc))
    return hit.astype(BF16)


def _overlap_t(n_ch, n_slc):
    j = lax.broadcasted_iota(jnp.int32, (LANES, n_ch), 0)
    n = lax.broadcasted_iota(jnp.int32, (LANES, n_ch), 1)
    hit = ((n * CMP_STRIDE <= j * SLC_LEN + SLC_LEN - 1) & (n * CMP_STRIDE + CMP_LEN - 1 >= j * SLC_LEN)
           & (n < n_ch - 1) & (j < n_slc))
    return hit.astype(BF16)


def _block_scores(imp, pos, n_slc, axis):
    blk = lax.broadcasted_iota(jnp.int32, imp.shape, axis)
    qblk = _div(pos, SLC_LEN)
    forced = (blk == 0) | (blk == qblk) | (blk == qblk - 1)
    valid = blk * SLC_LEN <= pos
    score = jnp.where(valid, jnp.where(forced, FORCE_SCORE, imp), -1.0)
    return jnp.where(blk < n_slc, score, -2.0)


def _count_beats_lanes(score, n_slc):
    blk = lax.broadcasted_iota(jnp.int32, score.shape, 1)
    cnt = jnp.zeros(score.shape, jnp.int32)
    for i in range(n_slc):
        si = score[:, i:i + 1]
        cnt = cnt + jnp.where((si > score) | ((si == score) & (blk > i)), 1, 0)
    return cnt


def _count_beats_rows(score, lo, hi, n_rows):
    out = []
    for v in range(n_rows // 8):
        s_v = score[8 * v:8 * v + 8, :]
        blk = 8 * v + lax.broadcasted_iota(jnp.int32, s_v.shape, 0)
        cnt = jnp.zeros(s_v.shape, jnp.int32)
        for i in range(lo, hi):
            si = score[i:i + 1, :]
            if 8 * v + 7 < i:
                beats = si > s_v
            elif 8 * v > i:
                beats = si >= s_v
            else:
                beats = (si > s_v) | ((si == s_v) & (blk > i))
            cnt = cnt + jnp.where(beats, 1, 0)
        out.append(cnt)
    return jnp.concatenate(out, axis=0)


def _nsa_kernel(q_ref, gate_ref, kc_ref, vc_ref, sk_ref, sv_ref, wk_ref, wv_ref, o_ref,
                q4_ref, q4s_ref, cnt_ref, m_ref, acc_ref, *, tq, tk, n_ch, n_slc, wlen):
    h = pl.program_id(1)
    q0 = pl.program_id(2) * tq
    g = GQA_GROUP
    n_sel = min(SLC_TOPN, n_slc)
    lane = lax.broadcasted_iota(jnp.int32, (tq, LANES), 1)
    lo = lane < HEAD_DIM
    pos = q0 + lax.broadcasted_iota(jnp.int32, (tq, 1), 0)
    for i in range(g):
        q4_ref[i * tq:(i + 1) * tq, :] = q_ref[:, i * LANES:(i + 1) * LANES]
    q4 = q4_ref[...]

    sc = _dot_nt(q4, kc_ref[...]).reshape(g, tq, n_ch)
    n_idx = lax.broadcasted_iota(jnp.int32, (tq, n_ch), 1)
    ok = ((n_idx * CMP_STRIDE + CMP_LEN - 1 <= pos) & (n_idx < n_ch - 1))[None]
    sc = jnp.where(ok, sc, NEG)
    e = jnp.where(ok, jnp.exp(sc - jnp.max(sc, axis=-1, keepdims=True)), 0.0)
    lc = jnp.sum(e, axis=-1, keepdims=True)
    a = (e * (1.0 / jnp.where(lc > 0.0, lc, 1.0))).reshape(g * tq, n_ch)
    a_hi = a.astype(BF16)
    a_lo = (a - a_hi.astype(F32)).astype(BF16)
    o_cmp = _dot(a_hi, vc_ref[...]).reshape(g, tq, LANES)
    ov_t = _overlap_t(n_ch, n_slc)
    imp4 = _dot_nt(ov_t, a_hi) + _dot_nt(ov_t, a_lo)
    imp_t = sum(imp4[:, i * tq:(i + 1) * tq] for i in range(g))

    score = _block_scores(imp_t, q0 + lax.broadcasted_iota(jnp.int32, (1, tq), 1), n_slc, 0)
    n_valid = _div(q0 + tq - 1, SLC_LEN) + 1
    n_rows = cnt_ref.shape[0]
    cnt_ref[...] = jnp.zeros(cnt_ref.shape, jnp.int32)
    for b0 in range(0, n_slc, 8):
        @pl.when((b0 < n_valid) & (n_valid > n_sel))
        def _():
            cnt_ref[...] += _count_beats_rows(score, b0, min(b0 + 8, n_slc), n_rows)
    sel_t = (cnt_ref[...] < n_sel) & (lax.broadcasted_iota(jnp.int32, (n_rows, tq), 0) < n_slc)
    bias_t = jnp.concatenate([jnp.where(sel_t, 0.0, NEG), jnp.full((LANES - n_rows, tq), NEG, F32)], axis=0)
    sel_bias = pltpu.roll(bias_t.T, HEAD_DIM, axis=1)
    for i in range(g):
        qi = q_ref[:, i * LANES:(i + 1) * LANES].astype(F32)
        q4s_ref[i * tq:(i + 1) * tq, :] = jnp.where(lo, qi, sel_bias).astype(BF16)

    m_ref[...] = jnp.full(m_ref.shape, NEG, F32)
    acc_ref[...] = jnp.zeros(acc_ref.shape, F32)

    pos4 = q0 + (lax.broadcasted_iota(jnp.int32, (g * tq, 1), 0) & (tq - 1))

    def scores(kt):
        return _scores(q4s_ref[...], sk_ref[pl.ds(pl.multiple_of(kt * tk, tk), tk), :])

    def update(kt, sm, masked):
        k0 = pl.multiple_of(kt * tk, tk)
        allow = (k0 + lax.broadcasted_iota(jnp.int32, (g * tq, tk), 1) <= pos4) if masked else None
        _flash_update(sm, sv_ref[pl.ds(k0, tk), :], m_ref, acc_ref, 0, allow=allow)

    def body(kt, sm):
        sm_next = scores(kt + 1)
        update(kt, sm, False)
        return sm_next

    n_full = _div(q0, tk)
    update(n_full, lax.fori_loop(0, n_full, body, scores(0)), True)
    o_slc = _normalise(acc_ref[0]).reshape(g, tq, LANES)

    ws = pl.multiple_of(jnp.maximum(q0 + tq - wlen, 0), tq)
    dist = pos4 - (ws + lax.broadcasted_iota(jnp.int32, (g * tq, wlen), 1))
    okw = (dist >= 0) & (dist < WINDOW)
    sw = jnp.where(okw, _dot_nt(q4, wk_ref[pl.ds(ws, wlen), :]), NEG)
    pw = jnp.exp(sw - jnp.max(sw, axis=-1, keepdims=True))
    o_win = _normalise(_dot(pw.astype(BF16), wv_ref[pl.ds(ws, wlen), :])).reshape(g, tq, LANES)

    gates = gate_ref[...]
    vals = []
    for i in range(g):
        col0 = GATE0 + 3 * (g * h + i)
        gc = [jnp.sum(jnp.where(lane == col0 + c, gates, 0.0), axis=1, keepdims=True) for c in range(3)]
        vals.append(gc[0] * o_cmp[i] + gc[1] * o_slc[i] + gc[2] * o_win[i])
    for pr in range(g // 2):
        o_ref[:, pr * LANES:(pr + 1) * LANES] = _pair_up(vals[2 * pr], vals[2 * pr + 1])


def _nsa_attention(nq_p, small, kc_p, vc_p, nkv_p, tq, tk):
    b, t, _ = nq_p.shape
    n_ch = kc_p.shape[1]
    n_slc = -(-t // SLC_LEN)
    assert n_slc <= LANES - HEAD_DIM, "the block mask rides in the 64 spare query lanes"
    wlen = min(t, WINDOW + tq)
    g = GQA_GROUP
    cblk = pl.BlockSpec((None, n_ch, LANES), lambda i, h, qi: (i, 0, h))
    lane_blk = lambda c: pl.BlockSpec((None, t, LANES), lambda i, h, qi: (i, 0, c * H_KV + h))
    return pl.pallas_call(
        functools.partial(_nsa_kernel, tq=tq, tk=tk, n_ch=n_ch, n_slc=n_slc, wlen=wlen),
        grid=(b, H_KV, t // tq),
        in_specs=[pl.BlockSpec((None, tq, g * LANES), lambda i, h, qi: (i, qi, h)),
                  pl.BlockSpec((None, tq, LANES), lambda i, h, qi: (i, qi, 0)), cblk, cblk,
                  lane_blk(0), lane_blk(1), lane_blk(2), lane_blk(3)],
        out_specs=pl.BlockSpec((None, tq, g * HEAD_DIM), lambda i, h, qi: (i, qi, h)),
        out_shape=jax.ShapeDtypeStruct((b, t, W_NSA), F32),
        scratch_shapes=[pltpu.VMEM((g * tq, LANES), BF16), pltpu.VMEM((g * tq, LANES), BF16),
                        pltpu.VMEM((-(-n_slc // 8) * 8, tq), jnp.int32),
                        pltpu.VMEM((1, g * tq, 1), F32), pltpu.VMEM((1, g * tq, LANES), F32)],
        compiler_params=_params(("parallel", "parallel", "arbitrary")),
        name="nsa_prompt",
    )(nq_p, small, kc_p, vc_p, nkv_p, nkv_p, nkv_p, nkv_p)


def _page_copies(pt_ref, step, slot, n_pages, page, streams, group):
    out = []
    for u in range(group):
        for pg in range(n_pages):
            idx = pt_ref[(step * group + u) * n_pages + pg]
            for hbm, buf, sem, on_lanes in streams:
                dst = buf.at[slot, u, :, pl.ds(pg * page, page)] if on_lanes else buf.at[slot, u, pg]
                out.append(pltpu.make_async_copy(hbm.at[idx], dst, sem.at[slot]))
    return out


def _gather_pages(pt_ref, n_pages, page, streams, group=1):
    i = pl.program_id(0)
    slot = lax.rem(i, 2)

    @pl.when(i == 0)
    def _():
        for c in _page_copies(pt_ref, 0, 0, n_pages, page, streams, group):
            c.start()

    @pl.when(i + 1 < pl.num_programs(0))
    def _():
        for c in _page_copies(pt_ref, i + 1, 1 - slot, n_pages, page, streams, group):
            c.start()

    for c in _page_copies(pt_ref, i, slot, n_pages, page, streams, group):
        c.wait()
    return slot


def _head_scores(q_col, k_view, s_ref, head, rows0, n_tiles, tile):
    qb = jnp.broadcast_to(q_col, (HEAD_DIM, tile))
    for pg in range(n_tiles):
        kt = k_view[rows0:rows0 + HEAD_DIM, pg * tile:(pg + 1) * tile]
        s_ref[pg, head:head + 1, :] = jnp.sum(kt * qb, axis=0, keepdims=True)


def _head_values(p_tile, v_view, rows0, n_tiles, tile):
    acc = jnp.zeros((HEAD_DIM, tile), F32)
    for pg in range(n_tiles):
        vt = v_view[rows0:rows0 + HEAD_DIM, pg * tile:(pg + 1) * tile]
        acc = acc + vt * jnp.broadcast_to(p_tile(pg), (HEAD_DIM, tile))
    return jnp.sum(acc, axis=1, keepdims=True)


def _softmax_tiles(s, s_new):
    m = jnp.maximum(jnp.max(jnp.max(s, axis=0), axis=1, keepdims=True), s_new)
    p = jnp.exp(s - m[None])
    p_new = jnp.exp(s_new - m)
    l = jnp.sum(jnp.sum(p, axis=0), axis=1, keepdims=True) + p_new
    return p, p_new, 1.0 / l


def _softmax_rows(s, s_new, allow):
    s = jnp.where(allow, s, NEG)
    m = jnp.maximum(jnp.max(s, axis=1, keepdims=True), s_new)
    p = jnp.where(allow, jnp.exp(s - m), 0.0)
    p_new = jnp.exp(s_new - m)
    return p, p_new, 1.0 / (jnp.sum(p, axis=1, keepdims=True) + p_new)


def _col_dot(a_col, b_col, n_heads):
    return jnp.sum((a_col * b_col).reshape(n_heads, HEAD_DIM, 1), axis=1)


def _as_column(row):
    return jnp.broadcast_to(row, (LANES, row.shape[1])).T[:, 0:1]


def _as_row(col):
    return jnp.broadcast_to(col, (col.shape[0], LANES)).T[0:1, :]


def _fox_dec_kernel(pt_ref, q_ref, kvnew_ref, small_ref, kv_hbm, lf_hbm, o_ref,
                    kvbuf, lfbuf, s_ref, p_ref, sem_kv, sem_lf, *, n_pages, page):
    slot = _gather_pages(pt_ref, n_pages, page,
                         [(kv_hbm, kvbuf, sem_kv, True), (lf_hbm, lfbuf, sem_lf, False)])
    kv = kvbuf.at[slot, 0]
    q = _as_column(q_ref[...])
    kv_new = _as_column(kvnew_ref[...])
    lf_new = jnp.concatenate([_as_column(small_ref[...])[0:H_FOX]] * n_pages, axis=0)
    for h in range(H_FOX):
        _head_scores(q[h * HEAD_DIM:(h + 1) * HEAD_DIM], kv, s_ref, h, h * HEAD_DIM, n_pages, page)

    rows = n_pages * H_FOX
    lf = lfbuf[slot, 0].reshape(rows, page)
    r = lax.broadcasted_iota(jnp.int32, (page, page), 0)
    c = lax.broadcasted_iota(jnp.int32, (page, page), 1)
    later = (r > c).astype(BF16)
    hi, mid, lo = _split3(lf)
    within = _dot(hi, later) + _dot(mid, later) + _dot(lo, later)
    r = lax.broadcasted_iota(jnp.int32, (rows, rows), 0)
    c = lax.broadcasted_iota(jnp.int32, (rows, rows), 1)
    later_pages = ((c > r) & (((c - r) & (H_FOX - 1)) == 0)).astype(BF16)
    tot = jnp.broadcast_to(jnp.sum(lf, axis=1, keepdims=True), (rows, page))
    hi, mid, lo = _split3(tot)
    beyond = _dot(later_pages, hi) + _dot(later_pages, mid) + _dot(later_pages, lo)
    bias = (within + beyond + lf_new).reshape(n_pages, H_FOX, page)

    s_new = _col_dot(q, kv_new[0:W_FOX], H_FOX)
    p, p_new, inv_l = _softmax_tiles(s_ref[...] + bias, s_new)
    p_ref[...] = p
    outs = []
    for h in range(H_FOX):
        o = _head_values(lambda pg, h=h: p_ref[pg, h:h + 1, :], kv, W_FOX + h * HEAD_DIM, n_pages, page)
        v_new = kv_new[W_FOX + h * HEAD_DIM: W_FOX + (h + 1) * HEAD_DIM]
        outs.append((o + p_new[h:h + 1] * v_new) * inv_l[h:h + 1])
    o_ref[...] = _as_row(jnp.concatenate(outs, axis=0))


def _fox_decode(page_table, q_row, kv_row, small_row, cache_kvt, cache_lft):
    s, n_pages = page_table.shape
    rows, page = cache_kvt.shape[1], cache_kvt.shape[2]
    col = lambda n: pl.BlockSpec((None, 1, n), lambda i, pt: (i, 0, 0))
    anyspec = pl.BlockSpec(memory_space=pl.ANY)
    return pl.pallas_call(
        functools.partial(_fox_dec_kernel, n_pages=n_pages, page=page),
        grid_spec=pltpu.PrefetchScalarGridSpec(
            num_scalar_prefetch=1,
            grid=(s,),
            in_specs=[col(W_FOX), col(2 * W_FOX), col(LANES), anyspec, anyspec],
            out_specs=col(W_FOX),
            scratch_shapes=[pltpu.VMEM((2, 1, rows, n_pages * page), F32),
                            pltpu.VMEM((2, 1, n_pages, H_FOX, page), F32),
                            pltpu.VMEM((n_pages, H_FOX, page), F32), pltpu.VMEM((n_pages, H_FOX, page), F32),
                            pltpu.SemaphoreType.DMA((2,)), pltpu.SemaphoreType.DMA((2,))],
        ),
        out_shape=jax.ShapeDtypeStruct((s, 1, W_FOX), F32),
        compiler_params=_params(("arbitrary",)),
        name="fox_decode",
    )(page_table.reshape(-1), q_row, kv_row, small_row, cache_kvt, cache_lft)


def _nsa_dec_kernel(pt_ref, qrow_ref, nkv4_ref, wnew_ref, gate_ref, win_ref, cache_hbm,
                    wj_ref, pe_ref, wpe_ref, w2_ref, w2t_ref, o_ref,
                    xbuf, xk_buf, xv_buf, p_ref, pw_ref, sem, *, n_pages, page, group):
    slot = _gather_pages(pt_ref, n_pages, page, [(cache_hbm, xbuf, sem, True)], group)
    n_ch = n_pages * page // CMP_STRIDE

    r = lax.broadcasted_iota(jnp.int32, (page, page), 0)
    t = lax.broadcasted_iota(jnp.int32, (page, page), 1)
    per = page // CMP_STRIDE
    perm = (t == CMP_STRIDE * (r & (per - 1)) + _div(r, per)).astype(BF16)
    for u in range(group):
        for pg in range(n_pages):
            xt = _dot_nt(perm, xbuf[slot, u, 0:2 * W_KV, pg * page:(pg + 1) * page].astype(BF16))
            c0 = u * n_ch + pg * per
            for j in range(CMP_STRIDE):
                xk_buf[j, c0:c0 + per, :] = xt[j * per:(j + 1) * per, 0:W_KV]
                xv_buf[j, c0:c0 + per, :] = xt[j * per:(j + 1) * per, W_KV:2 * W_KV]
    hid = _compress_hidden(lambda jj: (xk_buf[jj], xv_buf[jj]), group * n_ch, wj_ref, pe_ref, wpe_ref).astype(BF16)
    for u in range(group):
        o_ref[u] = _nsa_dec_one(hid[u * n_ch:(u + 1) * n_ch], xbuf.at[slot, u], qrow_ref[u], nkv4_ref[u],
                                wnew_ref[u], gate_ref[u], win_ref.at[u], w2_ref, w2t_ref, p_ref.at[u], pw_ref.at[u],
                                n_pages, page)


def _nsa_dec_one(hid, x, q_row, nkv4_new, win_new, gates, wv, w2_ref, w2t_ref, p_ref, pw_ref, n_pages, page):
    past_len = n_pages * page
    n_ch = past_len // CMP_STRIDE
    n_slc = past_len // SLC_LEN + 1
    n_sel = min(SLC_TOPN, n_slc)
    win_buf = wv.shape[1]
    g = GQA_GROUP
    row = lax.broadcasted_iota(jnp.int32, (H_NSA, LANES), 0)
    lane = lax.broadcasted_iota(jnp.int32, (H_NSA, LANES), 1)
    v_new = _as_column(nkv4_new)[3 * W_KV:4 * W_KV]
    vw_new = _as_column(win_new)[W_KV:2 * W_KV]
    kc = _dot(hid, w2_ref[:, 0:W_KV]).astype(BF16)
    vct = _dot_nt(w2t_ref[W_KV:2 * W_KV, :], hid).astype(BF16)

    qbd = jnp.zeros((H_NSA, LANES), F32)
    for i in range(H_NSA):
        piece = q_row[:, (i // 2) * LANES:(i // 2 + 1) * LANES]
        if (i % 2) != (i // g):
            piece = pltpu.roll(piece, HEAD_DIM, axis=1)
        qbd = jnp.where(row == i, jnp.broadcast_to(piece, (H_NSA, LANES)), qbd)
    qbd = jnp.where(_div(lane, HEAD_DIM) == _div(row, g), qbd, 0.0).astype(BF16)
    n_idx = lax.broadcasted_iota(jnp.int32, (H_NSA, n_ch), 1)
    ok = (n_idx * CMP_STRIDE + CMP_LEN - 1 <= past_len) & (n_idx < n_ch - 1)
    sc = jnp.where(ok, _dot_nt(qbd, kc), NEG)
    e = jnp.where(ok, jnp.exp(sc - jnp.max(sc, axis=1, keepdims=True)), 0.0)
    lc = jnp.sum(e, axis=1, keepdims=True)
    a = e * (1.0 / jnp.where(lc > 0.0, lc, 1.0))
    a_hi = a.astype(BF16)
    a_lo = (a - a_hi.astype(F32)).astype(BF16)
    o_cmp_t = _dot_nt(vct, a_hi)
    ov = _overlap(n_ch, n_slc)
    imp8 = _dot(a_hi, ov) + _dot(a_lo, ov)
    imp = jnp.zeros((H_NSA, LANES), F32)
    for h in range(H_KV):
        tot = jnp.sum(imp8[h * g:(h + 1) * g], axis=0, keepdims=True)
        imp = jnp.where(_div(row, g) == h, jnp.broadcast_to(tot, (H_NSA, LANES)), imp)
    score = _block_scores(imp, jnp.full((H_NSA, 1), past_len, jnp.int32), n_slc, 1)
    sel = jnp.where((_count_beats_lanes(score, n_slc) < n_sel) & (lane < n_slc), 1.0, 0.0)

    per_page = page // SLC_LEN
    allow = []
    for pg in range(n_pages):
        m = jnp.zeros((H_NSA, page), F32)
        for b in range(per_page):
            blk = pg * per_page + b
            lanes_b = _div(lax.broadcasted_iota(jnp.int32, (H_NSA, page), 1), SLC_LEN) == b
            m = jnp.where(lanes_b, jnp.broadcast_to(sel[:, blk:blk + 1], (H_NSA, page)), m)
        allow.append(m > 0.5)
    allow = jnp.concatenate(allow, axis=1)
    qf = qbd.astype(F32)
    s_new = jnp.sum(qf * nkv4_new[:, 2 * W_KV:3 * W_KV], axis=1, keepdims=True)
    p, p_new, inv_l = _softmax_rows(_dot(qbd, x[2 * W_KV:3 * W_KV, :].astype(BF16)), s_new, allow)
    p_ref[...] = p

    slot_idx = lax.broadcasted_iota(jnp.int32, (H_NSA, win_buf), 1)
    sw_new = jnp.sum(qf * win_new[:, 0:W_KV], axis=1, keepdims=True)
    pw, pw_new, inv_lw = _softmax_rows(_dot(qbd, wv[0:W_KV, :].astype(BF16)), sw_new, (win_buf - slot_idx) < WINDOW)
    pw_ref[...] = pw

    wtile = min(win_buf, 4 * LANES)
    outs = []
    for i in range(H_NSA):
        h = i // g
        hs = slice(h * HEAD_DIM, (h + 1) * HEAD_DIM)
        o_slc = _head_values(lambda pg, i=i: p_ref[i:i + 1, pg * page:(pg + 1) * page], x,
                             3 * W_KV + h * HEAD_DIM, n_pages, page)
        o_slc = (o_slc + p_new[i:i + 1] * v_new[hs]) * inv_l[i:i + 1]
        o_win = _head_values(lambda wt, i=i: pw_ref[i:i + 1, wt * wtile:(wt + 1) * wtile], wv,
                             W_KV + h * HEAD_DIM, win_buf // wtile, wtile)
        o_win = (o_win + pw_new[i:i + 1] * vw_new[hs]) * inv_lw[i:i + 1]
        c0 = GATE0 + 3 * i
        outs.append(gates[:, c0:c0 + 1] * o_cmp_t[hs, i:i + 1] + gates[:, c0 + 1:c0 + 2] * o_slc
                    + gates[:, c0 + 2:c0 + 3] * o_win)
    return _as_row(jnp.concatenate(outs, axis=0))


def _nsa_decode(page_table, q_row, nkv4_row, win_row, small, win_t, cache_t, cw):
    s, n_pages = page_table.shape
    rows, page = cache_t.shape[1], cache_t.shape[2]
    past_len = n_pages * page
    n_ch = past_len // CMP_STRIDE
    win_buf = win_t.shape[2]
    group = next(g for g in (4, 2, 1) if s % g == 0)
    rowspec = lambda n: pl.BlockSpec((group, 1, n), lambda i, pt: (i, 0, 0))
    const = lambda shape: pl.BlockSpec(shape, lambda i, pt: (0,) * len(shape), pipeline_mode=pl.Buffered(1))
    return pl.pallas_call(
        functools.partial(_nsa_dec_kernel, n_pages=n_pages, page=page, group=group),
        grid_spec=pltpu.PrefetchScalarGridSpec(
            num_scalar_prefetch=1,
            grid=(s // group,),
            in_specs=[rowspec(W_NSA), rowspec(4 * W_KV), rowspec(2 * W_KV), rowspec(LANES),
                      pl.BlockSpec((group, 2 * W_KV, win_buf), lambda i, pt: (i, 0, 0)),
                      pl.BlockSpec(memory_space=pl.ANY)] + [const(w.shape) for w in cw],
            out_specs=rowspec(W_NSA),
            scratch_shapes=[pltpu.VMEM((2, group, rows, past_len), F32),
                            pltpu.VMEM((CMP_STRIDE, group * n_ch, W_KV), F32),
                            pltpu.VMEM((CMP_STRIDE, group * n_ch, W_KV), F32),
                            pltpu.VMEM((group, H_NSA, past_len), F32), pltpu.VMEM((group, H_NSA, win_buf), F32),
                            pltpu.SemaphoreType.DMA((2,))],
        ),
        out_shape=jax.ShapeDtypeStruct((s, 1, W_NSA), F32),
        compiler_params=_params(("arbitrary",)),
        name="nsa_decode",
    )(page_table.reshape(-1), q_row, nkv4_row, win_row, small, win_t, cache_t, *cw)


def _rot_cols(w):
    d, n = w.shape
    w = w.reshape(d, n // HEAD_DIM, 2, HEAD_DIM // 2)
    return jnp.stack([-w[:, :, 1], w[:, :, 0]], axis=2).reshape(d, n)


def _prep_projection(w_in, b_f):
    c = [0, W_FOX, 2 * W_FOX, 3 * W_FOX, 3 * W_FOX + H_FOX, 3 * W_FOX + H_FOX + W_NSA,
         3 * W_FOX + H_FOX + W_NSA + 6 * W_KV]
    fq, fk, fv, ff, nq, kv = (w_in[:, c[i]:c[i + 1]] for i in range(6))
    gt = w_in[:, c[6]:]
    ks = jnp.concatenate([kv[:, br * 2 * W_KV: br * 2 * W_KV + W_KV] for br in range(3)], axis=1)
    w_big = jnp.concatenate([fq, fk, fv, nq, kv, _rot_cols(nq), _rot_cols(ks)], axis=1).astype(BF16)
    d = w_in.shape[0]
    w_small = jnp.concatenate([ff, gt, jnp.zeros((d, LANES - N_SMALL), w_in.dtype)], axis=1).astype(BF16)
    b_small = jnp.concatenate([b_f.astype(F32), jnp.zeros((LANES - H_FOX,), F32)])[None, :]
    return w_big, w_small, b_small


def _prep_compress(wk1, wk2, pek, wv1, wv2, pev):
    ratio = CMP_LEN // CMP_STRIDE
    eye = jnp.eye(4, dtype=F32)

    def blocks(w):
        return w.reshape(ratio, CMP_STRIDE, HEAD_DIM, wk1.shape[1])

    per_head = lambda w: jnp.einsum("rjde,hk->jhdrke", blocks(w), jnp.eye(H_KV, dtype=F32)).reshape(
        CMP_STRIDE, W_KV, ratio * W_KV)
    wj = jnp.concatenate([per_head(wk1), per_head(wv1)], axis=1).astype(BF16)
    pe = jnp.concatenate([pek.reshape(-1), pev.reshape(-1)])
    pe = jnp.broadcast_to(pe[None, :], (8, pe.shape[0])).astype(BF16)
    zero = jnp.zeros_like(wk1)
    wpe = jnp.concatenate([jnp.concatenate([wk1, wk1, zero, zero], axis=1),
                           jnp.concatenate([zero, zero, wv1, wv1], axis=1)], axis=0).astype(BF16)
    w2 = jnp.einsum("gde,gh->gdhe", jnp.stack([wk2, wk2, wv2, wv2]), eye)
    w2_pad = jnp.concatenate([w2, jnp.zeros_like(w2)], axis=3)
    w2 = w2.reshape(4 * HEAD_DIM, 4 * HEAD_DIM).astype(BF16)
    w2_pad = w2_pad.reshape(4 * HEAD_DIM, 4 * LANES).astype(BF16)
    return (wj, pe, wpe), w2, w2_pad


def _rope_tables(pos):
    half = HEAD_DIM // 2
    inv = ROPE_THETA ** (-jnp.arange(half, dtype=F32) / half)
    ang = pos.astype(F32)[:, None] * inv[None, :]
    reps = LANES // half
    return jnp.tile(jnp.cos(ang), (1, reps)), jnp.tile(jnp.sin(ang), (1, reps))


def _row_tile(n, cap):
    t = min(n, cap)
    while n % t:
        t //= 2
    return t


def kernel(x_prompt, x_sample, cache_fox_kv, cache_fox_logf, cache_nsa_kv, state_nsa_win_kv, page_table,
           g_ffn1_pre, w_ffn1_gate, w_ffn1_up, w_ffn1_down, g_ffn1_post, g_mix_pre, w_in, b_fox_f,
           w_cmpk_1, w_cmpk_2, pe_cmpk, w_cmpv_1, w_cmpv_2, pe_cmpv, g_fox_out, g_nsa_out, w_out,
           g_mix_post, g_ffn2_pre, w_ffn2_gate, w_ffn2_up, w_ffn2_down, g_ffn2_post):
    depth = w_in.shape[0]
    b, t, d = x_prompt.shape
    s, dec_seq, _ = x_sample.shape
    assert dec_seq == 1, "the sample group decodes one token per sequence"
    page = cache_fox_kv.shape[2]
    n_pages = page_table.shape[1]
    past_len = n_pages * page
    assert t % LANES == 0 and page % SLC_LEN == 0
    page_table = page_table.astype(jnp.int32)

    tm_p = _row_tile(t, 512)
    cos_p, sin_p = _rope_tables(jnp.arange(t, dtype=jnp.int32))
    cos_s, sin_s = _rope_tables(jnp.full((s,), past_len, jnp.int32))
    row = lambda v: v.astype(F32)[None, :]
    to_rows = lambda c: jnp.transpose(c, (0, 2, 3, 4, 1)).reshape(c.shape[0], -1, c.shape[1])

    yp = x_prompt.reshape(b * t, d)
    ys = x_sample.reshape(s, d)
    outs = [[] for _ in range(8)]
    for l in range(depth):
        ffn1 = (row(g_ffn1_pre[l]), w_ffn1_gate[l].astype(BF16), w_ffn1_up[l].astype(BF16),
                w_ffn1_down[l].astype(BF16), row(g_ffn1_post[l]))
        ffn2 = (row(g_ffn2_pre[l]), w_ffn2_gate[l].astype(BF16), w_ffn2_up[l].astype(BF16),
                w_ffn2_down[l].astype(BF16), row(g_ffn2_post[l]))
        w_big, w_small, b_small = _prep_projection(w_in[l], b_fox_f[l])
        cw, w2, w2_pad = _prep_compress(w_cmpk_1[l], w_cmpk_2[l], pe_cmpk[l], w_cmpv_1[l], w_cmpv_2[l], pe_cmpv[l])
        merge = (row(g_fox_out[l]), row(g_nsa_out[l]), w_out[l].astype(BF16), row(g_mix_post[l]))

        hp = _half_ffn(yp, *ffn1, tm_p)
        small, fkv_t, nkv4_t, win_t, cmp, fq_p, fk_p, fv_p, nq_p, nkv_p = _project(
            hp, row(g_mix_pre[l]), w_big, w_small, b_small, cos_p, sin_p, tm_p, t // tm_p, True)
        small3 = small.reshape(b, t, LANES)
        per_head = lambda x: x.reshape(b, t, x.shape[1])
        fq_t, fk_b, fv_t = _forget_bias(small3, per_head(fq_p), per_head(fk_p), per_head(fv_p), _row_tile(t, 512))
        o_fox = _fox_attention(fq_t, fk_b, fv_t, _row_tile(t, 512), _row_tile(t, 512))
        kc_p, vc_p = _compress(cmp.reshape(b, t, 2 * W_KV), cw + (w2_pad,))
        o_nsa = _nsa_attention(per_head(nq_p), small3, kc_p, vc_p, per_head(nkv_p), LANES, _row_tile(t, 512))
        yp = _merge_ffn(hp, o_fox.reshape(b * t, W_FOX), o_nsa.reshape(b * t, W_NSA), *merge, *ffn2, tm_p)
        keep = min(WINDOW, t)
        tokens_first = lambda x, *dims: jnp.transpose(x.reshape(b, *dims, x.shape[2]), (0, 4, 1, 2, 3))
        outs[0].append(tokens_first(fkv_t, 2, H_FOX, HEAD_DIM))
        outs[1].append(small3[:, :, :H_FOX])
        outs[2].append(tokens_first(nkv4_t, 4, H_KV, HEAD_DIM))
        outs[3].append(tokens_first(win_t[:, :, t - keep:], 2, H_KV, HEAD_DIM))

        hs = _half_ffn(ys, *ffn1, s)
        small, fkv, nkv4, win, fq, nq = _project(
            hs, row(g_mix_pre[l]), w_big, w_small, b_small, cos_s, sin_s, s, 1, False)
        as_rows = lambda x: x.reshape(s, 1, x.shape[1])
        o_fox = _fox_decode(page_table, as_rows(fq), as_rows(fkv), as_rows(small),
                            to_rows(cache_fox_kv[l]), jnp.swapaxes(cache_fox_logf[l], 1, 2))
        o_nsa = _nsa_decode(page_table, as_rows(nq), as_rows(nkv4), as_rows(win), as_rows(small),
                            to_rows(state_nsa_win_kv[l]), to_rows(cache_nsa_kv[l]), cw + (w2, w2.T))
        ys = _merge_ffn(hs, o_fox.reshape(s, W_FOX), o_nsa.reshape(s, W_NSA), *merge, *ffn2, s)
        keep = min(WINDOW, past_len + 1)
        kw_all = jnp.concatenate([state_nsa_win_kv[l], win.reshape(s, 1, 2, H_KV, HEAD_DIM)], axis=1)
        outs[4].append(fkv.reshape(s, 1, 2, H_FOX, HEAD_DIM))
        outs[5].append(small[:, :H_FOX].reshape(s, 1, H_FOX))
        outs[6].append(nkv4.reshape(s, 1, 4, H_KV, HEAD_DIM))
        outs[7].append(kw_all[:, kw_all.shape[1] - keep:])

    stacked = [jnp.stack(o, axis=0) for o in outs]
    return (yp.reshape(b, t, d), ys.reshape(s, 1, d), *stacked)
```

```python
import functools

import jax
import jax.numpy as jnp
from jax import lax
from jax.experimental import pallas as pl
from jax.experimental.pallas import tpu as pltpu

HEAD_DIM = 64
H_FOX = 8
H_NSA = 8
H_KV = 2
GQA_GROUP = H_NSA // H_KV
W_FOX = H_FOX * HEAD_DIM
W_NSA = H_NSA * HEAD_DIM
W_KV = H_KV * HEAD_DIM
CMP_STRIDE = 16
CMP_LEN = 32
SLC_LEN = 64
SLC_TOPN = 16
WINDOW = 512
ROPE_THETA = 10000.0
EPS = 1e-6
NEG = -1e30
FORCE_SCORE = 1e4
N_SMALL = H_FOX + 3 * H_NSA
GATE0 = H_FOX

LANES = 128
MXU_N = 256
VMEM_LIMIT = 56 * 1024 * 1024
FOX_HEADS_PER_STEP = 2

F32 = jnp.float32
BF16 = jnp.bfloat16


def _dot(a, b):
    return jnp.dot(a, b, preferred_element_type=F32)


def _dot_nt(a, b):
    return lax.dot_general(a, b, (((1,), (1,)), ((), ())), preferred_element_type=F32)


def _div(x, n):
    assert n & (n - 1) == 0
    return lax.shift_right_logical(x, jnp.int32(n.bit_length() - 1))


def _split3(x):
    hi = x.astype(BF16)
    r1 = x - hi.astype(F32)
    mid = r1.astype(BF16)
    lo = (r1 - mid.astype(F32)).astype(BF16)
    return hi, mid, lo


def _rms(x, g):
    return x * lax.rsqrt(jnp.mean(x * x, axis=-1, keepdims=True) + EPS) * g


def _ff_chunks(d_ff):
    step = 6 * MXU_N
    return tuple((c, min(c + step, d_ff)) for c in range(0, d_ff, step))


def _ffn_core(x, gpre, wg_ref, wu_ref, wd_ref, gpost, chunks):
    xn = _rms(x, gpre).astype(BF16)
    acc = jnp.zeros(x.shape, F32)
    for c0, c1 in chunks:
        g = _dot(xn, wg_ref[:, c0:c1])
        u = _dot(xn, wu_ref[:, c0:c1])
        hm = (g * jax.nn.sigmoid(g) * u).astype(BF16)
        acc = acc + _dot(hm, wd_ref[c0:c1, :])
    return x + 0.5 * _rms(acc, gpost)


def _const_spec(shape):
    nd = len(shape)
    return pl.BlockSpec(shape, lambda *_: (0,) * nd, pipeline_mode=pl.Buffered(1))


def _params(sem):
    return pltpu.CompilerParams(dimension_semantics=sem, vmem_limit_bytes=VMEM_LIMIT)


def _lo_half(rows):
    return lax.broadcasted_iota(jnp.int32, (rows, LANES), 1) < HEAD_DIM


def _pad_heads(x, n_heads, fill):
    lo = _lo_half(x.shape[0])
    out = []
    for h in range(n_heads):
        piece = x[:, (h // 2) * LANES:(h // 2 + 1) * LANES]
        if h % 2:
            piece = pltpu.roll(piece, HEAD_DIM, axis=1)
        out.append(jnp.where(lo, piece, fill))
    return out


def _normalise(acc):
    den = jnp.where(_lo_half(acc.shape[0]), pltpu.roll(acc, HEAD_DIM, axis=1), 1.0)
    return acc * (1.0 / den)


def _pair_up(even, odd):
    return jnp.where(_lo_half(even.shape[0]), even, pltpu.roll(odd, HEAD_DIM, axis=1))


def _ffn_kernel(x_ref, gpre_ref, wg_ref, wu_ref, wd_ref, gpost_ref, o_ref, *, chunks):
    o_ref[...] = _ffn_core(x_ref[...], gpre_ref[...], wg_ref, wu_ref, wd_ref, gpost_ref[...], chunks)


def _half_ffn(x, gpre, wg, wu, wd, gpost, tm):
    n, d = x.shape
    d_ff = wg.shape[1]
    row = pl.BlockSpec((tm, d), lambda i: (i, 0))
    return pl.pallas_call(
        functools.partial(_ffn_kernel, chunks=_ff_chunks(d_ff)),
        grid=(n // tm,),
        in_specs=[row, _const_spec((1, d)), _const_spec((d, d_ff)), _const_spec((d, d_ff)),
                  _const_spec((d_ff, d)), _const_spec((1, d))],
        out_specs=row,
        out_shape=jax.ShapeDtypeStruct((n, d), F32),
        compiler_params=_params(("parallel",)),
        name="half_ffn",
    )(x, gpre, wg, wu, wd, gpost)


_C_FQ, _C_FK, _C_FV, _C_NQ, _C_KV, _C_NQR, _C_KR, _C_END = 0, 512, 1024, 1536, 2048, 2816, 3328, 3712


def _proj_kernel(h_ref, g_ref, wb_ref, ws_ref, bf_ref, cos_ref, sin_ref, small_ref, *rest, packed, n_pos_tiles):
    n = _rms(h_ref[...], g_ref[...]).astype(BF16)
    tm = n.shape[0]

    def mm(c0, c1):
        return _dot(n, wb_ref[:, c0:c1])

    def put(ref, tiles):
        for i, t in enumerate(tiles):
            ref[:, i * LANES:(i + 1) * LANES] = t.astype(BF16)

    def put_t(ref, tiles):
        for i, t in enumerate(tiles):
            for c in range(t.shape[1] // LANES):
                r0 = i * t.shape[1] + c * LANES
                ref[r0:r0 + LANES, :] = t[:, c * LANES:(c + 1) * LANES].T

    scale = HEAD_DIM ** -0.5
    lane = lax.broadcasted_iota(jnp.int32, (tm, LANES), 1)
    fq = mm(_C_FQ, _C_FK) * scale
    fk = mm(_C_FK, _C_FV)
    fv = mm(_C_FV, _C_NQ)

    cos = cos_ref[...]
    sin = sin_ref[...]
    nq = mm(_C_NQ, _C_KV)
    nqr = mm(_C_NQR, _C_KR)
    nq = jnp.concatenate([(nq[:, c * LANES:(c + 1) * LANES] * cos + nqr[:, c * LANES:(c + 1) * LANES] * sin) * scale
                          for c in range(W_NSA // LANES)], axis=1)

    kv = mm(_C_KV, _C_NQR)
    kr = mm(_C_KR, _C_END)
    ks, vs = [], []
    for br in range(3):
        k = kv[:, br * 2 * W_KV: br * 2 * W_KV + W_KV] * cos + kr[:, br * W_KV:(br + 1) * W_KV] * sin
        v = kv[:, br * 2 * W_KV + W_KV:(br + 1) * 2 * W_KV]
        ks.append(k)
        vs.append(v)

    sm = _dot(n, ws_ref[...]) + bf_ref[...]
    log_sig = jnp.minimum(sm, 0.0) - jnp.log(1.0 + jnp.exp(-jnp.abs(sm)))
    small_ref[...] = jnp.where(lane < H_FOX, log_sig, jax.nn.sigmoid(sm))

    if not packed:
        fkv_ref, nkv4_ref, win_ref, fq_ref, nq_ref = rest
        fkv_ref[...] = jnp.concatenate([fk, fv], axis=1)
        nkv4_ref[...] = jnp.concatenate([ks[0], vs[0], ks[1], vs[1]], axis=1)
        win_ref[...] = jnp.concatenate([ks[2], vs[2]], axis=1)
        fq_ref[...] = fq
        nq_ref[...] = nq
        return
    fkvt_ref, nkv4t_ref, wint_ref, fqt_ref, fvt_ref, nqt_ref, nvt_ref, cmp_ref, fkp_ref, nkp_ref = rest
    put_t(fkvt_ref, [fk, fv])
    put_t(nkv4t_ref, [ks[0], vs[0], ks[1], vs[1]])
    put_t(wint_ref, [ks[2], vs[2]])
    cmp_ref[...] = jnp.concatenate([ks[0], vs[0]], axis=1)

    def put_heads_t(ref, tiles):
        for i, t in enumerate(tiles):
            ref[i * LANES:(i + 1) * LANES, :] = t.T.astype(BF16)

    put_heads_t(fqt_ref, _pad_heads(fq, H_FOX, 0.0))
    put_heads_t(fvt_ref, _pad_heads(fv, H_FOX, 1.0))
    put_heads_t(nqt_ref, _pad_heads(nq, H_NSA, 0.0))
    put_heads_t(nvt_ref, _pad_heads(vs[1], H_KV, 1.0) + _pad_heads(vs[2], H_KV, 1.0))
    ones3 = jnp.where((lane >= HEAD_DIM) & (lane < HEAD_DIM + 3), 1.0, 0.0)
    put(fkp_ref, _pad_heads(fk, H_FOX, ones3))
    pos = (lax.rem(pl.program_id(0), n_pos_tiles) * tm + lax.broadcasted_iota(jnp.int32, (tm, 1), 0))
    onehot = jnp.where(lane - HEAD_DIM == _div(pos, SLC_LEN), 1.0, 0.0)
    put(nkp_ref, _pad_heads(ks[1], H_KV, onehot) + _pad_heads(ks[2], H_KV, 0.0))


def _project(h, g, w_big, w_small, b_small, cos, sin, tm, n_pos_tiles, packed):
    n, d = h.shape
    row = lambda w: pl.BlockSpec((tm, w), lambda i: (i, 0))
    pos = pl.BlockSpec((tm, LANES), lambda i: (i % n_pos_tiles, 0))
    if packed:
        outs = [(LANES, F32), (2 * W_KV, F32), (H_FOX * LANES, BF16), (2 * H_KV * LANES, BF16)]
        t_outs = [(2 * W_FOX, F32), (4 * W_KV, F32), (2 * W_KV, F32), (H_FOX * LANES, BF16),
                  (H_FOX * LANES, BF16), (H_NSA * LANES, BF16), (2 * H_KV * LANES, BF16)]
    else:
        outs = [(LANES, F32), (2 * W_FOX, F32), (4 * W_KV, F32), (2 * W_KV, F32), (W_FOX, F32), (W_NSA, F32)]
        t_outs = []
    out_specs = [row(w) for w, _ in outs]
    out_shape = [jax.ShapeDtypeStruct((n, w), dt) for w, dt in outs]
    batch = n // (tm * n_pos_tiles)
    for k, (r, dt) in enumerate(t_outs):
        out_specs.insert(1 + k, pl.BlockSpec((None, r, tm), lambda i: (i // n_pos_tiles, 0, i % n_pos_tiles)))
        out_shape.insert(1 + k, jax.ShapeDtypeStruct((batch, r, tm * n_pos_tiles), dt))
    return pl.pallas_call(
        functools.partial(_proj_kernel, packed=packed, n_pos_tiles=n_pos_tiles),
        grid=(n // tm,),
        in_specs=[row(d), _const_spec((1, d)), _const_spec(w_big.shape), _const_spec(w_small.shape),
                  _const_spec((1, LANES)), pos, pos],
        out_specs=out_specs,
        out_shape=out_shape,
        compiler_params=_params(("parallel",)),
        name="project",
    )(h, g, w_big, w_small, b_small, cos, sin)


def _merge_ffn_kernel(h_ref, of_ref, on_ref, gf_ref, gn_ref, wo_ref, gmix_ref,
                      gpre_ref, wg_ref, wu_ref, wd_ref, gpost_ref, y_ref, *, chunks):
    of = _rms(of_ref[...], gf_ref[...]).astype(BF16)
    on = _rms(on_ref[...], gn_ref[...]).astype(BF16)
    mrg = _dot(of, wo_ref[0:W_FOX, :]) + _dot(on, wo_ref[W_FOX:W_FOX + W_NSA, :])
    h2 = h_ref[...] + _rms(mrg, gmix_ref[...])
    y_ref[...] = _ffn_core(h2, gpre_ref[...], wg_ref, wu_ref, wd_ref, gpost_ref[...], chunks)


def _merge_ffn(h, o_fox, o_nsa, gf, gn, w_out, gmix, gpre, wg, wu, wd, gpost, tm):
    n, d = h.shape
    d_ff = wg.shape[1]
    row = lambda w: pl.BlockSpec((tm, w), lambda i: (i, 0))
    return pl.pallas_call(
        functools.partial(_merge_ffn_kernel, chunks=_ff_chunks(d_ff)),
        grid=(n // tm,),
        in_specs=[row(d), row(W_FOX), row(W_NSA), _const_spec((1, W_FOX)), _const_spec((1, W_NSA)),
                  _const_spec(w_out.shape), _const_spec((1, d)), _const_spec((1, d)),
                  _const_spec((d, d_ff)), _const_spec((d, d_ff)), _const_spec((d_ff, d)), _const_spec((1, d))],
        out_specs=row(d),
        out_shape=jax.ShapeDtypeStruct((n, d), F32),
        compiler_params=_params(("parallel",)),
        name="merge_ffn",
    )(h, o_fox, o_nsa, gf, gn, w_out, gmix, gpre, wg, wu, wd, gpost)


def _forget_bias_kernel(x_ref, k_ref, cum_ref, k2_ref, carry_ref):
    @pl.when(pl.program_id(1) == 0)
    def _():
        carry_ref[...] = jnp.zeros(carry_ref.shape, F32)

    x = x_ref[...]
    tc = x.shape[0]
    r = lax.broadcasted_iota(jnp.int32, (tc, tc), 0)
    c = lax.broadcasted_iota(jnp.int32, (tc, tc), 1)
    tri = (c <= r).astype(BF16)
    hi, mid, lo = _split3(x)
    cs = _dot(tri, hi) + _dot(tri, mid) + _dot(tri, lo) + carry_ref[...]
    carry_ref[...] = cs[tc - 1:tc, :]
    cum_ref[...] = cs

    lane = lax.broadcasted_iota(jnp.int32, (tc, LANES), 1)
    for h in range(H_FOX):
        sl = slice(h * LANES, (h + 1) * LANES)
        neg = [p.astype(F32) for p in _split3(-cs[:, h:h + 1])]
        k = k_ref[:, sl].astype(F32)
        for i in range(3):
            k = jnp.where(lane == HEAD_DIM + 3 + i, neg[i], k)
        k2_ref[:, sl] = k.astype(BF16)


def _forget_bias(small, fk_p, tc):
    b, t, w = small.shape
    blk = lambda n: pl.BlockSpec((None, tc, n), lambda i, j: (i, j, 0))
    wide = H_FOX * LANES
    return pl.pallas_call(
        _forget_bias_kernel,
        grid=(b, t // tc),
        in_specs=[blk(w), blk(wide)],
        out_specs=[blk(w), blk(wide)],
        out_shape=[jax.ShapeDtypeStruct((b, t, w), F32), jax.ShapeDtypeStruct((b, t, wide), BF16)],
        scratch_shapes=[pltpu.VMEM((1, w), F32)],
        compiler_params=_params(("parallel", "arbitrary")),
        name="forget_bias",
    )(small, fk_p)


def _scores(q, k):
    return _dot_nt(q, k)


def _flash_update(s, v, m_ref, acc_ref, idx, allow=None):
    if allow is not None:
        s = jnp.where(allow, s, NEG)
    m_old = m_ref[idx]
    m_new = jnp.maximum(m_old, jnp.max(s, axis=1, keepdims=True))
    p = jnp.exp(s - m_new)
    acc_ref[idx] = jnp.exp(m_old - m_new) * acc_ref[idx] + _dot(p.astype(BF16), v)
    m_ref[idx] = m_new


def _flash_update_t(s, vt, m_ref, acc_ref, idx, allow=None):
    if allow is not None:
        s = jnp.where(allow, s, NEG)
    m_old = m_ref[idx]
    m_new = jnp.maximum(m_old, jnp.max(s, axis=0, keepdims=True))
    p = jnp.exp(s - m_new)
    acc_ref[idx] = jnp.exp(m_old - m_new) * acc_ref[idx] + _dot(vt, p.astype(BF16))
    m_ref[idx] = m_new


def _normalise_t(acc):
    return acc[0:HEAD_DIM] * (1.0 / acc[HEAD_DIM:HEAD_DIM + 1])


def _fox_kernel(qt_ref, cum_ref, k_ref, vt_ref, o_ref, qb_ref, m_ref, acc_ref, *, tq, tk, heads):
    hg = pl.program_id(1)
    q0 = pl.program_id(2) * tq
    m_ref[...] = jnp.full(m_ref.shape, NEG, F32)
    acc_ref[...] = jnp.zeros(acc_ref.shape, F32)
    krow = lax.broadcasted_iota(jnp.int32, (tk, tq), 0)
    qpos = q0 + lax.broadcasted_iota(jnp.int32, (tk, tq), 1)
    head = lambda g: slice(g * LANES, (g + 1) * LANES)
    row = lax.broadcasted_iota(jnp.int32, (LANES, tq), 0)
    row8 = lax.broadcasted_iota(jnp.int32, (H_FOX, tq), 0)
    cum_all = cum_ref[...]
    for g in range(heads):
        cum = jnp.sum(jnp.where(row8 == hg * heads + g, cum_all, 0.0), axis=0, keepdims=True)
        q = qt_ref[head(g), :].astype(F32)
        for i, piece in enumerate(_split3(cum)):
            q = jnp.where(row == HEAD_DIM + i, piece.astype(F32), q)
        qb_ref[g] = jnp.where((row >= HEAD_DIM + 3) & (row < HEAD_DIM + 6), 1.0, q).astype(BF16)

    def scores(kt):
        k0 = pl.multiple_of(kt * tk, tk)
        return tuple(_dot(k_ref[pl.ds(k0, tk), head(g)], qb_ref[g]) for g in range(heads))

    def update(kt, s, masked):
        k0 = pl.multiple_of(kt * tk, tk)
        allow = (k0 + krow <= qpos) if masked else None
        for g in range(heads):
            _flash_update_t(s[g], vt_ref[head(g), pl.ds(k0, tk)], m_ref, acc_ref, g, allow=allow)

    def body(kt, s):
        s_next = scores(kt + 1)
        update(kt, s, False)
        return s_next

    n_full = _div(q0, tk)
    update(n_full, lax.fori_loop(0, n_full, body, scores(0)), True)
    for pr in range(heads // 2):
        pair = jnp.concatenate([_normalise_t(acc_ref[2 * pr]), _normalise_t(acc_ref[2 * pr + 1])], axis=0)
        o_ref[:, pr * LANES:(pr + 1) * LANES] = pair.T


def _fox_attention(fq_t, cum_t, fk_p, fv_t, tq, tk):
    b, t, _ = fk_p.shape
    g = FOX_HEADS_PER_STEP
    qblk = pl.BlockSpec((None, g * LANES, tq), lambda i, hg, qi: (i, hg, qi))
    cblk = pl.BlockSpec((None, H_FOX, tq), lambda i, hg, qi: (i, 0, qi))
    kblk = pl.BlockSpec((None, t, g * LANES), lambda i, hg, qi: (i, 0, hg))
    vblk = pl.BlockSpec((None, g * LANES, t), lambda i, hg, qi: (i, hg, 0))
    oblk = pl.BlockSpec((None, tq, g * HEAD_DIM), lambda i, hg, qi: (i, qi, hg))
    return pl.pallas_call(
        functools.partial(_fox_kernel, tq=tq, tk=tk, heads=g),
        grid=(b, H_FOX // g, t // tq),
        in_specs=[qblk, cblk, kblk, vblk],
        out_specs=oblk,
        out_shape=jax.ShapeDtypeStruct((b, t, W_FOX), F32),
        scratch_shapes=[pltpu.VMEM((g, LANES, tq), BF16), pltpu.VMEM((g, 1, tq), F32),
                        pltpu.VMEM((g, LANES, tq), F32)],
        compiler_params=_params(("parallel", "parallel", "arbitrary")),
        name="fox_prompt",
    )(fq_t, cum_t, fk_p, fv_t)


def _compress_hidden(load_rows, n_ch, wj_ref, pe_ref, wpe_ref):
    acc_k = jnp.zeros((n_ch, 2 * W_KV), F32)
    acc_v = jnp.zeros((n_ch, 2 * W_KV), F32)
    for j in range(CMP_STRIDE):
        xk, xv = load_rows(j)
        acc_k = acc_k + _dot(xk.astype(BF16), wj_ref[j, 0:W_KV, :])
        acc_v = acc_v + _dot(xv.astype(BF16), wj_ref[j, W_KV:2 * W_KV, :])
    first = jnp.concatenate([acc_k[:, 0:W_KV], acc_v[:, 0:W_KV]], axis=1)
    second = jnp.concatenate([acc_k[:, W_KV:], acc_v[:, W_KV:]], axis=1)
    second = pltpu.roll(second, n_ch - 1, axis=0)
    pe_term = _dot(pe_ref[...], wpe_ref[...])[0:1, :]
    return jax.nn.gelu(first + second + pe_term)


def _compress_kernel(xk_ref, xv_ref, wj_ref, pe_ref, wpe_ref, w2p_ref, kc_ref, vc_ref, *, n_ch):
    rows = lambda j: (xk_ref[pl.ds(j, n_ch, stride=CMP_STRIDE), :], xv_ref[pl.ds(j, n_ch, stride=CMP_STRIDE), :])
    hid = _compress_hidden(rows, n_ch, wj_ref, pe_ref, wpe_ref).astype(BF16)
    out = _dot(hid, w2p_ref[...])
    kc_ref[...] = out[:, 0:H_KV * LANES].astype(BF16)
    for h in range(H_KV):
        vc = out[:, (H_KV + h) * LANES:(H_KV + h + 1) * LANES]
        vc_ref[h * LANES:(h + 1) * LANES, :] = jnp.where(_lo_half(n_ch), vc, 1.0).T.astype(BF16)


def _compress(nkv4, cw):
    b, t, _ = nkv4.shape
    n_ch = t // CMP_STRIDE
    out = pl.BlockSpec((None, n_ch, H_KV * LANES), lambda i: (i, 0, 0))
    return pl.pallas_call(
        functools.partial(_compress_kernel, n_ch=n_ch),
        grid=(b,),
        in_specs=[pl.BlockSpec((None, t, W_KV), lambda i: (i, 0, 0)),
                  pl.BlockSpec((None, t, W_KV), lambda i: (i, 0, 1))] + [_const_spec(w.shape) for w in cw],
        out_specs=[out, pl.BlockSpec((None, H_KV * LANES, n_ch), lambda i: (i, 0, 0))],
        out_shape=[jax.ShapeDtypeStruct((b, n_ch, H_KV * LANES), BF16),
                   jax.ShapeDtypeStruct((b, H_KV * LANES, n_ch), BF16)],
        compiler_params=_params(("parallel",)),
        name="compress_prompt",
    )(nkv4, nkv4, *cw)


def _overlap(n_ch, n_slc):
    n = lax.broadcasted_iota(jnp.int32, (n_ch, LANES), 0)
    j = lax.broadcasted_iota(jnp.int32, (n_ch, LANES), 1)
    hit = ((n * CMP_STRIDE <= j * SLC_LEN + SLC_LEN - 1) & (n * CMP_STRIDE + CMP_LEN - 1 >= j * SLC_LEN)
           & (n < n_ch - 1) & (j < n_slc))
    return hit.astype(BF16)


def _overlap_t(n_ch, n_slc):
    j = lax.broadcasted_iota(jnp.int32, (LANES, n_ch), 0)
    n = lax.broadcasted_iota(jnp.int32, (LANES, n_ch), 1)
    hit = ((n * CMP_STRIDE <= j * SLC_LEN + SLC_LEN - 1) & (n * CMP_STRIDE + CMP_LEN - 1 >= j * SLC_LEN)
           & (n < n_ch - 1) & (j < n_slc))
    return hit.astype(BF16)


def _block_scores(imp, pos, n_slc, axis):
    blk = lax.broadcasted_iota(jnp.int32, imp.shape, axis)
    qblk = _div(pos, SLC_LEN)
    forced = (blk == 0) | (blk == qblk) | (blk == qblk - 1)
    valid = blk * SLC_LEN <= pos
    score = jnp.where(valid, jnp.where(forced, FORCE_SCORE, imp), -1.0)
    return jnp.where(blk < n_slc, score, -2.0)


def _count_beats_lanes(score, n_slc):
    blk = lax.broadcasted_iota(jnp.int32, score.shape, 1)
    cnt = jnp.zeros(score.shape, jnp.int32)
    for i in range(n_slc):
        si = score[:, i:i + 1]
        cnt = cnt + jnp.where((si > score) | ((si == score) & (blk > i)), 1, 0)
    return cnt


def _count_beats_rows(score, lo, hi, n_rows):
    out = []
    for v in range(n_rows // 8):
        s_v = score[8 * v:8 * v + 8, :]
        blk = 8 * v + lax.broadcasted_iota(jnp.int32, s_v.shape, 0)
        cnt = jnp.zeros(s_v.shape, jnp.int32)
        for i in range(lo, hi):
            si = score[i:i + 1, :]
            if 8 * v + 7 < i:
                beats = si > s_v
            elif 8 * v > i:
                beats = si >= s_v
            else:
                beats = (si > s_v) | ((si == s_v) & (blk > i))
            cnt = cnt + jnp.where(beats, 1, 0)
        out.append(cnt)
    return jnp.concatenate(out, axis=0)


def _nsa_t_kernel(qt_ref, gate_ref, kc_ref, vct_ref, sk_ref, svt_ref, wk_ref, wvt_ref, o_ref,
                  q4_ref, q4s_ref, cnt_ref, m_ref, acc_ref, *, tq, tk, n_ch, n_slc, wlen):
    h = pl.program_id(1)
    q0 = pl.program_id(2) * tq
    g = GQA_GROUP
    w = g * tq
    n_sel = min(SLC_TOPN, n_slc)
    pos4 = q0 + (lax.broadcasted_iota(jnp.int32, (1, w), 1) & (tq - 1))
    for i in range(g):
        q4_ref[:, i * tq:(i + 1) * tq] = qt_ref[i * LANES:(i + 1) * LANES, :]
    q4 = q4_ref[...]

    n_idx = lax.broadcasted_iota(jnp.int32, (n_ch, w), 0)
    ok = (n_idx * CMP_STRIDE + CMP_LEN - 1 <= pos4) & (n_idx < n_ch - 1)
    sc = jnp.where(ok, _dot(kc_ref[...], q4), NEG)
    e = jnp.where(ok, jnp.exp(sc - jnp.max(sc, axis=0, keepdims=True)), 0.0)
    lc = jnp.sum(e, axis=0, keepdims=True)
    a = e * (1.0 / jnp.where(lc > 0.0, lc, 1.0))
    a_hi = a.astype(BF16)
    a_lo = (a - a_hi.astype(F32)).astype(BF16)
    o_cmp = _dot(vct_ref[...], a_hi)
    ov_t = _overlap_t(n_ch, n_slc)
    imp4 = _dot(ov_t, a_hi) + _dot(ov_t, a_lo)
    imp_t = sum(imp4[:, i * tq:(i + 1) * tq] for i in range(g))

    score = _block_scores(imp_t, q0 + lax.broadcasted_iota(jnp.int32, (1, tq), 1), n_slc, 0)
    n_valid = _div(q0 + tq - 1, SLC_LEN) + 1
    n_rows = cnt_ref.shape[0]
    cnt_ref[...] = jnp.zeros(cnt_ref.shape, jnp.int32)
    for b0 in range(0, n_slc, 8):
        @pl.when((b0 < n_valid) & (n_valid > n_sel))
        def _():
            cnt_ref[...] += _count_beats_rows(score, b0, min(b0 + 8, n_slc), n_rows)
    sel_t = (cnt_ref[...] < n_sel) & (lax.broadcasted_iota(jnp.int32, (n_rows, tq), 0) < n_slc)
    bias = jnp.where(sel_t, 0.0, NEG)
    if n_rows < HEAD_DIM:
        bias = jnp.concatenate([bias, jnp.full((HEAD_DIM - n_rows, tq), NEG, F32)], axis=0)
    q4s_ref[0:HEAD_DIM, :] = q4[0:HEAD_DIM]
    for i in range(g):
        q4s_ref[HEAD_DIM:2 * HEAD_DIM, i * tq:(i + 1) * tq] = bias.astype(BF16)

    m_ref[...] = jnp.full(m_ref.shape, NEG, F32)
    acc_ref[...] = jnp.zeros(acc_ref.shape, F32)
    krow = lax.broadcasted_iota(jnp.int32, (tk, w), 0)

    def scores(kt):
        return _dot(sk_ref[pl.ds(pl.multiple_of(kt * tk, tk), tk), :], q4s_ref[...])

    def update(kt, s, masked):
        k0 = pl.multiple_of(kt * tk, tk)
        allow = (k0 + krow <= pos4) if masked else None
        _flash_update_t(s, svt_ref[:, pl.ds(k0, tk)], m_ref, acc_ref, 0, allow=allow)

    def body(kt, s):
        s_next = scores(kt + 1)
        update(kt, s, False)
        return s_next

    n_full = _div(q0, tk)
    update(n_full, lax.fori_loop(0, n_full, body, scores(0)), True)
    o_slc = _normalise_t(acc_ref[0])

    ws = pl.multiple_of(jnp.maximum(q0 + tq - wlen, 0), tq)
    dist = pos4 - (ws + lax.broadcasted_iota(jnp.int32, (wlen, w), 0))
    okw = (dist >= 0) & (dist < WINDOW)
    sw = jnp.where(okw, _dot(wk_ref[pl.ds(ws, wlen), :], q4), NEG)
    pw = jnp.exp(sw - jnp.max(sw, axis=0, keepdims=True))
    o_win = _normalise_t(_dot(wvt_ref[:, pl.ds(ws, wlen)], pw.astype(BF16)))

    gates = gate_ref[...]
    grow = lax.broadcasted_iota(jnp.int32, (LANES, tq), 0)
    vals = []
    for i in range(g):
        r0 = GATE0 + 3 * (g * h + i)
        gc = [jnp.sum(jnp.where(grow == r0 + c, gates, 0.0), axis=0, keepdims=True) for c in range(3)]
        sl = slice(i * tq, (i + 1) * tq)
        vals.append(gc[0] * o_cmp[0:HEAD_DIM, sl] + gc[1] * o_slc[:, sl] + gc[2] * o_win[:, sl])
    for pr in range(g // 2):
        o_ref[:, pr * LANES:(pr + 1) * LANES] = jnp.concatenate([vals[2 * pr], vals[2 * pr + 1]], axis=0).T


def _nsa_attention_t(nq_t, small_t, kc_p, vc_t, nk_p, nv_t, tq, tk):
    b, _, t = nq_t.shape
    n_ch = kc_p.shape[1]
    n_slc = -(-t // SLC_LEN)
    assert n_slc <= HEAD_DIM, "the block mask rides in the 64 spare query rows"
    wlen = min(t, WINDOW + tq)
    g = GQA_GROUP
    kblk = lambda c: pl.BlockSpec((None, t, LANES), lambda i, h, qi: (i, 0, c * H_KV + h))
    vblk = lambda c: pl.BlockSpec((None, LANES, t), lambda i, h, qi: (i, c * H_KV + h, 0))
    return pl.pallas_call(
        functools.partial(_nsa_t_kernel, tq=tq, tk=tk, n_ch=n_ch, n_slc=n_slc, wlen=wlen),
        grid=(b, H_KV, t // tq),
        in_specs=[pl.BlockSpec((None, g * LANES, tq), lambda i, h, qi: (i, h, qi)),
                  pl.BlockSpec((None, LANES, tq), lambda i, h, qi: (i, 0, qi)),
                  pl.BlockSpec((None, n_ch, LANES), lambda i, h, qi: (i, 0, h)),
                  pl.BlockSpec((None, LANES, n_ch), lambda i, h, qi: (i, h, 0)),
                  kblk(0), vblk(0), kblk(1), vblk(1)],
        out_specs=pl.BlockSpec((None, tq, g * HEAD_DIM), lambda i, h, qi: (i, qi, h)),
        out_shape=jax.ShapeDtypeStruct((b, t, W_NSA), F32),
        scratch_shapes=[pltpu.VMEM((LANES, g * tq), BF16), pltpu.VMEM((LANES, g * tq), BF16),
                        pltpu.VMEM((-(-n_slc // 8) * 8, tq), jnp.int32),
                        pltpu.VMEM((1, 1, g * tq), F32), pltpu.VMEM((1, LANES, g * tq), F32)],
        compiler_params=_params(("parallel", "parallel", "arbitrary")),
        name="nsa_prompt",
    )(nq_t, small_t, kc_p, vc_t, nk_p, nv_t, nk_p, nv_t)


def _nsa_kernel(q_ref, gate_ref, kc_ref, vc_ref, sk_ref, sv_ref, wk_ref, wv_ref, o_ref,
                q4_ref, q4s_ref, cnt_ref, m_ref, acc_ref, *, tq, tk, n_ch, n_slc, wlen):
    h = pl.program_id(1)
    q0 = pl.program_id(2) * tq
    g = GQA_GROUP
    n_sel = min(SLC_TOPN, n_slc)
    lane = lax.broadcasted_iota(jnp.int32, (tq, LANES), 1)
    lo = lane < HEAD_DIM
    pos = q0 + lax.broadcasted_iota(jnp.int32, (tq, 1), 0)
    for i in range(g):
        q4_ref[i * tq:(i + 1) * tq, :] = q_ref[:, i * LANES:(i + 1) * LANES]
    q4 = q4_ref[...]

    sc = _dot_nt(q4, kc_ref[...]).reshape(g, tq, n_ch)
    n_idx = lax.broadcasted_iota(jnp.int32, (tq, n_ch), 1)
    ok = ((n_idx * CMP_STRIDE + CMP_LEN - 1 <= pos) & (n_idx < n_ch - 1))[None]
    sc = jnp.where(ok, sc, NEG)
    e = jnp.where(ok, jnp.exp(sc - jnp.max(sc, axis=-1, keepdims=True)), 0.0)
    lc = jnp.sum(e, axis=-1, keepdims=True)
    a = (e * (1.0 / jnp.where(lc > 0.0, lc, 1.0))).reshape(g * tq, n_ch)
    a_hi = a.astype(BF16)
    a_lo = (a - a_hi.astype(F32)).astype(BF16)
    o_cmp = _dot(a_hi, vc_ref[...]).reshape(g, tq, LANES)
    ov_t = _overlap_t(n_ch, n_slc)
    imp4 = _dot_nt(ov_t, a_hi) + _dot_nt(ov_t, a_lo)
    imp_t = sum(imp4[:, i * tq:(i + 1) * tq] for i in range(g))

    score = _block_scores(imp_t, q0 + lax.broadcasted_iota(jnp.int32, (1, tq), 1), n_slc, 0)
    n_valid = _div(q0 + tq - 1, SLC_LEN) + 1
    n_rows = cnt_ref.shape[0]
    cnt_ref[...] = jnp.zeros(cnt_ref.shape, jnp.int32)
    for b0 in range(0, n_slc, 8):
        @pl.when((b0 < n_valid) & (n_valid > n_sel))
        def _():
            cnt_ref[...] += _count_beats_rows(score, b0, min(b0 + 8, n_slc), n_rows)
    sel_t = (cnt_ref[...] < n_sel) & (lax.broadcasted_iota(jnp.int32, (n_rows, tq), 0) < n_slc)
    bias_t = jnp.concatenate([jnp.where(sel_t, 0.0, NEG), jnp.full((LANES - n_rows, tq), NEG, F32)], axis=0)
    sel_bias = pltpu.roll(bias_t.T, HEAD_DIM, axis=1)
    for i in range(g):
        qi = q_ref[:, i * LANES:(i + 1) * LANES].astype(F32)
        q4s_ref[i * tq:(i + 1) * tq, :] = jnp.where(lo, qi, sel_bias).astype(BF16)

    m_ref[...] = jnp.full(m_ref.shape, NEG, F32)
    acc_ref[...] = jnp.zeros(acc_ref.shape, F32)

    pos4 = q0 + (lax.broadcasted_iota(jnp.int32, (g * tq, 1), 0) & (tq - 1))

    def scores(kt):
        return _scores(q4s_ref[...], sk_ref[pl.ds(pl.multiple_of(kt * tk, tk), tk), :])

    def update(kt, sm, masked):
        k0 = pl.multiple_of(kt * tk, tk)
        allow = (k0 + lax.broadcasted_iota(jnp.int32, (g * tq, tk), 1) <= pos4) if masked else None
        _flash_update(sm, sv_ref[pl.ds(k0, tk), :], m_ref, acc_ref, 0, allow=allow)

    def body(kt, sm):
        sm_next = scores(kt + 1)
        update(kt, sm, False)
        return sm_next

    n_full = _div(q0, tk)
    update(n_full, lax.fori_loop(0, n_full, body, scores(0)), True)
    o_slc = _normalise(acc_ref[0]).reshape(g, tq, LANES)

    ws = pl.multiple_of(jnp.maximum(q0 + tq - wlen, 0), tq)
    dist = pos4 - (ws + lax.broadcasted_iota(jnp.int32, (g * tq, wlen), 1))
    okw = (dist >= 0) & (dist < WINDOW)
    sw = jnp.where(okw, _dot_nt(q4, wk_ref[pl.ds(ws, wlen), :]), NEG)
    pw = jnp.exp(sw - jnp.max(sw, axis=-1, keepdims=True))
    o_win = _normalise(_dot(pw.astype(BF16), wv_ref[pl.ds(ws, wlen), :])).reshape(g, tq, LANES)

    gates = gate_ref[...]
    vals = []
    for i in range(g):
        col0 = GATE0 + 3 * (g * h + i)
        gc = [jnp.sum(jnp.where(lane == col0 + c, gates, 0.0), axis=1, keepdims=True) for c in range(3)]
        vals.append(gc[0] * o_cmp[i] + gc[1] * o_slc[i] + gc[2] * o_win[i])
    for pr in range(g // 2):
        o_ref[:, pr * LANES:(pr + 1) * LANES] = _pair_up(vals[2 * pr], vals[2 * pr + 1])


def _nsa_attention(nq_p, small, kc_p, vc_p, nkv_p, tq, tk):
    b, t, _ = nq_p.shape
    n_ch = kc_p.shape[1]
    n_slc = -(-t // SLC_LEN)
    assert n_slc <= LANES - HEAD_DIM, "the block mask rides in the 64 spare query lanes"
    wlen = min(t, WINDOW + tq)
    g = GQA_GROUP
    cblk = pl.BlockSpec((None, n_ch, LANES), lambda i, h, qi: (i, 0, h))
    lane_blk = lambda c: pl.BlockSpec((None, t, LANES), lambda i, h, qi: (i, 0, c * H_KV + h))
    return pl.pallas_call(
        functools.partial(_nsa_kernel, tq=tq, tk=tk, n_ch=n_ch, n_slc=n_slc, wlen=wlen),
        grid=(b, H_KV, t // tq),
        in_specs=[pl.BlockSpec((None, tq, g * LANES), lambda i, h, qi: (i, qi, h)),
                  pl.BlockSpec((None, tq, LANES), lambda i, h, qi: (i, qi, 0)), cblk, cblk,
                  lane_blk(0), lane_blk(1), lane_blk(2), lane_blk(3)],
        out_specs=pl.BlockSpec((None, tq, g * HEAD_DIM), lambda i, h, qi: (i, qi, h)),
        out_shape=jax.ShapeDtypeStruct((b, t, W_NSA), F32),
        scratch_shapes=[pltpu.VMEM((g * tq, LANES), BF16), pltpu.VMEM((g * tq, LANES), BF16),
                        pltpu.VMEM((-(-n_slc // 8) * 8, tq), jnp.int32),
                        pltpu.VMEM((1, g * tq, 1), F32), pltpu.VMEM((1, g * tq, LANES), F32)],
        compiler_params=_params(("parallel", "parallel", "arbitrary")),
        name="nsa_prompt",
    )(nq_p, small, kc_p, vc_p, nkv_p, nkv_p, nkv_p, nkv_p)


def _page_copies(pt_ref, step, slot, n_pages, page, streams, group):
    out = []
    for u in range(group):
        for pg in range(n_pages):
            idx = pt_ref[(step * group + u) * n_pages + pg]
            for hbm, buf, sem, on_lanes in streams:
                dst = buf.at[slot, u, :, pl.ds(pg * page, page)] if on_lanes else buf.at[slot, u, pg]
                out.append(pltpu.make_async_copy(hbm.at[idx], dst, sem.at[slot]))
    return out


def _gather_pages(pt_ref, n_pages, page, streams, group=1):
    i = pl.program_id(0)
    slot = lax.rem(i, 2)

    @pl.when(i == 0)
    def _():
        for c in _page_copies(pt_ref, 0, 0, n_pages, page, streams, group):
            c.start()

    @pl.when(i + 1 < pl.num_programs(0))
    def _():
        for c in _page_copies(pt_ref, i + 1, 1 - slot, n_pages, page, streams, group):
            c.start()

    for c in _page_copies(pt_ref, i, slot, n_pages, page, streams, group):
        c.wait()
    return slot


def _head_scores(q_col, k_view, s_ref, head, rows0, n_tiles, tile):
    qb = jnp.broadcast_to(q_col, (HEAD_DIM, tile))
    for pg in range(n_tiles):
        kt = k_view[rows0:rows0 + HEAD_DIM, pg * tile:(pg + 1) * tile]
        s_ref[pg, head:head + 1, :] = jnp.sum(kt * qb, axis=0, keepdims=True)


def _head_values(p_tile, v_view, rows0, n_tiles, tile):
    acc = jnp.zeros((HEAD_DIM, tile), F32)
    for pg in range(n_tiles):
        vt = v_view[rows0:rows0 + HEAD_DIM, pg * tile:(pg + 1) * tile]
        acc = acc + vt * jnp.broadcast_to(p_tile(pg), (HEAD_DIM, tile))
    return jnp.sum(acc, axis=1, keepdims=True)


def _softmax_tiles(s, s_new):
    m = jnp.maximum(jnp.max(jnp.max(s, axis=0), axis=1, keepdims=True), s_new)
    p = jnp.exp(s - m[None])
    p_new = jnp.exp(s_new - m)
    l = jnp.sum(jnp.sum(p, axis=0), axis=1, keepdims=True) + p_new
    return p, p_new, 1.0 / l


def _softmax_rows(s, s_new, allow):
    s = jnp.where(allow, s, NEG)
    m = jnp.maximum(jnp.max(s, axis=1, keepdims=True), s_new)
    p = jnp.where(allow, jnp.exp(s - m), 0.0)
    p_new = jnp.exp(s_new - m)
    return p, p_new, 1.0 / (jnp.sum(p, axis=1, keepdims=True) + p_new)


def _col_dot(a_col, b_col, n_heads):
    return jnp.sum((a_col * b_col).reshape(n_heads, HEAD_DIM, 1), axis=1)


def _as_column(row):
    return jnp.broadcast_to(row, (LANES, row.shape[1])).T[:, 0:1]


def _as_row(col):
    return jnp.broadcast_to(col, (col.shape[0], LANES)).T[0:1, :]


def _fox_dec_kernel(pt_ref, q_ref, kvnew_ref, small_ref, kv_hbm, lf_hbm, o_ref,
                    kvbuf, lfbuf, s_ref, p_ref, sem_kv, sem_lf, *, n_pages, page):
    slot = _gather_pages(pt_ref, n_pages, page,
                         [(kv_hbm, kvbuf, sem_kv, True), (lf_hbm, lfbuf, sem_lf, False)])
    kv = kvbuf.at[slot, 0]
    q = _as_column(q_ref[...])
    kv_new = _as_column(kvnew_ref[...])
    lf_new = jnp.concatenate([_as_column(small_ref[...])[0:H_FOX]] * n_pages, axis=0)
    for h in range(H_FOX):
        _head_scores(q[h * HEAD_DIM:(h + 1) * HEAD_DIM], kv, s_ref, h, h * HEAD_DIM, n_pages, page)

    rows = n_pages * H_FOX
    lf = lfbuf[slot, 0].reshape(rows, page)
    r = lax.broadcasted_iota(jnp.int32, (page, page), 0)
    c = lax.broadcasted_iota(jnp.int32, (page, page), 1)
    later = (r > c).astype(BF16)
    hi, mid, lo = _split3(lf)
    within = _dot(hi, later) + _dot(mid, later) + _dot(lo, later)
    r = lax.broadcasted_iota(jnp.int32, (rows, rows), 0)
    c = lax.broadcasted_iota(jnp.int32, (rows, rows), 1)
    later_pages = ((c > r) & (((c - r) & (H_FOX - 1)) == 0)).astype(BF16)
    tot = jnp.broadcast_to(jnp.sum(lf, axis=1, keepdims=True), (rows, page))
    hi, mid, lo = _split3(tot)
    beyond = _dot(later_pages, hi) + _dot(later_pages, mid) + _dot(later_pages, lo)
    bias = (within + beyond + lf_new).reshape(n_pages, H_FOX, page)

    s_new = _col_dot(q, kv_new[0:W_FOX], H_FOX)
    p, p_new, inv_l = _softmax_tiles(s_ref[...] + bias, s_new)
    p_ref[...] = p
    outs = []
    for h in range(H_FOX):
        o = _head_values(lambda pg, h=h: p_ref[pg, h:h + 1, :], kv, W_FOX + h * HEAD_DIM, n_pages, page)
        v_new = kv_new[W_FOX + h * HEAD_DIM: W_FOX + (h + 1) * HEAD_DIM]
        outs.append((o + p_new[h:h + 1] * v_new) * inv_l[h:h + 1])
    o_ref[...] = _as_row(jnp.concatenate(outs, axis=0))


def _fox_decode(page_table, q_row, kv_row, small_row, cache_kvt, cache_lft):
    s, n_pages = page_table.shape
    rows, page = cache_kvt.shape[1], cache_kvt.shape[2]
    col = lambda n: pl.BlockSpec((None, 1, n), lambda i, pt: (i, 0, 0))
    anyspec = pl.BlockSpec(memory_space=pl.ANY)
    return pl.pallas_call(
        functools.partial(_fox_dec_kernel, n_pages=n_pages, page=page),
        grid_spec=pltpu.PrefetchScalarGridSpec(
            num_scalar_prefetch=1,
            grid=(s,),
            in_specs=[col(W_FOX), col(2 * W_FOX), col(LANES), anyspec, anyspec],
            out_specs=col(W_FOX),
            scratch_shapes=[pltpu.VMEM((2, 1, rows, n_pages * page), F32),
                            pltpu.VMEM((2, 1, n_pages, H_FOX, page), F32),
                            pltpu.VMEM((n_pages, H_FOX, page), F32), pltpu.VMEM((n_pages, H_FOX, page), F32),
                            pltpu.SemaphoreType.DMA((2,)), pltpu.SemaphoreType.DMA((2,))],
        ),
        out_shape=jax.ShapeDtypeStruct((s, 1, W_FOX), F32),
        compiler_params=_params(("arbitrary",)),
        name="fox_decode",
    )(page_table.reshape(-1), q_row, kv_row, small_row, cache_kvt, cache_lft)


def _nsa_dec_kernel(pt_ref, qrow_ref, nkv4_ref, wnew_ref, gate_ref, win_ref, cache_hbm,
                    wj_ref, pe_ref, wpe_ref, w2_ref, w2t_ref, o_ref,
                    xbuf, xk_buf, xv_buf, p_ref, pw_ref, sem, *, n_pages, page, group):
    slot = _gather_pages(pt_ref, n_pages, page, [(cache_hbm, xbuf, sem, True)], group)
    n_ch = n_pages * page // CMP_STRIDE

    r = lax.broadcasted_iota(jnp.int32, (page, page), 0)
    t = lax.broadcasted_iota(jnp.int32, (page, page), 1)
    per = page // CMP_STRIDE
    perm = (t == CMP_STRIDE * (r & (per - 1)) + _div(r, per)).astype(BF16)
    for u in range(group):
        for pg in range(n_pages):
            xt = _dot_nt(perm, xbuf[slot, u, 0:2 * W_KV, pg * page:(pg + 1) * page].astype(BF16))
            c0 = u * n_ch + pg * per
            for j in range(CMP_STRIDE):
                xk_buf[j, c0:c0 + per, :] = xt[j * per:(j + 1) * per, 0:W_KV]
                xv_buf[j, c0:c0 + per, :] = xt[j * per:(j + 1) * per, W_KV:2 * W_KV]
    hid = _compress_hidden(lambda jj: (xk_buf[jj], xv_buf[jj]), group * n_ch, wj_ref, pe_ref, wpe_ref).astype(BF16)
    for u in range(group):
        o_ref[u] = _nsa_dec_one(hid[u * n_ch:(u + 1) * n_ch], xbuf.at[slot, u], qrow_ref[u], nkv4_ref[u],
                                wnew_ref[u], gate_ref[u], win_ref.at[u], w2_ref, w2t_ref, p_ref.at[u], pw_ref.at[u],
                                n_pages, page)


def _nsa_dec_one(hid, x, q_row, nkv4_new, win_new, gates, wv, w2_ref, w2t_ref, p_ref, pw_ref, n_pages, page):
    past_len = n_pages * page
    n_ch = past_len // CMP_STRIDE
    n_slc = past_len // SLC_LEN + 1
    n_sel = min(SLC_TOPN, n_slc)
    win_buf = wv.shape[1]
    g = GQA_GROUP
    row = lax.broadcasted_iota(jnp.int32, (H_NSA, LANES), 0)
    lane = lax.broadcasted_iota(jnp.int32, (H_NSA, LANES), 1)
    v_new = _as_column(nkv4_new)[3 * W_KV:4 * W_KV]
    vw_new = _as_column(win_new)[W_KV:2 * W_KV]
    kc = _dot(hid, w2_ref[:, 0:W_KV]).astype(BF16)
    vct = _dot_nt(w2t_ref[W_KV:2 * W_KV, :], hid).astype(BF16)

    qbd = jnp.zeros((H_NSA, LANES), F32)
    for i in range(H_NSA):
        piece = q_row[:, (i // 2) * LANES:(i // 2 + 1) * LANES]
        if (i % 2) != (i // g):
            piece = pltpu.roll(piece, HEAD_DIM, axis=1)
        qbd = jnp.where(row == i, jnp.broadcast_to(piece, (H_NSA, LANES)), qbd)
    qbd = jnp.where(_div(lane, HEAD_DIM) == _div(row, g), qbd, 0.0).astype(BF16)
    n_idx = lax.broadcasted_iota(jnp.int32, (H_NSA, n_ch), 1)
    ok = (n_idx * CMP_STRIDE + CMP_LEN - 1 <= past_len) & (n_idx < n_ch - 1)
    sc = jnp.where(ok, _dot_nt(qbd, kc), NEG)
    e = jnp.where(ok, jnp.exp(sc - jnp.max(sc, axis=1, keepdims=True)), 0.0)
    lc = jnp.sum(e, axis=1, keepdims=True)
    a = e * (1.0 / jnp.where(lc > 0.0, lc, 1.0))
    a_hi = a.astype(BF16)
    a_lo = (a - a_hi.astype(F32)).astype(BF16)
    o_cmp_t = _dot_nt(vct, a_hi)
    ov = _overlap(n_ch, n_slc)
    imp8 = _dot(a_hi, ov) + _dot(a_lo, ov)
    imp = jnp.zeros((H_NSA, LANES), F32)
    for h in range(H_KV):
        tot = jnp.sum(imp8[h * g:(h + 1) * g], axis=0, keepdims=True)
        imp = jnp.where(_div(row, g) == h, jnp.broadcast_to(tot, (H_NSA, LANES)), imp)
    score = _block_scores(imp, jnp.full((H_NSA, 1), past_len, jnp.int32), n_slc, 1)
    sel = jnp.where((_count_beats_lanes(score, n_slc) < n_sel) & (lane < n_slc), 1.0, 0.0)

    per_page = page // SLC_LEN
    allow = []
    for pg in range(n_pages):
        m = jnp.zeros((H_NSA, page), F32)
        for b in range(per_page):
            blk = pg * per_page + b
            lanes_b = _div(lax.broadcasted_iota(jnp.int32, (H_NSA, page), 1), SLC_LEN) == b
            m = jnp.where(lanes_b, jnp.broadcast_to(sel[:, blk:blk + 1], (H_NSA, page)), m)
        allow.append(m > 0.5)
    allow = jnp.concatenate(allow, axis=1)
    qf = qbd.astype(F32)
    s_new = jnp.sum(qf * nkv4_new[:, 2 * W_KV:3 * W_KV], axis=1, keepdims=True)
    p, p_new, inv_l = _softmax_rows(_dot(qbd, x[2 * W_KV:3 * W_KV, :].astype(BF16)), s_new, allow)
    p_ref[...] = p

    slot_idx = lax.broadcasted_iota(jnp.int32, (H_NSA, win_buf), 1)
    sw_new = jnp.sum(qf * win_new[:, 0:W_KV], axis=1, keepdims=True)
    pw, pw_new, inv_lw = _softmax_rows(_dot(qbd, wv[0:W_KV, :].astype(BF16)), sw_new, (win_buf - slot_idx) < WINDOW)
    pw_ref[...] = pw

    wtile = min(win_buf, 4 * LANES)
    outs = []
    for i in range(H_NSA):
        h = i // g
        hs = slice(h * HEAD_DIM, (h + 1) * HEAD_DIM)
        o_slc = _head_values(lambda pg, i=i: p_ref[i:i + 1, pg * page:(pg + 1) * page], x,
                             3 * W_KV + h * HEAD_DIM, n_pages, page)
        o_slc = (o_slc + p_new[i:i + 1] * v_new[hs]) * inv_l[i:i + 1]
        o_win = _head_values(lambda wt, i=i: pw_ref[i:i + 1, wt * wtile:(wt + 1) * wtile], wv,
                             W_KV + h * HEAD_DIM, win_buf // wtile, wtile)
        o_win = (o_win + pw_new[i:i + 1] * vw_new[hs]) * inv_lw[i:i + 1]
        c0 = GATE0 + 3 * i
        outs.append(gates[:, c0:c0 + 1] * o_cmp_t[hs, i:i + 1] + gates[:, c0 + 1:c0 + 2] * o_slc
                    + gates[:, c0 + 2:c0 + 3] * o_win)
    return _as_row(jnp.concatenate(outs, axis=0))


def _nsa_decode(page_table, q_row, nkv4_row, win_row, small, win_t, cache_t, cw):
    s, n_pages = page_table.shape
    rows, page = cache_t.shape[1], cache_t.shape[2]
    past_len = n_pages * page
    n_ch = past_len // CMP_STRIDE
    win_buf = win_t.shape[2]
    group = next(g for g in (4, 2, 1) if s % g == 0)
    rowspec = lambda n: pl.BlockSpec((group, 1, n), lambda i, pt: (i, 0, 0))
    const = lambda shape: pl.BlockSpec(shape, lambda i, pt: (0,) * len(shape), pipeline_mode=pl.Buffered(1))
    return pl.pallas_call(
        functools.partial(_nsa_dec_kernel, n_pages=n_pages, page=page, group=group),
        grid_spec=pltpu.PrefetchScalarGridSpec(
            num_scalar_prefetch=1,
            grid=(s // group,),
            in_specs=[rowspec(W_NSA), rowspec(4 * W_KV), rowspec(2 * W_KV), rowspec(LANES),
                      pl.BlockSpec((group, 2 * W_KV, win_buf), lambda i, pt: (i, 0, 0)),
                      pl.BlockSpec(memory_space=pl.ANY)] + [const(w.shape) for w in cw],
            out_specs=rowspec(W_NSA),
            scratch_shapes=[pltpu.VMEM((2, group, rows, past_len), F32),
                            pltpu.VMEM((CMP_STRIDE, group * n_ch, W_KV), F32),
                            pltpu.VMEM((CMP_STRIDE, group * n_ch, W_KV), F32),
                            pltpu.VMEM((group, H_NSA, past_len), F32), pltpu.VMEM((group, H_NSA, win_buf), F32),
                            pltpu.SemaphoreType.DMA((2,))],
        ),
        out_shape=jax.ShapeDtypeStruct((s, 1, W_NSA), F32),
        compiler_params=_params(("arbitrary",)),
        name="nsa_decode",
    )(page_table.reshape(-1), q_row, nkv4_row, win_row, small, win_t, cache_t, *cw)


def _rot_cols(w):
    d, n = w.shape
    w = w.reshape(d, n // HEAD_DIM, 2, HEAD_DIM // 2)
    return jnp.stack([-w[:, :, 1], w[:, :, 0]], axis=2).reshape(d, n)


def _prep_projection(w_in, b_f):
    c = [0, W_FOX, 2 * W_FOX, 3 * W_FOX, 3 * W_FOX + H_FOX, 3 * W_FOX + H_FOX + W_NSA,
         3 * W_FOX + H_FOX + W_NSA + 6 * W_KV]
    fq, fk, fv, ff, nq, kv = (w_in[:, c[i]:c[i + 1]] for i in range(6))
    gt = w_in[:, c[6]:]
    ks = jnp.concatenate([kv[:, br * 2 * W_KV: br * 2 * W_KV + W_KV] for br in range(3)], axis=1)
    w_big = jnp.concatenate([fq, fk, fv, nq, kv, _rot_cols(nq), _rot_cols(ks)], axis=1).astype(BF16)
    d = w_in.shape[0]
    w_small = jnp.concatenate([ff, gt, jnp.zeros((d, LANES - N_SMALL), w_in.dtype)], axis=1).astype(BF16)
    b_small = jnp.concatenate([b_f.astype(F32), jnp.zeros((LANES - H_FOX,), F32)])[None, :]
    return w_big, w_small, b_small


def _prep_compress(wk1, wk2, pek, wv1, wv2, pev):
    ratio = CMP_LEN // CMP_STRIDE
    eye = jnp.eye(4, dtype=F32)

    def blocks(w):
        return w.reshape(ratio, CMP_STRIDE, HEAD_DIM, wk1.shape[1])

    per_head = lambda w: jnp.einsum("rjde,hk->jhdrke", blocks(w), jnp.eye(H_KV, dtype=F32)).reshape(
        CMP_STRIDE, W_KV, ratio * W_KV)
    wj = jnp.concatenate([per_head(wk1), per_head(wv1)], axis=1).astype(BF16)
    pe = jnp.concatenate([pek.reshape(-1), pev.reshape(-1)])
    pe = jnp.broadcast_to(pe[None, :], (8, pe.shape[0])).astype(BF16)
    zero = jnp.zeros_like(wk1)
    wpe = jnp.concatenate([jnp.concatenate([wk1, wk1, zero, zero], axis=1),
                           jnp.concatenate([zero, zero, wv1, wv1], axis=1)], axis=0).astype(BF16)
    w2 = jnp.einsum("gde,gh->gdhe", jnp.stack([wk2, wk2, wv2, wv2]), eye)
    w2_pad = jnp.concatenate([w2, jnp.zeros_like(w2)], axis=3)
    w2 = w2.reshape(4 * HEAD_DIM, 4 * HEAD_DIM).astype(BF16)
    w2_pad = w2_pad.reshape(4 * HEAD_DIM, 4 * LANES).astype(BF16)
    return (wj, pe, wpe), w2, w2_pad


def _rope_tables(pos):
    half = HEAD_DIM // 2
    inv = ROPE_THETA ** (-jnp.arange(half, dtype=F32) / half)
    ang = pos.astype(F32)[:, None] * inv[None, :]
    reps = LANES // half
    return jnp.tile(jnp.cos(ang), (1, reps)), jnp.tile(jnp.sin(ang), (1, reps))


def _row_tile(n, cap):
    t = min(n, cap)
    while n % t:
        t //= 2
    return t


def kernel(x_prompt, x_sample, cache_fox_kv, cache_fox_logf, cache_nsa_kv, state_nsa_win_kv, page_table,
           g_ffn1_pre, w_ffn1_gate, w_ffn1_up, w_ffn1_down, g_ffn1_post, g_mix_pre, w_in, b_fox_f,
           w_cmpk_1, w_cmpk_2, pe_cmpk, w_cmpv_1, w_cmpv_2, pe_cmpv, g_fox_out, g_nsa_out, w_out,
           g_mix_post, g_ffn2_pre, w_ffn2_gate, w_ffn2_up, w_ffn2_down, g_ffn2_post):
    depth = w_in.shape[0]
    b, t, d = x_prompt.shape
    s, dec_seq, _ = x_sample.shape
    assert dec_seq == 1, "the sample group decodes one token per sequence"
    page = cache_fox_kv.shape[2]
    n_pages = page_table.shape[1]
    past_len = n_pages * page
    assert t % LANES == 0 and page % SLC_LEN == 0
    page_table = page_table.astype(jnp.int32)

    tm_p = _row_tile(t, 512)
    cos_p, sin_p = _rope_tables(jnp.arange(t, dtype=jnp.int32))
    cos_s, sin_s = _rope_tables(jnp.full((s,), past_len, jnp.int32))
    row = lambda v: v.astype(F32)[None, :]
    to_rows = lambda c: jnp.transpose(c, (0, 2, 3, 4, 1)).reshape(c.shape[0], -1, c.shape[1])

    yp = x_prompt.reshape(b * t, d)
    ys = x_sample.reshape(s, d)
    outs = [[] for _ in range(8)]
    for l in range(depth):
        ffn1 = (row(g_ffn1_pre[l]), w_ffn1_gate[l].astype(BF16), w_ffn1_up[l].astype(BF16),
                w_ffn1_down[l].astype(BF16), row(g_ffn1_post[l]))
        ffn2 = (row(g_ffn2_pre[l]), w_ffn2_gate[l].astype(BF16), w_ffn2_up[l].astype(BF16),
                w_ffn2_down[l].astype(BF16), row(g_ffn2_post[l]))
        w_big, w_small, b_small = _prep_projection(w_in[l], b_fox_f[l])
        cw, w2, w2_pad = _prep_compress(w_cmpk_1[l], w_cmpk_2[l], pe_cmpk[l], w_cmpv_1[l], w_cmpv_2[l], pe_cmpv[l])
        merge = (row(g_fox_out[l]), row(g_nsa_out[l]), w_out[l].astype(BF16), row(g_mix_post[l]))

        hp = _half_ffn(yp, *ffn1, tm_p)
        small, fkv_t, nkv4_t, win_t, fq_t, fv_t, nq_t, nv_t, cmp, fk_p, nk_p = _project(
            hp, row(g_mix_pre[l]), w_big, w_small, b_small, cos_p, sin_p, tm_p, t // tm_p, True)
        small3 = small.reshape(b, t, LANES)
        per_head = lambda x: x.reshape(b, t, x.shape[1])
        cum, fk_b = _forget_bias(small3, per_head(fk_p), _row_tile(t, 512))
        o_fox = _fox_attention(fq_t, jnp.swapaxes(cum[:, :, :H_FOX], 1, 2), fk_b, fv_t,
                               _row_tile(t, 512), _row_tile(t, 512))
        kc_p, vc_t = _compress(cmp.reshape(b, t, 2 * W_KV), cw + (w2_pad,))
        o_nsa = _nsa_attention_t(nq_t, jnp.swapaxes(small3, 1, 2), kc_p, vc_t, per_head(nk_p), nv_t,
                                 LANES, _row_tile(t, 512))
        yp = _merge_ffn(hp, o_fox.reshape(b * t, W_FOX), o_nsa.reshape(b * t, W_NSA), *merge, *ffn2, tm_p)
        keep = min(WINDOW, t)
        tokens_first = lambda x, *dims: jnp.transpose(x.reshape(b, *dims, x.shape[2]), (0, 4, 1, 2, 3))
        outs[0].append(tokens_first(fkv_t, 2, H_FOX, HEAD_DIM))
        outs[1].append(small3[:, :, :H_FOX])
        outs[2].append(tokens_first(nkv4_t, 4, H_KV, HEAD_DIM))
        outs[3].append(tokens_first(win_t[:, :, t - keep:], 2, H_KV, HEAD_DIM))

        hs = _half_ffn(ys, *ffn1, s)
        small, fkv, nkv4, win, fq, nq = _project(
            hs, row(g_mix_pre[l]), w_big, w_small, b_small, cos_s, sin_s, s, 1, False)
        as_rows = lambda x: x.reshape(s, 1, x.shape[1])
        o_fox = _fox_decode(page_table, as_rows(fq), as_rows(fkv), as_rows(small),
                            to_rows(cache_fox_kv[l]), jnp.swapaxes(cache_fox_logf[l], 1, 2))
        o_nsa = _nsa_decode(page_table, as_rows(nq), as_rows(nkv4), as_rows(win), as_rows(small),
                            to_rows(state_nsa_win_kv[l]), to_rows(cache_nsa_kv[l]), cw + (w2, w2.T))
        ys = _merge_ffn(hs, o_fox.reshape(s, W_FOX), o_nsa.reshape(s, W_NSA), *merge, *ffn2, s)
        keep = min(WINDOW, past_len + 1)
        kw_all = jnp.concatenate([state_nsa_win_kv[l], win.reshape(s, 1, 2, H_KV, HEAD_DIM)], axis=1)
        outs[4].append(fkv.reshape(s, 1, 2, H_FOX, HEAD_DIM))
        outs[5].append(small[:, :H_FOX].reshape(s, 1, H_FOX))
        outs[6].append(nkv4.reshape(s, 1, 4, H_KV, HEAD_DIM))
        outs[7].append(kw_all[:, kw_all.shape[1] - keep:])

    stacked = [jnp.stack(o, axis=0) for o in outs]
    return (yp.reshape(b, t, d), ys.reshape(s, 1, d), *stacked)
```

```python
import functools

import jax
import jax.numpy as jnp
from jax import lax
from jax.experimental import pallas as pl
from jax.experimental.pallas import tpu as pltpu

HEAD_DIM = 64
H_FOX = 8
H_NSA = 8
H_KV = 2
GQA_GROUP = H_NSA // H_KV
W_FOX = H_FOX * HEAD_DIM
W_NSA = H_NSA * HEAD_DIM
W_KV = H_KV * HEAD_DIM
CMP_STRIDE = 16
CMP_LEN = 32
SLC_LEN = 64
SLC_TOPN = 16
WINDOW = 512
ROPE_THETA = 10000.0
EPS = 1e-6
NEG = -1e30
FORCE_SCORE = 1e4
N_SMALL = H_FOX + 3 * H_NSA
GATE0 = H_FOX

LANES = 128
MXU_N = 256
VMEM_LIMIT = 56 * 1024 * 1024
FOX_HEADS_PER_STEP = 2

F32 = jnp.float32
BF16 = jnp.bfloat16


def _dot(a, b):
    return jnp.dot(a, b, preferred_element_type=F32)


def _dot_nt(a, b):
    return lax.dot_general(a, b, (((1,), (1,)), ((), ())), preferred_element_type=F32)


def _div(x, n):
    assert n & (n - 1) == 0
    return lax.shift_right_logical(x, jnp.int32(n.bit_length() - 1))


def _split3(x):
    hi = x.astype(BF16)
    r1 = x - hi.astype(F32)
    mid = r1.astype(BF16)
    lo = (r1 - mid.astype(F32)).astype(BF16)
    return hi, mid, lo


def _rms(x, g):
    return x * lax.rsqrt(jnp.mean(x * x, axis=-1, keepdims=True) + EPS) * g


def _ff_chunks(d_ff):
    step = 6 * MXU_N
    return tuple((c, min(c + step, d_ff)) for c in range(0, d_ff, step))


def _ffn_core(x, gpre, wg_ref, wu_ref, wd_ref, gpost, chunks):
    xn = _rms(x, gpre).astype(BF16)
    acc = jnp.zeros(x.shape, F32)
    for c0, c1 in chunks:
        g = _dot(xn, wg_ref[:, c0:c1])
        u = _dot(xn, wu_ref[:, c0:c1])
        hm = (g * jax.nn.sigmoid(g) * u).astype(BF16)
        acc = acc + _dot(hm, wd_ref[c0:c1, :])
    return x + 0.5 * _rms(acc, gpost)


def _const_spec(shape):
    nd = len(shape)
    return pl.BlockSpec(shape, lambda *_: (0,) * nd, pipeline_mode=pl.Buffered(1))


def _params(sem):
    return pltpu.CompilerParams(dimension_semantics=sem, vmem_limit_bytes=VMEM_LIMIT)


def _lo_half(rows):
    return lax.broadcasted_iota(jnp.int32, (rows, LANES), 1) < HEAD_DIM


def _pad_heads(x, n_heads, fill):
    lo = _lo_half(x.shape[0])
    out = []
    for h in range(n_heads):
        piece = x[:, (h // 2) * LANES:(h // 2 + 1) * LANES]
        if h % 2:
            piece = pltpu.roll(piece, HEAD_DIM, axis=1)
        out.append(jnp.where(lo, piece, fill))
    return out


def _normalise(acc):
    den = jnp.where(_lo_half(acc.shape[0]), pltpu.roll(acc, HEAD_DIM, axis=1), 1.0)
    return acc * (1.0 / den)


def _pair_up(even, odd):
    return jnp.where(_lo_half(even.shape[0]), even, pltpu.roll(odd, HEAD_DIM, axis=1))


def _ffn_kernel(x_ref, gpre_ref, wg_ref, wu_ref, wd_ref, gpost_ref, o_ref, *, chunks):
    o_ref[...] = _ffn_core(x_ref[...], gpre_ref[...], wg_ref, wu_ref, wd_ref, gpost_ref[...], chunks)


def _half_ffn(x, gpre, wg, wu, wd, gpost, tm):
    n, d = x.shape
    d_ff = wg.shape[1]
    row = pl.BlockSpec((tm, d), lambda i: (i, 0))
    return pl.pallas_call(
        functools.partial(_ffn_kernel, chunks=_ff_chunks(d_ff)),
        grid=(n // tm,),
        in_specs=[row, _const_spec((1, d)), _const_spec((d, d_ff)), _const_spec((d, d_ff)),
                  _const_spec((d_ff, d)), _const_spec((1, d))],
        out_specs=row,
        out_shape=jax.ShapeDtypeStruct((n, d), F32),
        compiler_params=_params(("parallel",)),
        name="half_ffn",
    )(x, gpre, wg, wu, wd, gpost)


_C_FQ, _C_FK, _C_FV, _C_NQ, _C_KV, _C_NQR, _C_KR, _C_END = 0, 512, 1024, 1536, 2048, 2816, 3328, 3712


def _proj_kernel(h_ref, g_ref, wb_ref, ws_ref, bf_ref, cos_ref, sin_ref, small_ref, *rest, packed, n_pos_tiles):
    n = _rms(h_ref[...], g_ref[...]).astype(BF16)
    tm = n.shape[0]

    def mm(c0, c1):
        return _dot(n, wb_ref[:, c0:c1])

    def put(ref, tiles):
        for i, t in enumerate(tiles):
            ref[:, i * LANES:(i + 1) * LANES] = t.astype(BF16)

    def put_t(ref, tiles):
        for i, t in enumerate(tiles):
            for c in range(t.shape[1] // LANES):
                r0 = i * t.shape[1] + c * LANES
                ref[r0:r0 + LANES, :] = t[:, c * LANES:(c + 1) * LANES].T

    scale = HEAD_DIM ** -0.5
    lane = lax.broadcasted_iota(jnp.int32, (tm, LANES), 1)
    fq = mm(_C_FQ, _C_FK) * scale
    fk = mm(_C_FK, _C_FV)
    fv = mm(_C_FV, _C_NQ)

    cos = cos_ref[...]
    sin = sin_ref[...]
    nq = mm(_C_NQ, _C_KV)
    nqr = mm(_C_NQR, _C_KR)
    nq = jnp.concatenate([(nq[:, c * LANES:(c + 1) * LANES] * cos + nqr[:, c * LANES:(c + 1) * LANES] * sin) * scale
                          for c in range(W_NSA // LANES)], axis=1)

    kv = mm(_C_KV, _C_NQR)
    kr = mm(_C_KR, _C_END)
    ks, vs = [], []
    for br in range(3):
        k = kv[:, br * 2 * W_KV: br * 2 * W_KV + W_KV] * cos + kr[:, br * W_KV:(br + 1) * W_KV] * sin
        v = kv[:, br * 2 * W_KV + W_KV:(br + 1) * 2 * W_KV]
        ks.append(k)
        vs.append(v)

    sm = _dot(n, ws_ref[...]) + bf_ref[...]
    log_sig = jnp.minimum(sm, 0.0) - jnp.log(1.0 + jnp.exp(-jnp.abs(sm)))
    small_ref[...] = jnp.where(lane < H_FOX, log_sig, jax.nn.sigmoid(sm))

    if not packed:
        fkv_ref, nkv4_ref, win_ref, fq_ref, nq_ref = rest
        fkv_ref[...] = jnp.concatenate([fk, fv], axis=1)
        nkv4_ref[...] = jnp.concatenate([ks[0], vs[0], ks[1], vs[1]], axis=1)
        win_ref[...] = jnp.concatenate([ks[2], vs[2]], axis=1)
        fq_ref[...] = fq
        nq_ref[...] = nq
        return
    fkvt_ref, nkv4t_ref, wint_ref, fqt_ref, fvt_ref, nqt_ref, nvt_ref, cmp_ref, fkp_ref, nkp_ref = rest
    put_t(fkvt_ref, [fk, fv])
    put_t(nkv4t_ref, [ks[0], vs[0], ks[1], vs[1]])
    put_t(wint_ref, [ks[2], vs[2]])
    cmp_ref[...] = jnp.concatenate([ks[0], vs[0]], axis=1)

    def put_heads_t(ref, tiles):
        for i, t in enumerate(tiles):
            ref[i * LANES:(i + 1) * LANES, :] = t.T.astype(BF16)

    put_heads_t(fqt_ref, _pad_heads(fq, H_FOX, 0.0))
    put_heads_t(fvt_ref, _pad_heads(fv, H_FOX, 1.0))
    put_heads_t(nqt_ref, _pad_heads(nq, H_NSA, 0.0))
    put_heads_t(nvt_ref, _pad_heads(vs[1], H_KV, 1.0) + _pad_heads(vs[2], H_KV, 1.0))
    ones3 = jnp.where((lane >= HEAD_DIM) & (lane < HEAD_DIM + 3), 1.0, 0.0)
    put(fkp_ref, _pad_heads(fk, H_FOX, ones3))
    pos = (lax.rem(pl.program_id(0), n_pos_tiles) * tm + lax.broadcasted_iota(jnp.int32, (tm, 1), 0))
    onehot = jnp.where(lane - HEAD_DIM == _div(pos, SLC_LEN), 1.0, 0.0)
    put(nkp_ref, _pad_heads(ks[1], H_KV, onehot) + _pad_heads(ks[2], H_KV, 0.0))


def _project(h, g, w_big, w_small, b_small, cos, sin, tm, n_pos_tiles, packed):
    n, d = h.shape
    row = lambda w: pl.BlockSpec((tm, w), lambda i: (i, 0))
    pos = pl.BlockSpec((tm, LANES), lambda i: (i % n_pos_tiles, 0))
    if packed:
        outs = [(LANES, F32), (2 * W_KV, F32), (H_FOX * LANES, BF16), (2 * H_KV * LANES, BF16)]
        t_outs = [(2 * W_FOX, F32), (4 * W_KV, F32), (2 * W_KV, F32), (H_FOX * LANES, BF16),
                  (H_FOX * LANES, BF16), (H_NSA * LANES, BF16), (2 * H_KV * LANES, BF16)]
    else:
        outs = [(LANES, F32), (2 * W_FOX, F32), (4 * W_KV, F32), (2 * W_KV, F32), (W_FOX, F32), (W_NSA, F32)]
        t_outs = []
    out_specs = [row(w) for w, _ in outs]
    out_shape = [jax.ShapeDtypeStruct((n, w), dt) for w, dt in outs]
    batch = n // (tm * n_pos_tiles)
    for k, (r, dt) in enumerate(t_outs):
        out_specs.insert(1 + k, pl.BlockSpec((None, r, tm), lambda i: (i // n_pos_tiles, 0, i % n_pos_tiles)))
        out_shape.insert(1 + k, jax.ShapeDtypeStruct((batch, r, tm * n_pos_tiles), dt))
    return pl.pallas_call(
        functools.partial(_proj_kernel, packed=packed, n_pos_tiles=n_pos_tiles),
        grid=(n // tm,),
        in_specs=[row(d), _const_spec((1, d)), _const_spec(w_big.shape), _const_spec(w_small.shape),
                  _const_spec((1, LANES)), pos, pos],
        out_specs=out_specs,
        out_shape=out_shape,
        compiler_params=_params(("parallel",)),
        name="project",
    )(h, g, w_big, w_small, b_small, cos, sin)


def _merge_ffn_kernel(h_ref, of_ref, on_ref, gf_ref, gn_ref, wo_ref, gmix_ref,
                      gpre_ref, wg_ref, wu_ref, wd_ref, gpost_ref, y_ref, *, chunks):
    of = _rms(of_ref[...], gf_ref[...]).astype(BF16)
    on = _rms(on_ref[...], gn_ref[...]).astype(BF16)
    mrg = _dot(of, wo_ref[0:W_FOX, :]) + _dot(on, wo_ref[W_FOX:W_FOX + W_NSA, :])
    h2 = h_ref[...] + _rms(mrg, gmix_ref[...])
    y_ref[...] = _ffn_core(h2, gpre_ref[...], wg_ref, wu_ref, wd_ref, gpost_ref[...], chunks)


def _merge_ffn(h, o_fox, o_nsa, gf, gn, w_out, gmix, gpre, wg, wu, wd, gpost, tm):
    n, d = h.shape
    d_ff = wg.shape[1]
    row = lambda w: pl.BlockSpec((tm, w), lambda i: (i, 0))
    return pl.pallas_call(
        functools.partial(_merge_ffn_kernel, chunks=_ff_chunks(d_ff)),
        grid=(n // tm,),
        in_specs=[row(d), row(W_FOX), row(W_NSA), _const_spec((1, W_FOX)), _const_spec((1, W_NSA)),
                  _const_spec(w_out.shape), _const_spec((1, d)), _const_spec((1, d)),
                  _const_spec((d, d_ff)), _const_spec((d, d_ff)), _const_spec((d_ff, d)), _const_spec((1, d))],
        out_specs=row(d),
        out_shape=jax.ShapeDtypeStruct((n, d), F32),
        compiler_params=_params(("parallel",)),
        name="merge_ffn",
    )(h, o_fox, o_nsa, gf, gn, w_out, gmix, gpre, wg, wu, wd, gpost)


def _forget_bias_kernel(x_ref, k_ref, cum_ref, k2_ref, carry_ref):
    @pl.when(pl.program_id(1) == 0)
    def _():
        carry_ref[...] = jnp.zeros(carry_ref.shape, F32)

    x = x_ref[...]
    tc = x.shape[0]
    r = lax.broadcasted_iota(jnp.int32, (tc, tc), 0)
    c = lax.broadcasted_iota(jnp.int32, (tc, tc), 1)
    tri = (c <= r).astype(BF16)
    hi, mid, lo = _split3(x)
    cs = _dot(tri, hi) + _dot(tri, mid) + _dot(tri, lo) + carry_ref[...]
    carry_ref[...] = cs[tc - 1:tc, :]
    cum_ref[...] = cs

    lane = lax.broadcasted_iota(jnp.int32, (tc, LANES), 1)
    for h in range(H_FOX):
        sl = slice(h * LANES, (h + 1) * LANES)
        neg = [p.astype(F32) for p in _split3(-cs[:, h:h + 1])]
        k = k_ref[:, sl].astype(F32)
        for i in range(3):
            k = jnp.where(lane == HEAD_DIM + 3 + i, neg[i], k)
        k2_ref[:, sl] = k.astype(BF16)


def _forget_bias(small, fk_p, tc):
    b, t, w = small.shape
    blk = lambda n: pl.BlockSpec((None, tc, n), lambda i, j: (i, j, 0))
    wide = H_FOX * LANES
    return pl.pallas_call(
        _forget_bias_kernel,
        grid=(b, t // tc),
        in_specs=[blk(w), blk(wide)],
        out_specs=[blk(w), blk(wide)],
        out_shape=[jax.ShapeDtypeStruct((b, t, w), F32), jax.ShapeDtypeStruct((b, t, wide), BF16)],
        scratch_shapes=[pltpu.VMEM((1, w), F32)],
        compiler_params=_params(("parallel", "arbitrary")),
        name="forget_bias",
    )(small, fk_p)


def _scores(q, k):
    return _dot_nt(q, k)


def _flash_update(s, v, m_ref, acc_ref, idx, allow=None):
    if allow is not None:
        s = jnp.where(allow, s, NEG)
    m_old = m_ref[idx]
    m_new = jnp.maximum(m_old, jnp.max(s, axis=1, keepdims=True))
    p = jnp.exp(s - m_new)
    acc_ref[idx] = jnp.exp(m_old - m_new) * acc_ref[idx] + _dot(p.astype(BF16), v)
    m_ref[idx] = m_new


def _flash_update_t(s, vt, m_ref, acc_ref, idx, allow=None):
    if allow is not None:
        s = jnp.where(allow, s, NEG)
    m_old = m_ref[idx]
    m_new = jnp.maximum(m_old, jnp.max(s, axis=0, keepdims=True))
    p = jnp.exp(s - m_new)
    acc_ref[idx] = jnp.exp(m_old - m_new) * acc_ref[idx] + _dot(vt, p.astype(BF16))
    m_ref[idx] = m_new


def _normalise_t(acc):
    return acc[0:HEAD_DIM] * (1.0 / acc[HEAD_DIM:HEAD_DIM + 1])


def _fox_kernel(qt_ref, cum_ref, k_ref, vt_ref, o_ref, qb_ref, m_ref, acc_ref, *, tq, tk, heads):
    hg = pl.program_id(1)
    q0 = pl.program_id(2) * tq
    m_ref[...] = jnp.full(m_ref.shape, NEG, F32)
    acc_ref[...] = jnp.zeros(acc_ref.shape, F32)
    krow = lax.broadcasted_iota(jnp.int32, (tk, tq), 0)
    qpos = q0 + lax.broadcasted_iota(jnp.int32, (tk, tq), 1)
    head = lambda g: slice(g * LANES, (g + 1) * LANES)
    row = lax.broadcasted_iota(jnp.int32, (LANES, tq), 0)
    row8 = lax.broadcasted_iota(jnp.int32, (H_FOX, tq), 0)
    cum_all = cum_ref[...]
    for g in range(heads):
        cum = jnp.sum(jnp.where(row8 == hg * heads + g, cum_all, 0.0), axis=0, keepdims=True)
        q = qt_ref[head(g), :].astype(F32)
        for i, piece in enumerate(_split3(cum)):
            q = jnp.where(row == HEAD_DIM + i, piece.astype(F32), q)
        qb_ref[g] = jnp.where((row >= HEAD_DIM + 3) & (row < HEAD_DIM + 6), 1.0, q).astype(BF16)

    def scores(kt):
        k0 = pl.multiple_of(kt * tk, tk)
        return tuple(_dot(k_ref[pl.ds(k0, tk), head(g)], qb_ref[g]) for g in range(heads))

    def update(kt, s, masked):
        k0 = pl.multiple_of(kt * tk, tk)
        allow = (k0 + krow <= qpos) if masked else None
        for g in range(heads):
            _flash_update_t(s[g], vt_ref[head(g), pl.ds(k0, tk)], m_ref, acc_ref, g, allow=allow)

    def body(kt, s):
        s_next = scores(kt + 1)
        update(kt, s, False)
        return s_next

    n_full = _div(q0, tk)
    update(n_full, lax.fori_loop(0, n_full, body, scores(0)), True)
    for pr in range(heads // 2):
        pair = jnp.concatenate([_normalise_t(acc_ref[2 * pr]), _normalise_t(acc_ref[2 * pr + 1])], axis=0)
        o_ref[:, pr * LANES:(pr + 1) * LANES] = pair.T


def _fox_attention(fq_t, cum_t, fk_p, fv_t, tq, tk):
    b, t, _ = fk_p.shape
    g = FOX_HEADS_PER_STEP
    qblk = pl.BlockSpec((None, g * LANES, tq), lambda i, hg, qi: (i, hg, qi))
    cblk = pl.BlockSpec((None, H_FOX, tq), lambda i, hg, qi: (i, 0, qi))
    kblk = pl.BlockSpec((None, t, g * LANES), lambda i, hg, qi: (i, 0, hg))
    vblk = pl.BlockSpec((None, g * LANES, t), lambda i, hg, qi: (i, hg, 0))
    oblk = pl.BlockSpec((None, tq, g * HEAD_DIM), lambda i, hg, qi: (i, qi, hg))
    return pl.pallas_call(
        functools.partial(_fox_kernel, tq=tq, tk=tk, heads=g),
        grid=(b, H_FOX // g, t // tq),
        in_specs=[qblk, cblk, kblk, vblk],
        out_specs=oblk,
        out_shape=jax.ShapeDtypeStruct((b, t, W_FOX), F32),
        scratch_shapes=[pltpu.VMEM((g, LANES, tq), BF16), pltpu.VMEM((g, 1, tq), F32),
                        pltpu.VMEM((g, LANES, tq), F32)],
        compiler_params=_params(("parallel", "parallel", "arbitrary")),
        name="fox_prompt",
    )(fq_t, cum_t, fk_p, fv_t)


def _compress_hidden(load_rows, n_ch, wj_ref, pe_ref, wpe_ref):
    acc_k = jnp.zeros((n_ch, 2 * W_KV), F32)
    acc_v = jnp.zeros((n_ch, 2 * W_KV), F32)
    for j in range(CMP_STRIDE):
        xk, xv = load_rows(j)
        acc_k = acc_k + _dot(xk.astype(BF16), wj_ref[j, 0:W_KV, :])
        acc_v = acc_v + _dot(xv.astype(BF16), wj_ref[j, W_KV:2 * W_KV, :])
    first = jnp.concatenate([acc_k[:, 0:W_KV], acc_v[:, 0:W_KV]], axis=1)
    second = jnp.concatenate([acc_k[:, W_KV:], acc_v[:, W_KV:]], axis=1)
    second = pltpu.roll(second, n_ch - 1, axis=0)
    pe_term = _dot(pe_ref[...], wpe_ref[...])[0:1, :]
    return jax.nn.gelu(first + second + pe_term)


def _compress_kernel(xk_ref, xv_ref, wj_ref, pe_ref, wpe_ref, w2p_ref, kc_ref, vc_ref, *, n_ch):
    rows = lambda j: (xk_ref[pl.ds(j, n_ch, stride=CMP_STRIDE), :], xv_ref[pl.ds(j, n_ch, stride=CMP_STRIDE), :])
    hid = _compress_hidden(rows, n_ch, wj_ref, pe_ref, wpe_ref).astype(BF16)
    out = _dot(hid, w2p_ref[...])
    kc_ref[...] = out[:, 0:H_KV * LANES].astype(BF16)
    for h in range(H_KV):
        vc = out[:, (H_KV + h) * LANES:(H_KV + h + 1) * LANES]
        vc_ref[h * LANES:(h + 1) * LANES, :] = jnp.where(_lo_half(n_ch), vc, 1.0).T.astype(BF16)


def _compress(nkv4, cw):
    b, t, _ = nkv4.shape
    n_ch = t // CMP_STRIDE
    out = pl.BlockSpec((None, n_ch, H_KV * LANES), lambda i: (i, 0, 0))
    return pl.pallas_call(
        functools.partial(_compress_kernel, n_ch=n_ch),
        grid=(b,),
        in_specs=[pl.BlockSpec((None, t, W_KV), lambda i: (i, 0, 0)),
                  pl.BlockSpec((None, t, W_KV), lambda i: (i, 0, 1))] + [_const_spec(w.shape) for w in cw],
        out_specs=[out, pl.BlockSpec((None, H_KV * LANES, n_ch), lambda i: (i, 0, 0))],
        out_shape=[jax.ShapeDtypeStruct((b, n_ch, H_KV * LANES), BF16),
                   jax.ShapeDtypeStruct((b, H_KV * LANES, n_ch), BF16)],
        compiler_params=_params(("parallel",)),
        name="compress_prompt",
    )(nkv4, nkv4, *cw)


def _overlap(n_ch, n_slc):
    n = lax.broadcasted_iota(jnp.int32, (n_ch, LANES), 0)
    j = lax.broadcasted_iota(jnp.int32, (n_ch, LANES), 1)
    hit = ((n * CMP_STRIDE <= j * SLC_LEN + SLC_LEN - 1) & (n * CMP_STRIDE + CMP_LEN - 1 >= j * SLC_LEN)
           & (n < n_ch - 1) & (j < n_slc))
    return hit.astype(BF16)


def _overlap_t(n_ch, n_slc):
    j = lax.broadcasted_iota(jnp.int32, (LANES, n_ch), 0)
    n = lax.broadcasted_iota(jnp.int32, (LANES, n_ch), 1)
    hit = ((n * CMP_STRIDE <= j * SLC_LEN + SLC_LEN - 1) & (n * CMP_STRIDE + CMP_LEN - 1 >= j * SLC_LEN)
           & (n < n_ch - 1) & (j < n_slc))
    return hit.astype(BF16)


def _block_scores(imp, pos, n_slc, axis):
    blk = lax.broadcasted_iota(jnp.int32, imp.shape, axis)
    qblk = _div(pos, SLC_LEN)
    forced = (blk == 0) | (blk == qblk) | (blk == qblk - 1)
    valid = blk * SLC_LEN <= pos
    score = jnp.where(valid, jnp.where(forced, FORCE_SCORE, imp), -1.0)
    return jnp.where(blk < n_slc, score, -2.0)


def _count_beats_lanes(score, n_slc):
    blk = lax.broadcasted_iota(jnp.int32, score.shape, 1)
    cnt = jnp.zeros(score.shape, jnp.int32)
    for i in range(n_slc):
        si = score[:, i:i + 1]
        cnt = cnt + jnp.where((si > score) | ((si == score) & (blk > i)), 1, 0)
    return cnt


def _count_beats_rows(score, lo, hi, n_rows):
    out = []
    for v in range(n_rows // 8):
        s_v = score[8 * v:8 * v + 8, :]
        blk = 8 * v + lax.broadcasted_iota(jnp.int32, s_v.shape, 0)
        cnt = jnp.zeros(s_v.shape, jnp.int32)
        for i in range(lo, hi):
            si = score[i:i + 1, :]
            if 8 * v + 7 < i:
                beats = si > s_v
            elif 8 * v > i:
                beats = si >= s_v
            else:
                beats = (si > s_v) | ((si == s_v) & (blk > i))
            cnt = cnt + jnp.where(beats, 1, 0)
        out.append(cnt)
    return jnp.concatenate(out, axis=0)


def _nsa_t_kernel(qt_ref, gate_ref, kc_ref, vct_ref, sk_ref, svt_ref, wk_ref, wvt_ref, o_ref,
                  q4_ref, q4s_ref, cnt_ref, m_ref, acc_ref, *, tq, tk, n_ch, n_slc, wlen):
    h = pl.program_id(1)
    q0 = pl.program_id(2) * tq
    g = GQA_GROUP
    w = g * tq
    n_sel = min(SLC_TOPN, n_slc)
    pos4 = q0 + (lax.broadcasted_iota(jnp.int32, (1, w), 1) & (tq - 1))
    for i in range(g):
        q4_ref[:, i * tq:(i + 1) * tq] = qt_ref[i * LANES:(i + 1) * LANES, :]
    q4 = q4_ref[...]

    n_idx = lax.broadcasted_iota(jnp.int32, (n_ch, w), 0)
    ok = (n_idx * CMP_STRIDE + CMP_LEN - 1 <= pos4) & (n_idx < n_ch - 1)
    sc = jnp.where(ok, _dot(kc_ref[...], q4), NEG)
    e = jnp.where(ok, jnp.exp(sc - jnp.max(sc, axis=0, keepdims=True)), 0.0)
    lc = jnp.sum(e, axis=0, keepdims=True)
    a = e * (1.0 / jnp.where(lc > 0.0, lc, 1.0))
    a_hi = a.astype(BF16)
    a_lo = (a - a_hi.astype(F32)).astype(BF16)
    o_cmp = _dot(vct_ref[...], a_hi)
    ov_t = _overlap_t(n_ch, n_slc)
    imp4 = _dot(ov_t, a_hi) + _dot(ov_t, a_lo)
    imp_t = sum(imp4[:, i * tq:(i + 1) * tq] for i in range(g))

    score = _block_scores(imp_t, q0 + lax.broadcasted_iota(jnp.int32, (1, tq), 1), n_slc, 0)
    n_valid = _div(q0 + tq - 1, SLC_LEN) + 1
    n_rows = cnt_ref.shape[0]
    cnt_ref[...] = jnp.zeros(cnt_ref.shape, jnp.int32)
    for b0 in range(0, n_slc, 8):
        @pl.when((b0 < n_valid) & (n_valid > n_sel))
        def _():
            cnt_ref[...] += _count_beats_rows(score, b0, min(b0 + 8, n_slc), n_rows)
    sel_t = (cnt_ref[...] < n_sel) & (lax.broadcasted_iota(jnp.int32, (n_rows, tq), 0) < n_slc)
    bias = jnp.where(sel_t, 0.0, NEG)
    if n_rows < HEAD_DIM:
        bias = jnp.concatenate([bias, jnp.full((HEAD_DIM - n_rows, tq), NEG, F32)], axis=0)
    q4s_ref[0:HEAD_DIM, :] = q4[0:HEAD_DIM]
    for i in range(g):
        q4s_ref[HEAD_DIM:2 * HEAD_DIM, i * tq:(i + 1) * tq] = bias.astype(BF16)

    m_ref[...] = jnp.full(m_ref.shape, NEG, F32)
    acc_ref[...] = jnp.zeros(acc_ref.shape, F32)
    krow = lax.broadcasted_iota(jnp.int32, (tk, w), 0)

    def scores(kt):
        return _dot(sk_ref[pl.ds(pl.multiple_of(kt * tk, tk), tk), :], q4s_ref[...])

    def update(kt, s, masked):
        k0 = pl.multiple_of(kt * tk, tk)
        allow = (k0 + krow <= pos4) if masked else None
        _flash_update_t(s, svt_ref[:, pl.ds(k0, tk)], m_ref, acc_ref, 0, allow=allow)

    def body(kt, s):
        s_next = scores(kt + 1)
        update(kt, s, False)
        return s_next

    n_full = _div(q0, tk)
    update(n_full, lax.fori_loop(0, n_full, body, scores(0)), True)
    o_slc = _normalise_t(acc_ref[0])

    ws = pl.multiple_of(jnp.maximum(q0 + tq - wlen, 0), tq)
    dist = pos4 - (ws + lax.broadcasted_iota(jnp.int32, (wlen, w), 0))
    okw = (dist >= 0) & (dist < WINDOW)
    sw = jnp.where(okw, _dot(wk_ref[pl.ds(ws, wlen), :], q4), NEG)
    pw = jnp.exp(sw - jnp.max(sw, axis=0, keepdims=True))
    o_win = _normalise_t(_dot(wvt_ref[:, pl.ds(ws, wlen)], pw.astype(BF16)))

    gates = gate_ref[...]
    grow = lax.broadcasted_iota(jnp.int32, (LANES, tq), 0)
    vals = []
    for i in range(g):
        r0 = GATE0 + 3 * (g * h + i)
        gc = [jnp.sum(jnp.where(grow == r0 + c, gates, 0.0), axis=0, keepdims=True) for c in range(3)]
        sl = slice(i * tq, (i + 1) * tq)
        vals.append(gc[0] * o_cmp[0:HEAD_DIM, sl] + gc[1] * o_slc[:, sl] + gc[2] * o_win[:, sl])
    for pr in range(g // 2):
        o_ref[:, pr * LANES:(pr + 1) * LANES] = jnp.concatenate([vals[2 * pr], vals[2 * pr + 1]], axis=0).T


def _nsa_attention_t(nq_t, small_t, kc_p, vc_t, nk_p, nv_t, tq, tk):
    b, _, t = nq_t.shape
    n_ch = kc_p.shape[1]
    n_slc = -(-t // SLC_LEN)
    assert n_slc <= HEAD_DIM, "the block mask rides in the 64 spare query rows"
    wlen = min(t, WINDOW + tq)
    g = GQA_GROUP
    kblk = lambda c: pl.BlockSpec((None, t, LANES), lambda i, h, qi: (i, 0, c * H_KV + h))
    vblk = lambda c: pl.BlockSpec((None, LANES, t), lambda i, h, qi: (i, c * H_KV + h, 0))
    return pl.pallas_call(
        functools.partial(_nsa_t_kernel, tq=tq, tk=tk, n_ch=n_ch, n_slc=n_slc, wlen=wlen),
        grid=(b, H_KV, t // tq),
        in_specs=[pl.BlockSpec((None, g * LANES, tq), lambda i, h, qi: (i, h, qi)),
                  pl.BlockSpec((None, LANES, tq), lambda i, h, qi: (i, 0, qi)),
                  pl.BlockSpec((None, n_ch, LANES), lambda i, h, qi: (i, 0, h)),
                  pl.BlockSpec((None, LANES, n_ch), lambda i, h, qi: (i, h, 0)),
                  kblk(0), vblk(0), kblk(1), vblk(1)],
        out_specs=pl.BlockSpec((None, tq, g * HEAD_DIM), lambda i, h, qi: (i, qi, h)),
        out_shape=jax.ShapeDtypeStruct((b, t, W_NSA), F32),
        scratch_shapes=[pltpu.VMEM((LANES, g * tq), BF16), pltpu.VMEM((LANES, g * tq), BF16),
                        pltpu.VMEM((-(-n_slc // 8) * 8, tq), jnp.int32),
                        pltpu.VMEM((1, 1, g * tq), F32), pltpu.VMEM((1, LANES, g * tq), F32)],
        compiler_params=_params(("parallel", "parallel", "arbitrary")),
        name="nsa_prompt",
    )(nq_t, small_t, kc_p, vc_t, nk_p, nv_t, nk_p, nv_t)


def _nsa_kernel(q_ref, gate_ref, kc_ref, vc_ref, sk_ref, sv_ref, wk_ref, wv_ref, o_ref,
                q4_ref, q4s_ref, cnt_ref, m_ref, acc_ref, *, tq, tk, n_ch, n_slc, wlen):
    h = pl.program_id(1)
    q0 = pl.program_id(2) * tq
    g = GQA_GROUP
    n_sel = min(SLC_TOPN, n_slc)
    lane = lax.broadcasted_iota(jnp.int32, (tq, LANES), 1)
    lo = lane < HEAD_DIM
    pos = q0 + lax.broadcasted_iota(jnp.int32, (tq, 1), 0)
    for i in range(g):
        q4_ref[i * tq:(i + 1) * tq, :] = q_ref[:, i * LANES:(i + 1) * LANES]
    q4 = q4_ref[...]

    sc = _dot_nt(q4, kc_ref[...]).reshape(g, tq, n_ch)
    n_idx = lax.broadcasted_iota(jnp.int32, (tq, n_ch), 1)
    ok = ((n_idx * CMP_STRIDE + CMP_LEN - 1 <= pos) & (n_idx < n_ch - 1))[None]
    sc = jnp.where(ok, sc, NEG)
    e = jnp.where(ok, jnp.exp(sc - jnp.max(sc, axis=-1, keepdims=True)), 0.0)
    lc = jnp.sum(e, axis=-1, keepdims=True)
    a = (e * (1.0 / jnp.where(lc > 0.0, lc, 1.0))).reshape(g * tq, n_ch)
    a_hi = a.astype(BF16)
    a_lo = (a - a_hi.astype(F32)).astype(BF16)
    o_cmp = _dot(a_hi, vc_ref[...]).reshape(g, tq, LANES)
    ov_t = _overlap_t(n_ch, n_slc)
    imp4 = _dot_nt(ov_t, a_hi) + _dot_nt(ov_t, a_lo)
    imp_t = sum(imp4[:, i * tq:(i + 1) * tq] for i in range(g))

    score = _block_scores(imp_t, q0 + lax.broadcasted_iota(jnp.int32, (1, tq), 1), n_slc, 0)
    n_valid = _div(q0 + tq - 1, SLC_LEN) + 1
    n_rows = cnt_ref.shape[0]
    cnt_ref[...] = jnp.zeros(cnt_ref.shape, jnp.int32)
    for b0 in range(0, n_slc, 8):
        @pl.when((b0 < n_valid) & (n_valid > n_sel))
        def _():
            cnt_ref[...] += _count_beats_rows(score, b0, min(b0 + 8, n_slc), n_rows)
    sel_t = (cnt_ref[...] < n_sel) & (lax.broadcasted_iota(jnp.int32, (n_rows, tq), 0) < n_slc)
    bias_t = jnp.concatenate([jnp.where(sel_t, 0.0, NEG), jnp.full((LANES - n_rows, tq), NEG, F32)], axis=0)
    sel_bias = pltpu.roll(bias_t.T, HEAD_DIM, axis=1)
    for i in range(g):
        qi = q_ref[:, i * LANES:(i + 1) * LANES].astype(F32)
        q4s_ref[i * tq:(i + 1) * tq, :] = jnp.where(lo, qi, sel_bias).astype(BF16)

    m_ref[...] = jnp.full(m_ref.shape, NEG, F32)
    acc_ref[...] = jnp.zeros(acc_ref.shape, F32)

    pos4 = q0 + (lax.broadcasted_iota(jnp.int32, (g * tq, 1), 0) & (tq - 1))

    def scores(kt):
        return _scores(q4s_ref[...], sk_ref[pl.ds(pl.multiple_of(kt * tk, tk), tk), :])

    def update(kt, sm, masked):
        k0 = pl.multiple_of(kt * tk, tk)
        allow = (k0 + lax.broadcasted_iota(jnp.int32, (g * tq, tk), 1) <= pos4) if masked else None
        _flash_update(sm, sv_ref[pl.ds(k0, tk), :], m_ref, acc_ref, 0, allow=allow)

    def body(kt, sm):
        sm_next = scores(kt + 1)
        update(kt, sm, False)
        return sm_next

    n_full = _div(q0, tk)
    update(n_full, lax.fori_loop(0, n_full, body, scores(0)), True)
    o_slc = _normalise(acc_ref[0]).reshape(g, tq, LANES)

    ws = pl.multiple_of(jnp.maximum(q0 + tq - wlen, 0), tq)
    dist = pos4 - (ws + lax.broadcasted_iota(jnp.int32, (g * tq, wlen), 1))
    okw = (dist >= 0) & (dist < WINDOW)
    sw = jnp.where(okw, _dot_nt(q4, wk_ref[pl.ds(ws, wlen), :]), NEG)
    pw = jnp.exp(sw - jnp.max(sw, axis=-1, keepdims=True))
    o_win = _normalise(_dot(pw.astype(BF16), wv_ref[pl.ds(ws, wlen), :])).reshape(g, tq, LANES)

    gates = gate_ref[...]
    vals = []
    for i in range(g):
        col0 = GATE0 + 3 * (g * h + i)
        gc = [jnp.sum(jnp.where(lane == col0 + c, gates, 0.0), axis=1, keepdims=True) for c in range(3)]
        vals.append(gc[0] * o_cmp[i] + gc[1] * o_slc[i] + gc[2] * o_win[i])
    for pr in range(g // 2):
        o_ref[:, pr * LANES:(pr + 1) * LANES] = _pair_up(vals[2 * pr], vals[2 * pr + 1])


def _nsa_attention(nq_p, small, kc_p, vc_p, nkv_p, tq, tk):
    b, t, _ = nq_p.shape
    n_ch = kc_p.shape[1]
    n_slc = -(-t // SLC_LEN)
    assert n_slc <= LANES - HEAD_DIM, "the block mask rides in the 64 spare query lanes"
    wlen = min(t, WINDOW + tq)
    g = GQA_GROUP
    cblk = pl.BlockSpec((None, n_ch, LANES), lambda i, h, qi: (i, 0, h))
    lane_blk = lambda c: pl.BlockSpec((None, t, LANES), lambda i, h, qi: (i, 0, c * H_KV + h))
    return pl.pallas_call(
        functools.partial(_nsa_kernel, tq=tq, tk=tk, n_ch=n_ch, n_slc=n_slc, wlen=wlen),
        grid=(b, H_KV, t // tq),
        in_specs=[pl.BlockSpec((None, tq, g * LANES), lambda i, h, qi: (i, qi, h)),
                  pl.BlockSpec((None, tq, LANES), lambda i, h, qi: (i, qi, 0)), cblk, cblk,
                  lane_blk(0), lane_blk(1), lane_blk(2), lane_blk(3)],
        out_specs=pl.BlockSpec((None, tq, g * HEAD_DIM), lambda i, h, qi: (i, qi, h)),
        out_shape=jax.ShapeDtypeStruct((b, t, W_NSA), F32),
        scratch_shapes=[pltpu.VMEM((g * tq, LANES), BF16), pltpu.VMEM((g * tq, LANES), BF16),
                        pltpu.VMEM((-(-n_slc // 8) * 8, tq), jnp.int32),
                        pltpu.VMEM((1, g * tq, 1), F32), pltpu.VMEM((1, g * tq, LANES), F32)],
        compiler_params=_params(("parallel", "parallel", "arbitrary")),
        name="nsa_prompt",
    )(nq_p, small, kc_p, vc_p, nkv_p, nkv_p, nkv_p, nkv_p)


def _page_copies(pt_ref, step, slot, n_pages, page, streams, group):
    out = []
    for u in range(group):
        for pg in range(n_pages):
            idx = pt_ref[(step * group + u) * n_pages + pg]
            for hbm, buf, sem, on_lanes in streams:
                dst = buf.at[slot, u, :, pl.ds(pg * page, page)] if on_lanes else buf.at[slot, u, pg]
                out.append(pltpu.make_async_copy(hbm.at[idx], dst, sem.at[slot]))
    return out


def _gather_pages(pt_ref, n_pages, page, streams, group=1):
    i = pl.program_id(0)
    slot = lax.rem(i, 2)

    @pl.when(i == 0)
    def _():
        for c in _page_copies(pt_ref, 0, 0, n_pages, page, streams, group):
            c.start()

    @pl.when(i + 1 < pl.num_programs(0))
    def _():
        for c in _page_copies(pt_ref, i + 1, 1 - slot, n_pages, page, streams, group):
            c.start()

    for c in _page_copies(pt_ref, i, slot, n_pages, page, streams, group):
        c.wait()
    return slot


def _head_scores(q_col, k_view, s_ref, head, rows0, n_tiles, tile):
    qb = jnp.broadcast_to(q_col, (HEAD_DIM, tile))
    for pg in range(n_tiles):
        kt = k_view[rows0:rows0 + HEAD_DIM, pg * tile:(pg + 1) * tile]
        s_ref[pg, head:head + 1, :] = jnp.sum(kt * qb, axis=0, keepdims=True)


def _head_values(p_tile, v_view, rows0, n_tiles, tile):
    acc = jnp.zeros((HEAD_DIM, tile), F32)
    for pg in range(n_tiles):
        vt = v_view[rows0:rows0 + HEAD_DIM, pg * tile:(pg + 1) * tile]
        acc = acc + vt * jnp.broadcast_to(p_tile(pg), (HEAD_DIM, tile))
    return jnp.sum(acc, axis=1, keepdims=True)


def _softmax_tiles(s, s_new):
    m = jnp.maximum(jnp.max(jnp.max(s, axis=0), axis=1, keepdims=True), s_new)
    p = jnp.exp(s - m[None])
    p_new = jnp.exp(s_new - m)
    l = jnp.sum(jnp.sum(p, axis=0), axis=1, keepdims=True) + p_new
    return p, p_new, 1.0 / l


def _softmax_rows(s, s_new, allow):
    s = jnp.where(allow, s, NEG)
    m = jnp.maximum(jnp.max(s, axis=1, keepdims=True), s_new)
    p = jnp.where(allow, jnp.exp(s - m), 0.0)
    p_new = jnp.exp(s_new - m)
    return p, p_new, 1.0 / (jnp.sum(p, axis=1, keepdims=True) + p_new)


def _col_dot(a_col, b_col, n_heads):
    return jnp.sum((a_col * b_col).reshape(n_heads, HEAD_DIM, 1), axis=1)


def _as_column(row):
    return jnp.broadcast_to(row, (LANES, row.shape[1])).T[:, 0:1]


def _as_row(col):
    return jnp.broadcast_to(col, (col.shape[0], LANES)).T[0:1, :]


def _fox_dec_kernel(pt_ref, q_ref, kvnew_ref, small_ref, kv_hbm, lf_hbm, o_ref,
                    kvbuf, lfbuf, s_ref, p_ref, sem_kv, sem_lf, *, n_pages, page):
    slot = _gather_pages(pt_ref, n_pages, page,
                         [(kv_hbm, kvbuf, sem_kv, True), (lf_hbm, lfbuf, sem_lf, False)])
    kv = kvbuf.at[slot, 0]
    q = _as_column(q_ref[...])
    kv_new = _as_column(kvnew_ref[...])
    lf_new = jnp.concatenate([_as_column(small_ref[...])[0:H_FOX]] * n_pages, axis=0)
    for h in range(H_FOX):
        _head_scores(q[h * HEAD_DIM:(h + 1) * HEAD_DIM], kv, s_ref, h, h * HEAD_DIM, n_pages, page)

    rows = n_pages * H_FOX
    lf = lfbuf[slot, 0].reshape(rows, page)
    r = lax.broadcasted_iota(jnp.int32, (page, page), 0)
    c = lax.broadcasted_iota(jnp.int32, (page, page), 1)
    later = (r > c).astype(BF16)
    hi, mid, lo = _split3(lf)
    within = _dot(hi, later) + _dot(mid, later) + _dot(lo, later)
    r = lax.broadcasted_iota(jnp.int32, (rows, rows), 0)
    c = lax.broadcasted_iota(jnp.int32, (rows, rows), 1)
    later_pages = ((c > r) & (((c - r) & (H_FOX - 1)) == 0)).astype(BF16)
    tot = jnp.broadcast_to(jnp.sum(lf, axis=1, keepdims=True), (rows, page))
    hi, mid, lo = _split3(tot)
    beyond = _dot(later_pages, hi) + _dot(later_pages, mid) + _dot(later_pages, lo)
    bias = (within + beyond + lf_new).reshape(n_pages, H_FOX, page)

    s_new = _col_dot(q, kv_new[0:W_FOX], H_FOX)
    p, p_new, inv_l = _softmax_tiles(s_ref[...] + bias, s_new)
    p_ref[...] = p
    outs = []
    for h in range(H_FOX):
        o = _head_values(lambda pg, h=h: p_ref[pg, h:h + 1, :], kv, W_FOX + h * HEAD_DIM, n_pages, page)
        v_new = kv_new[W_FOX + h * HEAD_DIM: W_FOX + (h + 1) * HEAD_DIM]
        outs.append((o + p_new[h:h + 1] * v_new) * inv_l[h:h + 1])
    o_ref[...] = _as_row(jnp.concatenate(outs, axis=0))


def _fox_decode(page_table, q_row, kv_row, small_row, cache_kvt, cache_lft):
    s, n_pages = page_table.shape
    rows, page = cache_kvt.shape[1], cache_kvt.shape[2]
    col = lambda n: pl.BlockSpec((None, 1, n), lambda i, pt: (i, 0, 0))
    anyspec = pl.BlockSpec(memory_space=pl.ANY)
    return pl.pallas_call(
        functools.partial(_fox_dec_kernel, n_pages=n_pages, page=page),
        grid_spec=pltpu.PrefetchScalarGridSpec(
            num_scalar_prefetch=1,
            grid=(s,),
            in_specs=[col(W_FOX), col(2 * W_FOX), col(LANES), anyspec, anyspec],
            out_specs=col(W_FOX),
            scratch_shapes=[pltpu.VMEM((2, 1, rows, n_pages * page), F32),
                            pltpu.VMEM((2, 1, n_pages, H_FOX, page), F32),
                            pltpu.VMEM((n_pages, H_FOX, page), F32), pltpu.VMEM((n_pages, H_FOX, page), F32),
                            pltpu.SemaphoreType.DMA((2,)), pltpu.SemaphoreType.DMA((2,))],
        ),
        out_shape=jax.ShapeDtypeStruct((s, 1, W_FOX), F32),
        compiler_params=_params(("arbitrary",)),
        name="fox_decode",
    )(page_table.reshape(-1), q_row, kv_row, small_row, cache_kvt, cache_lft)


def _nsa_dec_kernel(pt_ref, qrow_ref, nkv4_ref, wnew_ref, gate_ref, win_ref, cache_hbm,
                    wj_ref, pe_ref, wpe_ref, w2_ref, o_ref,
                    xbuf, xk_buf, xv_buf, sem, *, n_pages, page, group):
    slot = _gather_pages(pt_ref, n_pages, page, [(cache_hbm, xbuf, sem, True)], group)
    n_ch = n_pages * page // CMP_STRIDE

    r = lax.broadcasted_iota(jnp.int32, (page, page), 0)
    t = lax.broadcasted_iota(jnp.int32, (page, page), 1)
    per = page // CMP_STRIDE
    perm = (t == CMP_STRIDE * (r & (per - 1)) + _div(r, per)).astype(BF16)
    for u in range(group):
        for pg in range(n_pages):
            xt = _dot_nt(perm, xbuf[slot, u, 0:2 * W_KV, pg * page:(pg + 1) * page].astype(BF16))
            c0 = u * n_ch + pg * per
            for j in range(CMP_STRIDE):
                xk_buf[j, c0:c0 + per, :] = xt[j * per:(j + 1) * per, 0:W_KV]
                xv_buf[j, c0:c0 + per, :] = xt[j * per:(j + 1) * per, W_KV:2 * W_KV]
    hid = _compress_hidden(lambda jj: (xk_buf[jj], xv_buf[jj]), group * n_ch, wj_ref, pe_ref, wpe_ref).astype(BF16)
    for u in range(group):
        o_ref[u] = _nsa_dec_one(hid[u * n_ch:(u + 1) * n_ch], xbuf.at[slot, u], qrow_ref[u], nkv4_ref[u],
                                wnew_ref[u], gate_ref[u], win_ref.at[u], w2_ref, n_pages, page)


def _nsa_dec_one(hid, x, q_row, nkv4_new, win_new, gates, wv, w2_ref, n_pages, page):
    past_len = n_pages * page
    n_ch = past_len // CMP_STRIDE
    n_slc = past_len // SLC_LEN + 1
    n_sel = min(SLC_TOPN, n_slc)
    win_buf = wv.shape[1]
    g = GQA_GROUP
    row = lax.broadcasted_iota(jnp.int32, (H_NSA, LANES), 0)
    lane = lax.broadcasted_iota(jnp.int32, (H_NSA, LANES), 1)
    kc = _dot(hid, w2_ref[:, 0:W_KV]).astype(BF16)
    vc = _dot(hid, w2_ref[:, W_KV:2 * W_KV]).astype(BF16)

    qbd = jnp.zeros((H_NSA, LANES), F32)
    for i in range(H_NSA):
        piece = q_row[:, (i // 2) * LANES:(i // 2 + 1) * LANES]
        if (i % 2) != (i // g):
            piece = pltpu.roll(piece, HEAD_DIM, axis=1)
        qbd = jnp.where(row == i, jnp.broadcast_to(piece, (H_NSA, LANES)), qbd)
    qbd = jnp.where(_div(lane, HEAD_DIM) == _div(row, g), qbd, 0.0).astype(BF16)
    n_idx = lax.broadcasted_iota(jnp.int32, (H_NSA, n_ch), 1)
    ok = (n_idx * CMP_STRIDE + CMP_LEN - 1 <= past_len) & (n_idx < n_ch - 1)
    sc = jnp.where(ok, _dot_nt(qbd, kc), NEG)
    e = jnp.where(ok, jnp.exp(sc - jnp.max(sc, axis=1, keepdims=True)), 0.0)
    lc = jnp.sum(e, axis=1, keepdims=True)
    a = e * (1.0 / jnp.where(lc > 0.0, lc, 1.0))
    a_hi = a.astype(BF16)
    a_lo = (a - a_hi.astype(F32)).astype(BF16)
    o_cmp = _dot(a_hi, vc)
    ov = _overlap(n_ch, n_slc)
    imp8 = _dot(a_hi, ov) + _dot(a_lo, ov)
    imp = jnp.zeros((H_NSA, LANES), F32)
    for h in range(H_KV):
        tot = jnp.sum(imp8[h * g:(h + 1) * g], axis=0, keepdims=True)
        imp = jnp.where(_div(row, g) == h, jnp.broadcast_to(tot, (H_NSA, LANES)), imp)
    score = _block_scores(imp, jnp.full((H_NSA, 1), past_len, jnp.int32), n_slc, 1)
    sel = jnp.where((_count_beats_lanes(score, n_slc) < n_sel) & (lane < n_slc), 1.0, 0.0)

    per_page = page // SLC_LEN
    allow = []
    for pg in range(n_pages):
        m = jnp.zeros((H_NSA, page), F32)
        for b in range(per_page):
            blk = pg * per_page + b
            lanes_b = _div(lax.broadcasted_iota(jnp.int32, (H_NSA, page), 1), SLC_LEN) == b
            m = jnp.where(lanes_b, jnp.broadcast_to(sel[:, blk:blk + 1], (H_NSA, page)), m)
        allow.append(m > 0.5)
    allow = jnp.concatenate(allow, axis=1)
    qf = qbd.astype(F32)
    s_new = jnp.sum(qf * nkv4_new[:, 2 * W_KV:3 * W_KV], axis=1, keepdims=True)
    p, p_new, inv_l = _softmax_rows(_dot(qbd, x[2 * W_KV:3 * W_KV, :].astype(BF16)), s_new, allow)
    o_slc = (_dot_nt(p.astype(BF16), x[3 * W_KV:4 * W_KV, :].astype(BF16))
             + p_new.astype(BF16).astype(F32) * nkv4_new[:, 3 * W_KV:4 * W_KV].astype(BF16).astype(F32)) * inv_l

    slot_idx = lax.broadcasted_iota(jnp.int32, (H_NSA, win_buf), 1)
    sw_new = jnp.sum(qf * win_new[:, 0:W_KV], axis=1, keepdims=True)
    pw, pw_new, inv_lw = _softmax_rows(_dot(qbd, wv[0:W_KV, :].astype(BF16)), sw_new, (win_buf - slot_idx) < WINDOW)
    o_win = (_dot_nt(pw.astype(BF16), wv[W_KV:2 * W_KV, :].astype(BF16))
             + pw_new.astype(BF16).astype(F32) * win_new[:, W_KV:2 * W_KV].astype(BF16).astype(F32)) * inv_lw

    gate_rows = jnp.broadcast_to(gates, (H_NSA, LANES))
    gc = [jnp.sum(jnp.where(lane == GATE0 + 3 * row + c, gate_rows, 0.0), axis=1, keepdims=True) for c in range(3)]
    val = gc[0] * o_cmp + gc[1] * o_slc + gc[2] * o_win
    pieces = []
    for i in range(H_NSA):
        piece = val[i:i + 1]
        if (i % 2) != (i // g):
            piece = pltpu.roll(piece, HEAD_DIM, axis=1)
        pieces.append(piece)
    lo = lax.broadcasted_iota(jnp.int32, (1, LANES), 1) < HEAD_DIM
    return jnp.concatenate([jnp.where(lo, pieces[2 * pr], pieces[2 * pr + 1]) for pr in range(H_NSA // 2)], axis=1)


def _nsa_decode(page_table, q_row, nkv4_row, win_row, small, win_t, cache_t, cw):
    s, n_pages = page_table.shape
    rows, page = cache_t.shape[1], cache_t.shape[2]
    past_len = n_pages * page
    n_ch = past_len // CMP_STRIDE
    win_buf = win_t.shape[2]
    group = next(g for g in (4, 2, 1) if s % g == 0)
    rowspec = lambda n: pl.BlockSpec((group, 1, n), lambda i, pt: (i, 0, 0))
    const = lambda shape: pl.BlockSpec(shape, lambda i, pt: (0,) * len(shape), pipeline_mode=pl.Buffered(1))
    return pl.pallas_call(
        functools.partial(_nsa_dec_kernel, n_pages=n_pages, page=page, group=group),
        grid_spec=pltpu.PrefetchScalarGridSpec(
            num_scalar_prefetch=1,
            grid=(s // group,),
            in_specs=[rowspec(W_NSA), rowspec(4 * W_KV), rowspec(2 * W_KV), rowspec(LANES),
                      pl.BlockSpec((group, 2 * W_KV, win_buf), lambda i, pt: (i, 0, 0)),
                      pl.BlockSpec(memory_space=pl.ANY)] + [const(w.shape) for w in cw],
            out_specs=rowspec(W_NSA),
            scratch_shapes=[pltpu.VMEM((2, group, rows, past_len), F32),
                            pltpu.VMEM((CMP_STRIDE, group * n_ch, W_KV), F32),
                            pltpu.VMEM((CMP_STRIDE, group * n_ch, W_KV), F32),
                            pltpu.SemaphoreType.DMA((2,))],
        ),
        out_shape=jax.ShapeDtypeStruct((s, 1, W_NSA), F32),
        compiler_params=_params(("arbitrary",)),
        name="nsa_decode",
    )(page_table.reshape(-1), q_row, nkv4_row, win_row, small, win_t, cache_t, *cw)


def _rot_cols(w):
    d, n = w.shape
    w = w.reshape(d, n // HEAD_DIM, 2, HEAD_DIM // 2)
    return jnp.stack([-w[:, :, 1], w[:, :, 0]], axis=2).reshape(d, n)


def _prep_projection(w_in, b_f):
    c = [0, W_FOX, 2 * W_FOX, 3 * W_FOX, 3 * W_FOX + H_FOX, 3 * W_FOX + H_FOX + W_NSA,
         3 * W_FOX + H_FOX + W_NSA + 6 * W_KV]
    fq, fk, fv, ff, nq, kv = (w_in[:, c[i]:c[i + 1]] for i in range(6))
    gt = w_in[:, c[6]:]
    ks = jnp.concatenate([kv[:, br * 2 * W_KV: br * 2 * W_KV + W_KV] for br in range(3)], axis=1)
    w_big = jnp.concatenate([fq, fk, fv, nq, kv, _rot_cols(nq), _rot_cols(ks)], axis=1).astype(BF16)
    d = w_in.shape[0]
    w_small = jnp.concatenate([ff, gt, jnp.zeros((d, LANES - N_SMALL), w_in.dtype)], axis=1).astype(BF16)
    b_small = jnp.concatenate([b_f.astype(F32), jnp.zeros((LANES - H_FOX,), F32)])[None, :]
    return w_big, w_small, b_small


def _prep_compress(wk1, wk2, pek, wv1, wv2, pev):
    ratio = CMP_LEN // CMP_STRIDE
    eye = jnp.eye(4, dtype=F32)

    def blocks(w):
        return w.reshape(ratio, CMP_STRIDE, HEAD_DIM, wk1.shape[1])

    per_head = lambda w: jnp.einsum("rjde,hk->jhdrke", blocks(w), jnp.eye(H_KV, dtype=F32)).reshape(
        CMP_STRIDE, W_KV, ratio * W_KV)
    wj = jnp.concatenate([per_head(wk1), per_head(wv1)], axis=1).astype(BF16)
    pe = jnp.concatenate([pek.reshape(-1), pev.reshape(-1)])
    pe = jnp.broadcast_to(pe[None, :], (8, pe.shape[0])).astype(BF16)
    zero = jnp.zeros_like(wk1)
    wpe = jnp.concatenate([jnp.concatenate([wk1, wk1, zero, zero], axis=1),
                           jnp.concatenate([zero, zero, wv1, wv1], axis=1)], axis=0).astype(BF16)
    w2 = jnp.einsum("gde,gh->gdhe", jnp.stack([wk2, wk2, wv2, wv2]), eye)
    w2_pad = jnp.concatenate([w2, jnp.zeros_like(w2)], axis=3)
    w2 = w2.reshape(4 * HEAD_DIM, 4 * HEAD_DIM).astype(BF16)
    w2_pad = w2_pad.reshape(4 * HEAD_DIM, 4 * LANES).astype(BF16)
    return (wj, pe, wpe), w2, w2_pad


def _rope_tables(pos):
    half = HEAD_DIM // 2
    inv = ROPE_THETA ** (-jnp.arange(half, dtype=F32) / half)
    ang = pos.astype(F32)[:, None] * inv[None, :]
    reps = LANES // half
    return jnp.tile(jnp.cos(ang), (1, reps)), jnp.tile(jnp.sin(ang), (1, reps))


def _row_tile(n, cap):
    t = min(n, cap)
    while n % t:
        t //= 2
    return t


def kernel(x_prompt, x_sample, cache_fox_kv, cache_fox_logf, cache_nsa_kv, state_nsa_win_kv, page_table,
           g_ffn1_pre, w_ffn1_gate, w_ffn1_up, w_ffn1_down, g_ffn1_post, g_mix_pre, w_in, b_fox_f,
           w_cmpk_1, w_cmpk_2, pe_cmpk, w_cmpv_1, w_cmpv_2, pe_cmpv, g_fox_out, g_nsa_out, w_out,
           g_mix_post, g_ffn2_pre, w_ffn2_gate, w_ffn2_up, w_ffn2_down, g_ffn2_post):
    depth = w_in.shape[0]
    b, t, d = x_prompt.shape
    s, dec_seq, _ = x_sample.shape
    assert dec_seq == 1, "the sample group decodes one token per sequence"
    page = cache_fox_kv.shape[2]
    n_pages = page_table.shape[1]
    past_len = n_pages * page
    assert t % LANES == 0 and page % SLC_LEN == 0
    page_table = page_table.astype(jnp.int32)

    tm_p = _row_tile(t, 512)
    cos_p, sin_p = _rope_tables(jnp.arange(t, dtype=jnp.int32))
    cos_s, sin_s = _rope_tables(jnp.full((s,), past_len, jnp.int32))
    row = lambda v: v.astype(F32)[None, :]
    to_rows = lambda c: jnp.transpose(c, (0, 2, 3, 4, 1)).reshape(c.shape[0], -1, c.shape[1])

    yp = x_prompt.reshape(b * t, d)
    ys = x_sample.reshape(s, d)
    outs = [[] for _ in range(8)]
    for l in range(depth):
        ffn1 = (row(g_ffn1_pre[l]), w_ffn1_gate[l].astype(BF16), w_ffn1_up[l].astype(BF16),
                w_ffn1_down[l].astype(BF16), row(g_ffn1_post[l]))
        ffn2 = (row(g_ffn2_pre[l]), w_ffn2_gate[l].astype(BF16), w_ffn2_up[l].astype(BF16),
                w_ffn2_down[l].astype(BF16), row(g_ffn2_post[l]))
        w_big, w_small, b_small = _prep_projection(w_in[l], b_fox_f[l])
        cw, w2, w2_pad = _prep_compress(w_cmpk_1[l], w_cmpk_2[l], pe_cmpk[l], w_cmpv_1[l], w_cmpv_2[l], pe_cmpv[l])
        merge = (row(g_fox_out[l]), row(g_nsa_out[l]), w_out[l].astype(BF16), row(g_mix_post[l]))

        hp = _half_ffn(yp, *ffn1, tm_p)
        small, fkv_t, nkv4_t, win_t, fq_t, fv_t, nq_t, nv_t, cmp, fk_p, nk_p = _project(
            hp, row(g_mix_pre[l]), w_big, w_small, b_small, cos_p, sin_p, tm_p, t // tm_p, True)
        small3 = small.reshape(b, t, LANES)
        per_head = lambda x: x.reshape(b, t, x.shape[1])
        cum, fk_b = _forget_bias(small3, per_head(fk_p), _row_tile(t, 512))
        o_fox = _fox_attention(fq_t, jnp.swapaxes(cum[:, :, :H_FOX], 1, 2), fk_b, fv_t,
                               _row_tile(t, 512), _row_tile(t, 512))
        kc_p, vc_t = _compress(cmp.reshape(b, t, 2 * W_KV), cw + (w2_pad,))
        o_nsa = _nsa_attention_t(nq_t, jnp.swapaxes(small3, 1, 2), kc_p, vc_t, per_head(nk_p), nv_t,
                                 _row_tile(t, 256), _row_tile(t, 512))
        yp = _merge_ffn(hp, o_fox.reshape(b * t, W_FOX), o_nsa.reshape(b * t, W_NSA), *merge, *ffn2, tm_p)
        keep = min(WINDOW, t)
        tokens_first = lambda x, *dims: jnp.transpose(x.reshape(b, *dims, x.shape[2]), (0, 4, 1, 2, 3))
        outs[0].append(tokens_first(fkv_t, 2, H_FOX, HEAD_DIM))
        outs[1].append(small3[:, :, :H_FOX])
        outs[2].append(tokens_first(nkv4_t, 4, H_KV, HEAD_DIM))
        outs[3].append(tokens_first(win_t[:, :, t - keep:], 2, H_KV, HEAD_DIM))

        hs = _half_ffn(ys, *ffn1, s)
        small, fkv, nkv4, win, fq, nq = _project(
            hs, row(g_mix_pre[l]), w_big, w_small, b_small, cos_s, sin_s, s, 1, False)
        as_rows = lambda x: x.reshape(s, 1, x.shape[1])
        o_fox = _fox_decode(page_table, as_rows(fq), as_rows(fkv), as_rows(small),
                            to_rows(cache_fox_kv[l]), jnp.swapaxes(cache_fox_logf[l], 1, 2))
        o_nsa = _nsa_decode(page_table, as_rows(nq), as_rows(nkv4), as_rows(win), as_rows(small),
                            to_rows(state_nsa_win_kv[l]), to_rows(cache_nsa_kv[l]), cw + (w2,))
        ys = _merge_ffn(hs, o_fox.reshape(s, W_FOX), o_nsa.reshape(s, W_NSA), *merge, *ffn2, s)
        keep = min(WINDOW, past_len + 1)
        kw_all = jnp.concatenate([state_nsa_win_kv[l], win.reshape(s, 1, 2, H_KV, HEAD_DIM)], axis=1)
        outs[4].append(fkv.reshape(s, 1, 2, H_FOX, HEAD_DIM))
        outs[5].append(small[:, :H_FOX].reshape(s, 1, H_FOX))
        outs[6].append(nkv4.reshape(s, 1, 4, H_KV, HEAD_DIM))
        outs[7].append(kw_all[:, kw_all.shape[1] - keep:])

    stacked = [jnp.stack(o, axis=0) for o in outs]
    return (yp.reshape(b, t, d), ys.reshape(s, 1, d), *stacked)
```

```python
import functools

import jax
import jax.numpy as jnp
from jax import lax
from jax.experimental import pallas as pl
from jax.experimental.pallas import tpu as pltpu

HEAD_DIM = 64
H_FOX = 8
H_NSA = 8
H_KV = 2
GQA_GROUP = H_NSA // H_KV
W_FOX = H_FOX * HEAD_DIM
W_NSA = H_NSA * HEAD_DIM
W_KV = H_KV * HEAD_DIM
CMP_STRIDE = 16
CMP_LEN = 32
SLC_LEN = 64
SLC_TOPN = 16
WINDOW = 512
ROPE_THETA = 10000.0
EPS = 1e-6
NEG = -1e30
FORCE_SCORE = 1e4
N_SMALL = H_FOX + 3 * H_NSA
GATE0 = H_FOX

LANES = 128
MXU_N = 256
VMEM_LIMIT = 56 * 1024 * 1024
FOX_HEADS_PER_STEP = 4

F32 = jnp.float32
BF16 = jnp.bfloat16


def _dot(a, b):
    return jnp.dot(a, b, preferred_element_type=F32)


def _dot_nt(a, b):
    return lax.dot_general(a, b, (((1,), (1,)), ((), ())), preferred_element_type=F32)


def _div(x, n):
    assert n & (n - 1) == 0
    return lax.shift_right_logical(x, jnp.int32(n.bit_length() - 1))


def _split3(x):
    hi = x.astype(BF16)
    r1 = x - hi.astype(F32)
    mid = r1.astype(BF16)
    lo = (r1 - mid.astype(F32)).astype(BF16)
    return hi, mid, lo


def _rms(x, g):
    return x * lax.rsqrt(jnp.mean(x * x, axis=-1, keepdims=True) + EPS) * g


def _ff_chunks(d_ff):
    step = 6 * MXU_N
    return tuple((c, min(c + step, d_ff)) for c in range(0, d_ff, step))


def _ffn_core(x, gpre, wg_ref, wu_ref, wd_ref, gpost, chunks):
    xn = _rms(x, gpre).astype(BF16)
    acc = jnp.zeros(x.shape, F32)
    for c0, c1 in chunks:
        g = _dot(xn, wg_ref[:, c0:c1])
        u = _dot(xn, wu_ref[:, c0:c1])
        hm = (g * jax.nn.sigmoid(g) * u).astype(BF16)
        acc = acc + _dot(hm, wd_ref[c0:c1, :])
    return x + 0.5 * _rms(acc, gpost)


def _const_spec(shape):
    nd = len(shape)
    return pl.BlockSpec(shape, lambda *_: (0,) * nd, pipeline_mode=pl.Buffered(1))


def _params(sem):
    return pltpu.CompilerParams(dimension_semantics=sem, vmem_limit_bytes=VMEM_LIMIT)


def _lo_half(rows):
    return lax.broadcasted_iota(jnp.int32, (rows, LANES), 1) < HEAD_DIM


def _pad_heads(x, n_heads, fill):
    lo = _lo_half(x.shape[0])
    out = []
    for h in range(n_heads):
        piece = x[:, (h // 2) * LANES:(h // 2 + 1) * LANES]
        if h % 2:
            piece = pltpu.roll(piece, HEAD_DIM, axis=1)
        out.append(jnp.where(lo, piece, fill))
    return out


def _ffn_kernel(x_ref, gpre_ref, wg_ref, wu_ref, wd_ref, gpost_ref, o_ref, *, chunks):
    o_ref[...] = _ffn_core(x_ref[...], gpre_ref[...], wg_ref, wu_ref, wd_ref, gpost_ref[...], chunks)


def _half_ffn(x, gpre, wg, wu, wd, gpost, tm):
    n, d = x.shape
    d_ff = wg.shape[1]
    row = pl.BlockSpec((tm, d), lambda i: (i, 0))
    return pl.pallas_call(
        functools.partial(_ffn_kernel, chunks=_ff_chunks(d_ff)),
        grid=(n // tm,),
        in_specs=[row, _const_spec((1, d)), _const_spec((d, d_ff)), _const_spec((d, d_ff)),
                  _const_spec((d_ff, d)), _const_spec((1, d))],
        out_specs=row,
        out_shape=jax.ShapeDtypeStruct((n, d), F32),
        compiler_params=_params(("parallel",)),
        name="half_ffn",
    )(x, gpre, wg, wu, wd, gpost)


_C_FQ, _C_FK, _C_FV, _C_NQ, _C_KV, _C_NQR, _C_KR, _C_END = 0, 512, 1024, 1536, 2048, 2816, 3328, 3712


def _proj_kernel(h_ref, g_ref, wb_ref, ws_ref, bf_ref, cos_ref, sin_ref, small_ref, *rest, packed, n_pos_tiles):
    n = _rms(h_ref[...], g_ref[...]).astype(BF16)
    tm = n.shape[0]

    def mm(c0, c1):
        return _dot(n, wb_ref[:, c0:c1])

    def put(ref, tiles):
        for i, t in enumerate(tiles):
            ref[:, i * LANES:(i + 1) * LANES] = t.astype(BF16)

    def put_t(ref, tiles):
        for i, t in enumerate(tiles):
            for c in range(t.shape[1] // LANES):
                r0 = i * t.shape[1] + c * LANES
                ref[r0:r0 + LANES, :] = t[:, c * LANES:(c + 1) * LANES].T

    scale = HEAD_DIM ** -0.5
    lane = lax.broadcasted_iota(jnp.int32, (tm, LANES), 1)
    fq = mm(_C_FQ, _C_FK) * scale
    fk = mm(_C_FK, _C_FV)
    fv = mm(_C_FV, _C_NQ)

    cos = cos_ref[...]
    sin = sin_ref[...]
    nq = mm(_C_NQ, _C_KV)
    nqr = mm(_C_NQR, _C_KR)
    nq = jnp.concatenate([(nq[:, c * LANES:(c + 1) * LANES] * cos + nqr[:, c * LANES:(c + 1) * LANES] * sin) * scale
                          for c in range(W_NSA // LANES)], axis=1)

    kv = mm(_C_KV, _C_NQR)
    kr = mm(_C_KR, _C_END)
    ks, vs = [], []
    for br in range(3):
        k = kv[:, br * 2 * W_KV: br * 2 * W_KV + W_KV] * cos + kr[:, br * W_KV:(br + 1) * W_KV] * sin
        v = kv[:, br * 2 * W_KV + W_KV:(br + 1) * 2 * W_KV]
        ks.append(k)
        vs.append(v)

    sm = _dot(n, ws_ref[...]) + bf_ref[...]
    log_sig = jnp.minimum(sm, 0.0) - jnp.log(1.0 + jnp.exp(-jnp.abs(sm)))
    small_ref[...] = jnp.where(lane < H_FOX, log_sig, jax.nn.sigmoid(sm))

    if not packed:
        fkv_ref, nkv4_ref, win_ref, fq_ref, nq_ref = rest
        fkv_ref[...] = jnp.concatenate([fk, fv], axis=1)
        nkv4_ref[...] = jnp.concatenate([ks[0], vs[0], ks[1], vs[1]], axis=1)
        win_ref[...] = jnp.concatenate([ks[2], vs[2]], axis=1)
        fq_ref[...] = fq
        nq_ref[...] = nq
        return
    fkvt_ref, nkv4t_ref, wint_ref, fqt_ref, fvt_ref, nqt_ref, nvt_ref, cmp_ref, fkp_ref, nkp_ref = rest
    put_t(fkvt_ref, [fk, fv])
    put_t(nkv4t_ref, [ks[0], vs[0], ks[1], vs[1]])
    put_t(wint_ref, [ks[2], vs[2]])
    cmp_ref[...] = jnp.concatenate([ks[0], vs[0]], axis=1)

    def put_heads_t(ref, tiles):
        for i, t in enumerate(tiles):
            ref[i * LANES:(i + 1) * LANES, :] = t.T.astype(BF16)

    put_heads_t(fqt_ref, _pad_heads(fq, H_FOX, 0.0))
    put_heads_t(fvt_ref, _pad_heads(fv, H_FOX, 1.0))
    put_heads_t(nqt_ref, _pad_heads(nq, H_NSA, 0.0))
    put_heads_t(nvt_ref, _pad_heads(vs[1], H_KV, 1.0) + _pad_heads(vs[2], H_KV, 1.0))
    ones3 = jnp.where((lane >= HEAD_DIM) & (lane < HEAD_DIM + 3), 1.0, 0.0)
    put(fkp_ref, _pad_heads(fk, H_FOX, ones3))
    pos = (lax.rem(pl.program_id(0), n_pos_tiles) * tm + lax.broadcasted_iota(jnp.int32, (tm, 1), 0))
    onehot = jnp.where(lane - HEAD_DIM == _div(pos, SLC_LEN), 1.0, 0.0)
    put(nkp_ref, _pad_heads(ks[1], H_KV, onehot) + _pad_heads(ks[2], H_KV, 0.0))


def _project(h, g, w_big, w_small, b_small, cos, sin, tm, n_pos_tiles, packed):
    n, d = h.shape
    row = lambda w: pl.BlockSpec((tm, w), lambda i: (i, 0))
    pos = pl.BlockSpec((tm, LANES), lambda i: (i % n_pos_tiles, 0))
    if packed:
        outs = [(LANES, F32), (2 * W_KV, F32), (H_FOX * LANES, BF16), (2 * H_KV * LANES, BF16)]
        t_outs = [(2 * W_FOX, F32), (4 * W_KV, F32), (2 * W_KV, F32), (H_FOX * LANES, BF16),
                  (H_FOX * LANES, BF16), (H_NSA * LANES, BF16), (2 * H_KV * LANES, BF16)]
    else:
        outs = [(LANES, F32), (2 * W_FOX, F32), (4 * W_KV, F32), (2 * W_KV, F32), (W_FOX, F32), (W_NSA, F32)]
        t_outs = []
    out_specs = [row(w) for w, _ in outs]
    out_shape = [jax.ShapeDtypeStruct((n, w), dt) for w, dt in outs]
    batch = n // (tm * n_pos_tiles)
    for k, (r, dt) in enumerate(t_outs):
        out_specs.insert(1 + k, pl.BlockSpec((None, r, tm), lambda i: (i // n_pos_tiles, 0, i % n_pos_tiles)))
        out_shape.insert(1 + k, jax.ShapeDtypeStruct((batch, r, tm * n_pos_tiles), dt))
    return pl.pallas_call(
        functools.partial(_proj_kernel, packed=packed, n_pos_tiles=n_pos_tiles),
        grid=(n // tm,),
        in_specs=[row(d), _const_spec((1, d)), _const_spec(w_big.shape), _const_spec(w_small.shape),
                  _const_spec((1, LANES)), pos, pos],
        out_specs=out_specs,
        out_shape=out_shape,
        compiler_params=_params(("parallel",)),
        name="project",
    )(h, g, w_big, w_small, b_small, cos, sin)


def _merge_ffn_kernel(h_ref, of_ref, on_ref, gf_ref, gn_ref, wo_ref, gmix_ref,
                      gpre_ref, wg_ref, wu_ref, wd_ref, gpost_ref, y_ref, *, chunks):
    of = _rms(of_ref[...], gf_ref[...]).astype(BF16)
    on = _rms(on_ref[...], gn_ref[...]).astype(BF16)
    mrg = _dot(of, wo_ref[0:W_FOX, :]) + _dot(on, wo_ref[W_FOX:W_FOX + W_NSA, :])
    h2 = h_ref[...] + _rms(mrg, gmix_ref[...])
    y_ref[...] = _ffn_core(h2, gpre_ref[...], wg_ref, wu_ref, wd_ref, gpost_ref[...], chunks)


def _merge_ffn(h, o_fox, o_nsa, gf, gn, w_out, gmix, gpre, wg, wu, wd, gpost, tm):
    n, d = h.shape
    d_ff = wg.shape[1]
    row = lambda w: pl.BlockSpec((tm, w), lambda i: (i, 0))
    return pl.pallas_call(
        functools.partial(_merge_ffn_kernel, chunks=_ff_chunks(d_ff)),
        grid=(n // tm,),
        in_specs=[row(d), row(W_FOX), row(W_NSA), _const_spec((1, W_FOX)), _const_spec((1, W_NSA)),
                  _const_spec(w_out.shape), _const_spec((1, d)), _const_spec((1, d)),
                  _const_spec((d, d_ff)), _const_spec((d, d_ff)), _const_spec((d_ff, d)), _const_spec((1, d))],
        out_specs=row(d),
        out_shape=jax.ShapeDtypeStruct((n, d), F32),
        compiler_params=_params(("parallel",)),
        name="merge_ffn",
    )(h, o_fox, o_nsa, gf, gn, w_out, gmix, gpre, wg, wu, wd, gpost)


def _forget_bias_kernel(x_ref, k_ref, cum_ref, k2_ref, carry_ref):
    @pl.when(pl.program_id(1) == 0)
    def _():
        carry_ref[...] = jnp.zeros(carry_ref.shape, F32)

    x = x_ref[...]
    tc = x.shape[0]
    r = lax.broadcasted_iota(jnp.int32, (tc, tc), 0)
    c = lax.broadcasted_iota(jnp.int32, (tc, tc), 1)
    tri = (c <= r).astype(BF16)
    hi, mid, lo = _split3(x)
    cs = _dot(tri, hi) + _dot(tri, mid) + _dot(tri, lo) + carry_ref[...]
    carry_ref[...] = cs[tc - 1:tc, :]
    cum_ref[...] = cs

    lane = lax.broadcasted_iota(jnp.int32, (tc, LANES), 1)
    for h in range(H_FOX):
        sl = slice(h * LANES, (h + 1) * LANES)
        neg = [p.astype(F32) for p in _split3(-cs[:, h:h + 1])]
        k = k_ref[:, sl].astype(F32)
        for i in range(3):
            k = jnp.where(lane == HEAD_DIM + 3 + i, neg[i], k)
        k2_ref[:, sl] = k.astype(BF16)


def _forget_bias(small, fk_p, tc):
    b, t, w = small.shape
    blk = lambda n: pl.BlockSpec((None, tc, n), lambda i, j: (i, j, 0))
    wide = H_FOX * LANES
    return pl.pallas_call(
        _forget_bias_kernel,
        grid=(b, t // tc),
        in_specs=[blk(w), blk(wide)],
        out_specs=[blk(w), blk(wide)],
        out_shape=[jax.ShapeDtypeStruct((b, t, w), F32), jax.ShapeDtypeStruct((b, t, wide), BF16)],
        scratch_shapes=[pltpu.VMEM((1, w), F32)],
        compiler_params=_params(("parallel", "arbitrary")),
        name="forget_bias",
    )(small, fk_p)


def _flash_update_t(s, vt, m_ref, acc_ref, idx, allow=None):
    if allow is not None:
        s = jnp.where(allow, s, NEG)
    m_old = m_ref[idx]
    m_new = jnp.maximum(m_old, jnp.max(s, axis=0, keepdims=True))
    p = jnp.exp(s - m_new)
    acc_ref[idx] = jnp.exp(m_old - m_new) * acc_ref[idx] + _dot(vt, p.astype(BF16))
    m_ref[idx] = m_new


def _normalise_t(acc):
    return acc[0:HEAD_DIM] * (1.0 / acc[HEAD_DIM:HEAD_DIM + 1])


def _fox_kernel(qt_ref, cum_ref, k_ref, vt_ref, o_ref, qb_ref, m_ref, acc_ref, *, tq, tk, heads):
    hg = pl.program_id(1)
    q0 = pl.program_id(2) * tq
    m_ref[...] = jnp.full(m_ref.shape, NEG, F32)
    acc_ref[...] = jnp.zeros(acc_ref.shape, F32)
    krow = lax.broadcasted_iota(jnp.int32, (tk, tq), 0)
    qpos = q0 + lax.broadcasted_iota(jnp.int32, (tk, tq), 1)
    head = lambda g: slice(g * LANES, (g + 1) * LANES)
    row = lax.broadcasted_iota(jnp.int32, (LANES, tq), 0)
    row8 = lax.broadcasted_iota(jnp.int32, (H_FOX, tq), 0)
    cum_all = cum_ref[...]
    for g in range(heads):
        cum = jnp.sum(jnp.where(row8 == hg * heads + g, cum_all, 0.0), axis=0, keepdims=True)
        q = qt_ref[head(g), :].astype(F32)
        for i, piece in enumerate(_split3(cum)):
            q = jnp.where(row == HEAD_DIM + i, piece.astype(F32), q)
        qb_ref[g] = jnp.where((row >= HEAD_DIM + 3) & (row < HEAD_DIM + 6), 1.0, q).astype(BF16)

    def scores(kt):
        k0 = pl.multiple_of(kt * tk, tk)
        return tuple(_dot(k_ref[pl.ds(k0, tk), head(g)], qb_ref[g]) for g in range(heads))

    def update(kt, s, masked):
        k0 = pl.multiple_of(kt * tk, tk)
        allow = (k0 + krow <= qpos) if masked else None
        for g in range(heads):
            _flash_update_t(s[g], vt_ref[head(g), pl.ds(k0, tk)], m_ref, acc_ref, g, allow=allow)

    def body(kt, s):
        s_next = scores(kt + 1)
        update(kt, s, False)
        return s_next

    n_full = _div(q0, tk)
    update(n_full, lax.fori_loop(0, n_full, body, scores(0)), True)
    for pr in range(heads // 2):
        pair = jnp.concatenate([_normalise_t(acc_ref[2 * pr]), _normalise_t(acc_ref[2 * pr + 1])], axis=0)
        o_ref[:, pr * LANES:(pr + 1) * LANES] = pair.T


def _fox_attention(fq_t, cum_t, fk_p, fv_t, tq, tk):
    b, t, _ = fk_p.shape
    g = FOX_HEADS_PER_STEP
    qblk = pl.BlockSpec((None, g * LANES, tq), lambda i, hg, qi: (i, hg, qi))
    cblk = pl.BlockSpec((None, H_FOX, tq), lambda i, hg, qi: (i, 0, qi))
    kblk = pl.BlockSpec((None, t, g * LANES), lambda i, hg, qi: (i, 0, hg))
    vblk = pl.BlockSpec((None, g * LANES, t), lambda i, hg, qi: (i, hg, 0))
    oblk = pl.BlockSpec((None, tq, g * HEAD_DIM), lambda i, hg, qi: (i, qi, hg))
    return pl.pallas_call(
        functools.partial(_fox_kernel, tq=tq, tk=tk, heads=g),
        grid=(b, H_FOX // g, t // tq),
        in_specs=[qblk, cblk, kblk, vblk],
        out_specs=oblk,
        out_shape=jax.ShapeDtypeStruct((b, t, W_FOX), F32),
        scratch_shapes=[pltpu.VMEM((g, LANES, tq), BF16), pltpu.VMEM((g, 1, tq), F32),
                        pltpu.VMEM((g, LANES, tq), F32)],
        compiler_params=_params(("parallel", "parallel", "arbitrary")),
        name="fox_prompt",
    )(fq_t, cum_t, fk_p, fv_t)


def _compress_hidden(load_rows, n_ch, wj_ref, pe_ref, wpe_ref):
    acc_k = jnp.zeros((n_ch, 2 * W_KV), F32)
    acc_v = jnp.zeros((n_ch, 2 * W_KV), F32)
    for j in range(CMP_STRIDE):
        xk, xv = load_rows(j)
        acc_k = acc_k + _dot(xk.astype(BF16), wj_ref[j, 0:W_KV, :])
        acc_v = acc_v + _dot(xv.astype(BF16), wj_ref[j, W_KV:2 * W_KV, :])
    first = jnp.concatenate([acc_k[:, 0:W_KV], acc_v[:, 0:W_KV]], axis=1)
    second = jnp.concatenate([acc_k[:, W_KV:], acc_v[:, W_KV:]], axis=1)
    second = pltpu.roll(second, n_ch - 1, axis=0)
    pe_term = _dot(pe_ref[...], wpe_ref[...])[0:1, :]
    return jax.nn.gelu(first + second + pe_term)


def _compress_kernel(xk_ref, xv_ref, wj_ref, pe_ref, wpe_ref, w2p_ref, kc_ref, vc_ref, *, n_ch):
    rows = lambda j: (xk_ref[pl.ds(j, n_ch, stride=CMP_STRIDE), :], xv_ref[pl.ds(j, n_ch, stride=CMP_STRIDE), :])
    hid = _compress_hidden(rows, n_ch, wj_ref, pe_ref, wpe_ref).astype(BF16)
    out = _dot(hid, w2p_ref[...])
    kc_ref[...] = out[:, 0:H_KV * LANES].astype(BF16)
    for h in range(H_KV):
        vc = out[:, (H_KV + h) * LANES:(H_KV + h + 1) * LANES]
        vc_ref[h * LANES:(h + 1) * LANES, :] = jnp.where(_lo_half(n_ch), vc, 1.0).T.astype(BF16)


def _compress(nkv4, cw):
    b, t, _ = nkv4.shape
    n_ch = t // CMP_STRIDE
    out = pl.BlockSpec((None, n_ch, H_KV * LANES), lambda i: (i, 0, 0))
    return pl.pallas_call(
        functools.partial(_compress_kernel, n_ch=n_ch),
        grid=(b,),
        in_specs=[pl.BlockSpec((None, t, W_KV), lambda i: (i, 0, 0)),
                  pl.BlockSpec((None, t, W_KV), lambda i: (i, 0, 1))] + [_const_spec(w.shape) for w in cw],
        out_specs=[out, pl.BlockSpec((None, H_KV * LANES, n_ch), lambda i: (i, 0, 0))],
        out_shape=[jax.ShapeDtypeStruct((b, n_ch, H_KV * LANES), BF16),
                   jax.ShapeDtypeStruct((b, H_KV * LANES, n_ch), BF16)],
        compiler_params=_params(("parallel",)),
        name="compress_prompt",
    )(nkv4, nkv4, *cw)


def _overlap(n_ch, n_slc):
    n = lax.broadcasted_iota(jnp.int32, (n_ch, LANES), 0)
    j = lax.broadcasted_iota(jnp.int32, (n_ch, LANES), 1)
    hit = ((n * CMP_STRIDE <= j * SLC_LEN + SLC_LEN - 1) & (n * CMP_STRIDE + CMP_LEN - 1 >= j * SLC_LEN)
           & (n < n_ch - 1) & (j < n_slc))
    return hit.astype(BF16)


def _overlap_t(n_ch, n_slc):
    j = lax.broadcasted_iota(jnp.int32, (LANES, n_ch), 0)
    n = lax.broadcasted_iota(jnp.int32, (LANES, n_ch), 1)
    hit = ((n * CMP_STRIDE <= j * SLC_LEN + SLC_LEN - 1) & (n * CMP_STRIDE + CMP_LEN - 1 >= j * SLC_LEN)
           & (n < n_ch - 1) & (j < n_slc))
    return hit.astype(BF16)


def _block_scores(imp, pos, n_slc, axis):
    blk = lax.broadcasted_iota(jnp.int32, imp.shape, axis)
    qblk = _div(pos, SLC_LEN)
    forced = (blk == 0) | (blk == qblk) | (blk == qblk - 1)
    valid = blk * SLC_LEN <= pos
    score = jnp.where(valid, jnp.where(forced, FORCE_SCORE, imp), -1.0)
    return jnp.where(blk < n_slc, score, -2.0)


def _count_beats_lanes(score, n_slc):
    blk = lax.broadcasted_iota(jnp.int32, score.shape, 1)
    cnt = jnp.zeros(score.shape, jnp.int32)
    for i in range(n_slc):
        si = score[:, i:i + 1]
        cnt = cnt + jnp.where((si > score) | ((si == score) & (blk > i)), 1, 0)
    return cnt


def _count_beats_rows(score, lo, hi, n_rows):
    out = []
    for v in range(n_rows // 8):
        s_v = score[8 * v:8 * v + 8, :]
        blk = 8 * v + lax.broadcasted_iota(jnp.int32, s_v.shape, 0)
        cnt = jnp.zeros(s_v.shape, jnp.int32)
        for i in range(lo, hi):
            si = score[i:i + 1, :]
            if 8 * v + 7 < i:
                beats = si > s_v
            elif 8 * v > i:
                beats = si >= s_v
            else:
                beats = (si > s_v) | ((si == s_v) & (blk > i))
            cnt = cnt + jnp.where(beats, 1, 0)
        out.append(cnt)
    return jnp.concatenate(out, axis=0)


def _nsa_t_kernel(qt_ref, gate_ref, kc_ref, vct_ref, sk_ref, svt_ref, wk_ref, wvt_ref, o_ref,
                  q4_ref, q4s_ref, cnt_ref, m_ref, acc_ref, *, tq, tk, n_ch, n_slc, wlen):
    h = pl.program_id(1)
    q0 = pl.program_id(2) * tq
    g = GQA_GROUP
    w = g * tq
    n_sel = min(SLC_TOPN, n_slc)
    pos4 = q0 + (lax.broadcasted_iota(jnp.int32, (1, w), 1) & (tq - 1))
    for i in range(g):
        q4_ref[:, i * tq:(i + 1) * tq] = qt_ref[i * LANES:(i + 1) * LANES, :]
    q4 = q4_ref[...]

    n_idx = lax.broadcasted_iota(jnp.int32, (n_ch, w), 0)
    ok = (n_idx * CMP_STRIDE + CMP_LEN - 1 <= pos4) & (n_idx < n_ch - 1)
    sc = jnp.where(ok, _dot(kc_ref[...], q4), NEG)
    e = jnp.where(ok, jnp.exp(sc - jnp.max(sc, axis=0, keepdims=True)), 0.0)
    lc = jnp.sum(e, axis=0, keepdims=True)
    a = e * (1.0 / jnp.where(lc > 0.0, lc, 1.0))
    a_hi = a.astype(BF16)
    a_lo = (a - a_hi.astype(F32)).astype(BF16)
    o_cmp = _dot(vct_ref[...], a_hi)
    ov_t = _overlap_t(n_ch, n_slc)
    imp4 = _dot(ov_t, a_hi) + _dot(ov_t, a_lo)
    imp_t = sum(imp4[:, i * tq:(i + 1) * tq] for i in range(g))

    score = _block_scores(imp_t, q0 + lax.broadcasted_iota(jnp.int32, (1, tq), 1), n_slc, 0)
    n_valid = _div(q0 + tq - 1, SLC_LEN) + 1
    n_rows = cnt_ref.shape[0]
    cnt_ref[...] = jnp.zeros(cnt_ref.shape, jnp.int32)
    for b0 in range(0, n_slc, 8):
        @pl.when((b0 < n_valid) & (n_valid > n_sel))
        def _():
            cnt_ref[...] += _count_beats_rows(score, b0, min(b0 + 8, n_slc), n_rows)
    sel_t = (cnt_ref[...] < n_sel) & (lax.broadcasted_iota(jnp.int32, (n_rows, tq), 0) < n_slc)
    bias = jnp.where(sel_t, 0.0, NEG)
    if n_rows < HEAD_DIM:
        bias = jnp.concatenate([bias, jnp.full((HEAD_DIM - n_rows, tq), NEG, F32)], axis=0)
    q4s_ref[0:HEAD_DIM, :] = q4[0:HEAD_DIM]
    for i in range(g):
        q4s_ref[HEAD_DIM:2 * HEAD_DIM, i * tq:(i + 1) * tq] = bias.astype(BF16)

    m_ref[...] = jnp.full(m_ref.shape, NEG, F32)
    acc_ref[...] = jnp.zeros(acc_ref.shape, F32)
    krow = lax.broadcasted_iota(jnp.int32, (tk, w), 0)

    def scores(kt):
        return _dot(sk_ref[pl.ds(pl.multiple_of(kt * tk, tk), tk), :], q4s_ref[...])

    def update(kt, s, masked):
        k0 = pl.multiple_of(kt * tk, tk)
        allow = (k0 + krow <= pos4) if masked else None
        _flash_update_t(s, svt_ref[:, pl.ds(k0, tk)], m_ref, acc_ref, 0, allow=allow)

    def body(kt, s):
        s_next = scores(kt + 1)
        update(kt, s, False)
        return s_next

    n_full = _div(q0, tk)
    update(n_full, lax.fori_loop(0, n_full, body, scores(0)), True)
    o_slc = _normalise_t(acc_ref[0])

    ws = pl.multiple_of(jnp.maximum(q0 + tq - wlen, 0), tq)
    dist = pos4 - (ws + lax.broadcasted_iota(jnp.int32, (wlen, w), 0))
    okw = (dist >= 0) & (dist < WINDOW)
    sw = jnp.where(okw, _dot(wk_ref[pl.ds(ws, wlen), :], q4), NEG)
    pw = jnp.exp(sw - jnp.max(sw, axis=0, keepdims=True))
    o_win = _normalise_t(_dot(wvt_ref[:, pl.ds(ws, wlen)], pw.astype(BF16)))

    gates = gate_ref[...]
    grow = lax.broadcasted_iota(jnp.int32, (LANES, tq), 0)
    vals = []
    for i in range(g):
        r0 = GATE0 + 3 * (g * h + i)
        gc = [jnp.sum(jnp.where(grow == r0 + c, gates, 0.0), axis=0, keepdims=True) for c in range(3)]
        sl = slice(i * tq, (i + 1) * tq)
        vals.append(gc[0] * o_cmp[0:HEAD_DIM, sl] + gc[1] * o_slc[:, sl] + gc[2] * o_win[:, sl])
    for pr in range(g // 2):
        o_ref[:, pr * LANES:(pr + 1) * LANES] = jnp.concatenate([vals[2 * pr], vals[2 * pr + 1]], axis=0).T


def _nsa_attention_t(nq_t, small_t, kc_p, vc_t, nk_p, nv_t, tq, tk):
    b, _, t = nq_t.shape
    n_ch = kc_p.shape[1]
    n_slc = -(-t // SLC_LEN)
    assert n_slc <= HEAD_DIM, "the block mask rides in the 64 spare query rows"
    wlen = min(t, WINDOW + tq)
    g = GQA_GROUP
    kblk = lambda c: pl.BlockSpec((None, t, LANES), lambda i, h, qi: (i, 0, c * H_KV + h))
    vblk = lambda c: pl.BlockSpec((None, LANES, t), lambda i, h, qi: (i, c * H_KV + h, 0))
    return pl.pallas_call(
        functools.partial(_nsa_t_kernel, tq=tq, tk=tk, n_ch=n_ch, n_slc=n_slc, wlen=wlen),
        grid=(b, H_KV, t // tq),
        in_specs=[pl.BlockSpec((None, g * LANES, tq), lambda i, h, qi: (i, h, qi)),
                  pl.BlockSpec((None, LANES, tq), lambda i, h, qi: (i, 0, qi)),
                  pl.BlockSpec((None, n_ch, LANES), lambda i, h, qi: (i, 0, h)),
                  pl.BlockSpec((None, LANES, n_ch), lambda i, h, qi: (i, h, 0)),
                  kblk(0), vblk(0), kblk(1), vblk(1)],
        out_specs=pl.BlockSpec((None, tq, g * HEAD_DIM), lambda i, h, qi: (i, qi, h)),
        out_shape=jax.ShapeDtypeStruct((b, t, W_NSA), F32),
        scratch_shapes=[pltpu.VMEM((LANES, g * tq), BF16), pltpu.VMEM((LANES, g * tq), BF16),
                        pltpu.VMEM((-(-n_slc // 8) * 8, tq), jnp.int32),
                        pltpu.VMEM((1, 1, g * tq), F32), pltpu.VMEM((1, LANES, g * tq), F32)],
        compiler_params=_params(("parallel", "parallel", "arbitrary")),
        name="nsa_prompt",
    )(nq_t, small_t, kc_p, vc_t, nk_p, nv_t, nk_p, nv_t)


def _page_copies(pt_ref, step, slot, n_pages, page, streams, group):
    out = []
    for u in range(group):
        for pg in range(n_pages):
            idx = pt_ref[(step * group + u) * n_pages + pg]
            for hbm, buf, sem, on_lanes in streams:
                dst = buf.at[slot, u, :, pl.ds(pg * page, page)] if on_lanes else buf.at[slot, u, pg]
                out.append(pltpu.make_async_copy(hbm.at[idx], dst, sem.at[slot]))
    return out


def _gather_pages(pt_ref, n_pages, page, streams, group=1):
    i = pl.program_id(0)
    slot = lax.rem(i, 2)

    @pl.when(i == 0)
    def _():
        for c in _page_copies(pt_ref, 0, 0, n_pages, page, streams, group):
            c.start()

    @pl.when(i + 1 < pl.num_programs(0))
    def _():
        for c in _page_copies(pt_ref, i + 1, 1 - slot, n_pages, page, streams, group):
            c.start()

    for c in _page_copies(pt_ref, i, slot, n_pages, page, streams, group):
        c.wait()
    return slot


def _head_scores(q_col, k_view, s_ref, head, rows0, n_tiles, tile):
    qb = jnp.broadcast_to(q_col, (HEAD_DIM, tile))
    for pg in range(n_tiles):
        kt = k_view[rows0:rows0 + HEAD_DIM, pg * tile:(pg + 1) * tile]
        s_ref[pg, head:head + 1, :] = jnp.sum(kt * qb, axis=0, keepdims=True)


def _head_values(p_tile, v_view, rows0, n_tiles, tile):
    acc = jnp.zeros((HEAD_DIM, tile), F32)
    for pg in range(n_tiles):
        vt = v_view[rows0:rows0 + HEAD_DIM, pg * tile:(pg + 1) * tile]
        acc = acc + vt * jnp.broadcast_to(p_tile(pg), (HEAD_DIM, tile))
    return jnp.sum(acc, axis=1, keepdims=True)


def _softmax_tiles(s, s_new):
    m = jnp.maximum(jnp.max(jnp.max(s, axis=0), axis=1, keepdims=True), s_new)
    p = jnp.exp(s - m[None])
    p_new = jnp.exp(s_new - m)
    l = jnp.sum(jnp.sum(p, axis=0), axis=1, keepdims=True) + p_new
    return p, p_new, 1.0 / l


def _softmax_rows(s, s_new, allow):
    s = jnp.where(allow, s, NEG)
    m = jnp.maximum(jnp.max(s, axis=1, keepdims=True), s_new)
    p = jnp.where(allow, jnp.exp(s - m), 0.0)
    p_new = jnp.exp(s_new - m)
    return p, p_new, 1.0 / (jnp.sum(p, axis=1, keepdims=True) + p_new)


def _col_dot(a_col, b_col, n_heads):
    return jnp.sum((a_col * b_col).reshape(n_heads, HEAD_DIM, 1), axis=1)


def _as_column(row):
    return jnp.broadcast_to(row, (LANES, row.shape[1])).T[:, 0:1]


def _as_row(col):
    return jnp.broadcast_to(col, (col.shape[0], LANES)).T[0:1, :]


def _fox_dec_kernel(pt_ref, q_ref, kvnew_ref, small_ref, kv_hbm, lf_hbm, o_ref,
                    kvbuf, lfbuf, s_ref, p_ref, sem_kv, sem_lf, *, n_pages, page):
    slot = _gather_pages(pt_ref, n_pages, page,
                         [(kv_hbm, kvbuf, sem_kv, True), (lf_hbm, lfbuf, sem_lf, False)])
    kv = kvbuf.at[slot, 0]
    q = _as_column(q_ref[...])
    kv_new = _as_column(kvnew_ref[...])
    lf_new = jnp.concatenate([_as_column(small_ref[...])[0:H_FOX]] * n_pages, axis=0)
    for h in range(H_FOX):
        _head_scores(q[h * HEAD_DIM:(h + 1) * HEAD_DIM], kv, s_ref, h, h * HEAD_DIM, n_pages, page)

    rows = n_pages * H_FOX
    lf = lfbuf[slot, 0].reshape(rows, page)
    r = lax.broadcasted_iota(jnp.int32, (page, page), 0)
    c = lax.broadcasted_iota(jnp.int32, (page, page), 1)
    later = (r > c).astype(BF16)
    hi, mid, lo = _split3(lf)
    within = _dot(hi, later) + _dot(mid, later) + _dot(lo, later)
    r = lax.broadcasted_iota(jnp.int32, (rows, rows), 0)
    c = lax.broadcasted_iota(jnp.int32, (rows, rows), 1)
    later_pages = ((c > r) & (((c - r) & (H_FOX - 1)) == 0)).astype(BF16)
    tot = jnp.broadcast_to(jnp.sum(lf, axis=1, keepdims=True), (rows, page))
    hi, mid, lo = _split3(tot)
    beyond = _dot(later_pages, hi) + _dot(later_pages, mid) + _dot(later_pages, lo)
    bias = (within + beyond + lf_new).reshape(n_pages, H_FOX, page)

    s_new = _col_dot(q, kv_new[0:W_FOX], H_FOX)
    p, p_new, inv_l = _softmax_tiles(s_ref[...] + bias, s_new)
    p_ref[...] = p
    outs = []
    for h in range(H_FOX):
        o = _head_values(lambda pg, h=h: p_ref[pg, h:h + 1, :], kv, W_FOX + h * HEAD_DIM, n_pages, page)
        v_new = kv_new[W_FOX + h * HEAD_DIM: W_FOX + (h + 1) * HEAD_DIM]
        outs.append((o + p_new[h:h + 1] * v_new) * inv_l[h:h + 1])
    o_ref[...] = _as_row(jnp.concatenate(outs, axis=0))


def _fox_decode(page_table, q_row, kv_row, small_row, cache_kvt, cache_lft):
    s, n_pages = page_table.shape
    rows, page = cache_kvt.shape[1], cache_kvt.shape[2]
    col = lambda n: pl.BlockSpec((None, 1, n), lambda i, pt: (i, 0, 0))
    anyspec = pl.BlockSpec(memory_space=pl.ANY)
    return pl.pallas_call(
        functools.partial(_fox_dec_kernel, n_pages=n_pages, page=page),
        grid_spec=pltpu.PrefetchScalarGridSpec(
            num_scalar_prefetch=1,
            grid=(s,),
            in_specs=[col(W_FOX), col(2 * W_FOX), col(LANES), anyspec, anyspec],
            out_specs=col(W_FOX),
            scratch_shapes=[pltpu.VMEM((2, 1, rows, n_pages * page), F32),
                            pltpu.VMEM((2, 1, n_pages, H_FOX, page), F32),
                            pltpu.VMEM((n_pages, H_FOX, page), F32), pltpu.VMEM((n_pages, H_FOX, page), F32),
                            pltpu.SemaphoreType.DMA((2,)), pltpu.SemaphoreType.DMA((2,))],
        ),
        out_shape=jax.ShapeDtypeStruct((s, 1, W_FOX), F32),
        compiler_params=_params(("arbitrary",)),
        name="fox_decode",
    )(page_table.reshape(-1), q_row, kv_row, small_row, cache_kvt, cache_lft)


def _nsa_dec_kernel(pt_ref, qrow_ref, nkv4_ref, wnew_ref, gate_ref, win_ref, cache_hbm,
                    wj_ref, pe_ref, wpe_ref, w2_ref, o_ref,
                    xbuf, xk_buf, xv_buf, sem, *, n_pages, page, group):
    slot = _gather_pages(pt_ref, n_pages, page, [(cache_hbm, xbuf, sem, True)], group)
    n_ch = n_pages * page // CMP_STRIDE

    r = lax.broadcasted_iota(jnp.int32, (page, page), 0)
    t = lax.broadcasted_iota(jnp.int32, (page, page), 1)
    per = page // CMP_STRIDE
    perm = (t == CMP_STRIDE * (r & (per - 1)) + _div(r, per)).astype(BF16)
    for u in range(group):
        for pg in range(n_pages):
            xt = _dot_nt(perm, xbuf[slot, u, 0:2 * W_KV, pg * page:(pg + 1) * page].astype(BF16))
            c0 = u * n_ch + pg * per
            for j in range(CMP_STRIDE):
                xk_buf[j, c0:c0 + per, :] = xt[j * per:(j + 1) * per, 0:W_KV]
                xv_buf[j, c0:c0 + per, :] = xt[j * per:(j + 1) * per, W_KV:2 * W_KV]
    hid = _compress_hidden(lambda jj: (xk_buf[jj], xv_buf[jj]), group * n_ch, wj_ref, pe_ref, wpe_ref).astype(BF16)
    for u in range(group):
        o_ref[u] = _nsa_dec_one(hid[u * n_ch:(u + 1) * n_ch], xbuf.at[slot, u], qrow_ref[u], nkv4_ref[u],
                                wnew_ref[u], gate_ref[u], win_ref.at[u], w2_ref, n_pages, page)


def _nsa_dec_one(hid, x, q_row, nkv4_new, win_new, gates, wv, w2_ref, n_pages, page):
    past_len = n_pages * page
    n_ch = past_len // CMP_STRIDE
    n_slc = past_len // SLC_LEN + 1
    n_sel = min(SLC_TOPN, n_slc)
    win_buf = wv.shape[1]
    g = GQA_GROUP
    row = lax.broadcasted_iota(jnp.int32, (H_NSA, LANES), 0)
    lane = lax.broadcasted_iota(jnp.int32, (H_NSA, LANES), 1)
    kc = _dot(hid, w2_ref[:, 0:W_KV]).astype(BF16)
    vc = _dot(hid, w2_ref[:, W_KV:2 * W_KV]).astype(BF16)

    qbd = jnp.zeros((H_NSA, LANES), F32)
    for i in range(H_NSA):
        piece = q_row[:, (i // 2) * LANES:(i // 2 + 1) * LANES]
        if (i % 2) != (i // g):
            piece = pltpu.roll(piece, HEAD_DIM, axis=1)
        qbd = jnp.where(row == i, jnp.broadcast_to(piece, (H_NSA, LANES)), qbd)
    qbd = jnp.where(_div(lane, HEAD_DIM) == _div(row, g), qbd, 0.0).astype(BF16)
    n_idx = lax.broadcasted_iota(jnp.int32, (H_NSA, n_ch), 1)
    ok = (n_idx * CMP_STRIDE + CMP_LEN - 1 <= past_len) & (n_idx < n_ch - 1)
    sc = jnp.where(ok, _dot_nt(qbd, kc), NEG)
    e = jnp.where(ok, jnp.exp(sc - jnp.max(sc, axis=1, keepdims=True)), 0.0)
    lc = jnp.sum(e, axis=1, keepdims=True)
    a = e * (1.0 / jnp.where(lc > 0.0, lc, 1.0))
    a_hi = a.astype(BF16)
    a_lo = (a - a_hi.astype(F32)).astype(BF16)
    o_cmp = _dot(a_hi, vc)
    ov = _overlap(n_ch, n_slc)
    imp8 = _dot(a_hi, ov) + _dot(a_lo, ov)
    imp = jnp.zeros((H_NSA, LANES), F32)
    for h in range(H_KV):
        tot = jnp.sum(imp8[h * g:(h + 1) * g], axis=0, keepdims=True)
        imp = jnp.where(_div(row, g) == h, jnp.broadcast_to(tot, (H_NSA, LANES)), imp)
    score = _block_scores(imp, jnp.full((H_NSA, 1), past_len, jnp.int32), n_slc, 1)
    sel = jnp.where((_count_beats_lanes(score, n_slc) < n_sel) & (lane < n_slc), 1.0, 0.0)

    per_page = page // SLC_LEN
    allow = []
    for pg in range(n_pages):
        m = jnp.zeros((H_NSA, page), F32)
        for b in range(per_page):
            blk = pg * per_page + b
            lanes_b = _div(lax.broadcasted_iota(jnp.int32, (H_NSA, page), 1), SLC_LEN) == b
            m = jnp.where(lanes_b, jnp.broadcast_to(sel[:, blk:blk + 1], (H_NSA, page)), m)
        allow.append(m > 0.5)
    allow = jnp.concatenate(allow, axis=1)
    qf = qbd.astype(F32)
    s_new = jnp.sum(qf * nkv4_new[:, 2 * W_KV:3 * W_KV], axis=1, keepdims=True)
    p, p_new, inv_l = _softmax_rows(_dot(qbd, x[2 * W_KV:3 * W_KV, :].astype(BF16)), s_new, allow)
    o_slc = (_dot_nt(p.astype(BF16), x[3 * W_KV:4 * W_KV, :].astype(BF16))
             + p_new.astype(BF16).astype(F32) * nkv4_new[:, 3 * W_KV:4 * W_KV].astype(BF16).astype(F32)) * inv_l

    slot_idx = lax.broadcasted_iota(jnp.int32, (H_NSA, win_buf), 1)
    sw_new = jnp.sum(qf * win_new[:, 0:W_KV], axis=1, keepdims=True)
    pw, pw_new, inv_lw = _softmax_rows(_dot(qbd, wv[0:W_KV, :].astype(BF16)), sw_new, (win_buf - slot_idx) < WINDOW)
    o_win = (_dot_nt(pw.astype(BF16), wv[W_KV:2 * W_KV, :].astype(BF16))
             + pw_new.astype(BF16).astype(F32) * win_new[:, W_KV:2 * W_KV].astype(BF16).astype(F32)) * inv_lw

    gate_rows = jnp.broadcast_to(gates, (H_NSA, LANES))
    gc = [jnp.sum(jnp.where(lane == GATE0 + 3 * row + c, gate_rows, 0.0), axis=1, keepdims=True) for c in range(3)]
    val = gc[0] * o_cmp + gc[1] * o_slc + gc[2] * o_win
    pieces = []
    for i in range(H_NSA):
        piece = val[i:i + 1]
        if (i % 2) != (i // g):
            piece = pltpu.roll(piece, HEAD_DIM, axis=1)
        pieces.append(piece)
    lo = lax.broadcasted_iota(jnp.int32, (1, LANES), 1) < HEAD_DIM
    return jnp.concatenate([jnp.where(lo, pieces[2 * pr], pieces[2 * pr + 1]) for pr in range(H_NSA // 2)], axis=1)


def _nsa_decode(page_table, q_row, nkv4_row, win_row, small, win_t, cache_t, cw):
    s, n_pages = page_table.shape
    rows, page = cache_t.shape[1], cache_t.shape[2]
    past_len = n_pages * page
    n_ch = past_len // CMP_STRIDE
    win_buf = win_t.shape[2]
    group = next(g for g in (4, 2, 1) if s % g == 0)
    rowspec = lambda n: pl.BlockSpec((group, 1, n), lambda i, pt: (i, 0, 0))
    const = lambda shape: pl.BlockSpec(shape, lambda i, pt: (0,) * len(shape), pipeline_mode=pl.Buffered(1))
    return pl.pallas_call(
        functools.partial(_nsa_dec_kernel, n_pages=n_pages, page=page, group=group),
        grid_spec=pltpu.PrefetchScalarGridSpec(
            num_scalar_prefetch=1,
            grid=(s // group,),
            in_specs=[rowspec(W_NSA), rowspec(4 * W_KV), rowspec(2 * W_KV), rowspec(LANES),
                      pl.BlockSpec((group, 2 * W_KV, win_buf), lambda i, pt: (i, 0, 0)),
                      pl.BlockSpec(memory_space=pl.ANY)] + [const(w.shape) for w in cw],
            out_specs=rowspec(W_NSA),
            scratch_shapes=[pltpu.VMEM((2, group, rows, past_len), F32),
                            pltpu.VMEM((CMP_STRIDE, group * n_ch, W_KV), F32),
                            pltpu.VMEM((CMP_STRIDE, group * n_ch, W_KV), F32),
                            pltpu.SemaphoreType.DMA((2,))],
        ),
        out_shape=jax.ShapeDtypeStruct((s, 1, W_NSA), F32),
        compiler_params=_params(("arbitrary",)),
        name="nsa_decode",
    )(page_table.reshape(-1), q_row, nkv4_row, win_row, small, win_t, cache_t, *cw)


def _rot_cols(w):
    d, n = w.shape
    w = w.reshape(d, n // HEAD_DIM, 2, HEAD_DIM // 2)
    return jnp.stack([-w[:, :, 1], w[:, :, 0]], axis=2).reshape(d, n)


def _prep_projection(w_in, b_f):
    c = [0, W_FOX, 2 * W_FOX, 3 * W_FOX, 3 * W_FOX + H_FOX, 3 * W_FOX + H_FOX + W_NSA,
         3 * W_FOX + H_FOX + W_NSA + 6 * W_KV]
    fq, fk, fv, ff, nq, kv = (w_in[:, c[i]:c[i + 1]] for i in range(6))
    gt = w_in[:, c[6]:]
    ks = jnp.concatenate([kv[:, br * 2 * W_KV: br * 2 * W_KV + W_KV] for br in range(3)], axis=1)
    w_big = jnp.concatenate([fq, fk, fv, nq, kv, _rot_cols(nq), _rot_cols(ks)], axis=1).astype(BF16)
    d = w_in.shape[0]
    w_small = jnp.concatenate([ff, gt, jnp.zeros((d, LANES - N_SMALL), w_in.dtype)], axis=1).astype(BF16)
    b_small = jnp.concatenate([b_f.astype(F32), jnp.zeros((LANES - H_FOX,), F32)])[None, :]
    return w_big, w_small, b_small


def _prep_compress(wk1, wk2, pek, wv1, wv2, pev):
    ratio = CMP_LEN // CMP_STRIDE
    eye = jnp.eye(4, dtype=F32)

    def blocks(w):
        return w.reshape(ratio, CMP_STRIDE, HEAD_DIM, wk1.shape[1])

    per_head = lambda w: jnp.einsum("rjde,hk->jhdrke", blocks(w), jnp.eye(H_KV, dtype=F32)).reshape(
        CMP_STRIDE, W_KV, ratio * W_KV)
    wj = jnp.concatenate([per_head(wk1), per_head(wv1)], axis=1).astype(BF16)
    pe = jnp.concatenate([pek.reshape(-1), pev.reshape(-1)])
    pe = jnp.broadcast_to(pe[None, :], (8, pe.shape[0])).astype(BF16)
    zero = jnp.zeros_like(wk1)
    wpe = jnp.concatenate([jnp.concatenate([wk1, wk1, zero, zero], axis=1),
                           jnp.concatenate([zero, zero, wv1, wv1], axis=1)], axis=0).astype(BF16)
    w2 = jnp.einsum("gde,gh->gdhe", jnp.stack([wk2, wk2, wv2, wv2]), eye)
    w2_pad = jnp.concatenate([w2, jnp.zeros_like(w2)], axis=3)
    w2 = w2.reshape(4 * HEAD_DIM, 4 * HEAD_DIM).astype(BF16)
    w2_pad = w2_pad.reshape(4 * HEAD_DIM, 4 * LANES).astype(BF16)
    return (wj, pe, wpe), w2, w2_pad


def _rope_tables(pos):
    half = HEAD_DIM // 2
    inv = ROPE_THETA ** (-jnp.arange(half, dtype=F32) / half)
    ang = pos.astype(F32)[:, None] * inv[None, :]
    reps = LANES // half
    return jnp.tile(jnp.cos(ang), (1, reps)), jnp.tile(jnp.sin(ang), (1, reps))


def _row_tile(n, cap):
    t = min(n, cap)
    while n % t:
        t //= 2
    return t


def kernel(x_prompt, x_sample, cache_fox_kv, cache_fox_logf, cache_nsa_kv, state_nsa_win_kv, page_table,
           g_ffn1_pre, w_ffn1_gate, w_ffn1_up, w_ffn1_down, g_ffn1_post, g_mix_pre, w_in, b_fox_f,
           w_cmpk_1, w_cmpk_2, pe_cmpk, w_cmpv_1, w_cmpv_2, pe_cmpv, g_fox_out, g_nsa_out, w_out,
           g_mix_post, g_ffn2_pre, w_ffn2_gate, w_ffn2_up, w_ffn2_down, g_ffn2_post):
    depth = w_in.shape[0]
    b, t, d = x_prompt.shape
    s, dec_seq, _ = x_sample.shape
    assert dec_seq == 1, "the sample group decodes one token per sequence"
    page = cache_fox_kv.shape[2]
    n_pages = page_table.shape[1]
    past_len = n_pages * page
    assert t % LANES == 0 and page % SLC_LEN == 0
    page_table = page_table.astype(jnp.int32)

    tm_p = _row_tile(t, 512)
    cos_p, sin_p = _rope_tables(jnp.arange(t, dtype=jnp.int32))
    cos_s, sin_s = _rope_tables(jnp.full((s,), past_len, jnp.int32))
    row = lambda v: v.astype(F32)[None, :]
    to_rows = lambda c: jnp.transpose(c, (0, 2, 3, 4, 1)).reshape(c.shape[0], -1, c.shape[1])

    yp = x_prompt.reshape(b * t, d)
    ys = x_sample.reshape(s, d)
    outs = [[] for _ in range(8)]
    for l in range(depth):
        ffn1 = (row(g_ffn1_pre[l]), w_ffn1_gate[l].astype(BF16), w_ffn1_up[l].astype(BF16),
                w_ffn1_down[l].astype(BF16), row(g_ffn1_post[l]))
        ffn2 = (row(g_ffn2_pre[l]), w_ffn2_gate[l].astype(BF16), w_ffn2_up[l].astype(BF16),
                w_ffn2_down[l].astype(BF16), row(g_ffn2_post[l]))
        w_big, w_small, b_small = _prep_projection(w_in[l], b_fox_f[l])
        cw, w2, w2_pad = _prep_compress(w_cmpk_1[l], w_cmpk_2[l], pe_cmpk[l], w_cmpv_1[l], w_cmpv_2[l], pe_cmpv[l])
        merge = (row(g_fox_out[l]), row(g_nsa_out[l]), w_out[l].astype(BF16), row(g_mix_post[l]))

        hp = _half_ffn(yp, *ffn1, tm_p)
        small, fkv_t, nkv4_t, win_t, fq_t, fv_t, nq_t, nv_t, cmp, fk_p, nk_p = _project(
            hp, row(g_mix_pre[l]), w_big, w_small, b_small, cos_p, sin_p, tm_p, t // tm_p, True)
        small3 = small.reshape(b, t, LANES)
        per_head = lambda x: x.reshape(b, t, x.shape[1])
        cum, fk_b = _forget_bias(small3, per_head(fk_p), _row_tile(t, 512))
        o_fox = _fox_attention(fq_t, jnp.swapaxes(cum[:, :, :H_FOX], 1, 2), fk_b, fv_t,
                               _row_tile(t, 512), _row_tile(t, 512))
        kc_p, vc_t = _compress(cmp.reshape(b, t, 2 * W_KV), cw + (w2_pad,))
        o_nsa = _nsa_attention_t(nq_t, jnp.swapaxes(small3, 1, 2), kc_p, vc_t, per_head(nk_p), nv_t,
                                 _row_tile(t, 256), _row_tile(t, 512))
        yp = _merge_ffn(hp, o_fox.reshape(b * t, W_FOX), o_nsa.reshape(b * t, W_NSA), *merge, *ffn2, tm_p)
        keep = min(WINDOW, t)
        tokens_first = lambda x, *dims: jnp.transpose(x.reshape(b, *dims, x.shape[2]), (0, 4, 1, 2, 3))
        outs[0].append(tokens_first(fkv_t, 2, H_FOX, HEAD_DIM))
        outs[1].append(small3[:, :, :H_FOX])
        outs[2].append(tokens_first(nkv4_t, 4, H_KV, HEAD_DIM))
        outs[3].append(tokens_first(win_t[:, :, t - keep:], 2, H_KV, HEAD_DIM))

        hs = _half_ffn(ys, *ffn1, s)
        small, fkv, nkv4, win, fq, nq = _project(
            hs, row(g_mix_pre[l]), w_big, w_small, b_small, cos_s, sin_s, s, 1, False)
        as_rows = lambda x: x.reshape(s, 1, x.shape[1])
        o_fox = _fox_decode(page_table, as_rows(fq), as_rows(fkv), as_rows(small),
                            to_rows(cache_fox_kv[l]), jnp.swapaxes(cache_fox_logf[l], 1, 2))
        o_nsa = _nsa_decode(page_table, as_rows(nq), as_rows(nkv4), as_rows(win), as_rows(small),
                            to_rows(state_nsa_win_kv[l]), to_rows(cache_nsa_kv[l]), cw + (w2,))
        ys = _merge_ffn(hs, o_fox.reshape(s, W_FOX), o_nsa.reshape(s, W_NSA), *merge, *ffn2, s)
        keep = min(WINDOW, past_len + 1)
        kw_all = jnp.concatenate([state_nsa_win_kv[l], win.reshape(s, 1, 2, H_KV, HEAD_DIM)], axis=1)
        outs[4].append(fkv.reshape(s, 1, 2, H_FOX, HEAD_DIM))
        outs[5].append(small[:, :H_FOX].reshape(s, 1, H_FOX))
        outs[6].append(nkv4.reshape(s, 1, 4, H_KV, HEAD_DIM))
        outs[7].append(kw_all[:, kw_all.shape[1] - keep:])

    stacked = [jnp.stack(o, axis=0) for o in outs]
    return (yp.reshape(b, t, d), ys.reshape(s, 1, d), *stacked)
```

```python
import functools

import jax
import jax.numpy as jnp
from jax import lax
from jax.experimental import pallas as pl
from jax.experimental.pallas import tpu as pltpu

HEAD_DIM = 64
H_FOX = 8
H_NSA = 8
H_KV = 2
GQA_GROUP = H_NSA // H_KV
W_FOX = H_FOX * HEAD_DIM
W_NSA = H_NSA * HEAD_DIM
W_KV = H_KV * HEAD_DIM
CMP_STRIDE = 16
CMP_LEN = 32
SLC_LEN = 64
SLC_TOPN = 16
WINDOW = 512
ROPE_THETA = 10000.0
EPS = 1e-6
NEG = -1e30
FORCE_SCORE = 1e4
N_SMALL = H_FOX + 3 * H_NSA
GATE0 = H_FOX

LANES = 128
MXU_N = 256
VMEM_LIMIT = 56 * 1024 * 1024
FOX_HEADS_PER_STEP = 4

F32 = jnp.float32
BF16 = jnp.bfloat16


def _dot(a, b):
    return jnp.dot(a, b, preferred_element_type=F32)


def _dot_nt(a, b):
    return lax.dot_general(a, b, (((1,), (1,)), ((), ())), preferred_element_type=F32)


def _div(x, n):
    assert n & (n - 1) == 0
    return lax.shift_right_logical(x, jnp.int32(n.bit_length() - 1))


def _split3(x):
    hi = x.astype(BF16)
    r1 = x - hi.astype(F32)
    mid = r1.astype(BF16)
    lo = (r1 - mid.astype(F32)).astype(BF16)
    return hi, mid, lo


def _rms(x, g):
    return x * lax.rsqrt(jnp.mean(x * x, axis=-1, keepdims=True) + EPS) * g


def _ff_chunks(d_ff):
    step = 6 * MXU_N
    return tuple((c, min(c + step, d_ff)) for c in range(0, d_ff, step))


def _ffn_core(x, gpre, wg_ref, wu_ref, wd_ref, gpost, chunks):
    xn = _rms(x, gpre).astype(BF16)
    acc = jnp.zeros(x.shape, F32)
    for c0, c1 in chunks:
        g = _dot(xn, wg_ref[:, c0:c1])
        u = _dot(xn, wu_ref[:, c0:c1])
        hm = (g * jax.nn.sigmoid(g) * u).astype(BF16)
        acc = acc + _dot(hm, wd_ref[c0:c1, :])
    return x + 0.5 * _rms(acc, gpost)


def _const_spec(shape):
    nd = len(shape)
    return pl.BlockSpec(shape, lambda *_: (0,) * nd, pipeline_mode=pl.Buffered(1))


def _params(sem):
    return pltpu.CompilerParams(dimension_semantics=sem, vmem_limit_bytes=VMEM_LIMIT)


def _lo_half(rows):
    return lax.broadcasted_iota(jnp.int32, (rows, LANES), 1) < HEAD_DIM


def _pad_heads(x, n_heads, fill):
    lo = _lo_half(x.shape[0])
    out = []
    for h in range(n_heads):
        piece = x[:, (h // 2) * LANES:(h // 2 + 1) * LANES]
        if h % 2:
            piece = pltpu.roll(piece, HEAD_DIM, axis=1)
        out.append(jnp.where(lo, piece, fill))
    return out


def _ffn_kernel(x_ref, gpre_ref, wg_ref, wu_ref, wd_ref, gpost_ref, o_ref, *, chunks):
    o_ref[...] = _ffn_core(x_ref[...], gpre_ref[...], wg_ref, wu_ref, wd_ref, gpost_ref[...], chunks)


def _half_ffn(x, gpre, wg, wu, wd, gpost, tm):
    n, d = x.shape
    d_ff = wg.shape[1]
    row = pl.BlockSpec((tm, d), lambda i: (i, 0))
    return pl.pallas_call(
        functools.partial(_ffn_kernel, chunks=_ff_chunks(d_ff)),
        grid=(n // tm,),
        in_specs=[row, _const_spec((1, d)), _const_spec((d, d_ff)), _const_spec((d, d_ff)),
                  _const_spec((d_ff, d)), _const_spec((1, d))],
        out_specs=row,
        out_shape=jax.ShapeDtypeStruct((n, d), F32),
        compiler_params=_params(("parallel",)),
        name="half_ffn",
    )(x, gpre, wg, wu, wd, gpost)


_C_FQ, _C_FK, _C_FV, _C_NQ, _C_KV, _C_NQR, _C_KR, _C_END = 0, 512, 1024, 1536, 2048, 2816, 3328, 3712


def _proj_kernel(h_ref, g_ref, wb_ref, ws_ref, bf_ref, cos_ref, sin_ref, small_ref, *rest, packed, n_pos_tiles):
    n = _rms(h_ref[...], g_ref[...]).astype(BF16)
    tm = n.shape[0]

    def mm(c0, c1):
        return _dot(n, wb_ref[:, c0:c1])

    def put(ref, tiles):
        for i, t in enumerate(tiles):
            ref[:, i * LANES:(i + 1) * LANES] = t.astype(BF16)

    def put_t(ref, tiles):
        for i, t in enumerate(tiles):
            for c in range(t.shape[1] // LANES):
                r0 = i * t.shape[1] + c * LANES
                ref[r0:r0 + LANES, :] = t[:, c * LANES:(c + 1) * LANES].T

    scale = HEAD_DIM ** -0.5
    lane = lax.broadcasted_iota(jnp.int32, (tm, LANES), 1)
    fq = mm(_C_FQ, _C_FK) * scale
    fk = mm(_C_FK, _C_FV)
    fv = mm(_C_FV, _C_NQ)

    cos = cos_ref[...]
    sin = sin_ref[...]
    nq = mm(_C_NQ, _C_KV)
    nqr = mm(_C_NQR, _C_KR)
    nq = jnp.concatenate([(nq[:, c * LANES:(c + 1) * LANES] * cos + nqr[:, c * LANES:(c + 1) * LANES] * sin) * scale
                          for c in range(W_NSA // LANES)], axis=1)

    kv = mm(_C_KV, _C_NQR)
    kr = mm(_C_KR, _C_END)
    ks, vs = [], []
    for br in range(3):
        k = kv[:, br * 2 * W_KV: br * 2 * W_KV + W_KV] * cos + kr[:, br * W_KV:(br + 1) * W_KV] * sin
        v = kv[:, br * 2 * W_KV + W_KV:(br + 1) * 2 * W_KV]
        ks.append(k)
        vs.append(v)

    sm = _dot(n, ws_ref[...]) + bf_ref[...]
    log_sig = jnp.minimum(sm, 0.0) - jnp.log(1.0 + jnp.exp(-jnp.abs(sm)))
    small_ref[...] = jnp.where(lane < H_FOX, log_sig, jax.nn.sigmoid(sm))

    if not packed:
        fkv_ref, nkv4_ref, win_ref, fq_ref, nq_ref = rest
        fkv_ref[...] = jnp.concatenate([fk, fv], axis=1)
        nkv4_ref[...] = jnp.concatenate([ks[0], vs[0], ks[1], vs[1]], axis=1)
        win_ref[...] = jnp.concatenate([ks[2], vs[2]], axis=1)
        fq_ref[...] = fq
        nq_ref[...] = nq
        return
    fkvt_ref, nkv4t_ref, wint_ref, fqt_ref, fvt_ref, nqt_ref, nvt_ref, cmp_ref, fkp_ref, nkp_ref = rest
    put_t(fkvt_ref, [fk, fv])
    put_t(nkv4t_ref, [ks[0], vs[0], ks[1], vs[1]])
    put_t(wint_ref, [ks[2], vs[2]])
    cmp_ref[...] = jnp.concatenate([ks[0], vs[0]], axis=1)

    def put_heads_t(ref, tiles):
        for i, t in enumerate(tiles):
            ref[i * LANES:(i + 1) * LANES, :] = t.T.astype(BF16)

    put_heads_t(fqt_ref, _pad_heads(fq, H_FOX, 0.0))
    put_heads_t(fvt_ref, _pad_heads(fv, H_FOX, 1.0))
    put_heads_t(nqt_ref, _pad_heads(nq, H_NSA, 0.0))
    put_heads_t(nvt_ref, _pad_heads(vs[1], H_KV, 1.0) + _pad_heads(vs[2], H_KV, 1.0))
    ones3 = jnp.where((lane >= HEAD_DIM) & (lane < HEAD_DIM + 3), 1.0, 0.0)
    put(fkp_ref, _pad_heads(fk, H_FOX, ones3))
    pos = (lax.rem(pl.program_id(0), n_pos_tiles) * tm + lax.broadcasted_iota(jnp.int32, (tm, 1), 0))
    onehot = jnp.where(lane - HEAD_DIM == _div(pos, SLC_LEN), 1.0, 0.0)
    put(nkp_ref, _pad_heads(ks[1], H_KV, onehot) + _pad_heads(ks[2], H_KV, 0.0))


def _project(h, g, w_big, w_small, b_small, cos, sin, tm, n_pos_tiles, packed):
    n, d = h.shape
    row = lambda w: pl.BlockSpec((tm, w), lambda i: (i, 0))
    pos = pl.BlockSpec((tm, LANES), lambda i: (i % n_pos_tiles, 0))
    if packed:
        outs = [(LANES, F32), (2 * W_KV, F32), (H_FOX * LANES, BF16), (2 * H_KV * LANES, BF16)]
        t_outs = [(2 * W_FOX, F32), (4 * W_KV, F32), (2 * W_KV, F32), (H_FOX * LANES, BF16),
                  (H_FOX * LANES, BF16), (H_NSA * LANES, BF16), (2 * H_KV * LANES, BF16)]
    else:
        outs = [(LANES, F32), (2 * W_FOX, F32), (4 * W_KV, F32), (2 * W_KV, F32), (W_FOX, F32), (W_NSA, F32)]
        t_outs = []
    out_specs = [row(w) for w, _ in outs]
    out_shape = [jax.ShapeDtypeStruct((n, w), dt) for w, dt in outs]
    batch = n // (tm * n_pos_tiles)
    for k, (r, dt) in enumerate(t_outs):
        out_specs.insert(1 + k, pl.BlockSpec((None, r, tm), lambda i: (i // n_pos_tiles, 0, i % n_pos_tiles)))
        out_shape.insert(1 + k, jax.ShapeDtypeStruct((batch, r, tm * n_pos_tiles), dt))
    return pl.pallas_call(
        functools.partial(_proj_kernel, packed=packed, n_pos_tiles=n_pos_tiles),
        grid=(n // tm,),
        in_specs=[row(d), _const_spec((1, d)), _const_spec(w_big.shape), _const_spec(w_small.shape),
                  _const_spec((1, LANES)), pos, pos],
        out_specs=out_specs,
        out_shape=out_shape,
        compiler_params=_params(("parallel",)),
        name="project",
    )(h, g, w_big, w_small, b_small, cos, sin)


def _merge_ffn_kernel(h_ref, of_ref, on_ref, gf_ref, gn_ref, wo_ref, gmix_ref,
                      gpre_ref, wg_ref, wu_ref, wd_ref, gpost_ref, y_ref, *, chunks):
    of = _rms(of_ref[...], gf_ref[...]).astype(BF16)
    on = _rms(on_ref[...], gn_ref[...]).astype(BF16)
    mrg = _dot(of, wo_ref[0:W_FOX, :]) + _dot(on, wo_ref[W_FOX:W_FOX + W_NSA, :])
    h2 = h_ref[...] + _rms(mrg, gmix_ref[...])
    y_ref[...] = _ffn_core(h2, gpre_ref[...], wg_ref, wu_ref, wd_ref, gpost_ref[...], chunks)


def _merge_ffn(h, o_fox, o_nsa, gf, gn, w_out, gmix, gpre, wg, wu, wd, gpost, tm):
    n, d = h.shape
    d_ff = wg.shape[1]
    row = lambda w: pl.BlockSpec((tm, w), lambda i: (i, 0))
    return pl.pallas_call(
        functools.partial(_merge_ffn_kernel, chunks=_ff_chunks(d_ff)),
        grid=(n // tm,),
        in_specs=[row(d), row(W_FOX), row(W_NSA), _const_spec((1, W_FOX)), _const_spec((1, W_NSA)),
                  _const_spec(w_out.shape), _const_spec((1, d)), _const_spec((1, d)),
                  _const_spec((d, d_ff)), _const_spec((d, d_ff)), _const_spec((d_ff, d)), _const_spec((1, d))],
        out_specs=row(d),
        out_shape=jax.ShapeDtypeStruct((n, d), F32),
        compiler_params=_params(("parallel",)),
        name="merge_ffn",
    )(h, o_fox, o_nsa, gf, gn, w_out, gmix, gpre, wg, wu, wd, gpost)


def _forget_bias_kernel(x_ref, k_ref, cum_ref, k2_ref, carry_ref):
    @pl.when(pl.program_id(1) == 0)
    def _():
        carry_ref[...] = jnp.zeros(carry_ref.shape, F32)

    x = x_ref[...]
    tc = x.shape[0]
    r = lax.broadcasted_iota(jnp.int32, (tc, tc), 0)
    c = lax.broadcasted_iota(jnp.int32, (tc, tc), 1)
    tri = (c <= r).astype(BF16)
    hi, mid, lo = _split3(x)
    cs = _dot(tri, hi) + _dot(tri, mid) + _dot(tri, lo) + carry_ref[...]
    carry_ref[...] = cs[tc - 1:tc, :]
    cum_ref[...] = cs

    lane = lax.broadcasted_iota(jnp.int32, (tc, LANES), 1)
    for h in range(H_FOX):
        sl = slice(h * LANES, (h + 1) * LANES)
        neg = [p.astype(F32) for p in _split3(-cs[:, h:h + 1])]
        k = k_ref[:, sl].astype(F32)
        for i in range(3):
            k = jnp.where(lane == HEAD_DIM + 3 + i, neg[i], k)
        k2_ref[:, sl] = k.astype(BF16)


def _forget_bias(small, fk_p, tc):
    b, t, w = small.shape
    blk = lambda n: pl.BlockSpec((None, tc, n), lambda i, j: (i, j, 0))
    wide = H_FOX * LANES
    return pl.pallas_call(
        _forget_bias_kernel,
        grid=(b, t // tc),
        in_specs=[blk(w), blk(wide)],
        out_specs=[blk(w), blk(wide)],
        out_shape=[jax.ShapeDtypeStruct((b, t, w), F32), jax.ShapeDtypeStruct((b, t, wide), BF16)],
        scratch_shapes=[pltpu.VMEM((1, w), F32)],
        compiler_params=_params(("parallel", "arbitrary")),
        name="forget_bias",
    )(small, fk_p)


def _flash_update_t(s, vt, m_ref, acc_ref, idx, allow=None, bias=None):
    if allow is not None:
        s = jnp.where(allow, s, NEG)
    if bias is not None:
        s = s + bias
    m_old = m_ref[idx]
    m_new = jnp.maximum(m_old, jnp.max(s, axis=0, keepdims=True))
    p = jnp.exp(s - m_new)
    acc_ref[idx] = jnp.exp(m_old - m_new) * acc_ref[idx] + _dot(vt, p.astype(BF16))
    m_ref[idx] = m_new


def _normalise_t(acc):
    return acc[0:HEAD_DIM] * (1.0 / acc[HEAD_DIM:HEAD_DIM + 1])


def _fox_kernel(qt_ref, cum_ref, k_ref, vt_ref, o_ref, qb_ref, m_ref, acc_ref, *, tq, tk, heads):
    hg = pl.program_id(1)
    q0 = pl.program_id(2) * tq
    m_ref[...] = jnp.full(m_ref.shape, NEG, F32)
    acc_ref[...] = jnp.zeros(acc_ref.shape, F32)
    krow = lax.broadcasted_iota(jnp.int32, (tk, tq), 0)
    qpos = q0 + lax.broadcasted_iota(jnp.int32, (tk, tq), 1)
    head = lambda g: slice(g * LANES, (g + 1) * LANES)
    row = lax.broadcasted_iota(jnp.int32, (LANES, tq), 0)
    row8 = lax.broadcasted_iota(jnp.int32, (H_FOX, tq), 0)
    cum_all = cum_ref[...]
    for g in range(heads):
        cum = jnp.sum(jnp.where(row8 == hg * heads + g, cum_all, 0.0), axis=0, keepdims=True)
        q = qt_ref[head(g), :].astype(F32)
        for i, piece in enumerate(_split3(cum)):
            q = jnp.where(row == HEAD_DIM + i, piece.astype(F32), q)
        qb_ref[g] = jnp.where((row >= HEAD_DIM + 3) & (row < HEAD_DIM + 6), 1.0, q).astype(BF16)

    def scores(kt):
        k0 = pl.multiple_of(kt * tk, tk)
        return tuple(_dot(k_ref[pl.ds(k0, tk), head(g)], qb_ref[g]) for g in range(heads))

    def update(kt, s, masked):
        k0 = pl.multiple_of(kt * tk, tk)
        allow = (k0 + krow <= qpos) if masked else None
        for g in range(heads):
            _flash_update_t(s[g], vt_ref[head(g), pl.ds(k0, tk)], m_ref, acc_ref, g, allow=allow)

    def body(kt, s):
        s_next = scores(kt + 1)
        update(kt, s, False)
        return s_next

    n_full = _div(q0, tk)
    update(n_full, lax.fori_loop(0, n_full, body, scores(0)), True)
    for pr in range(heads // 2):
        pair = jnp.concatenate([_normalise_t(acc_ref[2 * pr]), _normalise_t(acc_ref[2 * pr + 1])], axis=0)
        o_ref[:, pr * LANES:(pr + 1) * LANES] = pair.T


def _fox_attention(fq_t, cum_t, fk_p, fv_t, tq, tk):
    b, t, _ = fk_p.shape
    g = FOX_HEADS_PER_STEP
    qblk = pl.BlockSpec((None, g * LANES, tq), lambda i, hg, qi: (i, hg, qi))
    cblk = pl.BlockSpec((None, H_FOX, tq), lambda i, hg, qi: (i, 0, qi))
    kblk = pl.BlockSpec((None, t, g * LANES), lambda i, hg, qi: (i, 0, hg))
    vblk = pl.BlockSpec((None, g * LANES, t), lambda i, hg, qi: (i, hg, 0))
    oblk = pl.BlockSpec((None, tq, g * HEAD_DIM), lambda i, hg, qi: (i, qi, hg))
    return pl.pallas_call(
        functools.partial(_fox_kernel, tq=tq, tk=tk, heads=g),
        grid=(b, H_FOX // g, t // tq),
        in_specs=[qblk, cblk, kblk, vblk],
        out_specs=oblk,
        out_shape=jax.ShapeDtypeStruct((b, t, W_FOX), F32),
        scratch_shapes=[pltpu.VMEM((g, LANES, tq), BF16), pltpu.VMEM((g, 1, tq), F32),
                        pltpu.VMEM((g, LANES, tq), F32)],
        compiler_params=_params(("parallel", "parallel", "arbitrary")),
        name="fox_prompt",
    )(fq_t, cum_t, fk_p, fv_t)


def _compress_hidden(load_rows, n_ch, wj_ref, pe_ref, wpe_ref):
    acc_k = jnp.zeros((n_ch, 2 * W_KV), F32)
    acc_v = jnp.zeros((n_ch, 2 * W_KV), F32)
    for j in range(CMP_STRIDE):
        xk, xv = load_rows(j)
        acc_k = acc_k + _dot(xk.astype(BF16), wj_ref[j, 0:W_KV, :])
        acc_v = acc_v + _dot(xv.astype(BF16), wj_ref[j, W_KV:2 * W_KV, :])
    first = jnp.concatenate([acc_k[:, 0:W_KV], acc_v[:, 0:W_KV]], axis=1)
    second = jnp.concatenate([acc_k[:, W_KV:], acc_v[:, W_KV:]], axis=1)
    second = pltpu.roll(second, n_ch - 1, axis=0)
    pe_term = _dot(pe_ref[...], wpe_ref[...])[0:1, :]
    return jax.nn.gelu(first + second + pe_term)


def _compress_kernel(xk_ref, xv_ref, wj_ref, pe_ref, wpe_ref, w2p_ref, kc_ref, vc_ref, *, n_ch):
    rows = lambda j: (xk_ref[pl.ds(j, n_ch, stride=CMP_STRIDE), :], xv_ref[pl.ds(j, n_ch, stride=CMP_STRIDE), :])
    hid = _compress_hidden(rows, n_ch, wj_ref, pe_ref, wpe_ref).astype(BF16)
    out = _dot(hid, w2p_ref[...])
    kc_ref[...] = out[:, 0:H_KV * LANES].astype(BF16)
    for h in range(H_KV):
        vc = out[:, (H_KV + h) * LANES:(H_KV + h + 1) * LANES]
        vc_ref[h * LANES:(h + 1) * LANES, :] = jnp.where(_lo_half(n_ch), vc, 1.0).T.astype(BF16)


def _compress(nkv4, cw):
    b, t, _ = nkv4.shape
    n_ch = t // CMP_STRIDE
    out = pl.BlockSpec((None, n_ch, H_KV * LANES), lambda i: (i, 0, 0))
    return pl.pallas_call(
        functools.partial(_compress_kernel, n_ch=n_ch),
        grid=(b,),
        in_specs=[pl.BlockSpec((None, t, W_KV), lambda i: (i, 0, 0)),
                  pl.BlockSpec((None, t, W_KV), lambda i: (i, 0, 1))] + [_const_spec(w.shape) for w in cw],
        out_specs=[out, pl.BlockSpec((None, H_KV * LANES, n_ch), lambda i: (i, 0, 0))],
        out_shape=[jax.ShapeDtypeStruct((b, n_ch, H_KV * LANES), BF16),
                   jax.ShapeDtypeStruct((b, H_KV * LANES, n_ch), BF16)],
        compiler_params=_params(("parallel",)),
        name="compress_prompt",
    )(nkv4, nkv4, *cw)


def _overlap(n_ch, n_slc):
    n = lax.broadcasted_iota(jnp.int32, (n_ch, LANES), 0)
    j = lax.broadcasted_iota(jnp.int32, (n_ch, LANES), 1)
    hit = ((n * CMP_STRIDE <= j * SLC_LEN + SLC_LEN - 1) & (n * CMP_STRIDE + CMP_LEN - 1 >= j * SLC_LEN)
           & (n < n_ch - 1) & (j < n_slc))
    return hit.astype(BF16)


def _overlap_t(n_ch, n_slc):
    j = lax.broadcasted_iota(jnp.int32, (LANES, n_ch), 0)
    n = lax.broadcasted_iota(jnp.int32, (LANES, n_ch), 1)
    hit = ((n * CMP_STRIDE <= j * SLC_LEN + SLC_LEN - 1) & (n * CMP_STRIDE + CMP_LEN - 1 >= j * SLC_LEN)
           & (n < n_ch - 1) & (j < n_slc))
    return hit.astype(BF16)


def _block_scores(imp, pos, n_slc, axis):
    blk = lax.broadcasted_iota(jnp.int32, imp.shape, axis)
    qblk = _div(pos, SLC_LEN)
    forced = (blk == 0) | (blk == qblk) | (blk == qblk - 1)
    valid = blk * SLC_LEN <= pos
    score = jnp.where(valid, jnp.where(forced, FORCE_SCORE, imp), -1.0)
    return jnp.where(blk < n_slc, score, -2.0)


def _count_beats_lanes(score, n_slc):
    blk = lax.broadcasted_iota(jnp.int32, score.shape, 1)
    cnt = jnp.zeros(score.shape, jnp.int32)
    for i in range(n_slc):
        si = score[:, i:i + 1]
        cnt = cnt + jnp.where((si > score) | ((si == score) & (blk > i)), 1, 0)
    return cnt


def _count_beats_rows(score, lo, hi, n_rows):
    out = []
    for v in range(n_rows // 8):
        s_v = score[8 * v:8 * v + 8, :]
        blk = 8 * v + lax.broadcasted_iota(jnp.int32, s_v.shape, 0)
        cnt = jnp.zeros(s_v.shape, jnp.int32)
        for i in range(lo, hi):
            si = score[i:i + 1, :]
            if 8 * v + 7 < i:
                beats = si > s_v
            elif 8 * v > i:
                beats = si >= s_v
            else:
                beats = (si > s_v) | ((si == s_v) & (blk > i))
            cnt = cnt + jnp.where(beats, 1, 0)
        out.append(cnt)
    return jnp.concatenate(out, axis=0)


def _nsa_t_kernel(qt_ref, gate_ref, kc_ref, vct_ref, sk_ref, svt_ref, wk_ref, wvt_ref, o_ref,
                  q4_ref, q4s_ref, cnt_ref, m_ref, acc_ref, *, tq, tk, n_ch, n_slc, wlen):
    h = pl.program_id(1)
    q0 = pl.program_id(2) * tq
    g = GQA_GROUP
    w = g * tq
    n_sel = min(SLC_TOPN, n_slc)
    pos = q0 + lax.broadcasted_iota(jnp.int32, (1, tq), 1)
    heads = lambda x: jnp.concatenate([x] * g, axis=1)
    for i in range(g):
        q4_ref[:, i * tq:(i + 1) * tq] = qt_ref[i * LANES:(i + 1) * LANES, :]
    q4 = q4_ref[...]

    n_idx = lax.broadcasted_iota(jnp.int32, (n_ch, tq), 0)
    ok = (n_idx * CMP_STRIDE + CMP_LEN - 1 <= pos) & (n_idx < n_ch - 1)
    sc = _dot(kc_ref[...], q4) + heads(jnp.where(ok, 0.0, NEG))
    e = jnp.exp(sc - jnp.max(sc, axis=0, keepdims=True)) * heads(jnp.where(ok, 1.0, 0.0))
    lc = jnp.sum(e, axis=0, keepdims=True)
    a = e * (1.0 / jnp.where(lc > 0.0, lc, 1.0))
    a_hi = a.astype(BF16)
    a_lo = (a - a_hi.astype(F32)).astype(BF16)
    o_cmp = _dot(vct_ref[...], a_hi)
    ov_t = _overlap_t(n_ch, n_slc)
    imp4 = _dot(ov_t, a_hi) + _dot(ov_t, a_lo)
    imp_t = sum(imp4[:, i * tq:(i + 1) * tq] for i in range(g))

    score = _block_scores(imp_t, pos, n_slc, 0)
    n_valid = _div(q0 + tq - 1, SLC_LEN) + 1
    n_rows = cnt_ref.shape[0]
    cnt_ref[...] = jnp.zeros(cnt_ref.shape, jnp.int32)
    for b0 in range(0, n_slc, 8):
        @pl.when((b0 < n_valid) & (n_valid > n_sel))
        def _():
            cnt_ref[...] += _count_beats_rows(score, b0, min(b0 + 8, n_slc), n_rows)
    sel_t = (cnt_ref[...] < n_sel) & (lax.broadcasted_iota(jnp.int32, (n_rows, tq), 0) < n_slc)
    bias = jnp.where(sel_t, 0.0, NEG)
    if n_rows < HEAD_DIM:
        bias = jnp.concatenate([bias, jnp.full((HEAD_DIM - n_rows, tq), NEG, F32)], axis=0)
    q4s_ref[0:HEAD_DIM, :] = q4[0:HEAD_DIM]
    for i in range(g):
        q4s_ref[HEAD_DIM:2 * HEAD_DIM, i * tq:(i + 1) * tq] = bias.astype(BF16)

    m_ref[...] = jnp.full(m_ref.shape, NEG, F32)
    acc_ref[...] = jnp.zeros(acc_ref.shape, F32)
    krow = lax.broadcasted_iota(jnp.int32, (tk, tq), 0)

    def scores(kt):
        return _dot(sk_ref[pl.ds(pl.multiple_of(kt * tk, tk), tk), :], q4s_ref[...])

    def update(kt, s, masked):
        k0 = pl.multiple_of(kt * tk, tk)
        bias = heads(jnp.where(k0 + krow <= pos, 0.0, NEG)) if masked else None
        _flash_update_t(s, svt_ref[:, pl.ds(k0, tk)], m_ref, acc_ref, 0, bias=bias)

    def body(kt, s):
        s_next = scores(kt + 1)
        update(kt, s, False)
        return s_next

    n_full = _div(q0, tk)
    update(n_full, lax.fori_loop(0, n_full, body, scores(0)), True)
    o_slc = _normalise_t(acc_ref[0])

    ws = pl.multiple_of(jnp.maximum(q0 + tq - wlen, 0), tq)
    dist = pos - (ws + lax.broadcasted_iota(jnp.int32, (wlen, tq), 0))
    okw = (dist >= 0) & (dist < WINDOW)
    sw = _dot(wk_ref[pl.ds(ws, wlen), :], q4) + heads(jnp.where(okw, 0.0, NEG))
    pw = jnp.exp(sw - jnp.max(sw, axis=0, keepdims=True))
    o_win = _normalise_t(_dot(wvt_ref[:, pl.ds(ws, wlen)], pw.astype(BF16)))

    gates = gate_ref[...]
    grow = lax.broadcasted_iota(jnp.int32, (LANES, tq), 0)
    vals = []
    for i in range(g):
        r0 = GATE0 + 3 * (g * h + i)
        gc = [jnp.sum(jnp.where(grow == r0 + c, gates, 0.0), axis=0, keepdims=True) for c in range(3)]
        sl = slice(i * tq, (i + 1) * tq)
        vals.append(gc[0] * o_cmp[0:HEAD_DIM, sl] + gc[1] * o_slc[:, sl] + gc[2] * o_win[:, sl])
    for pr in range(g // 2):
        o_ref[:, pr * LANES:(pr + 1) * LANES] = jnp.concatenate([vals[2 * pr], vals[2 * pr + 1]], axis=0).T


def _nsa_attention_t(nq_t, small_t, kc_p, vc_t, nk_p, nv_t, tq, tk):
    b, _, t = nq_t.shape
    n_ch = kc_p.shape[1]
    n_slc = -(-t // SLC_LEN)
    assert n_slc <= HEAD_DIM, "the block mask rides in the 64 spare query rows"
    wlen = min(t, WINDOW + tq)
    g = GQA_GROUP
    kblk = lambda c: pl.BlockSpec((None, t, LANES), lambda i, h, qi: (i, 0, c * H_KV + h))
    vblk = lambda c: pl.BlockSpec((None, LANES, t), lambda i, h, qi: (i, c * H_KV + h, 0))
    return pl.pallas_call(
        functools.partial(_nsa_t_kernel, tq=tq, tk=tk, n_ch=n_ch, n_slc=n_slc, wlen=wlen),
        grid=(b, H_KV, t // tq),
        in_specs=[pl.BlockSpec((None, g * LANES, tq), lambda i, h, qi: (i, h, qi)),
                  pl.BlockSpec((None, LANES, tq), lambda i, h, qi: (i, 0, qi)),
                  pl.BlockSpec((None, n_ch, LANES), lambda i, h, qi: (i, 0, h)),
                  pl.BlockSpec((None, LANES, n_ch), lambda i, h, qi: (i, h, 0)),
                  kblk(0), vblk(0), kblk(1), vblk(1)],
        out_specs=pl.BlockSpec((None, tq, g * HEAD_DIM), lambda i, h, qi: (i, qi, h)),
        out_shape=jax.ShapeDtypeStruct((b, t, W_NSA), F32),
        scratch_shapes=[pltpu.VMEM((LANES, g * tq), BF16), pltpu.VMEM((LANES, g * tq), BF16),
                        pltpu.VMEM((-(-n_slc // 8) * 8, tq), jnp.int32),
                        pltpu.VMEM((1, 1, g * tq), F32), pltpu.VMEM((1, LANES, g * tq), F32)],
        compiler_params=_params(("parallel", "parallel", "arbitrary")),
        name="nsa_prompt",
    )(nq_t, small_t, kc_p, vc_t, nk_p, nv_t, nk_p, nv_t)


def _page_copies(pt_ref, step, slot, n_pages, page, streams, group):
    out = []
    for u in range(group):
        for pg in range(n_pages):
            idx = pt_ref[(step * group + u) * n_pages + pg]
            for hbm, buf, sem, on_lanes in streams:
                dst = buf.at[slot, u, :, pl.ds(pg * page, page)] if on_lanes else buf.at[slot, u, pg]
                out.append(pltpu.make_async_copy(hbm.at[idx], dst, sem.at[slot]))
    return out


def _gather_pages(pt_ref, n_pages, page, streams, group=1):
    i = pl.program_id(0)
    slot = lax.rem(i, 2)

    @pl.when(i == 0)
    def _():
        for c in _page_copies(pt_ref, 0, 0, n_pages, page, streams, group):
            c.start()

    @pl.when(i + 1 < pl.num_programs(0))
    def _():
        for c in _page_copies(pt_ref, i + 1, 1 - slot, n_pages, page, streams, group):
            c.start()

    for c in _page_copies(pt_ref, i, slot, n_pages, page, streams, group):
        c.wait()
    return slot


def _head_scores(q_col, k_view, s_ref, head, rows0, n_tiles, tile):
    qb = jnp.broadcast_to(q_col, (HEAD_DIM, tile))
    for pg in range(n_tiles):
        kt = k_view[rows0:rows0 + HEAD_DIM, pg * tile:(pg + 1) * tile]
        s_ref[pg, head:head + 1, :] = jnp.sum(kt * qb, axis=0, keepdims=True)


def _head_values(p_tile, v_view, rows0, n_tiles, tile):
    acc = jnp.zeros((HEAD_DIM, tile), F32)
    for pg in range(n_tiles):
        vt = v_view[rows0:rows0 + HEAD_DIM, pg * tile:(pg + 1) * tile]
        acc = acc + vt * jnp.broadcast_to(p_tile(pg), (HEAD_DIM, tile))
    return jnp.sum(acc, axis=1, keepdims=True)


def _softmax_tiles(s, s_new):
    m = jnp.maximum(jnp.max(jnp.max(s, axis=0), axis=1, keepdims=True), s_new)
    p = jnp.exp(s - m[None])
    p_new = jnp.exp(s_new - m)
    l = jnp.sum(jnp.sum(p, axis=0), axis=1, keepdims=True) + p_new
    return p, p_new, 1.0 / l


def _softmax_rows(s, s_new, allow):
    s = jnp.where(allow, s, NEG)
    m = jnp.maximum(jnp.max(s, axis=1, keepdims=True), s_new)
    p = jnp.where(allow, jnp.exp(s - m), 0.0)
    p_new = jnp.exp(s_new - m)
    return p, p_new, 1.0 / (jnp.sum(p, axis=1, keepdims=True) + p_new)


def _col_dot(a_col, b_col, n_heads):
    return jnp.sum((a_col * b_col).reshape(n_heads, HEAD_DIM, 1), axis=1)


def _as_column(row):
    return jnp.broadcast_to(row, (LANES, row.shape[1])).T[:, 0:1]


def _as_row(col):
    return jnp.broadcast_to(col, (col.shape[0], LANES)).T[0:1, :]


def _fox_dec_kernel(pt_ref, q_ref, kvnew_ref, small_ref, kv_hbm, lf_hbm, o_ref,
                    kvbuf, lfbuf, s_ref, p_ref, sem_kv, sem_lf, *, n_pages, page):
    slot = _gather_pages(pt_ref, n_pages, page,
                         [(kv_hbm, kvbuf, sem_kv, True), (lf_hbm, lfbuf, sem_lf, False)])
    kv = kvbuf.at[slot, 0]
    q = _as_column(q_ref[...])
    kv_new = _as_column(kvnew_ref[...])
    lf_new = jnp.concatenate([_as_column(small_ref[...])[0:H_FOX]] * n_pages, axis=0)
    for h in range(H_FOX):
        _head_scores(q[h * HEAD_DIM:(h + 1) * HEAD_DIM], kv, s_ref, h, h * HEAD_DIM, n_pages, page)

    rows = n_pages * H_FOX
    lf = lfbuf[slot, 0].reshape(rows, page)
    r = lax.broadcasted_iota(jnp.int32, (page, page), 0)
    c = lax.broadcasted_iota(jnp.int32, (page, page), 1)
    later = (r > c).astype(BF16)
    hi, mid, lo = _split3(lf)
    within = _dot(hi, later) + _dot(mid, later) + _dot(lo, later)
    r = lax.broadcasted_iota(jnp.int32, (rows, rows), 0)
    c = lax.broadcasted_iota(jnp.int32, (rows, rows), 1)
    later_pages = ((c > r) & (((c - r) & (H_FOX - 1)) == 0)).astype(BF16)
    tot = jnp.broadcast_to(jnp.sum(lf, axis=1, keepdims=True), (rows, page))
    hi, mid, lo = _split3(tot)
    beyond = _dot(later_pages, hi) + _dot(later_pages, mid) + _dot(later_pages, lo)
    bias = (within + beyond + lf_new).reshape(n_pages, H_FOX, page)

    s_new = _col_dot(q, kv_new[0:W_FOX], H_FOX)
    p, p_new, inv_l = _softmax_tiles(s_ref[...] + bias, s_new)
    p_ref[...] = p
    outs = []
    for h in range(H_FOX):
        o = _head_values(lambda pg, h=h: p_ref[pg, h:h + 1, :], kv, W_FOX + h * HEAD_DIM, n_pages, page)
        v_new = kv_new[W_FOX + h * HEAD_DIM: W_FOX + (h + 1) * HEAD_DIM]
        outs.append((o + p_new[h:h + 1] * v_new) * inv_l[h:h + 1])
    o_ref[...] = _as_row(jnp.concatenate(outs, axis=0))


def _fox_decode(page_table, q_row, kv_row, small_row, cache_kvt, cache_lft):
    s, n_pages = page_table.shape
    rows, page = cache_kvt.shape[1], cache_kvt.shape[2]
    col = lambda n: pl.BlockSpec((None, 1, n), lambda i, pt: (i, 0, 0))
    anyspec = pl.BlockSpec(memory_space=pl.ANY)
    return pl.pallas_call(
        functools.partial(_fox_dec_kernel, n_pages=n_pages, page=page),
        grid_spec=pltpu.PrefetchScalarGridSpec(
            num_scalar_prefetch=1,
            grid=(s,),
            in_specs=[col(W_FOX), col(2 * W_FOX), col(LANES), anyspec, anyspec],
            out_specs=col(W_FOX),
            scratch_shapes=[pltpu.VMEM((2, 1, rows, n_pages * page), F32),
                            pltpu.VMEM((2, 1, n_pages, H_FOX, page), F32),
                            pltpu.VMEM((n_pages, H_FOX, page), F32), pltpu.VMEM((n_pages, H_FOX, page), F32),
                            pltpu.SemaphoreType.DMA((2,)), pltpu.SemaphoreType.DMA((2,))],
        ),
        out_shape=jax.ShapeDtypeStruct((s, 1, W_FOX), F32),
        compiler_params=_params(("arbitrary",)),
        name="fox_decode",
    )(page_table.reshape(-1), q_row, kv_row, small_row, cache_kvt, cache_lft)


def _nsa_dec_kernel(pt_ref, qrow_ref, nkv4_ref, wnew_ref, gate_ref, win_ref, cache_hbm,
                    wj_ref, pe_ref, wpe_ref, w2_ref, o_ref,
                    xbuf, xk_buf, xv_buf, sem, *, n_pages, page, group):
    slot = _gather_pages(pt_ref, n_pages, page, [(cache_hbm, xbuf, sem, True)], group)
    n_ch = n_pages * page // CMP_STRIDE

    r = lax.broadcasted_iota(jnp.int32, (page, page), 0)
    t = lax.broadcasted_iota(jnp.int32, (page, page), 1)
    per = page // CMP_STRIDE
    perm = (t == CMP_STRIDE * (r & (per - 1)) + _div(r, per)).astype(BF16)
    for u in range(group):
        for pg in range(n_pages):
            xt = _dot_nt(perm, xbuf[slot, u, 0:2 * W_KV, pg * page:(pg + 1) * page].astype(BF16))
            c0 = u * n_ch + pg * per
            for j in range(CMP_STRIDE):
                xk_buf[j, c0:c0 + per, :] = xt[j * per:(j + 1) * per, 0:W_KV]
                xv_buf[j, c0:c0 + per, :] = xt[j * per:(j + 1) * per, W_KV:2 * W_KV]
    hid = _compress_hidden(lambda jj: (xk_buf[jj], xv_buf[jj]), group * n_ch, wj_ref, pe_ref, wpe_ref).astype(BF16)
    for u in range(group):
        o_ref[u] = _nsa_dec_one(hid[u * n_ch:(u + 1) * n_ch], xbuf.at[slot, u], qrow_ref[u], nkv4_ref[u],
                                wnew_ref[u], gate_ref[u], win_ref.at[u], w2_ref, n_pages, page)


def _nsa_dec_one(hid, x, q_row, nkv4_new, win_new, gates, wv, w2_ref, n_pages, page):
    past_len = n_pages * page
    n_ch = past_len // CMP_STRIDE
    n_slc = past_len // SLC_LEN + 1
    n_sel = min(SLC_TOPN, n_slc)
    win_buf = wv.shape[1]
    g = GQA_GROUP
    row = lax.broadcasted_iota(jnp.int32, (H_NSA, LANES), 0)
    lane = lax.broadcasted_iota(jnp.int32, (H_NSA, LANES), 1)
    kc = _dot(hid, w2_ref[:, 0:W_KV]).astype(BF16)
    vc = _dot(hid, w2_ref[:, W_KV:2 * W_KV]).astype(BF16)

    qbd = jnp.zeros((H_NSA, LANES), F32)
    for i in range(H_NSA):
        piece = q_row[:, (i // 2) * LANES:(i // 2 + 1) * LANES]
        if (i % 2) != (i // g):
            piece = pltpu.roll(piece, HEAD_DIM, axis=1)
        qbd = jnp.where(row == i, jnp.broadcast_to(piece, (H_NSA, LANES)), qbd)
    qbd = jnp.where(_div(lane, HEAD_DIM) == _div(row, g), qbd, 0.0).astype(BF16)
    n_idx = lax.broadcasted_iota(jnp.int32, (H_NSA, n_ch), 1)
    ok = (n_idx * CMP_STRIDE + CMP_LEN - 1 <= past_len) & (n_idx < n_ch - 1)
    sc = jnp.where(ok, _dot_nt(qbd, kc), NEG)
    e = jnp.where(ok, jnp.exp(sc - jnp.max(sc, axis=1, keepdims=True)), 0.0)
    lc = jnp.sum(e, axis=1, keepdims=True)
    a = e * (1.0 / jnp.where(lc > 0.0, lc, 1.0))
    a_hi = a.astype(BF16)
    a_lo = (a - a_hi.astype(F32)).astype(BF16)
    o_cmp = _dot(a_hi, vc)
    ov = _overlap(n_ch, n_slc)
    imp8 = _dot(a_hi, ov) + _dot(a_lo, ov)
    imp = jnp.zeros((H_NSA, LANES), F32)
    for h in range(H_KV):
        tot = jnp.sum(imp8[h * g:(h + 1) * g], axis=0, keepdims=True)
        imp = jnp.where(_div(row, g) == h, jnp.broadcast_to(tot, (H_NSA, LANES)), imp)
    score = _block_scores(imp, jnp.full((H_NSA, 1), past_len, jnp.int32), n_slc, 1)
    sel = jnp.where((_count_beats_lanes(score, n_slc) < n_sel) & (lane < n_slc), 1.0, 0.0)

    per_page = page // SLC_LEN
    allow = []
    for pg in range(n_pages):
        m = jnp.zeros((H_NSA, page), F32)
        for b in range(per_page):
            blk = pg * per_page + b
            lanes_b = _div(lax.broadcasted_iota(jnp.int32, (H_NSA, page), 1), SLC_LEN) == b
            m = jnp.where(lanes_b, jnp.broadcast_to(sel[:, blk:blk + 1], (H_NSA, page)), m)
        allow.append(m > 0.5)
    allow = jnp.concatenate(allow, axis=1)
    qf = qbd.astype(F32)
    s_new = jnp.sum(qf * nkv4_new[:, 2 * W_KV:3 * W_KV], axis=1, keepdims=True)
    p, p_new, inv_l = _softmax_rows(_dot(qbd, x[2 * W_KV:3 * W_KV, :].astype(BF16)), s_new, allow)
    o_slc = (_dot_nt(p.astype(BF16), x[3 * W_KV:4 * W_KV, :].astype(BF16))
             + p_new.astype(BF16).astype(F32) * nkv4_new[:, 3 * W_KV:4 * W_KV].astype(BF16).astype(F32)) * inv_l

    slot_idx = lax.broadcasted_iota(jnp.int32, (H_NSA, win_buf), 1)
    sw_new = jnp.sum(qf * win_new[:, 0:W_KV], axis=1, keepdims=True)
    pw, pw_new, inv_lw = _softmax_rows(_dot(qbd, wv[0:W_KV, :].astype(BF16)), sw_new, (win_buf - slot_idx) < WINDOW)
    o_win = (_dot_nt(pw.astype(BF16), wv[W_KV:2 * W_KV, :].astype(BF16))
             + pw_new.astype(BF16).astype(F32) * win_new[:, W_KV:2 * W_KV].astype(BF16).astype(F32)) * inv_lw

    gate_rows = jnp.broadcast_to(gates, (H_NSA, LANES))
    gc = [jnp.sum(jnp.where(lane == GATE0 + 3 * row + c, gate_rows, 0.0), axis=1, keepdims=True) for c in range(3)]
    val = gc[0] * o_cmp + gc[1] * o_slc + gc[2] * o_win
    pieces = []
    for i in range(H_NSA):
        piece = val[i:i + 1]
        if (i % 2) != (i // g):
            piece = pltpu.roll(piece, HEAD_DIM, axis=1)
        pieces.append(piece)
    lo = lax.broadcasted_iota(jnp.int32, (1, LANES), 1) < HEAD_DIM
    return jnp.concatenate([jnp.where(lo, pieces[2 * pr], pieces[2 * pr + 1]) for pr in range(H_NSA // 2)], axis=1)


def _nsa_decode(page_table, q_row, nkv4_row, win_row, small, win_t, cache_t, cw):
    s, n_pages = page_table.shape
    rows, page = cache_t.shape[1], cache_t.shape[2]
    past_len = n_pages * page
    n_ch = past_len // CMP_STRIDE
    win_buf = win_t.shape[2]
    group = next(g for g in (4, 2, 1) if s % g == 0)
    rowspec = lambda n: pl.BlockSpec((group, 1, n), lambda i, pt: (i, 0, 0))
    const = lambda shape: pl.BlockSpec(shape, lambda i, pt: (0,) * len(shape), pipeline_mode=pl.Buffered(1))
    return pl.pallas_call(
        functools.partial(_nsa_dec_kernel, n_pages=n_pages, page=page, group=group),
        grid_spec=pltpu.PrefetchScalarGridSpec(
            num_scalar_prefetch=1,
            grid=(s // group,),
            in_specs=[rowspec(W_NSA), rowspec(4 * W_KV), rowspec(2 * W_KV), rowspec(LANES),
                      pl.BlockSpec((group, 2 * W_KV, win_buf), lambda i, pt: (i, 0, 0)),
                      pl.BlockSpec(memory_space=pl.ANY)] + [const(w.shape) for w in cw],
            out_specs=rowspec(W_NSA),
            scratch_shapes=[pltpu.VMEM((2, group, rows, past_len), F32),
                            pltpu.VMEM((CMP_STRIDE, group * n_ch, W_KV), F32),
                            pltpu.VMEM((CMP_STRIDE, group * n_ch, W_KV), F32),
                            pltpu.SemaphoreType.DMA((2,))],
        ),
        out_shape=jax.ShapeDtypeStruct((s, 1, W_NSA), F32),
        compiler_params=_params(("arbitrary",)),
        name="nsa_decode",
    )(page_table.reshape(-1), q_row, nkv4_row, win_row, small, win_t, cache_t, *cw)


def _rot_cols(w):
    d, n = w.shape
    w = w.reshape(d, n // HEAD_DIM, 2, HEAD_DIM // 2)
    return jnp.stack([-w[:, :, 1], w[:, :, 0]], axis=2).reshape(d, n)


def _prep_projection(w_in, b_f):
    c = [0, W_FOX, 2 * W_FOX, 3 * W_FOX, 3 * W_FOX + H_FOX, 3 * W_FOX + H_FOX + W_NSA,
         3 * W_FOX + H_FOX + W_NSA + 6 * W_KV]
    fq, fk, fv, ff, nq, kv = (w_in[:, c[i]:c[i + 1]] for i in range(6))
    gt = w_in[:, c[6]:]
    ks = jnp.concatenate([kv[:, br * 2 * W_KV: br * 2 * W_KV + W_KV] for br in range(3)], axis=1)
    w_big = jnp.concatenate([fq, fk, fv, nq, kv, _rot_cols(nq), _rot_cols(ks)], axis=1).astype(BF16)
    d = w_in.shape[0]
    w_small = jnp.concatenate([ff, gt, jnp.zeros((d, LANES - N_SMALL), w_in.dtype)], axis=1).astype(BF16)
    b_small = jnp.concatenate([b_f.astype(F32), jnp.zeros((LANES - H_FOX,), F32)])[None, :]
    return w_big, w_small, b_small


def _prep_compress(wk1, wk2, pek, wv1, wv2, pev):
    ratio = CMP_LEN // CMP_STRIDE
    eye = jnp.eye(4, dtype=F32)

    def blocks(w):
        return w.reshape(ratio, CMP_STRIDE, HEAD_DIM, wk1.shape[1])

    per_head = lambda w: jnp.einsum("rjde,hk->jhdrke", blocks(w), jnp.eye(H_KV, dtype=F32)).reshape(
        CMP_STRIDE, W_KV, ratio * W_KV)
    wj = jnp.concatenate([per_head(wk1), per_head(wv1)], axis=1).astype(BF16)
    pe = jnp.concatenate([pek.reshape(-1), pev.reshape(-1)])
    pe = jnp.broadcast_to(pe[None, :], (8, pe.shape[0])).astype(BF16)
    zero = jnp.zeros_like(wk1)
    wpe = jnp.concatenate([jnp.concatenate([wk1, wk1, zero, zero], axis=1),
                           jnp.concatenate([zero, zero, wv1, wv1], axis=1)], axis=0).astype(BF16)
    w2 = jnp.einsum("gde,gh->gdhe", jnp.stack([wk2, wk2, wv2, wv2]), eye)
    w2_pad = jnp.concatenate([w2, jnp.zeros_like(w2)], axis=3)
    w2 = w2.reshape(4 * HEAD_DIM, 4 * HEAD_DIM).astype(BF16)
    w2_pad = w2_pad.reshape(4 * HEAD_DIM, 4 * LANES).astype(BF16)
    return (wj, pe, wpe), w2, w2_pad


def _rope_tables(pos):
    half = HEAD_DIM // 2
    inv = ROPE_THETA ** (-jnp.arange(half, dtype=F32) / half)
    ang = pos.astype(F32)[:, None] * inv[None, :]
    reps = LANES // half
    return jnp.tile(jnp.cos(ang), (1, reps)), jnp.tile(jnp.sin(ang), (1, reps))


def _row_tile(n, cap):
    t = min(n, cap)
    while n % t:
        t //= 2
    return t


def kernel(x_prompt, x_sample, cache_fox_kv, cache_fox_logf, cache_nsa_kv, state_nsa_win_kv, page_table,
           g_ffn1_pre, w_ffn1_gate, w_ffn1_up, w_ffn1_down, g_ffn1_post, g_mix_pre, w_in, b_fox_f,
           w_cmpk_1, w_cmpk_2, pe_cmpk, w_cmpv_1, w_cmpv_2, pe_cmpv, g_fox_out, g_nsa_out, w_out,
           g_mix_post, g_ffn2_pre, w_ffn2_gate, w_ffn2_up, w_ffn2_down, g_ffn2_post):
    depth = w_in.shape[0]
    b, t, d = x_prompt.shape
    s, dec_seq, _ = x_sample.shape
    assert dec_seq == 1, "the sample group decodes one token per sequence"
    page = cache_fox_kv.shape[2]
    n_pages = page_table.shape[1]
    past_len = n_pages * page
    assert t % LANES == 0 and page % SLC_LEN == 0
    page_table = page_table.astype(jnp.int32)

    tm_p = _row_tile(t, 512)
    cos_p, sin_p = _rope_tables(jnp.arange(t, dtype=jnp.int32))
    cos_s, sin_s = _rope_tables(jnp.full((s,), past_len, jnp.int32))
    row = lambda v: v.astype(F32)[None, :]
    to_rows = lambda c: jnp.transpose(c, (0, 2, 3, 4, 1)).reshape(c.shape[0], -1, c.shape[1])

    yp = x_prompt.reshape(b * t, d)
    ys = x_sample.reshape(s, d)
    outs = [[] for _ in range(8)]
    for l in range(depth):
        ffn1 = (row(g_ffn1_pre[l]), w_ffn1_gate[l].astype(BF16), w_ffn1_up[l].astype(BF16),
                w_ffn1_down[l].astype(BF16), row(g_ffn1_post[l]))
        ffn2 = (row(g_ffn2_pre[l]), w_ffn2_gate[l].astype(BF16), w_ffn2_up[l].astype(BF16),
                w_ffn2_down[l].astype(BF16), row(g_ffn2_post[l]))
        w_big, w_small, b_small = _prep_projection(w_in[l], b_fox_f[l])
        cw, w2, w2_pad = _prep_compress(w_cmpk_1[l], w_cmpk_2[l], pe_cmpk[l], w_cmpv_1[l], w_cmpv_2[l], pe_cmpv[l])
        merge = (row(g_fox_out[l]), row(g_nsa_out[l]), w_out[l].astype(BF16), row(g_mix_post[l]))

        hp = _half_ffn(yp, *ffn1, tm_p)
        small, fkv_t, nkv4_t, win_t, fq_t, fv_t, nq_t, nv_t, cmp, fk_p, nk_p = _project(
            hp, row(g_mix_pre[l]), w_big, w_small, b_small, cos_p, sin_p, tm_p, t // tm_p, True)
        small3 = small.reshape(b, t, LANES)
        per_head = lambda x: x.reshape(b, t, x.shape[1])
        cum, fk_b = _forget_bias(small3, per_head(fk_p), _row_tile(t, 512))
        o_fox = _fox_attention(fq_t, jnp.swapaxes(cum[:, :, :H_FOX], 1, 2), fk_b, fv_t,
                               _row_tile(t, 512), _row_tile(t, 512))
        kc_p, vc_t = _compress(cmp.reshape(b, t, 2 * W_KV), cw + (w2_pad,))
        o_nsa = _nsa_attention_t(nq_t, jnp.swapaxes(small3, 1, 2), kc_p, vc_t, per_head(nk_p), nv_t,
                                 _row_tile(t, 256), _row_tile(t, 512))
        yp = _merge_ffn(hp, o_fox.reshape(b * t, W_FOX), o_nsa.reshape(b * t, W_NSA), *merge, *ffn2, tm_p)
        keep = min(WINDOW, t)
        tokens_first = lambda x, *dims: jnp.transpose(x.reshape(b, *dims, x.shape[2]), (0, 4, 1, 2, 3))
        outs[0].append(tokens_first(fkv_t, 2, H_FOX, HEAD_DIM))
        outs[1].append(small3[:, :, :H_FOX])
        outs[2].append(tokens_first(nkv4_t, 4, H_KV, HEAD_DIM))
        outs[3].append(tokens_first(win_t[:, :, t - keep:], 2, H_KV, HEAD_DIM))

        hs = _half_ffn(ys, *ffn1, s)
        small, fkv, nkv4, win, fq, nq = _project(
            hs, row(g_mix_pre[l]), w_big, w_small, b_small, cos_s, sin_s, s, 1, False)
        as_rows = lambda x: x.reshape(s, 1, x.shape[1])
        o_fox = _fox_decode(page_table, as_rows(fq), as_rows(fkv), as_rows(small),
                            to_rows(cache_fox_kv[l]), jnp.swapaxes(cache_fox_logf[l], 1, 2))
        o_nsa = _nsa_decode(page_table, as_rows(nq), as_rows(nkv4), as_rows(win), as_rows(small),
                            to_rows(state_nsa_win_kv[l]), to_rows(cache_nsa_kv[l]), cw + (w2,))
        ys = _merge_ffn(hs, o_fox.reshape(s, W_FOX), o_nsa.reshape(s, W_NSA), *merge, *ffn2, s)
        keep = min(WINDOW, past_len + 1)
        kw_all = jnp.concatenate([state_nsa_win_kv[l], win.reshape(s, 1, 2, H_KV, HEAD_DIM)], axis=1)
        outs[4].append(fkv.reshape(s, 1, 2, H_FOX, HEAD_DIM))
        outs[5].append(small[:, :H_FOX].reshape(s, 1, H_FOX))
        outs[6].append(nkv4.reshape(s, 1, 4, H_KV, HEAD_DIM))
        outs[7].append(kw_all[:, kw_all.shape[1] - keep:])

    stacked = [jnp.stack(o, axis=0) for o in outs]
    return (yp.reshape(b, t, d), ys.reshape(s, 1, d), *stacked)
```

```python
import functools

import jax
import jax.numpy as jnp
from jax import lax
from jax.experimental import pallas as pl
from jax.experimental.pallas import tpu as pltpu

HEAD_DIM = 64
H_FOX = 8
H_NSA = 8
H_KV = 2
GQA_GROUP = H_NSA // H_KV
W_FOX = H_FOX * HEAD_DIM
W_NSA = H_NSA * HEAD_DIM
W_KV = H_KV * HEAD_DIM
CMP_STRIDE = 16
CMP_LEN = 32
SLC_LEN = 64
SLC_TOPN = 16
WINDOW = 512
ROPE_THETA = 10000.0
EPS = 1e-6
NEG = -1e30
FORCE_SCORE = 1e4
N_SMALL = H_FOX + 3 * H_NSA
GATE0 = H_FOX

LANES = 128
MXU_N = 256
VMEM_LIMIT = 56 * 1024 * 1024
FOX_HEADS_PER_STEP = 4

F32 = jnp.float32
BF16 = jnp.bfloat16


def _dot(a, b):
    return jnp.dot(a, b, preferred_element_type=F32)


def _dot_nt(a, b):
    return lax.dot_general(a, b, (((1,), (1,)), ((), ())), preferred_element_type=F32)


def _div(x, n):
    assert n & (n - 1) == 0
    return lax.shift_right_logical(x, jnp.int32(n.bit_length() - 1))


def _split3(x):
    hi = x.astype(BF16)
    r1 = x - hi.astype(F32)
    mid = r1.astype(BF16)
    lo = (r1 - mid.astype(F32)).astype(BF16)
    return hi, mid, lo


def _rms(x, g):
    return x * lax.rsqrt(jnp.mean(x * x, axis=-1, keepdims=True) + EPS) * g


def _ff_chunks(d_ff):
    step = 6 * MXU_N
    return tuple((c, min(c + step, d_ff)) for c in range(0, d_ff, step))


def _ffn_core(x, gpre, wg_ref, wu_ref, wd_ref, gpost, chunks):
    xn = _rms(x, gpre).astype(BF16)
    acc = jnp.zeros(x.shape, F32)
    for c0, c1 in chunks:
        g = _dot(xn, wg_ref[:, c0:c1])
        u = _dot(xn, wu_ref[:, c0:c1])
        hm = (g * jax.nn.sigmoid(g) * u).astype(BF16)
        acc = acc + _dot(hm, wd_ref[c0:c1, :])
    return x + 0.5 * _rms(acc, gpost)


def _const_spec(shape):
    nd = len(shape)
    return pl.BlockSpec(shape, lambda *_: (0,) * nd, pipeline_mode=pl.Buffered(1))


def _params(sem):
    return pltpu.CompilerParams(dimension_semantics=sem, vmem_limit_bytes=VMEM_LIMIT)


def _lo_half(rows):
    return lax.broadcasted_iota(jnp.int32, (rows, LANES), 1) < HEAD_DIM


def _pad_heads(x, n_heads, fill):
    lo = _lo_half(x.shape[0])
    out = []
    for h in range(n_heads):
        piece = x[:, (h // 2) * LANES:(h // 2 + 1) * LANES]
        if h % 2:
            piece = pltpu.roll(piece, HEAD_DIM, axis=1)
        out.append(jnp.where(lo, piece, fill))
    return out


def _ffn_kernel(x_ref, gpre_ref, wg_ref, wu_ref, wd_ref, gpost_ref, o_ref, *, chunks):
    o_ref[...] = _ffn_core(x_ref[...], gpre_ref[...], wg_ref, wu_ref, wd_ref, gpost_ref[...], chunks)


def _half_ffn(x, gpre, wg, wu, wd, gpost, tm):
    n, d = x.shape
    d_ff = wg.shape[1]
    row = pl.BlockSpec((tm, d), lambda i: (i, 0))
    return pl.pallas_call(
        functools.partial(_ffn_kernel, chunks=_ff_chunks(d_ff)),
        grid=(n // tm,),
        in_specs=[row, _const_spec((1, d)), _const_spec((d, d_ff)), _const_spec((d, d_ff)),
                  _const_spec((d_ff, d)), _const_spec((1, d))],
        out_specs=row,
        out_shape=jax.ShapeDtypeStruct((n, d), F32),
        compiler_params=_params(("parallel",)),
        name="half_ffn",
    )(x, gpre, wg, wu, wd, gpost)


_C_FQ, _C_FK, _C_FV, _C_NQ, _C_KV, _C_NQR, _C_KR, _C_END = 0, 512, 1024, 1536, 2048, 2816, 3328, 3712


def _proj_kernel(h_ref, g_ref, wb_ref, ws_ref, bf_ref, cos_ref, sin_ref, small_ref, *rest, packed, n_pos_tiles):
    n = _rms(h_ref[...], g_ref[...]).astype(BF16)
    tm = n.shape[0]

    def mm(c0, c1):
        return _dot(n, wb_ref[:, c0:c1])

    def put(ref, tiles):
        for i, t in enumerate(tiles):
            ref[:, i * LANES:(i + 1) * LANES] = t.astype(BF16)

    def put_t(ref, tiles):
        for i, t in enumerate(tiles):
            for c in range(t.shape[1] // LANES):
                r0 = i * t.shape[1] + c * LANES
                ref[r0:r0 + LANES, :] = t[:, c * LANES:(c + 1) * LANES].T

    scale = HEAD_DIM ** -0.5
    lane = lax.broadcasted_iota(jnp.int32, (tm, LANES), 1)
    fq = mm(_C_FQ, _C_FK) * scale
    fk = mm(_C_FK, _C_FV)
    fv = mm(_C_FV, _C_NQ)

    cos = cos_ref[...]
    sin = sin_ref[...]
    nq = mm(_C_NQ, _C_KV)
    nqr = mm(_C_NQR, _C_KR)
    nq = jnp.concatenate([(nq[:, c * LANES:(c + 1) * LANES] * cos + nqr[:, c * LANES:(c + 1) * LANES] * sin) * scale
                          for c in range(W_NSA // LANES)], axis=1)

    kv = mm(_C_KV, _C_NQR)
    kr = mm(_C_KR, _C_END)
    ks, vs = [], []
    for br in range(3):
        k = kv[:, br * 2 * W_KV: br * 2 * W_KV + W_KV] * cos + kr[:, br * W_KV:(br + 1) * W_KV] * sin
        v = kv[:, br * 2 * W_KV + W_KV:(br + 1) * 2 * W_KV]
        ks.append(k)
        vs.append(v)

    sm = _dot(n, ws_ref[...]) + bf_ref[...]
    log_sig = jnp.minimum(sm, 0.0) - jnp.log(1.0 + jnp.exp(-jnp.abs(sm)))
    small_ref[...] = jnp.where(lane < H_FOX, log_sig, jax.nn.sigmoid(sm))

    if not packed:
        fkv_ref, nkv4_ref, win_ref, fq_ref, nq_ref = rest
        fkv_ref[...] = jnp.concatenate([fk, fv], axis=1)
        nkv4_ref[...] = jnp.concatenate([ks[0], vs[0], ks[1], vs[1]], axis=1)
        win_ref[...] = jnp.concatenate([ks[2], vs[2]], axis=1)
        fq_ref[...] = fq
        nq_ref[...] = nq
        return
    fkvt_ref, nkv4t_ref, wint_ref, fqt_ref, fvt_ref, nqt_ref, nvt_ref, cmp_ref, fkp_ref, nkp_ref = rest
    put_t(fkvt_ref, [fk, fv])
    put_t(nkv4t_ref, [ks[0], vs[0], ks[1], vs[1]])
    put_t(wint_ref, [ks[2], vs[2]])
    cmp_ref[...] = jnp.concatenate([ks[0], vs[0]], axis=1)

    def put_heads_t(ref, tiles):
        for i, t in enumerate(tiles):
            ref[i * LANES:(i + 1) * LANES, :] = t.T.astype(BF16)

    put_heads_t(fqt_ref, _pad_heads(fq, H_FOX, 0.0))
    put_heads_t(fvt_ref, _pad_heads(fv, H_FOX, 1.0))
    put_heads_t(nqt_ref, _pad_heads(nq, H_NSA, 0.0))
    put_heads_t(nvt_ref, _pad_heads(vs[1], H_KV, 1.0) + _pad_heads(vs[2], H_KV, 1.0))
    ones3 = jnp.where((lane >= HEAD_DIM) & (lane < HEAD_DIM + 3), 1.0, 0.0)
    put(fkp_ref, _pad_heads(fk, H_FOX, ones3))
    pos = (lax.rem(pl.program_id(0), n_pos_tiles) * tm + lax.broadcasted_iota(jnp.int32, (tm, 1), 0))
    onehot = jnp.where(lane - HEAD_DIM == _div(pos, SLC_LEN), 1.0, 0.0)
    put(nkp_ref, _pad_heads(ks[1], H_KV, onehot) + _pad_heads(ks[2], H_KV, 0.0))


def _project(h, g, w_big, w_small, b_small, cos, sin, tm, n_pos_tiles, packed):
    n, d = h.shape
    row = lambda w: pl.BlockSpec((tm, w), lambda i: (i, 0))
    pos = pl.BlockSpec((tm, LANES), lambda i: (i % n_pos_tiles, 0))
    if packed:
        outs = [(LANES, F32), (2 * W_KV, F32), (H_FOX * LANES, BF16), (2 * H_KV * LANES, BF16)]
        t_outs = [(2 * W_FOX, F32), (4 * W_KV, F32), (2 * W_KV, F32), (H_FOX * LANES, BF16),
                  (H_FOX * LANES, BF16), (H_NSA * LANES, BF16), (2 * H_KV * LANES, BF16)]
    else:
        outs = [(LANES, F32), (2 * W_FOX, F32), (4 * W_KV, F32), (2 * W_KV, F32), (W_FOX, F32), (W_NSA, F32)]
        t_outs = []
    out_specs = [row(w) for w, _ in outs]
    out_shape = [jax.ShapeDtypeStruct((n, w), dt) for w, dt in outs]
    batch = n // (tm * n_pos_tiles)
    for k, (r, dt) in enumerate(t_outs):
        out_specs.insert(1 + k, pl.BlockSpec((None, r, tm), lambda i: (i // n_pos_tiles, 0, i % n_pos_tiles)))
        out_shape.insert(1 + k, jax.ShapeDtypeStruct((batch, r, tm * n_pos_tiles), dt))
    return pl.pallas_call(
        functools.partial(_proj_kernel, packed=packed, n_pos_tiles=n_pos_tiles),
        grid=(n // tm,),
        in_specs=[row(d), _const_spec((1, d)), _const_spec(w_big.shape), _const_spec(w_small.shape),
                  _const_spec((1, LANES)), pos, pos],
        out_specs=out_specs,
        out_shape=out_shape,
        compiler_params=_params(("parallel",)),
        name="project",
    )(h, g, w_big, w_small, b_small, cos, sin)


def _merge_ffn_kernel(h_ref, of_ref, on_ref, gf_ref, gn_ref, wo_ref, gmix_ref,
                      gpre_ref, wg_ref, wu_ref, wd_ref, gpost_ref, y_ref, *, chunks):
    of = _rms(of_ref[...], gf_ref[...]).astype(BF16)
    on = _rms(on_ref[...], gn_ref[...]).astype(BF16)
    mrg = _dot(of, wo_ref[0:W_FOX, :]) + _dot(on, wo_ref[W_FOX:W_FOX + W_NSA, :])
    h2 = h_ref[...] + _rms(mrg, gmix_ref[...])
    y_ref[...] = _ffn_core(h2, gpre_ref[...], wg_ref, wu_ref, wd_ref, gpost_ref[...], chunks)


def _merge_ffn(h, o_fox, o_nsa, gf, gn, w_out, gmix, gpre, wg, wu, wd, gpost, tm):
    n, d = h.shape
    d_ff = wg.shape[1]
    row = lambda w: pl.BlockSpec((tm, w), lambda i: (i, 0))
    return pl.pallas_call(
        functools.partial(_merge_ffn_kernel, chunks=_ff_chunks(d_ff)),
        grid=(n // tm,),
        in_specs=[row(d), row(W_FOX), row(W_NSA), _const_spec((1, W_FOX)), _const_spec((1, W_NSA)),
                  _const_spec(w_out.shape), _const_spec((1, d)), _const_spec((1, d)),
                  _const_spec((d, d_ff)), _const_spec((d, d_ff)), _const_spec((d_ff, d)), _const_spec((1, d))],
        out_specs=row(d),
        out_shape=jax.ShapeDtypeStruct((n, d), F32),
        compiler_params=_params(("parallel",)),
        name="merge_ffn",
    )(h, o_fox, o_nsa, gf, gn, w_out, gmix, gpre, wg, wu, wd, gpost)


def _forget_bias_kernel(x_ref, k_ref, cum_ref, k2_ref, carry_ref):
    @pl.when(pl.program_id(1) == 0)
    def _():
        carry_ref[...] = jnp.zeros(carry_ref.shape, F32)

    x = x_ref[...]
    tc = x.shape[0]
    r = lax.broadcasted_iota(jnp.int32, (tc, tc), 0)
    c = lax.broadcasted_iota(jnp.int32, (tc, tc), 1)
    tri = (c <= r).astype(BF16)
    hi, mid, lo = _split3(x)
    cs = _dot(tri, hi) + _dot(tri, mid) + _dot(tri, lo) + carry_ref[...]
    carry_ref[...] = cs[tc - 1:tc, :]
    cum_ref[...] = cs

    lane = lax.broadcasted_iota(jnp.int32, (tc, LANES), 1)
    for h in range(H_FOX):
        sl = slice(h * LANES, (h + 1) * LANES)
        neg = [p.astype(F32) for p in _split3(-cs[:, h:h + 1])]
        k = k_ref[:, sl].astype(F32)
        for i in range(3):
            k = jnp.where(lane == HEAD_DIM + 3 + i, neg[i], k)
        k2_ref[:, sl] = k.astype(BF16)


def _forget_bias(small, fk_p, tc):
    b, t, w = small.shape
    blk = lambda n: pl.BlockSpec((None, tc, n), lambda i, j: (i, j, 0))
    wide = H_FOX * LANES
    return pl.pallas_call(
        _forget_bias_kernel,
        grid=(b, t // tc),
        in_specs=[blk(w), blk(wide)],
        out_specs=[blk(w), blk(wide)],
        out_shape=[jax.ShapeDtypeStruct((b, t, w), F32), jax.ShapeDtypeStruct((b, t, wide), BF16)],
        scratch_shapes=[pltpu.VMEM((1, w), F32)],
        compiler_params=_params(("parallel", "arbitrary")),
        name="forget_bias",
    )(small, fk_p)


def _flash_update_t(s, vt, m_ref, acc_ref, idx, allow=None, bias=None):
    if allow is not None:
        s = jnp.where(allow, s, NEG)
    if bias is not None:
        s = s + bias
    m_old = m_ref[idx]
    m_new = jnp.maximum(m_old, jnp.max(s, axis=0, keepdims=True))
    p = jnp.exp(s - m_new)
    acc_ref[idx] = jnp.exp(m_old - m_new) * acc_ref[idx] + _dot(vt, p.astype(BF16))
    m_ref[idx] = m_new


def _normalise_t(acc):
    return acc[0:HEAD_DIM] * (1.0 / acc[HEAD_DIM:HEAD_DIM + 1])


def _fox_kernel(qt_ref, cum_ref, k_ref, vt_ref, o_ref, qb_ref, m_ref, acc_ref, *, tq, tk, heads):
    hg = pl.program_id(1)
    q0 = pl.program_id(2) * tq
    m_ref[...] = jnp.full(m_ref.shape, NEG, F32)
    acc_ref[...] = jnp.zeros(acc_ref.shape, F32)
    krow = lax.broadcasted_iota(jnp.int32, (tk, tq), 0)
    qpos = q0 + lax.broadcasted_iota(jnp.int32, (tk, tq), 1)
    head = lambda g: slice(g * LANES, (g + 1) * LANES)
    row = lax.broadcasted_iota(jnp.int32, (LANES, tq), 0)
    row8 = lax.broadcasted_iota(jnp.int32, (H_FOX, tq), 0)
    cum_all = cum_ref[...]
    for g in range(heads):
        cum = jnp.sum(jnp.where(row8 == hg * heads + g, cum_all, 0.0), axis=0, keepdims=True)
        q = qt_ref[head(g), :].astype(F32)
        for i, piece in enumerate(_split3(cum)):
            q = jnp.where(row == HEAD_DIM + i, piece.astype(F32), q)
        qb_ref[g] = jnp.where((row >= HEAD_DIM + 3) & (row < HEAD_DIM + 6), 1.0, q).astype(BF16)

    def scores(kt):
        k0 = pl.multiple_of(kt * tk, tk)
        return tuple(_dot(k_ref[pl.ds(k0, tk), head(g)], qb_ref[g]) for g in range(heads))

    def update(kt, s, masked):
        k0 = pl.multiple_of(kt * tk, tk)
        allow = (k0 + krow <= qpos) if masked else None
        for g in range(heads):
            _flash_update_t(s[g], vt_ref[head(g), pl.ds(k0, tk)], m_ref, acc_ref, g, allow=allow)

    def body(kt, s):
        s_next = scores(kt + 1)
        update(kt, s, False)
        return s_next

    n_full = _div(q0, tk)
    update(n_full, lax.fori_loop(0, n_full, body, scores(0)), True)
    for pr in range(heads // 2):
        pair = jnp.concatenate([_normalise_t(acc_ref[2 * pr]), _normalise_t(acc_ref[2 * pr + 1])], axis=0)
        o_ref[:, pr * LANES:(pr + 1) * LANES] = pair.T


def _fox_attention(fq_t, cum_t, fk_p, fv_t, tq, tk):
    b, t, _ = fk_p.shape
    g = FOX_HEADS_PER_STEP
    qblk = pl.BlockSpec((None, g * LANES, tq), lambda i, hg, qi: (i, hg, qi))
    cblk = pl.BlockSpec((None, H_FOX, tq), lambda i, hg, qi: (i, 0, qi))
    kblk = pl.BlockSpec((None, t, g * LANES), lambda i, hg, qi: (i, 0, hg))
    vblk = pl.BlockSpec((None, g * LANES, t), lambda i, hg, qi: (i, hg, 0))
    oblk = pl.BlockSpec((None, tq, g * HEAD_DIM), lambda i, hg, qi: (i, qi, hg))
    return pl.pallas_call(
        functools.partial(_fox_kernel, tq=tq, tk=tk, heads=g),
        grid=(b, H_FOX // g, t // tq),
        in_specs=[qblk, cblk, kblk, vblk],
        out_specs=oblk,
        out_shape=jax.ShapeDtypeStruct((b, t, W_FOX), F32),
        scratch_shapes=[pltpu.VMEM((g, LANES, tq), BF16), pltpu.VMEM((g, 1, tq), F32),
                        pltpu.VMEM((g, LANES, tq), F32)],
        compiler_params=_params(("parallel", "parallel", "arbitrary")),
        name="fox_prompt",
    )(fq_t, cum_t, fk_p, fv_t)


def _compress_hidden(load_rows, n_ch, wj_ref, pe_ref, wpe_ref):
    acc_k = jnp.zeros((n_ch, 2 * W_KV), F32)
    acc_v = jnp.zeros((n_ch, 2 * W_KV), F32)
    for j in range(CMP_STRIDE):
        xk, xv = load_rows(j)
        acc_k = acc_k + _dot(xk.astype(BF16), wj_ref[j, 0:W_KV, :])
        acc_v = acc_v + _dot(xv.astype(BF16), wj_ref[j, W_KV:2 * W_KV, :])
    first = jnp.concatenate([acc_k[:, 0:W_KV], acc_v[:, 0:W_KV]], axis=1)
    second = jnp.concatenate([acc_k[:, W_KV:], acc_v[:, W_KV:]], axis=1)
    second = pltpu.roll(second, n_ch - 1, axis=0)
    pe_term = _dot(pe_ref[...], wpe_ref[...])[0:1, :]
    return jax.nn.gelu(first + second + pe_term)


def _compress_kernel(xk_ref, xv_ref, wj_ref, pe_ref, wpe_ref, w2p_ref, kc_ref, vc_ref, *, n_ch):
    rows = lambda j: (xk_ref[pl.ds(j, n_ch, stride=CMP_STRIDE), :], xv_ref[pl.ds(j, n_ch, stride=CMP_STRIDE), :])
    hid = _compress_hidden(rows, n_ch, wj_ref, pe_ref, wpe_ref).astype(BF16)
    out = _dot(hid, w2p_ref[...])
    kc_ref[...] = out[:, 0:H_KV * LANES].astype(BF16)
    for h in range(H_KV):
        vc = out[:, (H_KV + h) * LANES:(H_KV + h + 1) * LANES]
        vc_ref[h * LANES:(h + 1) * LANES, :] = jnp.where(_lo_half(n_ch), vc, 1.0).T.astype(BF16)


def _compress(nkv4, cw):
    b, t, _ = nkv4.shape
    n_ch = t // CMP_STRIDE
    out = pl.BlockSpec((None, n_ch, H_KV * LANES), lambda i: (i, 0, 0))
    return pl.pallas_call(
        functools.partial(_compress_kernel, n_ch=n_ch),
        grid=(b,),
        in_specs=[pl.BlockSpec((None, t, W_KV), lambda i: (i, 0, 0)),
                  pl.BlockSpec((None, t, W_KV), lambda i: (i, 0, 1))] + [_const_spec(w.shape) for w in cw],
        out_specs=[out, pl.BlockSpec((None, H_KV * LANES, n_ch), lambda i: (i, 0, 0))],
        out_shape=[jax.ShapeDtypeStruct((b, n_ch, H_KV * LANES), BF16),
                   jax.ShapeDtypeStruct((b, H_KV * LANES, n_ch), BF16)],
        compiler_params=_params(("parallel",)),
        name="compress_prompt",
    )(nkv4, nkv4, *cw)


def _overlap(n_ch, n_slc):
    n = lax.broadcasted_iota(jnp.int32, (n_ch, LANES), 0)
    j = lax.broadcasted_iota(jnp.int32, (n_ch, LANES), 1)
    hit = ((n * CMP_STRIDE <= j * SLC_LEN + SLC_LEN - 1) & (n * CMP_STRIDE + CMP_LEN - 1 >= j * SLC_LEN)
           & (n < n_ch - 1) & (j < n_slc))
    return hit.astype(BF16)


def _overlap_t(n_ch, n_slc):
    j = lax.broadcasted_iota(jnp.int32, (LANES, n_ch), 0)
    n = lax.broadcasted_iota(jnp.int32, (LANES, n_ch), 1)
    hit = ((n * CMP_STRIDE <= j * SLC_LEN + SLC_LEN - 1) & (n * CMP_STRIDE + CMP_LEN - 1 >= j * SLC_LEN)
           & (n < n_ch - 1) & (j < n_slc))
    return hit.astype(BF16)


def _block_scores(imp, pos, n_slc, axis):
    blk = lax.broadcasted_iota(jnp.int32, imp.shape, axis)
    qblk = _div(pos, SLC_LEN)
    forced = (blk == 0) | (blk == qblk) | (blk == qblk - 1)
    valid = blk * SLC_LEN <= pos
    score = jnp.where(valid, jnp.where(forced, FORCE_SCORE, imp), -1.0)
    return jnp.where(blk < n_slc, score, -2.0)


def _count_beats_lanes(score, n_slc):
    blk = lax.broadcasted_iota(jnp.int32, score.shape, 1)
    cnt = jnp.zeros(score.shape, jnp.int32)
    for i in range(n_slc):
        si = score[:, i:i + 1]
        cnt = cnt + jnp.where((si > score) | ((si == score) & (blk > i)), 1, 0)
    return cnt


def _count_beats_rows(score, lo, hi, n_rows):
    out = []
    for v in range(n_rows // 8):
        s_v = score[8 * v:8 * v + 8, :]
        blk = 8 * v + lax.broadcasted_iota(jnp.int32, s_v.shape, 0)
        cnt = jnp.zeros(s_v.shape, jnp.int32)
        for i in range(lo, hi):
            si = score[i:i + 1, :]
            if 8 * v + 7 < i:
                beats = si > s_v
            elif 8 * v > i:
                beats = si >= s_v
            else:
                beats = (si > s_v) | ((si == s_v) & (blk > i))
            cnt = cnt + jnp.where(beats, 1, 0)
        out.append(cnt)
    return jnp.concatenate(out, axis=0)


def _nsa_t2_kernel(qt_ref, gate_ref, kc_ref, vct_ref, sk_ref, svt_ref, wk_ref, wvt_ref, o_ref,
                   q4_ref, q4s_ref, cnt_ref, m_ref, acc_ref, *, tq, tk, n_ch, n_slc, wlen):
    q0 = pl.program_id(1) * tq
    g = GQA_GROUP
    w = g * tq
    n_sel = min(SLC_TOPN, n_slc)
    hk = range(H_KV)
    blk = lambda h: slice(h * LANES, (h + 1) * LANES)
    pos = q0 + lax.broadcasted_iota(jnp.int32, (1, tq), 1)
    heads = lambda x: jnp.concatenate([x] * g, axis=1)
    for h in hk:
        for i in range(g):
            q4_ref[h, :, i * tq:(i + 1) * tq] = qt_ref[(g * h + i) * LANES:(g * h + i + 1) * LANES, :]
    q4 = [q4_ref[h] for h in hk]

    n_idx = lax.broadcasted_iota(jnp.int32, (n_ch, tq), 0)
    ok = (n_idx * CMP_STRIDE + CMP_LEN - 1 <= pos) & (n_idx < n_ch - 1)
    ok_bias = heads(jnp.where(ok, 0.0, NEG))
    ok_keep = heads(jnp.where(ok, 1.0, 0.0))
    ov_t = _overlap_t(n_ch, n_slc)
    o_cmp, score = [], []
    for h in hk:
        sc = _dot(kc_ref[:, blk(h)], q4[h]) + ok_bias
        e = jnp.exp(sc - jnp.max(sc, axis=0, keepdims=True)) * ok_keep
        lc = jnp.sum(e, axis=0, keepdims=True)
        a = e * (1.0 / jnp.where(lc > 0.0, lc, 1.0))
        a_hi = a.astype(BF16)
        a_lo = (a - a_hi.astype(F32)).astype(BF16)
        o_cmp.append(_dot(vct_ref[blk(h), :], a_hi))
        imp4 = _dot(ov_t, a_hi) + _dot(ov_t, a_lo)
        imp_t = sum(imp4[:, i * tq:(i + 1) * tq] for i in range(g))
        score.append(_block_scores(imp_t, pos, n_slc, 0))

    n_valid = _div(q0 + tq - 1, SLC_LEN) + 1
    n_rows = cnt_ref.shape[1]
    cnt_ref[...] = jnp.zeros(cnt_ref.shape, jnp.int32)
    for b0 in range(0, n_slc, 8):
        @pl.when((b0 < n_valid) & (n_valid > n_sel))
        def _():
            for h in hk:
                cnt_ref[h] += _count_beats_rows(score[h], b0, min(b0 + 8, n_slc), n_rows)
    for h in hk:
        sel_t = (cnt_ref[h] < n_sel) & (lax.broadcasted_iota(jnp.int32, (n_rows, tq), 0) < n_slc)
        bias = jnp.where(sel_t, 0.0, NEG)
        if n_rows < HEAD_DIM:
            bias = jnp.concatenate([bias, jnp.full((HEAD_DIM - n_rows, tq), NEG, F32)], axis=0)
        q4s_ref[h, 0:HEAD_DIM, :] = q4[h][0:HEAD_DIM]
        for i in range(g):
            q4s_ref[h, HEAD_DIM:2 * HEAD_DIM, i * tq:(i + 1) * tq] = bias.astype(BF16)

    m_ref[...] = jnp.full(m_ref.shape, NEG, F32)
    acc_ref[...] = jnp.zeros(acc_ref.shape, F32)
    krow = lax.broadcasted_iota(jnp.int32, (tk, tq), 0)

    def scores(kt):
        k0 = pl.multiple_of(kt * tk, tk)
        return tuple(_dot(sk_ref[pl.ds(k0, tk), blk(h)], q4s_ref[h]) for h in hk)

    def update(kt, s, masked):
        k0 = pl.multiple_of(kt * tk, tk)
        bias = heads(jnp.where(k0 + krow <= pos, 0.0, NEG)) if masked else None
        for h in hk:
            _flash_update_t(s[h], svt_ref[blk(h), pl.ds(k0, tk)], m_ref, acc_ref, h, bias=bias)

    def body(kt, s):
        s_next = scores(kt + 1)
        update(kt, s, False)
        return s_next

    n_full = _div(q0, tk)
    update(n_full, lax.fori_loop(0, n_full, body, scores(0)), True)

    ws = pl.multiple_of(jnp.maximum(q0 + tq - wlen, 0), tq)
    dist = pos - (ws + lax.broadcasted_iota(jnp.int32, (wlen, tq), 0))
    win_bias = heads(jnp.where((dist >= 0) & (dist < WINDOW), 0.0, NEG))
    gates = gate_ref[...]
    grow = lax.broadcasted_iota(jnp.int32, (LANES, tq), 0)
    for h in hk:
        o_slc = _normalise_t(acc_ref[h])
        sw = _dot(wk_ref[pl.ds(ws, wlen), blk(h)], q4[h]) + win_bias
        pw = jnp.exp(sw - jnp.max(sw, axis=0, keepdims=True))
        o_win = _normalise_t(_dot(wvt_ref[blk(h), pl.ds(ws, wlen)], pw.astype(BF16)))
        vals = []
        for i in range(g):
            r0 = GATE0 + 3 * (g * h + i)
            gc = [jnp.sum(jnp.where(grow == r0 + c, gates, 0.0), axis=0, keepdims=True) for c in range(3)]
            sl = slice(i * tq, (i + 1) * tq)
            vals.append(gc[0] * o_cmp[h][0:HEAD_DIM, sl] + gc[1] * o_slc[:, sl] + gc[2] * o_win[:, sl])
        for pr in range(g // 2):
            lanes = slice((h * (g // 2) + pr) * LANES, (h * (g // 2) + pr + 1) * LANES)
            o_ref[:, lanes] = jnp.concatenate([vals[2 * pr], vals[2 * pr + 1]], axis=0).T


def _nsa_attention_t2(nq_t, small_t, kc_p, vc_t, nk_p, nv_t, tq, tk):
    b, _, t = nq_t.shape
    n_ch = kc_p.shape[1]
    n_slc = -(-t // SLC_LEN)
    assert n_slc <= HEAD_DIM, "the block mask rides in the 64 spare query rows"
    wlen = min(t, WINDOW + tq)
    g = GQA_GROUP
    wide = H_KV * LANES
    kblk = lambda c: pl.BlockSpec((None, t, wide), lambda i, qi: (i, 0, c))
    vblk = lambda c: pl.BlockSpec((None, wide, t), lambda i, qi: (i, c, 0))
    return pl.pallas_call(
        functools.partial(_nsa_t2_kernel, tq=tq, tk=tk, n_ch=n_ch, n_slc=n_slc, wlen=wlen),
        grid=(b, t // tq),
        in_specs=[pl.BlockSpec((None, H_NSA * LANES, tq), lambda i, qi: (i, 0, qi)),
                  pl.BlockSpec((None, LANES, tq), lambda i, qi: (i, 0, qi)),
                  pl.BlockSpec((None, n_ch, wide), lambda i, qi: (i, 0, 0)),
                  pl.BlockSpec((None, wide, n_ch), lambda i, qi: (i, 0, 0)),
                  kblk(0), vblk(0), kblk(1), vblk(1)],
        out_specs=pl.BlockSpec((None, tq, W_NSA), lambda i, qi: (i, qi, 0)),
        out_shape=jax.ShapeDtypeStruct((b, t, W_NSA), F32),
        scratch_shapes=[pltpu.VMEM((H_KV, LANES, g * tq), BF16), pltpu.VMEM((H_KV, LANES, g * tq), BF16),
                        pltpu.VMEM((H_KV, -(-n_slc // 8) * 8, tq), jnp.int32),
                        pltpu.VMEM((H_KV, 1, g * tq), F32), pltpu.VMEM((H_KV, LANES, g * tq), F32)],
        compiler_params=_params(("parallel", "arbitrary")),
        name="nsa_prompt",
    )(nq_t, small_t, kc_p, vc_t, nk_p, nv_t, nk_p, nv_t)


def _page_copies(pt_ref, step, slot, n_pages, page, streams, group):
    out = []
    for u in range(group):
        for pg in range(n_pages):
            idx = pt_ref[(step * group + u) * n_pages + pg]
            for hbm, buf, sem, on_lanes in streams:
                dst = buf.at[slot, u, :, pl.ds(pg * page, page)] if on_lanes else buf.at[slot, u, pg]
                out.append(pltpu.make_async_copy(hbm.at[idx], dst, sem.at[slot]))
    return out


def _gather_pages(pt_ref, n_pages, page, streams, group=1):
    i = pl.program_id(0)
    slot = lax.rem(i, 2)

    @pl.when(i == 0)
    def _():
        for c in _page_copies(pt_ref, 0, 0, n_pages, page, streams, group):
            c.start()

    @pl.when(i + 1 < pl.num_programs(0))
    def _():
        for c in _page_copies(pt_ref, i + 1, 1 - slot, n_pages, page, streams, group):
            c.start()

    for c in _page_copies(pt_ref, i, slot, n_pages, page, streams, group):
        c.wait()
    return slot


def _head_scores(q_col, k_view, s_ref, head, rows0, n_tiles, tile):
    qb = jnp.broadcast_to(q_col, (HEAD_DIM, tile))
    for pg in range(n_tiles):
        kt = k_view[rows0:rows0 + HEAD_DIM, pg * tile:(pg + 1) * tile]
        s_ref[pg, head:head + 1, :] = jnp.sum(kt * qb, axis=0, keepdims=True)


def _head_values(p_tile, v_view, rows0, n_tiles, tile):
    acc = jnp.zeros((HEAD_DIM, tile), F32)
    for pg in range(n_tiles):
        vt = v_view[rows0:rows0 + HEAD_DIM, pg * tile:(pg + 1) * tile]
        acc = acc + vt * jnp.broadcast_to(p_tile(pg), (HEAD_DIM, tile))
    return jnp.sum(acc, axis=1, keepdims=True)


def _softmax_tiles(s, s_new):
    m = jnp.maximum(jnp.max(jnp.max(s, axis=0), axis=1, keepdims=True), s_new)
    p = jnp.exp(s - m[None])
    p_new = jnp.exp(s_new - m)
    l = jnp.sum(jnp.sum(p, axis=0), axis=1, keepdims=True) + p_new
    return p, p_new, 1.0 / l


def _softmax_rows(s, s_new, allow):
    s = jnp.where(allow, s, NEG)
    m = jnp.maximum(jnp.max(s, axis=1, keepdims=True), s_new)
    p = jnp.where(allow, jnp.exp(s - m), 0.0)
    p_new = jnp.exp(s_new - m)
    return p, p_new, 1.0 / (jnp.sum(p, axis=1, keepdims=True) + p_new)


def _col_dot(a_col, b_col, n_heads):
    return jnp.sum((a_col * b_col).reshape(n_heads, HEAD_DIM, 1), axis=1)


def _as_column(row):
    return jnp.broadcast_to(row, (LANES, row.shape[1])).T[:, 0:1]


def _as_row(col):
    return jnp.broadcast_to(col, (col.shape[0], LANES)).T[0:1, :]


def _fox_dec_kernel(pt_ref, q_ref, kvnew_ref, small_ref, kv_hbm, lf_hbm, o_ref,
                    kvbuf, lfbuf, s_ref, p_ref, sem_kv, sem_lf, *, n_pages, page):
    slot = _gather_pages(pt_ref, n_pages, page,
                         [(kv_hbm, kvbuf, sem_kv, True), (lf_hbm, lfbuf, sem_lf, False)])
    kv = kvbuf.at[slot, 0]
    q = _as_column(q_ref[...])
    kv_new = _as_column(kvnew_ref[...])
    lf_new = jnp.concatenate([_as_column(small_ref[...])[0:H_FOX]] * n_pages, axis=0)
    for h in range(H_FOX):
        _head_scores(q[h * HEAD_DIM:(h + 1) * HEAD_DIM], kv, s_ref, h, h * HEAD_DIM, n_pages, page)

    rows = n_pages * H_FOX
    lf = lfbuf[slot, 0].reshape(rows, page)
    r = lax.broadcasted_iota(jnp.int32, (page, page), 0)
    c = lax.broadcasted_iota(jnp.int32, (page, page), 1)
    later = (r > c).astype(BF16)
    hi, mid, lo = _split3(lf)
    within = _dot(hi, later) + _dot(mid, later) + _dot(lo, later)
    r = lax.broadcasted_iota(jnp.int32, (rows, rows), 0)
    c = lax.broadcasted_iota(jnp.int32, (rows, rows), 1)
    later_pages = ((c > r) & (((c - r) & (H_FOX - 1)) == 0)).astype(BF16)
    tot = jnp.broadcast_to(jnp.sum(lf, axis=1, keepdims=True), (rows, page))
    hi, mid, lo = _split3(tot)
    beyond = _dot(later_pages, hi) + _dot(later_pages, mid) + _dot(later_pages, lo)
    bias = (within + beyond + lf_new).reshape(n_pages, H_FOX, page)

    s_new = _col_dot(q, kv_new[0:W_FOX], H_FOX)
    p, p_new, inv_l = _softmax_tiles(s_ref[...] + bias, s_new)
    p_ref[...] = p
    outs = []
    for h in range(H_FOX):
        o = _head_values(lambda pg, h=h: p_ref[pg, h:h + 1, :], kv, W_FOX + h * HEAD_DIM, n_pages, page)
        v_new = kv_new[W_FOX + h * HEAD_DIM: W_FOX + (h + 1) * HEAD_DIM]
        outs.append((o + p_new[h:h + 1] * v_new) * inv_l[h:h + 1])
    o_ref[...] = _as_row(jnp.concatenate(outs, axis=0))


def _fox_decode(page_table, q_row, kv_row, small_row, cache_kvt, cache_lft):
    s, n_pages = page_table.shape
    rows, page = cache_kvt.shape[1], cache_kvt.shape[2]
    col = lambda n: pl.BlockSpec((None, 1, n), lambda i, pt: (i, 0, 0))
    anyspec = pl.BlockSpec(memory_space=pl.ANY)
    return pl.pallas_call(
        functools.partial(_fox_dec_kernel, n_pages=n_pages, page=page),
        grid_spec=pltpu.PrefetchScalarGridSpec(
            num_scalar_prefetch=1,
            grid=(s,),
            in_specs=[col(W_FOX), col(2 * W_FOX), col(LANES), anyspec, anyspec],
            out_specs=col(W_FOX),
            scratch_shapes=[pltpu.VMEM((2, 1, rows, n_pages * page), F32),
                            pltpu.VMEM((2, 1, n_pages, H_FOX, page), F32),
                            pltpu.VMEM((n_pages, H_FOX, page), F32), pltpu.VMEM((n_pages, H_FOX, page), F32),
                            pltpu.SemaphoreType.DMA((2,)), pltpu.SemaphoreType.DMA((2,))],
        ),
        out_shape=jax.ShapeDtypeStruct((s, 1, W_FOX), F32),
        compiler_params=_params(("arbitrary",)),
        name="fox_decode",
    )(page_table.reshape(-1), q_row, kv_row, small_row, cache_kvt, cache_lft)


def _nsa_dec_kernel(pt_ref, qrow_ref, nkv4_ref, wnew_ref, gate_ref, win_ref, cache_hbm,
                    wj_ref, pe_ref, wpe_ref, w2_ref, o_ref,
                    xbuf, xk_buf, xv_buf, sem, *, n_pages, page, group):
    slot = _gather_pages(pt_ref, n_pages, page, [(cache_hbm, xbuf, sem, True)], group)
    n_ch = n_pages * page // CMP_STRIDE

    r = lax.broadcasted_iota(jnp.int32, (page, page), 0)
    t = lax.broadcasted_iota(jnp.int32, (page, page), 1)
    per = page // CMP_STRIDE
    perm = (t == CMP_STRIDE * (r & (per - 1)) + _div(r, per)).astype(BF16)
    for u in range(group):
        for pg in range(n_pages):
            xt = _dot_nt(perm, xbuf[slot, u, 0:2 * W_KV, pg * page:(pg + 1) * page].astype(BF16))
            c0 = u * n_ch + pg * per
            for j in range(CMP_STRIDE):
                xk_buf[j, c0:c0 + per, :] = xt[j * per:(j + 1) * per, 0:W_KV]
                xv_buf[j, c0:c0 + per, :] = xt[j * per:(j + 1) * per, W_KV:2 * W_KV]
    hid = _compress_hidden(lambda jj: (xk_buf[jj], xv_buf[jj]), group * n_ch, wj_ref, pe_ref, wpe_ref).astype(BF16)
    for u in range(group):
        o_ref[u] = _nsa_dec_one(hid[u * n_ch:(u + 1) * n_ch], xbuf.at[slot, u], qrow_ref[u], nkv4_ref[u],
                                wnew_ref[u], gate_ref[u], win_ref.at[u], w2_ref, n_pages, page)


def _nsa_dec_one(hid, x, q_row, nkv4_new, win_new, gates, wv, w2_ref, n_pages, page):
    past_len = n_pages * page
    n_ch = past_len // CMP_STRIDE
    n_slc = past_len // SLC_LEN + 1
    n_sel = min(SLC_TOPN, n_slc)
    win_buf = wv.shape[1]
    g = GQA_GROUP
    row = lax.broadcasted_iota(jnp.int32, (H_NSA, LANES), 0)
    lane = lax.broadcasted_iota(jnp.int32, (H_NSA, LANES), 1)
    kc = _dot(hid, w2_ref[:, 0:W_KV]).astype(BF16)
    vc = _dot(hid, w2_ref[:, W_KV:2 * W_KV]).astype(BF16)

    qbd = jnp.zeros((H_NSA, LANES), F32)
    for i in range(H_NSA):
        piece = q_row[:, (i // 2) * LANES:(i // 2 + 1) * LANES]
        if (i % 2) != (i // g):
            piece = pltpu.roll(piece, HEAD_DIM, axis=1)
        qbd = jnp.where(row == i, jnp.broadcast_to(piece, (H_NSA, LANES)), qbd)
    qbd = jnp.where(_div(lane, HEAD_DIM) == _div(row, g), qbd, 0.0).astype(BF16)
    n_idx = lax.broadcasted_iota(jnp.int32, (H_NSA, n_ch), 1)
    ok = (n_idx * CMP_STRIDE + CMP_LEN - 1 <= past_len) & (n_idx < n_ch - 1)
    sc = jnp.where(ok, _dot_nt(qbd, kc), NEG)
    e = jnp.where(ok, jnp.exp(sc - jnp.max(sc, axis=1, keepdims=True)), 0.0)
    lc = jnp.sum(e, axis=1, keepdims=True)
    a = e * (1.0 / jnp.where(lc > 0.0, lc, 1.0))
    a_hi = a.astype(BF16)
    a_lo = (a - a_hi.astype(F32)).astype(BF16)
    o_cmp = _dot(a_hi, vc)
    ov = _overlap(n_ch, n_slc)
    imp8 = _dot(a_hi, ov) + _dot(a_lo, ov)
    imp = jnp.zeros((H_NSA, LANES), F32)
    for h in range(H_KV):
        tot = jnp.sum(imp8[h * g:(h + 1) * g], axis=0, keepdims=True)
        imp = jnp.where(_div(row, g) == h, jnp.broadcast_to(tot, (H_NSA, LANES)), imp)
    score = _block_scores(imp, jnp.full((H_NSA, 1), past_len, jnp.int32), n_slc, 1)
    sel = jnp.where((_count_beats_lanes(score, n_slc) < n_sel) & (lane < n_slc), 1.0, 0.0)

    per_page = page // SLC_LEN
    allow = []
    for pg in range(n_pages):
        m = jnp.zeros((H_NSA, page), F32)
        for b in range(per_page):
            blk = pg * per_page + b
            lanes_b = _div(lax.broadcasted_iota(jnp.int32, (H_NSA, page), 1), SLC_LEN) == b
            m = jnp.where(lanes_b, jnp.broadcast_to(sel[:, blk:blk + 1], (H_NSA, page)), m)
        allow.append(m > 0.5)
    allow = jnp.concatenate(allow, axis=1)
    qf = qbd.astype(F32)
    s_new = jnp.sum(qf * nkv4_new[:, 2 * W_KV:3 * W_KV], axis=1, keepdims=True)
    p, p_new, inv_l = _softmax_rows(_dot(qbd, x[2 * W_KV:3 * W_KV, :].astype(BF16)), s_new, allow)
    o_slc = (_dot_nt(p.astype(BF16), x[3 * W_KV:4 * W_KV, :].astype(BF16))
             + p_new.astype(BF16).astype(F32) * nkv4_new[:, 3 * W_KV:4 * W_KV].astype(BF16).astype(F32)) * inv_l

    slot_idx = lax.broadcasted_iota(jnp.int32, (H_NSA, win_buf), 1)
    sw_new = jnp.sum(qf * win_new[:, 0:W_KV], axis=1, keepdims=True)
    pw, pw_new, inv_lw = _softmax_rows(_dot(qbd, wv[0:W_KV, :].astype(BF16)), sw_new, (win_buf - slot_idx) < WINDOW)
    o_win = (_dot_nt(pw.astype(BF16), wv[W_KV:2 * W_KV, :].astype(BF16))
             + pw_new.astype(BF16).astype(F32) * win_new[:, W_KV:2 * W_KV].astype(BF16).astype(F32)) * inv_lw

    gate_rows = jnp.broadcast_to(gates, (H_NSA, LANES))
    gc = [jnp.sum(jnp.where(lane == GATE0 + 3 * row + c, gate_rows, 0.0), axis=1, keepdims=True) for c in range(3)]
    val = gc[0] * o_cmp + gc[1] * o_slc + gc[2] * o_win
    pieces = []
    for i in range(H_NSA):
        piece = val[i:i + 1]
        if (i % 2) != (i // g):
            piece = pltpu.roll(piece, HEAD_DIM, axis=1)
        pieces.append(piece)
    lo = lax.broadcasted_iota(jnp.int32, (1, LANES), 1) < HEAD_DIM
    return jnp.concatenate([jnp.where(lo, pieces[2 * pr], pieces[2 * pr + 1]) for pr in range(H_NSA // 2)], axis=1)


def _nsa_decode(page_table, q_row, nkv4_row, win_row, small, win_t, cache_t, cw):
    s, n_pages = page_table.shape
    rows, page = cache_t.shape[1], cache_t.shape[2]
    past_len = n_pages * page
    n_ch = past_len // CMP_STRIDE
    win_buf = win_t.shape[2]
    group = next(g for g in (4, 2, 1) if s % g == 0)
    rowspec = lambda n: pl.BlockSpec((group, 1, n), lambda i, pt: (i, 0, 0))
    const = lambda shape: pl.BlockSpec(shape, lambda i, pt: (0,) * len(shape), pipeline_mode=pl.Buffered(1))
    return pl.pallas_call(
        functools.partial(_nsa_dec_kernel, n_pages=n_pages, page=page, group=group),
        grid_spec=pltpu.PrefetchScalarGridSpec(
            num_scalar_prefetch=1,
            grid=(s // group,),
            in_specs=[rowspec(W_NSA), rowspec(4 * W_KV), rowspec(2 * W_KV), rowspec(LANES),
                      pl.BlockSpec((group, 2 * W_KV, win_buf), lambda i, pt: (i, 0, 0)),
                      pl.BlockSpec(memory_space=pl.ANY)] + [const(w.shape) for w in cw],
            out_specs=rowspec(W_NSA),
            scratch_shapes=[pltpu.VMEM((2, group, rows, past_len), F32),
                            pltpu.VMEM((CMP_STRIDE, group * n_ch, W_KV), F32),
                            pltpu.VMEM((CMP_STRIDE, group * n_ch, W_KV), F32),
                            pltpu.SemaphoreType.DMA((2,))],
        ),
        out_shape=jax.ShapeDtypeStruct((s, 1, W_NSA), F32),
        compiler_params=_params(("arbitrary",)),
        name="nsa_decode",
    )(page_table.reshape(-1), q_row, nkv4_row, win_row, small, win_t, cache_t, *cw)


def _rot_cols(w):
    d, n = w.shape
    w = w.reshape(d, n // HEAD_DIM, 2, HEAD_DIM // 2)
    return jnp.stack([-w[:, :, 1], w[:, :, 0]], axis=2).reshape(d, n)


def _prep_projection(w_in, b_f):
    c = [0, W_FOX, 2 * W_FOX, 3 * W_FOX, 3 * W_FOX + H_FOX, 3 * W_FOX + H_FOX + W_NSA,
         3 * W_FOX + H_FOX + W_NSA + 6 * W_KV]
    fq, fk, fv, ff, nq, kv = (w_in[:, c[i]:c[i + 1]] for i in range(6))
    gt = w_in[:, c[6]:]
    ks = jnp.concatenate([kv[:, br * 2 * W_KV: br * 2 * W_KV + W_KV] for br in range(3)], axis=1)
    w_big = jnp.concatenate([fq, fk, fv, nq, kv, _rot_cols(nq), _rot_cols(ks)], axis=1).astype(BF16)
    d = w_in.shape[0]
    w_small = jnp.concatenate([ff, gt, jnp.zeros((d, LANES - N_SMALL), w_in.dtype)], axis=1).astype(BF16)
    b_small = jnp.concatenate([b_f.astype(F32), jnp.zeros((LANES - H_FOX,), F32)])[None, :]
    return w_big, w_small, b_small


def _prep_compress(wk1, wk2, pek, wv1, wv2, pev):
    ratio = CMP_LEN // CMP_STRIDE
    eye = jnp.eye(4, dtype=F32)

    def blocks(w):
        return w.reshape(ratio, CMP_STRIDE, HEAD_DIM, wk1.shape[1])

    per_head = lambda w: jnp.einsum("rjde,hk->jhdrke", blocks(w), jnp.eye(H_KV, dtype=F32)).reshape(
        CMP_STRIDE, W_KV, ratio * W_KV)
    wj = jnp.concatenate([per_head(wk1), per_head(wv1)], axis=1).astype(BF16)
    pe = jnp.concatenate([pek.reshape(-1), pev.reshape(-1)])
    pe = jnp.broadcast_to(pe[None, :], (8, pe.shape[0])).astype(BF16)
    zero = jnp.zeros_like(wk1)
    wpe = jnp.concatenate([jnp.concatenate([wk1, wk1, zero, zero], axis=1),
                           jnp.concatenate([zero, zero, wv1, wv1], axis=1)], axis=0).astype(BF16)
    w2 = jnp.einsum("gde,gh->gdhe", jnp.stack([wk2, wk2, wv2, wv2]), eye)
    w2_pad = jnp.concatenate([w2, jnp.zeros_like(w2)], axis=3)
    w2 = w2.reshape(4 * HEAD_DIM, 4 * HEAD_DIM).astype(BF16)
    w2_pad = w2_pad.reshape(4 * HEAD_DIM, 4 * LANES).astype(BF16)
    return (wj, pe, wpe), w2, w2_pad


def _rope_tables(pos):
    half = HEAD_DIM // 2
    inv = ROPE_THETA ** (-jnp.arange(half, dtype=F32) / half)
    ang = pos.astype(F32)[:, None] * inv[None, :]
    reps = LANES // half
    return jnp.tile(jnp.cos(ang), (1, reps)), jnp.tile(jnp.sin(ang), (1, reps))


def _row_tile(n, cap):
    t = min(n, cap)
    while n % t:
        t //= 2
    return t


def kernel(x_prompt, x_sample, cache_fox_kv, cache_fox_logf, cache_nsa_kv, state_nsa_win_kv, page_table,
           g_ffn1_pre, w_ffn1_gate, w_ffn1_up, w_ffn1_down, g_ffn1_post, g_mix_pre, w_in, b_fox_f,
           w_cmpk_1, w_cmpk_2, pe_cmpk, w_cmpv_1, w_cmpv_2, pe_cmpv, g_fox_out, g_nsa_out, w_out,
           g_mix_post, g_ffn2_pre, w_ffn2_gate, w_ffn2_up, w_ffn2_down, g_ffn2_post):
    depth = w_in.shape[0]
    b, t, d = x_prompt.shape
    s, dec_seq, _ = x_sample.shape
    assert dec_seq == 1, "the sample group decodes one token per sequence"
    page = cache_fox_kv.shape[2]
    n_pages = page_table.shape[1]
    past_len = n_pages * page
    assert t % LANES == 0 and page % SLC_LEN == 0
    page_table = page_table.astype(jnp.int32)

    tm_p = _row_tile(t, 512)
    cos_p, sin_p = _rope_tables(jnp.arange(t, dtype=jnp.int32))
    cos_s, sin_s = _rope_tables(jnp.full((s,), past_len, jnp.int32))
    row = lambda v: v.astype(F32)[None, :]
    to_rows = lambda c: jnp.transpose(c, (0, 2, 3, 4, 1)).reshape(c.shape[0], -1, c.shape[1])

    yp = x_prompt.reshape(b * t, d)
    ys = x_sample.reshape(s, d)
    outs = [[] for _ in range(8)]
    for l in range(depth):
        ffn1 = (row(g_ffn1_pre[l]), w_ffn1_gate[l].astype(BF16), w_ffn1_up[l].astype(BF16),
                w_ffn1_down[l].astype(BF16), row(g_ffn1_post[l]))
        ffn2 = (row(g_ffn2_pre[l]), w_ffn2_gate[l].astype(BF16), w_ffn2_up[l].astype(BF16),
                w_ffn2_down[l].astype(BF16), row(g_ffn2_post[l]))
        w_big, w_small, b_small = _prep_projection(w_in[l], b_fox_f[l])
        cw, w2, w2_pad = _prep_compress(w_cmpk_1[l], w_cmpk_2[l], pe_cmpk[l], w_cmpv_1[l], w_cmpv_2[l], pe_cmpv[l])
        merge = (row(g_fox_out[l]), row(g_nsa_out[l]), w_out[l].astype(BF16), row(g_mix_post[l]))

        hp = _half_ffn(yp, *ffn1, tm_p)
        small, fkv_t, nkv4_t, win_t, fq_t, fv_t, nq_t, nv_t, cmp, fk_p, nk_p = _project(
            hp, row(g_mix_pre[l]), w_big, w_small, b_small, cos_p, sin_p, tm_p, t // tm_p, True)
        small3 = small.reshape(b, t, LANES)
        per_head = lambda x: x.reshape(b, t, x.shape[1])
        cum, fk_b = _forget_bias(small3, per_head(fk_p), _row_tile(t, 512))
        o_fox = _fox_attention(fq_t, jnp.swapaxes(cum[:, :, :H_FOX], 1, 2), fk_b, fv_t,
                               _row_tile(t, 512), _row_tile(t, 512))
        kc_p, vc_t = _compress(cmp.reshape(b, t, 2 * W_KV), cw + (w2_pad,))
        o_nsa = _nsa_attention_t2(nq_t, jnp.swapaxes(small3, 1, 2), kc_p, vc_t, per_head(nk_p), nv_t,
                                 _row_tile(t, 256), _row_tile(t, 512))
        yp = _merge_ffn(hp, o_fox.reshape(b * t, W_FOX), o_nsa.reshape(b * t, W_NSA), *merge, *ffn2, tm_p)
        keep = min(WINDOW, t)
        tokens_first = lambda x, *dims: jnp.transpose(x.reshape(b, *dims, x.shape[2]), (0, 4, 1, 2, 3))
        outs[0].append(tokens_first(fkv_t, 2, H_FOX, HEAD_DIM))
        outs[1].append(small3[:, :, :H_FOX])
        outs[2].append(tokens_first(nkv4_t, 4, H_KV, HEAD_DIM))
        outs[3].append(tokens_first(win_t[:, :, t - keep:], 2, H_KV, HEAD_DIM))

        hs = _half_ffn(ys, *ffn1, s)
        small, fkv, nkv4, win, fq, nq = _project(
            hs, row(g_mix_pre[l]), w_big, w_small, b_small, cos_s, sin_s, s, 1, False)
        as_rows = lambda x: x.reshape(s, 1, x.shape[1])
        o_fox = _fox_decode(page_table, as_rows(fq), as_rows(fkv), as_rows(small),
                            to_rows(cache_fox_kv[l]), jnp.swapaxes(cache_fox_logf[l], 1, 2))
        o_nsa = _nsa_decode(page_table, as_rows(nq), as_rows(nkv4), as_rows(win), as_rows(small),
                            to_rows(state_nsa_win_kv[l]), to_rows(cache_nsa_kv[l]), cw + (w2,))
        ys = _merge_ffn(hs, o_fox.reshape(s, W_FOX), o_nsa.reshape(s, W_NSA), *merge, *ffn2, s)
        keep = min(WINDOW, past_len + 1)
        kw_all = jnp.concatenate([state_nsa_win_kv[l], win.reshape(s, 1, 2, H_KV, HEAD_DIM)], axis=1)
        outs[4].append(fkv.reshape(s, 1, 2, H_FOX, HEAD_DIM))
        outs[5].append(small[:, :H_FOX].reshape(s, 1, H_FOX))
        outs[6].append(nkv4.reshape(s, 1, 4, H_KV, HEAD_DIM))
        outs[7].append(kw_all[:, kw_all.shape[1] - keep:])

    stacked = [jnp.stack(o, axis=0) for o in outs]
    return (yp.reshape(b, t, d), ys.reshape(s, 1, d), *stacked)
```

```python
import functools

import jax
import jax.numpy as jnp
from jax import lax
from jax.experimental import pallas as pl
from jax.experimental.pallas import tpu as pltpu

HEAD_DIM = 64
H_FOX = 8
H_NSA = 8
H_KV = 2
GQA_GROUP = H_NSA // H_KV
W_FOX = H_FOX * HEAD_DIM
W_NSA = H_NSA * HEAD_DIM
W_KV = H_KV * HEAD_DIM
CMP_STRIDE = 16
CMP_LEN = 32
SLC_LEN = 64
SLC_TOPN = 16
WINDOW = 512
ROPE_THETA = 10000.0
EPS = 1e-6
NEG = -1e30
FORCE_SCORE = 1e4
N_SMALL = H_FOX + 3 * H_NSA
GATE0 = H_FOX

LANES = 128
MXU_N = 256
VMEM_LIMIT = 56 * 1024 * 1024
FOX_HEADS_PER_STEP = 4

F32 = jnp.float32
BF16 = jnp.bfloat16


def _dot(a, b):
    return jnp.dot(a, b, preferred_element_type=F32)


def _dot_nt(a, b):
    return lax.dot_general(a, b, (((1,), (1,)), ((), ())), preferred_element_type=F32)


def _div(x, n):
    assert n & (n - 1) == 0
    return lax.shift_right_logical(x, jnp.int32(n.bit_length() - 1))


def _split3(x):
    hi = x.astype(BF16)
    r1 = x - hi.astype(F32)
    mid = r1.astype(BF16)
    lo = (r1 - mid.astype(F32)).astype(BF16)
    return hi, mid, lo


def _rms(x, g):
    return x * lax.rsqrt(jnp.mean(x * x, axis=-1, keepdims=True) + EPS) * g


def _ff_chunks(d_ff):
    step = 6 * MXU_N
    return tuple((c, min(c + step, d_ff)) for c in range(0, d_ff, step))


def _ffn_core(x, gpre, wg_ref, wu_ref, wd_ref, gpost, chunks):
    xn = _rms(x, gpre).astype(BF16)
    acc = jnp.zeros(x.shape, F32)
    for c0, c1 in chunks:
        g = _dot(xn, wg_ref[:, c0:c1])
        u = _dot(xn, wu_ref[:, c0:c1])
        hm = (g * jax.nn.sigmoid(g) * u).astype(BF16)
        acc = acc + _dot(hm, wd_ref[c0:c1, :])
    return x + 0.5 * _rms(acc, gpost)


def _const_spec(shape):
    nd = len(shape)
    return pl.BlockSpec(shape, lambda *_: (0,) * nd, pipeline_mode=pl.Buffered(1))


def _params(sem):
    return pltpu.CompilerParams(dimension_semantics=sem, vmem_limit_bytes=VMEM_LIMIT)


def _lo_half(rows):
    return lax.broadcasted_iota(jnp.int32, (rows, LANES), 1) < HEAD_DIM


def _pad_heads(x, n_heads, fill):
    lo = _lo_half(x.shape[0])
    out = []
    for h in range(n_heads):
        piece = x[:, (h // 2) * LANES:(h // 2 + 1) * LANES]
        if h % 2:
            piece = pltpu.roll(piece, HEAD_DIM, axis=1)
        out.append(jnp.where(lo, piece, fill))
    return out


def _ffn_kernel(x_ref, gpre_ref, wg_ref, wu_ref, wd_ref, gpost_ref, o_ref, *, chunks):
    o_ref[...] = _ffn_core(x_ref[...], gpre_ref[...], wg_ref, wu_ref, wd_ref, gpost_ref[...], chunks)


def _half_ffn(x, gpre, wg, wu, wd, gpost, tm):
    n, d = x.shape
    d_ff = wg.shape[1]
    row = pl.BlockSpec((tm, d), lambda i: (i, 0))
    return pl.pallas_call(
        functools.partial(_ffn_kernel, chunks=_ff_chunks(d_ff)),
        grid=(n // tm,),
        in_specs=[row, _const_spec((1, d)), _const_spec((d, d_ff)), _const_spec((d, d_ff)),
                  _const_spec((d_ff, d)), _const_spec((1, d))],
        out_specs=row,
        out_shape=jax.ShapeDtypeStruct((n, d), F32),
        compiler_params=_params(("parallel",)),
        name="half_ffn",
    )(x, gpre, wg, wu, wd, gpost)


_C_FQ, _C_FK, _C_FV, _C_NQ, _C_KV, _C_NQR, _C_KR, _C_END = 0, 512, 1024, 1536, 2048, 2816, 3328, 3712


def _proj_kernel(h_ref, g_ref, wb_ref, ws_ref, bf_ref, cos_ref, sin_ref, small_ref, *rest, packed, n_pos_tiles):
    n = _rms(h_ref[...], g_ref[...]).astype(BF16)
    tm = n.shape[0]

    def mm(c0, c1):
        return _dot(n, wb_ref[:, c0:c1])

    def put(ref, tiles):
        for i, t in enumerate(tiles):
            ref[:, i * LANES:(i + 1) * LANES] = t.astype(BF16)

    def put_t(ref, tiles):
        for i, t in enumerate(tiles):
            for c in range(t.shape[1] // LANES):
                r0 = i * t.shape[1] + c * LANES
                ref[r0:r0 + LANES, :] = t[:, c * LANES:(c + 1) * LANES].T

    scale = HEAD_DIM ** -0.5
    lane = lax.broadcasted_iota(jnp.int32, (tm, LANES), 1)
    fq = mm(_C_FQ, _C_FK) * scale
    fk = mm(_C_FK, _C_FV)
    fv = mm(_C_FV, _C_NQ)

    cos = cos_ref[...]
    sin = sin_ref[...]
    nq = mm(_C_NQ, _C_KV)
    nqr = mm(_C_NQR, _C_KR)
    nq = jnp.concatenate([(nq[:, c * LANES:(c + 1) * LANES] * cos + nqr[:, c * LANES:(c + 1) * LANES] * sin) * scale
                          for c in range(W_NSA // LANES)], axis=1)

    kv = mm(_C_KV, _C_NQR)
    kr = mm(_C_KR, _C_END)
    ks, vs = [], []
    for br in range(3):
        k = kv[:, br * 2 * W_KV: br * 2 * W_KV + W_KV] * cos + kr[:, br * W_KV:(br + 1) * W_KV] * sin
        v = kv[:, br * 2 * W_KV + W_KV:(br + 1) * 2 * W_KV]
        ks.append(k)
        vs.append(v)

    sm = _dot(n, ws_ref[...]) + bf_ref[...]
    log_sig = jnp.minimum(sm, 0.0) - jnp.log(1.0 + jnp.exp(-jnp.abs(sm)))
    small_ref[...] = jnp.where(lane < H_FOX, log_sig, jax.nn.sigmoid(sm))

    if not packed:
        fkv_ref, nkv4_ref, win_ref, fq_ref, nq_ref = rest
        fkv_ref[...] = jnp.concatenate([fk, fv], axis=1)
        nkv4_ref[...] = jnp.concatenate([ks[0], vs[0], ks[1], vs[1]], axis=1)
        win_ref[...] = jnp.concatenate([ks[2], vs[2]], axis=1)
        fq_ref[...] = fq
        nq_ref[...] = nq
        return
    fkvt_ref, nkv4t_ref, wint_ref, fqt_ref, fvt_ref, nqt_ref, nvt_ref, cmp_ref, fkp_ref, nkp_ref = rest
    put_t(fkvt_ref, [fk, fv])
    put_t(nkv4t_ref, [ks[0], vs[0], ks[1], vs[1]])
    put_t(wint_ref, [ks[2], vs[2]])
    cmp_ref[...] = jnp.concatenate([ks[0], vs[0]], axis=1)

    def put_heads_t(ref, tiles):
        for i, t in enumerate(tiles):
            ref[i * LANES:(i + 1) * LANES, :] = t.T.astype(BF16)

    put_heads_t(fqt_ref, _pad_heads(fq, H_FOX, 0.0))
    put_heads_t(fvt_ref, _pad_heads(fv, H_FOX, 1.0))
    put_heads_t(nqt_ref, _pad_heads(nq, H_NSA, 0.0))
    put_heads_t(nvt_ref, _pad_heads(vs[1], H_KV, 1.0) + _pad_heads(vs[2], H_KV, 1.0))
    ones3 = jnp.where((lane >= HEAD_DIM) & (lane < HEAD_DIM + 3), 1.0, 0.0)
    put(fkp_ref, _pad_heads(fk, H_FOX, ones3))
    pos = (lax.rem(pl.program_id(0), n_pos_tiles) * tm + lax.broadcasted_iota(jnp.int32, (tm, 1), 0))
    onehot = jnp.where(lane - HEAD_DIM == _div(pos, SLC_LEN), 1.0, 0.0)
    put(nkp_ref, _pad_heads(ks[1], H_KV, onehot) + _pad_heads(ks[2], H_KV, 0.0))


def _project(h, g, w_big, w_small, b_small, cos, sin, tm, n_pos_tiles, packed):
    n, d = h.shape
    row = lambda w: pl.BlockSpec((tm, w), lambda i: (i, 0))
    pos = pl.BlockSpec((tm, LANES), lambda i: (i % n_pos_tiles, 0))
    if packed:
        outs = [(LANES, F32), (2 * W_KV, F32), (H_FOX * LANES, BF16), (2 * H_KV * LANES, BF16)]
        t_outs = [(2 * W_FOX, F32), (4 * W_KV, F32), (2 * W_KV, F32), (H_FOX * LANES, BF16),
                  (H_FOX * LANES, BF16), (H_NSA * LANES, BF16), (2 * H_KV * LANES, BF16)]
    else:
        outs = [(LANES, F32), (2 * W_FOX, F32), (4 * W_KV, F32), (2 * W_KV, F32), (W_FOX, F32), (W_NSA, F32)]
        t_outs = []
    out_specs = [row(w) for w, _ in outs]
    out_shape = [jax.ShapeDtypeStruct((n, w), dt) for w, dt in outs]
    batch = n // (tm * n_pos_tiles)
    for k, (r, dt) in enumerate(t_outs):
        out_specs.insert(1 + k, pl.BlockSpec((None, r, tm), lambda i: (i // n_pos_tiles, 0, i % n_pos_tiles)))
        out_shape.insert(1 + k, jax.ShapeDtypeStruct((batch, r, tm * n_pos_tiles), dt))
    return pl.pallas_call(
        functools.partial(_proj_kernel, packed=packed, n_pos_tiles=n_pos_tiles),
        grid=(n // tm,),
        in_specs=[row(d), _const_spec((1, d)), _const_spec(w_big.shape), _const_spec(w_small.shape),
                  _const_spec((1, LANES)), pos, pos],
        out_specs=out_specs,
        out_shape=out_shape,
        compiler_params=_params(("parallel",)),
        name="project",
    )(h, g, w_big, w_small, b_small, cos, sin)


def _merge_ffn_kernel(h_ref, of_ref, on_ref, gf_ref, gn_ref, wo_ref, gmix_ref,
                      gpre_ref, wg_ref, wu_ref, wd_ref, gpost_ref, y_ref, *, chunks):
    of = _rms(of_ref[...], gf_ref[...]).astype(BF16)
    on = _rms(on_ref[...], gn_ref[...]).astype(BF16)
    mrg = _dot(of, wo_ref[0:W_FOX, :]) + _dot(on, wo_ref[W_FOX:W_FOX + W_NSA, :])
    h2 = h_ref[...] + _rms(mrg, gmix_ref[...])
    y_ref[...] = _ffn_core(h2, gpre_ref[...], wg_ref, wu_ref, wd_ref, gpost_ref[...], chunks)


def _merge_ffn(h, o_fox, o_nsa, gf, gn, w_out, gmix, gpre, wg, wu, wd, gpost, tm):
    n, d = h.shape
    d_ff = wg.shape[1]
    row = lambda w: pl.BlockSpec((tm, w), lambda i: (i, 0))
    return pl.pallas_call(
        functools.partial(_merge_ffn_kernel, chunks=_ff_chunks(d_ff)),
        grid=(n // tm,),
        in_specs=[row(d), row(W_FOX), row(W_NSA), _const_spec((1, W_FOX)), _const_spec((1, W_NSA)),
                  _const_spec(w_out.shape), _const_spec((1, d)), _const_spec((1, d)),
                  _const_spec((d, d_ff)), _const_spec((d, d_ff)), _const_spec((d_ff, d)), _const_spec((1, d))],
        out_specs=row(d),
        out_shape=jax.ShapeDtypeStruct((n, d), F32),
        compiler_params=_params(("parallel",)),
        name="merge_ffn",
    )(h, o_fox, o_nsa, gf, gn, w_out, gmix, gpre, wg, wu, wd, gpost)


def _forget_bias_kernel(x_ref, k_ref, cum_ref, k2_ref, carry_ref):
    @pl.when(pl.program_id(1) == 0)
    def _():
        carry_ref[...] = jnp.zeros(carry_ref.shape, F32)

    x = x_ref[...]
    tc = x.shape[0]
    r = lax.broadcasted_iota(jnp.int32, (tc, tc), 0)
    c = lax.broadcasted_iota(jnp.int32, (tc, tc), 1)
    tri = (c <= r).astype(BF16)
    hi, mid, lo = _split3(x)
    cs = _dot(tri, hi) + _dot(tri, mid) + _dot(tri, lo) + carry_ref[...]
    carry_ref[...] = cs[tc - 1:tc, :]
    cum_ref[...] = cs

    lane = lax.broadcasted_iota(jnp.int32, (tc, LANES), 1)
    for h in range(H_FOX):
        sl = slice(h * LANES, (h + 1) * LANES)
        neg = [p.astype(F32) for p in _split3(-cs[:, h:h + 1])]
        k = k_ref[:, sl].astype(F32)
        for i in range(3):
            k = jnp.where(lane == HEAD_DIM + 3 + i, neg[i], k)
        k2_ref[:, sl] = k.astype(BF16)


def _forget_bias(small, fk_p, tc):
    b, t, w = small.shape
    blk = lambda n: pl.BlockSpec((None, tc, n), lambda i, j: (i, j, 0))
    wide = H_FOX * LANES
    return pl.pallas_call(
        _forget_bias_kernel,
        grid=(b, t // tc),
        in_specs=[blk(w), blk(wide)],
        out_specs=[blk(w), blk(wide)],
        out_shape=[jax.ShapeDtypeStruct((b, t, w), F32), jax.ShapeDtypeStruct((b, t, wide), BF16)],
        scratch_shapes=[pltpu.VMEM((1, w), F32)],
        compiler_params=_params(("parallel", "arbitrary")),
        name="forget_bias",
    )(small, fk_p)


def _flash_update_t(s, vt, m_ref, acc_ref, idx, allow=None, bias=None):
    if allow is not None:
        s = jnp.where(allow, s, NEG)
    if bias is not None:
        s = s + bias
    m_old = m_ref[idx]
    m_new = jnp.maximum(m_old, jnp.max(s, axis=0, keepdims=True))
    p = jnp.exp(s - m_new)
    acc_ref[idx] = jnp.exp(m_old - m_new) * acc_ref[idx] + _dot(vt, p.astype(BF16))
    m_ref[idx] = m_new


def _normalise_t(acc):
    return acc[0:HEAD_DIM] * (1.0 / acc[HEAD_DIM:HEAD_DIM + 1])


def _fox_kernel(qt_ref, cum_ref, k_ref, vt_ref, o_ref, qb_ref, m_ref, acc_ref, *, tq, tk, heads):
    hg = pl.program_id(1)
    q0 = pl.program_id(2) * tq
    m_ref[...] = jnp.full(m_ref.shape, NEG, F32)
    acc_ref[...] = jnp.zeros(acc_ref.shape, F32)
    krow = lax.broadcasted_iota(jnp.int32, (tk, tq), 0)
    qpos = q0 + lax.broadcasted_iota(jnp.int32, (tk, tq), 1)
    head = lambda g: slice(g * LANES, (g + 1) * LANES)
    row = lax.broadcasted_iota(jnp.int32, (LANES, tq), 0)
    row8 = lax.broadcasted_iota(jnp.int32, (H_FOX, tq), 0)
    cum_all = cum_ref[...]
    for g in range(heads):
        cum = jnp.sum(jnp.where(row8 == hg * heads + g, cum_all, 0.0), axis=0, keepdims=True)
        q = qt_ref[head(g), :].astype(F32)
        for i, piece in enumerate(_split3(cum)):
            q = jnp.where(row == HEAD_DIM + i, piece.astype(F32), q)
        qb_ref[g] = jnp.where((row >= HEAD_DIM + 3) & (row < HEAD_DIM + 6), 1.0, q).astype(BF16)

    def scores(kt):
        k0 = pl.multiple_of(kt * tk, tk)
        return tuple(_dot(k_ref[pl.ds(k0, tk), head(g)], qb_ref[g]) for g in range(heads))

    def update(kt, s, masked):
        k0 = pl.multiple_of(kt * tk, tk)
        allow = (k0 + krow <= qpos) if masked else None
        for g in range(heads):
            _flash_update_t(s[g], vt_ref[head(g), pl.ds(k0, tk)], m_ref, acc_ref, g, allow=allow)

    def body(kt, s):
        s_next = scores(kt + 1)
        update(kt, s, False)
        return s_next

    n_full = _div(q0, tk)
    update(n_full, lax.fori_loop(0, n_full, body, scores(0)), True)
    for pr in range(heads // 2):
        pair = jnp.concatenate([_normalise_t(acc_ref[2 * pr]), _normalise_t(acc_ref[2 * pr + 1])], axis=0)
        o_ref[:, pr * LANES:(pr + 1) * LANES] = pair.T


def _fox_attention(fq_t, cum_t, fk_p, fv_t, tq, tk):
    b, t, _ = fk_p.shape
    g = FOX_HEADS_PER_STEP
    qblk = pl.BlockSpec((None, g * LANES, tq), lambda i, hg, qi: (i, hg, qi))
    cblk = pl.BlockSpec((None, H_FOX, tq), lambda i, hg, qi: (i, 0, qi))
    kblk = pl.BlockSpec((None, t, g * LANES), lambda i, hg, qi: (i, 0, hg))
    vblk = pl.BlockSpec((None, g * LANES, t), lambda i, hg, qi: (i, hg, 0))
    oblk = pl.BlockSpec((None, tq, g * HEAD_DIM), lambda i, hg, qi: (i, qi, hg))
    return pl.pallas_call(
        functools.partial(_fox_kernel, tq=tq, tk=tk, heads=g),
        grid=(b, H_FOX // g, t // tq),
        in_specs=[qblk, cblk, kblk, vblk],
        out_specs=oblk,
        out_shape=jax.ShapeDtypeStruct((b, t, W_FOX), F32),
        scratch_shapes=[pltpu.VMEM((g, LANES, tq), BF16), pltpu.VMEM((g, 1, tq), F32),
                        pltpu.VMEM((g, LANES, tq), F32)],
        compiler_params=_params(("parallel", "parallel", "arbitrary")),
        name="fox_prompt",
    )(fq_t, cum_t, fk_p, fv_t)


def _compress_hidden(load_rows, n_ch, wj_ref, pe_ref, wpe_ref):
    acc_k = jnp.zeros((n_ch, 2 * W_KV), F32)
    acc_v = jnp.zeros((n_ch, 2 * W_KV), F32)
    for j in range(CMP_STRIDE):
        xk, xv = load_rows(j)
        acc_k = acc_k + _dot(xk.astype(BF16), wj_ref[j, 0:W_KV, :])
        acc_v = acc_v + _dot(xv.astype(BF16), wj_ref[j, W_KV:2 * W_KV, :])
    first = jnp.concatenate([acc_k[:, 0:W_KV], acc_v[:, 0:W_KV]], axis=1)
    second = jnp.concatenate([acc_k[:, W_KV:], acc_v[:, W_KV:]], axis=1)
    second = pltpu.roll(second, n_ch - 1, axis=0)
    pe_term = _dot(pe_ref[...], wpe_ref[...])[0:1, :]
    return jax.nn.gelu(first + second + pe_term)


def _compress_kernel(xk_ref, xv_ref, wj_ref, pe_ref, wpe_ref, w2p_ref, kc_ref, vc_ref, *, n_ch):
    rows = lambda j: (xk_ref[pl.ds(j, n_ch, stride=CMP_STRIDE), :], xv_ref[pl.ds(j, n_ch, stride=CMP_STRIDE), :])
    hid = _compress_hidden(rows, n_ch, wj_ref, pe_ref, wpe_ref).astype(BF16)
    out = _dot(hid, w2p_ref[...])
    kc_ref[...] = out[:, 0:H_KV * LANES].astype(BF16)
    for h in range(H_KV):
        vc = out[:, (H_KV + h) * LANES:(H_KV + h + 1) * LANES]
        vc_ref[h * LANES:(h + 1) * LANES, :] = jnp.where(_lo_half(n_ch), vc, 1.0).T.astype(BF16)


def _compress(nkv4, cw):
    b, t, _ = nkv4.shape
    n_ch = t // CMP_STRIDE
    out = pl.BlockSpec((None, n_ch, H_KV * LANES), lambda i: (i, 0, 0))
    return pl.pallas_call(
        functools.partial(_compress_kernel, n_ch=n_ch),
        grid=(b,),
        in_specs=[pl.BlockSpec((None, t, W_KV), lambda i: (i, 0, 0)),
                  pl.BlockSpec((None, t, W_KV), lambda i: (i, 0, 1))] + [_const_spec(w.shape) for w in cw],
        out_specs=[out, pl.BlockSpec((None, H_KV * LANES, n_ch), lambda i: (i, 0, 0))],
        out_shape=[jax.ShapeDtypeStruct((b, n_ch, H_KV * LANES), BF16),
                   jax.ShapeDtypeStruct((b, H_KV * LANES, n_ch), BF16)],
        compiler_params=_params(("parallel",)),
        name="compress_prompt",
    )(nkv4, nkv4, *cw)


def _overlap(n_ch, n_slc):
    n = lax.broadcasted_iota(jnp.int32, (n_ch, LANES), 0)
    j = lax.broadcasted_iota(jnp.int32, (n_ch, LANES), 1)
    hit = ((n * CMP_STRIDE <= j * SLC_LEN + SLC_LEN - 1) & (n * CMP_STRIDE + CMP_LEN - 1 >= j * SLC_LEN)
           & (n < n_ch - 1) & (j < n_slc))
    return hit.astype(BF16)


def _overlap_t(n_ch, n_slc):
    j = lax.broadcasted_iota(jnp.int32, (LANES, n_ch), 0)
    n = lax.broadcasted_iota(jnp.int32, (LANES, n_ch), 1)
    hit = ((n * CMP_STRIDE <= j * SLC_LEN + SLC_LEN - 1) & (n * CMP_STRIDE + CMP_LEN - 1 >= j * SLC_LEN)
           & (n < n_ch - 1) & (j < n_slc))
    return hit.astype(BF16)


def _block_scores(imp, pos, n_slc, axis):
    blk = lax.broadcasted_iota(jnp.int32, imp.shape, axis)
    qblk = _div(pos, SLC_LEN)
    forced = (blk == 0) | (blk == qblk) | (blk == qblk - 1)
    valid = blk * SLC_LEN <= pos
    score = jnp.where(valid, jnp.where(forced, FORCE_SCORE, imp), -1.0)
    return jnp.where(blk < n_slc, score, -2.0)


def _count_beats_lanes(score, n_slc):
    blk = lax.broadcasted_iota(jnp.int32, score.shape, 1)
    cnt = jnp.zeros(score.shape, jnp.int32)
    for i in range(n_slc):
        si = score[:, i:i + 1]
        cnt = cnt + jnp.where((si > score) | ((si == score) & (blk > i)), 1, 0)
    return cnt


def _count_beats_rows(score, lo, hi, n_rows):
    out = []
    for v in range(n_rows // 8):
        s_v = score[8 * v:8 * v + 8, :]
        blk = 8 * v + lax.broadcasted_iota(jnp.int32, s_v.shape, 0)
        cnt = jnp.zeros(s_v.shape, jnp.int32)
        for i in range(lo, hi):
            si = score[i:i + 1, :]
            if 8 * v + 7 < i:
                beats = si > s_v
            elif 8 * v > i:
                beats = si >= s_v
            else:
                beats = (si > s_v) | ((si == s_v) & (blk > i))
            cnt = cnt + jnp.where(beats, 1, 0)
        out.append(cnt)
    return jnp.concatenate(out, axis=0)


def _nsa_t2_kernel(qt_ref, gate_ref, kc_ref, vct_ref, sk_ref, svt_ref, wk_ref, wvt_ref, o_ref,
                   q4_ref, q4s_ref, cnt_ref, m_ref, acc_ref, *, tq, tk, n_ch, n_slc, wlen):
    q0 = pl.program_id(1) * tq
    g = GQA_GROUP
    w = g * tq
    n_sel = min(SLC_TOPN, n_slc)
    hk = range(H_KV)
    blk = lambda h: slice(h * LANES, (h + 1) * LANES)
    pos = q0 + lax.broadcasted_iota(jnp.int32, (1, tq), 1)
    heads = lambda x: jnp.concatenate([x] * g, axis=1)
    for h in hk:
        for i in range(g):
            q4_ref[h, :, i * tq:(i + 1) * tq] = qt_ref[(g * h + i) * LANES:(g * h + i + 1) * LANES, :]
    q4 = [q4_ref[h] for h in hk]

    n_idx = lax.broadcasted_iota(jnp.int32, (n_ch, tq), 0)
    ok = (n_idx * CMP_STRIDE + CMP_LEN - 1 <= pos) & (n_idx < n_ch - 1)
    ok_bias = heads(jnp.where(ok, 0.0, NEG))
    ok_keep = heads(jnp.where(ok, 1.0, 0.0))
    ov_t = _overlap_t(n_ch, n_slc)
    o_cmp, score = [], []
    for h in hk:
        sc = _dot(kc_ref[:, blk(h)], q4[h]) + ok_bias
        e = jnp.exp(sc - jnp.max(sc, axis=0, keepdims=True)) * ok_keep
        lc = jnp.sum(e, axis=0, keepdims=True)
        a = e * (1.0 / jnp.where(lc > 0.0, lc, 1.0))
        a_hi = a.astype(BF16)
        a_lo = (a - a_hi.astype(F32)).astype(BF16)
        o_cmp.append(_dot(vct_ref[blk(h), :], a_hi))
        imp4 = _dot(ov_t, a_hi) + _dot(ov_t, a_lo)
        imp_t = sum(imp4[:, i * tq:(i + 1) * tq] for i in range(g))
        score.append(_block_scores(imp_t, pos, n_slc, 0))

    n_valid = _div(q0 + tq - 1, SLC_LEN) + 1
    n_rows = cnt_ref.shape[1]
    cnt_ref[...] = jnp.zeros(cnt_ref.shape, jnp.int32)
    for b0 in range(0, n_slc, 8):
        @pl.when((b0 < n_valid) & (n_valid > n_sel))
        def _():
            for h in hk:
                cnt_ref[h] += _count_beats_rows(score[h], b0, min(b0 + 8, n_slc), n_rows)
    for h in hk:
        sel_t = (cnt_ref[h] < n_sel) & (lax.broadcasted_iota(jnp.int32, (n_rows, tq), 0) < n_slc)
        bias = jnp.where(sel_t, 0.0, NEG)
        if n_rows < HEAD_DIM:
            bias = jnp.concatenate([bias, jnp.full((HEAD_DIM - n_rows, tq), NEG, F32)], axis=0)
        q4s_ref[h, 0:HEAD_DIM, :] = q4[h][0:HEAD_DIM]
        for i in range(g):
            q4s_ref[h, HEAD_DIM:2 * HEAD_DIM, i * tq:(i + 1) * tq] = bias.astype(BF16)

    m_ref[...] = jnp.full(m_ref.shape, NEG, F32)
    acc_ref[...] = jnp.zeros(acc_ref.shape, F32)
    krow = lax.broadcasted_iota(jnp.int32, (tk, tq), 0)

    def scores(kt):
        k0 = pl.multiple_of(kt * tk, tk)
        return tuple(_dot(sk_ref[pl.ds(k0, tk), blk(h)], q4s_ref[h]) for h in hk)

    def update(kt, s, masked):
        k0 = pl.multiple_of(kt * tk, tk)
        bias = heads(jnp.where(k0 + krow <= pos, 0.0, NEG)) if masked else None
        for h in hk:
            _flash_update_t(s[h], svt_ref[blk(h), pl.ds(k0, tk)], m_ref, acc_ref, h, bias=bias)

    def body(kt, s):
        s_next = scores(kt + 1)
        update(kt, s, False)
        return s_next

    n_full = _div(q0, tk)
    update(n_full, lax.fori_loop(0, n_full, body, scores(0)), True)

    ws = pl.multiple_of(jnp.maximum(q0 + tq - wlen, 0), tq)
    dist = pos - (ws + lax.broadcasted_iota(jnp.int32, (wlen, tq), 0))
    win_bias = heads(jnp.where((dist >= 0) & (dist < WINDOW), 0.0, NEG))
    gates = gate_ref[...]
    grow = lax.broadcasted_iota(jnp.int32, (LANES, tq), 0)
    for h in hk:
        o_slc = _normalise_t(acc_ref[h])
        sw = _dot(wk_ref[pl.ds(ws, wlen), blk(h)], q4[h]) + win_bias
        pw = jnp.exp(sw - jnp.max(sw, axis=0, keepdims=True))
        o_win = _normalise_t(_dot(wvt_ref[blk(h), pl.ds(ws, wlen)], pw.astype(BF16)))
        vals = []
        for i in range(g):
            r0 = GATE0 + 3 * (g * h + i)
            gc = [jnp.sum(jnp.where(grow == r0 + c, gates, 0.0), axis=0, keepdims=True) for c in range(3)]
            sl = slice(i * tq, (i + 1) * tq)
            vals.append(gc[0] * o_cmp[h][0:HEAD_DIM, sl] + gc[1] * o_slc[:, sl] + gc[2] * o_win[:, sl])
        for pr in range(g // 2):
            lanes = slice((h * (g // 2) + pr) * LANES, (h * (g // 2) + pr + 1) * LANES)
            o_ref[:, lanes] = jnp.concatenate([vals[2 * pr], vals[2 * pr + 1]], axis=0).T


def _nsa_attention_t2(nq_t, small_t, kc_p, vc_t, nk_p, nv_t, tq, tk):
    b, _, t = nq_t.shape
    n_ch = kc_p.shape[1]
    n_slc = -(-t // SLC_LEN)
    assert n_slc <= HEAD_DIM, "the block mask rides in the 64 spare query rows"
    wlen = min(t, WINDOW + tq)
    g = GQA_GROUP
    wide = H_KV * LANES
    kblk = lambda c: pl.BlockSpec((None, t, wide), lambda i, qi: (i, 0, c))
    vblk = lambda c: pl.BlockSpec((None, wide, t), lambda i, qi: (i, c, 0))
    return pl.pallas_call(
        functools.partial(_nsa_t2_kernel, tq=tq, tk=tk, n_ch=n_ch, n_slc=n_slc, wlen=wlen),
        grid=(b, t // tq),
        in_specs=[pl.BlockSpec((None, H_NSA * LANES, tq), lambda i, qi: (i, 0, qi)),
                  pl.BlockSpec((None, LANES, tq), lambda i, qi: (i, 0, qi)),
                  pl.BlockSpec((None, n_ch, wide), lambda i, qi: (i, 0, 0)),
                  pl.BlockSpec((None, wide, n_ch), lambda i, qi: (i, 0, 0)),
                  kblk(0), vblk(0), kblk(1), vblk(1)],
        out_specs=pl.BlockSpec((None, tq, W_NSA), lambda i, qi: (i, qi, 0)),
        out_shape=jax.ShapeDtypeStruct((b, t, W_NSA), F32),
        scratch_shapes=[pltpu.VMEM((H_KV, LANES, g * tq), BF16), pltpu.VMEM((H_KV, LANES, g * tq), BF16),
                        pltpu.VMEM((H_KV, -(-n_slc // 8) * 8, tq), jnp.int32),
                        pltpu.VMEM((H_KV, 1, g * tq), F32), pltpu.VMEM((H_KV, LANES, g * tq), F32)],
        compiler_params=_params(("parallel", "arbitrary")),
        name="nsa_prompt",
    )(nq_t, small_t, kc_p, vc_t, nk_p, nv_t, nk_p, nv_t)


def _page_copies(pt_ref, step, slot, n_pages, page, streams, group):
    out = []
    for u in range(group):
        for pg in range(n_pages):
            idx = pt_ref[(step * group + u) * n_pages + pg]
            for hbm, buf, sem, on_lanes in streams:
                dst = buf.at[slot, u, :, pl.ds(pg * page, page)] if on_lanes else buf.at[slot, u, pg]
                out.append(pltpu.make_async_copy(hbm.at[idx], dst, sem.at[slot]))
    return out


def _gather_pages(pt_ref, n_pages, page, streams, group=1):
    i = pl.program_id(0)
    slot = lax.rem(i, 2)

    @pl.when(i == 0)
    def _():
        for c in _page_copies(pt_ref, 0, 0, n_pages, page, streams, group):
            c.start()

    @pl.when(i + 1 < pl.num_programs(0))
    def _():
        for c in _page_copies(pt_ref, i + 1, 1 - slot, n_pages, page, streams, group):
            c.start()

    for c in _page_copies(pt_ref, i, slot, n_pages, page, streams, group):
        c.wait()
    return slot


def _head_scores(q_col, k_view, s_ref, head, rows0, n_tiles, tile):
    qb = jnp.broadcast_to(q_col, (HEAD_DIM, tile))
    for pg in range(n_tiles):
        kt = k_view[rows0:rows0 + HEAD_DIM, pg * tile:(pg + 1) * tile]
        s_ref[pg, head:head + 1, :] = jnp.sum(kt * qb, axis=0, keepdims=True)


def _head_values(p_tile, v_view, rows0, n_tiles, tile):
    acc = jnp.zeros((HEAD_DIM, tile), F32)
    for pg in range(n_tiles):
        vt = v_view[rows0:rows0 + HEAD_DIM, pg * tile:(pg + 1) * tile]
        acc = acc + vt * jnp.broadcast_to(p_tile(pg), (HEAD_DIM, tile))
    return jnp.sum(acc, axis=1, keepdims=True)


def _softmax_tiles(s, s_new):
    m = jnp.maximum(jnp.max(jnp.max(s, axis=0), axis=1, keepdims=True), s_new)
    p = jnp.exp(s - m[None])
    p_new = jnp.exp(s_new - m)
    l = jnp.sum(jnp.sum(p, axis=0), axis=1, keepdims=True) + p_new
    return p, p_new, 1.0 / l


def _softmax_rows(s, s_new, allow):
    s = jnp.where(allow, s, NEG)
    m = jnp.maximum(jnp.max(s, axis=1, keepdims=True), s_new)
    p = jnp.where(allow, jnp.exp(s - m), 0.0)
    p_new = jnp.exp(s_new - m)
    return p, p_new, 1.0 / (jnp.sum(p, axis=1, keepdims=True) + p_new)


def _col_dot(a_col, b_col, n_heads):
    return jnp.sum((a_col * b_col).reshape(n_heads, HEAD_DIM, 1), axis=1)


def _as_column(row):
    return jnp.broadcast_to(row, (LANES, row.shape[1])).T[:, 0:1]


def _as_row(col):
    return jnp.broadcast_to(col, (col.shape[0], LANES)).T[0:1, :]


def _fox_dec_kernel(pt_ref, q_ref, kvnew_ref, small_ref, kv_hbm, lf_hbm, o_ref,
                    kvbuf, lfbuf, s_ref, p_ref, sem_kv, sem_lf, *, n_pages, page, group):
    slot = _gather_pages(pt_ref, n_pages, page,
                         [(kv_hbm, kvbuf, sem_kv, True), (lf_hbm, lfbuf, sem_lf, False)], group)
    for u in range(group):
        o_ref[u] = _fox_dec_one(kvbuf.at[slot, u], lfbuf[slot, u], q_ref[u], kvnew_ref[u], small_ref[u],
                                s_ref, p_ref, n_pages, page)


def _fox_dec_one(kv, lf, q_row, kvnew_row, small_row, s_ref, p_ref, n_pages, page):
    q = _as_column(q_row)
    kv_new = _as_column(kvnew_row)
    lf_new = jnp.concatenate([_as_column(small_row)[0:H_FOX]] * n_pages, axis=0)
    for h in range(H_FOX):
        _head_scores(q[h * HEAD_DIM:(h + 1) * HEAD_DIM], kv, s_ref, h, h * HEAD_DIM, n_pages, page)

    rows = n_pages * H_FOX
    lf = lf.reshape(rows, page)
    r = lax.broadcasted_iota(jnp.int32, (page, page), 0)
    c = lax.broadcasted_iota(jnp.int32, (page, page), 1)
    later = (r > c).astype(BF16)
    hi, mid, lo = _split3(lf)
    within = _dot(hi, later) + _dot(mid, later) + _dot(lo, later)
    r = lax.broadcasted_iota(jnp.int32, (rows, rows), 0)
    c = lax.broadcasted_iota(jnp.int32, (rows, rows), 1)
    later_pages = ((c > r) & (((c - r) & (H_FOX - 1)) == 0)).astype(BF16)
    tot = jnp.broadcast_to(jnp.sum(lf, axis=1, keepdims=True), (rows, page))
    hi, mid, lo = _split3(tot)
    beyond = _dot(later_pages, hi) + _dot(later_pages, mid) + _dot(later_pages, lo)
    bias = (within + beyond + lf_new).reshape(n_pages, H_FOX, page)

    s_new = _col_dot(q, kv_new[0:W_FOX], H_FOX)
    p, p_new, inv_l = _softmax_tiles(s_ref[...] + bias, s_new)
    p_ref[...] = p
    outs = []
    for h in range(H_FOX):
        o = _head_values(lambda pg, h=h: p_ref[pg, h:h + 1, :], kv, W_FOX + h * HEAD_DIM, n_pages, page)
        v_new = kv_new[W_FOX + h * HEAD_DIM: W_FOX + (h + 1) * HEAD_DIM]
        outs.append((o + p_new[h:h + 1] * v_new) * inv_l[h:h + 1])
    return _as_row(jnp.concatenate(outs, axis=0))


def _fox_decode(page_table, q_row, kv_row, small_row, cache_kvt, cache_lft):
    s, n_pages = page_table.shape
    rows, page = cache_kvt.shape[1], cache_kvt.shape[2]
    group = 2 if s % 2 == 0 else 1
    col = lambda n: pl.BlockSpec((group, 1, n), lambda i, pt: (i, 0, 0))
    anyspec = pl.BlockSpec(memory_space=pl.ANY)
    return pl.pallas_call(
        functools.partial(_fox_dec_kernel, n_pages=n_pages, page=page, group=group),
        grid_spec=pltpu.PrefetchScalarGridSpec(
            num_scalar_prefetch=1,
            grid=(s // group,),
            in_specs=[col(W_FOX), col(2 * W_FOX), col(LANES), anyspec, anyspec],
            out_specs=col(W_FOX),
            scratch_shapes=[pltpu.VMEM((2, group, rows, n_pages * page), F32),
                            pltpu.VMEM((2, group, n_pages, H_FOX, page), F32),
                            pltpu.VMEM((n_pages, H_FOX, page), F32), pltpu.VMEM((n_pages, H_FOX, page), F32),
                            pltpu.SemaphoreType.DMA((2,)), pltpu.SemaphoreType.DMA((2,))],
        ),
        out_shape=jax.ShapeDtypeStruct((s, 1, W_FOX), F32),
        compiler_params=_params(("arbitrary",)),
        name="fox_decode",
    )(page_table.reshape(-1), q_row, kv_row, small_row, cache_kvt, cache_lft)


def _nsa_dec_kernel(pt_ref, qrow_ref, nkv4_ref, wnew_ref, gate_ref, win_ref, cache_hbm,
                    wj_ref, pe_ref, wpe_ref, w2_ref, o_ref,
                    xbuf, xk_buf, xv_buf, sem, *, n_pages, page, group):
    slot = _gather_pages(pt_ref, n_pages, page, [(cache_hbm, xbuf, sem, True)], group)
    n_ch = n_pages * page // CMP_STRIDE

    r = lax.broadcasted_iota(jnp.int32, (page, page), 0)
    t = lax.broadcasted_iota(jnp.int32, (page, page), 1)
    per = page // CMP_STRIDE
    perm = (t == CMP_STRIDE * (r & (per - 1)) + _div(r, per)).astype(BF16)
    for u in range(group):
        for pg in range(n_pages):
            xt = _dot_nt(perm, xbuf[slot, u, 0:2 * W_KV, pg * page:(pg + 1) * page].astype(BF16))
            c0 = u * n_ch + pg * per
            for j in range(CMP_STRIDE):
                xk_buf[j, c0:c0 + per, :] = xt[j * per:(j + 1) * per, 0:W_KV]
                xv_buf[j, c0:c0 + per, :] = xt[j * per:(j + 1) * per, W_KV:2 * W_KV]
    hid = _compress_hidden(lambda jj: (xk_buf[jj], xv_buf[jj]), group * n_ch, wj_ref, pe_ref, wpe_ref).astype(BF16)
    for u in range(group):
        o_ref[u] = _nsa_dec_one(hid[u * n_ch:(u + 1) * n_ch], xbuf.at[slot, u], qrow_ref[u], nkv4_ref[u],
                                wnew_ref[u], gate_ref[u], win_ref.at[u], w2_ref, n_pages, page)


def _nsa_dec_one(hid, x, q_row, nkv4_new, win_new, gates, wv, w2_ref, n_pages, page):
    past_len = n_pages * page
    n_ch = past_len // CMP_STRIDE
    n_slc = past_len // SLC_LEN + 1
    n_sel = min(SLC_TOPN, n_slc)
    win_buf = wv.shape[1]
    g = GQA_GROUP
    row = lax.broadcasted_iota(jnp.int32, (H_NSA, LANES), 0)
    lane = lax.broadcasted_iota(jnp.int32, (H_NSA, LANES), 1)
    kc = _dot(hid, w2_ref[:, 0:W_KV]).astype(BF16)
    vc = _dot(hid, w2_ref[:, W_KV:2 * W_KV]).astype(BF16)

    qbd = jnp.zeros((H_NSA, LANES), F32)
    for i in range(H_NSA):
        piece = q_row[:, (i // 2) * LANES:(i // 2 + 1) * LANES]
        if (i % 2) != (i // g):
            piece = pltpu.roll(piece, HEAD_DIM, axis=1)
        qbd = jnp.where(row == i, jnp.broadcast_to(piece, (H_NSA, LANES)), qbd)
    qbd = jnp.where(_div(lane, HEAD_DIM) == _div(row, g), qbd, 0.0).astype(BF16)
    n_idx = lax.broadcasted_iota(jnp.int32, (H_NSA, n_ch), 1)
    ok = (n_idx * CMP_STRIDE + CMP_LEN - 1 <= past_len) & (n_idx < n_ch - 1)
    sc = jnp.where(ok, _dot_nt(qbd, kc), NEG)
    e = jnp.where(ok, jnp.exp(sc - jnp.max(sc, axis=1, keepdims=True)), 0.0)
    lc = jnp.sum(e, axis=1, keepdims=True)
    a = e * (1.0 / jnp.where(lc > 0.0, lc, 1.0))
    a_hi = a.astype(BF16)
    a_lo = (a - a_hi.astype(F32)).astype(BF16)
    o_cmp = _dot(a_hi, vc)
    ov = _overlap(n_ch, n_slc)
    imp8 = _dot(a_hi, ov) + _dot(a_lo, ov)
    imp = jnp.zeros((H_NSA, LANES), F32)
    for h in range(H_KV):
        tot = jnp.sum(imp8[h * g:(h + 1) * g], axis=0, keepdims=True)
        imp = jnp.where(_div(row, g) == h, jnp.broadcast_to(tot, (H_NSA, LANES)), imp)
    score = _block_scores(imp, jnp.full((H_NSA, 1), past_len, jnp.int32), n_slc, 1)
    sel = jnp.where((_count_beats_lanes(score, n_slc) < n_sel) & (lane < n_slc), 1.0, 0.0)

    per_page = page // SLC_LEN
    allow = []
    for pg in range(n_pages):
        m = jnp.zeros((H_NSA, page), F32)
        for b in range(per_page):
            blk = pg * per_page + b
            lanes_b = _div(lax.broadcasted_iota(jnp.int32, (H_NSA, page), 1), SLC_LEN) == b
            m = jnp.where(lanes_b, jnp.broadcast_to(sel[:, blk:blk + 1], (H_NSA, page)), m)
        allow.append(m > 0.5)
    allow = jnp.concatenate(allow, axis=1)
    qf = qbd.astype(F32)
    s_new = jnp.sum(qf * nkv4_new[:, 2 * W_KV:3 * W_KV], axis=1, keepdims=True)
    p, p_new, inv_l = _softmax_rows(_dot(qbd, x[2 * W_KV:3 * W_KV, :].astype(BF16)), s_new, allow)
    o_slc = (_dot_nt(p.astype(BF16), x[3 * W_KV:4 * W_KV, :].astype(BF16))
             + p_new.astype(BF16).astype(F32) * nkv4_new[:, 3 * W_KV:4 * W_KV].astype(BF16).astype(F32)) * inv_l

    slot_idx = lax.broadcasted_iota(jnp.int32, (H_NSA, win_buf), 1)
    sw_new = jnp.sum(qf * win_new[:, 0:W_KV], axis=1, keepdims=True)
    pw, pw_new, inv_lw = _softmax_rows(_dot(qbd, wv[0:W_KV, :].astype(BF16)), sw_new, (win_buf - slot_idx) < WINDOW)
    o_win = (_dot_nt(pw.astype(BF16), wv[W_KV:2 * W_KV, :].astype(BF16))
             + pw_new.astype(BF16).astype(F32) * win_new[:, W_KV:2 * W_KV].astype(BF16).astype(F32)) * inv_lw

    gate_rows = jnp.broadcast_to(gates, (H_NSA, LANES))
    gc = [jnp.sum(jnp.where(lane == GATE0 + 3 * row + c, gate_rows, 0.0), axis=1, keepdims=True) for c in range(3)]
    val = gc[0] * o_cmp + gc[1] * o_slc + gc[2] * o_win
    pieces = []
    for i in range(H_NSA):
        piece = val[i:i + 1]
        if (i % 2) != (i // g):
            piece = pltpu.roll(piece, HEAD_DIM, axis=1)
        pieces.append(piece)
    lo = lax.broadcasted_iota(jnp.int32, (1, LANES), 1) < HEAD_DIM
    return jnp.concatenate([jnp.where(lo, pieces[2 * pr], pieces[2 * pr + 1]) for pr in range(H_NSA // 2)], axis=1)


def _nsa_decode(page_table, q_row, nkv4_row, win_row, small, win_t, cache_t, cw):
    s, n_pages = page_table.shape
    rows, page = cache_t.shape[1], cache_t.shape[2]
    past_len = n_pages * page
    n_ch = past_len // CMP_STRIDE
    win_buf = win_t.shape[2]
    group = next(g for g in (4, 2, 1) if s % g == 0)
    rowspec = lambda n: pl.BlockSpec((group, 1, n), lambda i, pt: (i, 0, 0))
    const = lambda shape: pl.BlockSpec(shape, lambda i, pt: (0,) * len(shape), pipeline_mode=pl.Buffered(1))
    return pl.pallas_call(
        functools.partial(_nsa_dec_kernel, n_pages=n_pages, page=page, group=group),
        grid_spec=pltpu.PrefetchScalarGridSpec(
            num_scalar_prefetch=1,
            grid=(s // group,),
            in_specs=[rowspec(W_NSA), rowspec(4 * W_KV), rowspec(2 * W_KV), rowspec(LANES),
                      pl.BlockSpec((group, 2 * W_KV, win_buf), lambda i, pt: (i, 0, 0)),
                      pl.BlockSpec(memory_space=pl.ANY)] + [const(w.shape) for w in cw],
            out_specs=rowspec(W_NSA),
            scratch_shapes=[pltpu.VMEM((2, group, rows, past_len), F32),
                            pltpu.VMEM((CMP_STRIDE, group * n_ch, W_KV), F32),
                            pltpu.VMEM((CMP_STRIDE, group * n_ch, W_KV), F32),
                            pltpu.SemaphoreType.DMA((2,))],
        ),
        out_shape=jax.ShapeDtypeStruct((s, 1, W_NSA), F32),
        compiler_params=_params(("arbitrary",)),
        name="nsa_decode",
    )(page_table.reshape(-1), q_row, nkv4_row, win_row, small, win_t, cache_t, *cw)


def _rot_cols(w):
    d, n = w.shape
    w = w.reshape(d, n // HEAD_DIM, 2, HEAD_DIM // 2)
    return jnp.stack([-w[:, :, 1], w[:, :, 0]], axis=2).reshape(d, n)


def _prep_projection(w_in, b_f):
    c = [0, W_FOX, 2 * W_FOX, 3 * W_FOX, 3 * W_FOX + H_FOX, 3 * W_FOX + H_FOX + W_NSA,
         3 * W_FOX + H_FOX + W_NSA + 6 * W_KV]
    fq, fk, fv, ff, nq, kv = (w_in[:, c[i]:c[i + 1]] for i in range(6))
    gt = w_in[:, c[6]:]
    ks = jnp.concatenate([kv[:, br * 2 * W_KV: br * 2 * W_KV + W_KV] for br in range(3)], axis=1)
    w_big = jnp.concatenate([fq, fk, fv, nq, kv, _rot_cols(nq), _rot_cols(ks)], axis=1).astype(BF16)
    d = w_in.shape[0]
    w_small = jnp.concatenate([ff, gt, jnp.zeros((d, LANES - N_SMALL), w_in.dtype)], axis=1).astype(BF16)
    b_small = jnp.concatenate([b_f.astype(F32), jnp.zeros((LANES - H_FOX,), F32)])[None, :]
    return w_big, w_small, b_small


def _prep_compress(wk1, wk2, pek, wv1, wv2, pev):
    ratio = CMP_LEN // CMP_STRIDE
    eye = jnp.eye(4, dtype=F32)

    def blocks(w):
        return w.reshape(ratio, CMP_STRIDE, HEAD_DIM, wk1.shape[1])

    per_head = lambda w: jnp.einsum("rjde,hk->jhdrke", blocks(w), jnp.eye(H_KV, dtype=F32)).reshape(
        CMP_STRIDE, W_KV, ratio * W_KV)
    wj = jnp.concatenate([per_head(wk1), per_head(wv1)], axis=1).astype(BF16)
    pe = jnp.concatenate([pek.reshape(-1), pev.reshape(-1)])
    pe = jnp.broadcast_to(pe[None, :], (8, pe.shape[0])).astype(BF16)
    zero = jnp.zeros_like(wk1)
    wpe = jnp.concatenate([jnp.concatenate([wk1, wk1, zero, zero], axis=1),
                           jnp.concatenate([zero, zero, wv1, wv1], axis=1)], axis=0).astype(BF16)
    w2 = jnp.einsum("gde,gh->gdhe", jnp.stack([wk2, wk2, wv2, wv2]), eye)
    w2_pad = jnp.concatenate([w2, jnp.zeros_like(w2)], axis=3)
    w2 = w2.reshape(4 * HEAD_DIM, 4 * HEAD_DIM).astype(BF16)
    w2_pad = w2_pad.reshape(4 * HEAD_DIM, 4 * LANES).astype(BF16)
    return (wj, pe, wpe), w2, w2_pad


def _rope_tables(pos):
    half = HEAD_DIM // 2
    inv = ROPE_THETA ** (-jnp.arange(half, dtype=F32) / half)
    ang = pos.astype(F32)[:, None] * inv[None, :]
    reps = LANES // half
    return jnp.tile(jnp.cos(ang), (1, reps)), jnp.tile(jnp.sin(ang), (1, reps))


def _row_tile(n, cap):
    t = min(n, cap)
    while n % t:
        t //= 2
    return t


def kernel(x_prompt, x_sample, cache_fox_kv, cache_fox_logf, cache_nsa_kv, state_nsa_win_kv, page_table,
           g_ffn1_pre, w_ffn1_gate, w_ffn1_up, w_ffn1_down, g_ffn1_post, g_mix_pre, w_in, b_fox_f,
           w_cmpk_1, w_cmpk_2, pe_cmpk, w_cmpv_1, w_cmpv_2, pe_cmpv, g_fox_out, g_nsa_out, w_out,
           g_mix_post, g_ffn2_pre, w_ffn2_gate, w_ffn2_up, w_ffn2_down, g_ffn2_post):
    depth = w_in.shape[0]
    b, t, d = x_prompt.shape
    s, dec_seq, _ = x_sample.shape
    assert dec_seq == 1, "the sample group decodes one token per sequence"
    page = cache_fox_kv.shape[2]
    n_pages = page_table.shape[1]
    past_len = n_pages * page
    assert t % LANES == 0 and page % SLC_LEN == 0
    page_table = page_table.astype(jnp.int32)

    tm_p = _row_tile(t, 512)
    cos_p, sin_p = _rope_tables(jnp.arange(t, dtype=jnp.int32))
    cos_s, sin_s = _rope_tables(jnp.full((s,), past_len, jnp.int32))
    row = lambda v: v.astype(F32)[None, :]
    to_rows = lambda c: jnp.transpose(c, (0, 2, 3, 4, 1)).reshape(c.shape[0], -1, c.shape[1])

    yp = x_prompt.reshape(b * t, d)
    ys = x_sample.reshape(s, d)
    outs = [[] for _ in range(8)]
    for l in range(depth):
        ffn1 = (row(g_ffn1_pre[l]), w_ffn1_gate[l].astype(BF16), w_ffn1_up[l].astype(BF16),
                w_ffn1_down[l].astype(BF16), row(g_ffn1_post[l]))
        ffn2 = (row(g_ffn2_pre[l]), w_ffn2_gate[l].astype(BF16), w_ffn2_up[l].astype(BF16),
                w_ffn2_down[l].astype(BF16), row(g_ffn2_post[l]))
        w_big, w_small, b_small = _prep_projection(w_in[l], b_fox_f[l])
        cw, w2, w2_pad = _prep_compress(w_cmpk_1[l], w_cmpk_2[l], pe_cmpk[l], w_cmpv_1[l], w_cmpv_2[l], pe_cmpv[l])
        merge = (row(g_fox_out[l]), row(g_nsa_out[l]), w_out[l].astype(BF16), row(g_mix_post[l]))

        hp = _half_ffn(yp, *ffn1, tm_p)
        small, fkv_t, nkv4_t, win_t, fq_t, fv_t, nq_t, nv_t, cmp, fk_p, nk_p = _project(
            hp, row(g_mix_pre[l]), w_big, w_small, b_small, cos_p, sin_p, tm_p, t // tm_p, True)
        small3 = small.reshape(b, t, LANES)
        per_head = lambda x: x.reshape(b, t, x.shape[1])
        cum, fk_b = _forget_bias(small3, per_head(fk_p), _row_tile(t, 512))
        o_fox = _fox_attention(fq_t, jnp.swapaxes(cum[:, :, :H_FOX], 1, 2), fk_b, fv_t,
                               _row_tile(t, 512), _row_tile(t, 512))
        kc_p, vc_t = _compress(cmp.reshape(b, t, 2 * W_KV), cw + (w2_pad,))
        o_nsa = _nsa_attention_t2(nq_t, jnp.swapaxes(small3, 1, 2), kc_p, vc_t, per_head(nk_p), nv_t,
                                 _row_tile(t, 256), _row_tile(t, 512))
        yp = _merge_ffn(hp, o_fox.reshape(b * t, W_FOX), o_nsa.reshape(b * t, W_NSA), *merge, *ffn2, tm_p)
        keep = min(WINDOW, t)
        tokens_first = lambda x, *dims: jnp.transpose(x.reshape(b, *dims, x.shape[2]), (0, 4, 1, 2, 3))
        outs[0].append(tokens_first(fkv_t, 2, H_FOX, HEAD_DIM))
        outs[1].append(small3[:, :, :H_FOX])
        outs[2].append(tokens_first(nkv4_t, 4, H_KV, HEAD_DIM))
        outs[3].append(tokens_first(win_t[:, :, t - keep:], 2, H_KV, HEAD_DIM))

        hs = _half_ffn(ys, *ffn1, s)
        small, fkv, nkv4, win, fq, nq = _project(
            hs, row(g_mix_pre[l]), w_big, w_small, b_small, cos_s, sin_s, s, 1, False)
        as_rows = lambda x: x.reshape(s, 1, x.shape[1])
        o_fox = _fox_decode(page_table, as_rows(fq), as_rows(fkv), as_rows(small),
                            to_rows(cache_fox_kv[l]), jnp.swapaxes(cache_fox_logf[l], 1, 2))
        o_nsa = _nsa_decode(page_table, as_rows(nq), as_rows(nkv4), as_rows(win), as_rows(small),
                            to_rows(state_nsa_win_kv[l]), to_rows(cache_nsa_kv[l]), cw + (w2,))
        ys = _merge_ffn(hs, o_fox.reshape(s, W_FOX), o_nsa.reshape(s, W_NSA), *merge, *ffn2, s)
        keep = min(WINDOW, past_len + 1)
        kw_all = jnp.concatenate([state_nsa_win_kv[l], win.reshape(s, 1, 2, H_KV, HEAD_DIM)], axis=1)
        outs[4].append(fkv.reshape(s, 1, 2, H_FOX, HEAD_DIM))
        outs[5].append(small[:, :H_FOX].reshape(s, 1, H_FOX))
        outs[6].append(nkv4.reshape(s, 1, 4, H_KV, HEAD_DIM))
        outs[7].append(kw_all[:, kw_all.shape[1] - keep:])

    stacked = [jnp.stack(o, axis=0) for o in outs]
    return (yp.reshape(b, t, d), ys.reshape(s, 1, d), *stacked)
```

```python
import functools

import jax
import jax.numpy as jnp
from jax import lax
from jax.experimental import pallas as pl
from jax.experimental.pallas import tpu as pltpu

HEAD_DIM = 64
H_FOX = 8
H_NSA = 8
H_KV = 2
GQA_GROUP = H_NSA // H_KV
W_FOX = H_FOX * HEAD_DIM
W_NSA = H_NSA * HEAD_DIM
W_KV = H_KV * HEAD_DIM
CMP_STRIDE = 16
CMP_LEN = 32
SLC_LEN = 64
SLC_TOPN = 16
WINDOW = 512
ROPE_THETA = 10000.0
EPS = 1e-6
NEG = -1e30
FORCE_SCORE = 1e4
N_SMALL = H_FOX + 3 * H_NSA
GATE0 = H_FOX

LANES = 128
MXU_N = 256
VMEM_LIMIT = 56 * 1024 * 1024
FOX_HEADS_PER_STEP = 4

F32 = jnp.float32
BF16 = jnp.bfloat16


def _dot(a, b):
    return jnp.dot(a, b, preferred_element_type=F32)


def _dot_nt(a, b):
    return lax.dot_general(a, b, (((1,), (1,)), ((), ())), preferred_element_type=F32)


def _div(x, n):
    assert n & (n - 1) == 0
    return lax.shift_right_logical(x, jnp.int32(n.bit_length() - 1))


def _split3(x):
    hi = x.astype(BF16)
    r1 = x - hi.astype(F32)
    mid = r1.astype(BF16)
    lo = (r1 - mid.astype(F32)).astype(BF16)
    return hi, mid, lo


def _rms(x, g):
    return x * lax.rsqrt(jnp.mean(x * x, axis=-1, keepdims=True) + EPS) * g


def _ff_chunks(d_ff):
    step = 6 * MXU_N
    return tuple((c, min(c + step, d_ff)) for c in range(0, d_ff, step))


def _ffn_core(x, gpre, wg_ref, wu_ref, wd_ref, gpost, chunks):
    xn = _rms(x, gpre).astype(BF16)
    acc = jnp.zeros(x.shape, F32)
    for c0, c1 in chunks:
        g = _dot(xn, wg_ref[:, c0:c1])
        u = _dot(xn, wu_ref[:, c0:c1])
        hm = (g * jax.nn.sigmoid(g) * u).astype(BF16)
        acc = acc + _dot(hm, wd_ref[c0:c1, :])
    return x + 0.5 * _rms(acc, gpost)


def _const_spec(shape):
    nd = len(shape)
    return pl.BlockSpec(shape, lambda *_: (0,) * nd, pipeline_mode=pl.Buffered(1))


def _params(sem):
    return pltpu.CompilerParams(dimension_semantics=sem, vmem_limit_bytes=VMEM_LIMIT)


def _lo_half(rows):
    return lax.broadcasted_iota(jnp.int32, (rows, LANES), 1) < HEAD_DIM


def _pad_heads(x, n_heads, fill):
    lo = _lo_half(x.shape[0])
    out = []
    for h in range(n_heads):
        piece = x[:, (h // 2) * LANES:(h // 2 + 1) * LANES]
        if h % 2:
            piece = pltpu.roll(piece, HEAD_DIM, axis=1)
        out.append(jnp.where(lo, piece, fill))
    return out


def _ffn_kernel(x_ref, gpre_ref, wg_ref, wu_ref, wd_ref, gpost_ref, o_ref, *, chunks):
    o_ref[...] = _ffn_core(x_ref[...], gpre_ref[...], wg_ref, wu_ref, wd_ref, gpost_ref[...], chunks)


def _half_ffn(x, gpre, wg, wu, wd, gpost, tm):
    n, d = x.shape
    d_ff = wg.shape[1]
    row = pl.BlockSpec((tm, d), lambda i: (i, 0))
    return pl.pallas_call(
        functools.partial(_ffn_kernel, chunks=_ff_chunks(d_ff)),
        grid=(n // tm,),
        in_specs=[row, _const_spec((1, d)), _const_spec((d, d_ff)), _const_spec((d, d_ff)),
                  _const_spec((d_ff, d)), _const_spec((1, d))],
        out_specs=row,
        out_shape=jax.ShapeDtypeStruct((n, d), F32),
        compiler_params=_params(("parallel",)),
        name="half_ffn",
    )(x, gpre, wg, wu, wd, gpost)


_C_FQ, _C_FK, _C_FV, _C_NQ, _C_KV, _C_NQR, _C_KR, _C_END = 0, 512, 1024, 1536, 2048, 2816, 3328, 3712


def _proj_kernel(h_ref, g_ref, wb_ref, ws_ref, bf_ref, cos_ref, sin_ref, small_ref, *rest, packed, n_pos_tiles):
    n = _rms(h_ref[...], g_ref[...]).astype(BF16)
    tm = n.shape[0]

    def mm(c0, c1):
        return _dot(n, wb_ref[:, c0:c1])

    def put(ref, tiles):
        for i, t in enumerate(tiles):
            ref[:, i * LANES:(i + 1) * LANES] = t.astype(BF16)

    def put_t(ref, tiles):
        for i, t in enumerate(tiles):
            for c in range(t.shape[1] // LANES):
                r0 = i * t.shape[1] + c * LANES
                ref[r0:r0 + LANES, :] = t[:, c * LANES:(c + 1) * LANES].T

    scale = HEAD_DIM ** -0.5
    lane = lax.broadcasted_iota(jnp.int32, (tm, LANES), 1)
    fq = mm(_C_FQ, _C_FK) * scale
    fk = mm(_C_FK, _C_FV)
    fv = mm(_C_FV, _C_NQ)

    cos = cos_ref[...]
    sin = sin_ref[...]
    nq = mm(_C_NQ, _C_KV)
    nqr = mm(_C_NQR, _C_KR)
    nq = jnp.concatenate([(nq[:, c * LANES:(c + 1) * LANES] * cos + nqr[:, c * LANES:(c + 1) * LANES] * sin) * scale
                          for c in range(W_NSA // LANES)], axis=1)

    kv = mm(_C_KV, _C_NQR)
    kr = mm(_C_KR, _C_END)
    ks, vs = [], []
    for br in range(3):
        k = kv[:, br * 2 * W_KV: br * 2 * W_KV + W_KV] * cos + kr[:, br * W_KV:(br + 1) * W_KV] * sin
        v = kv[:, br * 2 * W_KV + W_KV:(br + 1) * 2 * W_KV]
        ks.append(k)
        vs.append(v)

    sm = _dot(n, ws_ref[...]) + bf_ref[...]
    log_sig = jnp.minimum(sm, 0.0) - jnp.log(1.0 + jnp.exp(-jnp.abs(sm)))
    small_ref[...] = jnp.where(lane < H_FOX, log_sig, jax.nn.sigmoid(sm))

    if not packed:
        fkv_ref, nkv4_ref, win_ref, fq_ref, nq_ref = rest
        fkv_ref[...] = jnp.concatenate([fk, fv], axis=1)
        nkv4_ref[...] = jnp.concatenate([ks[0], vs[0], ks[1], vs[1]], axis=1)
        win_ref[...] = jnp.concatenate([ks[2], vs[2]], axis=1)
        fq_ref[...] = fq
        nq_ref[...] = nq
        return
    fkvt_ref, nkv4t_ref, wint_ref, fqt_ref, fvt_ref, nqt_ref, nvt_ref, cmp_ref, fkp_ref, nkp_ref = rest
    put_t(fkvt_ref, [fk, fv])
    put_t(nkv4t_ref, [ks[0], vs[0], ks[1], vs[1]])
    put_t(wint_ref, [ks[2], vs[2]])
    cmp_ref[...] = jnp.concatenate([ks[0], vs[0]], axis=1)

    def put_heads_t(ref, tiles):
        for i, t in enumerate(tiles):
            ref[i * LANES:(i + 1) * LANES, :] = t.T.astype(BF16)

    put_heads_t(fqt_ref, _pad_heads(fq, H_FOX, 0.0))
    put_heads_t(fvt_ref, _pad_heads(fv, H_FOX, 1.0))
    put_heads_t(nqt_ref, _pad_heads(nq, H_NSA, 0.0))
    put_heads_t(nvt_ref, _pad_heads(vs[1], H_KV, 1.0) + _pad_heads(vs[2], H_KV, 1.0))
    ones3 = jnp.where((lane >= HEAD_DIM) & (lane < HEAD_DIM + 3), 1.0, 0.0)
    put(fkp_ref, _pad_heads(fk, H_FOX, ones3))
    pos = (lax.rem(pl.program_id(0), n_pos_tiles) * tm + lax.broadcasted_iota(jnp.int32, (tm, 1), 0))
    onehot = jnp.where(lane - HEAD_DIM == _div(pos, SLC_LEN), 1.0, 0.0)
    put(nkp_ref, _pad_heads(ks[1], H_KV, onehot) + _pad_heads(ks[2], H_KV, 0.0))


def _project(h, g, w_big, w_small, b_small, cos, sin, tm, n_pos_tiles, packed):
    n, d = h.shape
    row = lambda w: pl.BlockSpec((tm, w), lambda i: (i, 0))
    pos = pl.BlockSpec((tm, LANES), lambda i: (i % n_pos_tiles, 0))
    if packed:
        outs = [(LANES, F32), (2 * W_KV, F32), (H_FOX * LANES, BF16), (2 * H_KV * LANES, BF16)]
        t_outs = [(2 * W_FOX, F32), (4 * W_KV, F32), (2 * W_KV, F32), (H_FOX * LANES, BF16),
                  (H_FOX * LANES, BF16), (H_NSA * LANES, BF16), (2 * H_KV * LANES, BF16)]
    else:
        outs = [(LANES, F32), (2 * W_FOX, F32), (4 * W_KV, F32), (2 * W_KV, F32), (W_FOX, F32), (W_NSA, F32)]
        t_outs = []
    out_specs = [row(w) for w, _ in outs]
    out_shape = [jax.ShapeDtypeStruct((n, w), dt) for w, dt in outs]
    batch = n // (tm * n_pos_tiles)
    for k, (r, dt) in enumerate(t_outs):
        out_specs.insert(1 + k, pl.BlockSpec((None, r, tm), lambda i: (i // n_pos_tiles, 0, i % n_pos_tiles)))
        out_shape.insert(1 + k, jax.ShapeDtypeStruct((batch, r, tm * n_pos_tiles), dt))
    return pl.pallas_call(
        functools.partial(_proj_kernel, packed=packed, n_pos_tiles=n_pos_tiles),
        grid=(n // tm,),
        in_specs=[row(d), _const_spec((1, d)), _const_spec(w_big.shape), _const_spec(w_small.shape),
                  _const_spec((1, LANES)), pos, pos],
        out_specs=out_specs,
        out_shape=out_shape,
        compiler_params=_params(("parallel",)),
        name="project",
    )(h, g, w_big, w_small, b_small, cos, sin)


def _merge_ffn_kernel(h_ref, of_ref, on_ref, gf_ref, gn_ref, wo_ref, gmix_ref,
                      gpre_ref, wg_ref, wu_ref, wd_ref, gpost_ref, y_ref, *, chunks):
    of = _rms(of_ref[...], gf_ref[...]).astype(BF16)
    on = _rms(on_ref[...], gn_ref[...]).astype(BF16)
    mrg = _dot(of, wo_ref[0:W_FOX, :]) + _dot(on, wo_ref[W_FOX:W_FOX + W_NSA, :])
    h2 = h_ref[...] + _rms(mrg, gmix_ref[...])
    y_ref[...] = _ffn_core(h2, gpre_ref[...], wg_ref, wu_ref, wd_ref, gpost_ref[...], chunks)


def _merge_ffn(h, o_fox, o_nsa, gf, gn, w_out, gmix, gpre, wg, wu, wd, gpost, tm):
    n, d = h.shape
    d_ff = wg.shape[1]
    row = lambda w: pl.BlockSpec((tm, w), lambda i: (i, 0))
    return pl.pallas_call(
        functools.partial(_merge_ffn_kernel, chunks=_ff_chunks(d_ff)),
        grid=(n // tm,),
        in_specs=[row(d), row(W_FOX), row(W_NSA), _const_spec((1, W_FOX)), _const_spec((1, W_NSA)),
                  _const_spec(w_out.shape), _const_spec((1, d)), _const_spec((1, d)),
                  _const_spec((d, d_ff)), _const_spec((d, d_ff)), _const_spec((d_ff, d)), _const_spec((1, d))],
        out_specs=row(d),
        out_shape=jax.ShapeDtypeStruct((n, d), F32),
        compiler_params=_params(("parallel",)),
        name="merge_ffn",
    )(h, o_fox, o_nsa, gf, gn, w_out, gmix, gpre, wg, wu, wd, gpost)


def _forget_bias_kernel(x_ref, k_ref, cum_ref, k2_ref, carry_ref):
    @pl.when(pl.program_id(1) == 0)
    def _():
        carry_ref[...] = jnp.zeros(carry_ref.shape, F32)

    x = x_ref[...]
    tc = x.shape[0]
    r = lax.broadcasted_iota(jnp.int32, (tc, tc), 0)
    c = lax.broadcasted_iota(jnp.int32, (tc, tc), 1)
    tri = (c <= r).astype(BF16)
    hi, mid, lo = _split3(x)
    cs = _dot(tri, hi) + _dot(tri, mid) + _dot(tri, lo) + carry_ref[...]
    carry_ref[...] = cs[tc - 1:tc, :]
    cum_ref[...] = cs

    lane = lax.broadcasted_iota(jnp.int32, (tc, LANES), 1)
    for h in range(H_FOX):
        sl = slice(h * LANES, (h + 1) * LANES)
        neg = [p.astype(F32) for p in _split3(-cs[:, h:h + 1])]
        k = k_ref[:, sl].astype(F32)
        for i in range(3):
            k = jnp.where(lane == HEAD_DIM + 3 + i, neg[i], k)
        k2_ref[:, sl] = k.astype(BF16)


def _forget_bias(small, fk_p, tc):
    b, t, w = small.shape
    blk = lambda n: pl.BlockSpec((None, tc, n), lambda i, j: (i, j, 0))
    wide = H_FOX * LANES
    return pl.pallas_call(
        _forget_bias_kernel,
        grid=(b, t // tc),
        in_specs=[blk(w), blk(wide)],
        out_specs=[blk(w), blk(wide)],
        out_shape=[jax.ShapeDtypeStruct((b, t, w), F32), jax.ShapeDtypeStruct((b, t, wide), BF16)],
        scratch_shapes=[pltpu.VMEM((1, w), F32)],
        compiler_params=_params(("parallel", "arbitrary")),
        name="forget_bias",
    )(small, fk_p)


def _flash_update_t(s, vt, m_ref, acc_ref, idx, allow=None, bias=None):
    if allow is not None:
        s = jnp.where(allow, s, NEG)
    if bias is not None:
        s = s + bias
    m_old = m_ref[idx]
    m_new = jnp.maximum(m_old, jnp.max(s, axis=0, keepdims=True))
    p = jnp.exp(s - m_new)
    acc_ref[idx] = jnp.exp(m_old - m_new) * acc_ref[idx] + _dot(vt, p.astype(BF16))
    m_ref[idx] = m_new


def _normalise_t(acc):
    return acc[0:HEAD_DIM] * (1.0 / acc[HEAD_DIM:HEAD_DIM + 1])


def _fox_kernel(qt_ref, cum_ref, k_ref, vt_ref, o_ref, qb_ref, m_ref, acc_ref, *, tq, tk, heads):
    hg = pl.program_id(1)
    q0 = pl.program_id(2) * tq
    m_ref[...] = jnp.full(m_ref.shape, NEG, F32)
    acc_ref[...] = jnp.zeros(acc_ref.shape, F32)
    krow = lax.broadcasted_iota(jnp.int32, (tk, tq), 0)
    qpos = q0 + lax.broadcasted_iota(jnp.int32, (tk, tq), 1)
    head = lambda g: slice(g * LANES, (g + 1) * LANES)
    row = lax.broadcasted_iota(jnp.int32, (LANES, tq), 0)
    row8 = lax.broadcasted_iota(jnp.int32, (H_FOX, tq), 0)
    cum_all = cum_ref[...]
    for g in range(heads):
        cum = jnp.sum(jnp.where(row8 == hg * heads + g, cum_all, 0.0), axis=0, keepdims=True)
        q = qt_ref[head(g), :].astype(F32)
        for i, piece in enumerate(_split3(cum)):
            q = jnp.where(row == HEAD_DIM + i, piece.astype(F32), q)
        qb_ref[g] = jnp.where((row >= HEAD_DIM + 3) & (row < HEAD_DIM + 6), 1.0, q).astype(BF16)

    def scores(kt):
        k0 = pl.multiple_of(kt * tk, tk)
        return tuple(_dot(k_ref[pl.ds(k0, tk), head(g)], qb_ref[g]) for g in range(heads))

    def update(kt, s, masked):
        k0 = pl.multiple_of(kt * tk, tk)
        allow = (k0 + krow <= qpos) if masked else None
        for g in range(heads):
            _flash_update_t(s[g], vt_ref[head(g), pl.ds(k0, tk)], m_ref, acc_ref, g, allow=allow)

    def body(kt, s):
        s_next = scores(kt + 1)
        update(kt, s, False)
        return s_next

    n_full = _div(q0, tk)
    update(n_full, lax.fori_loop(0, n_full, body, scores(0)), True)
    for pr in range(heads // 2):
        pair = jnp.concatenate([_normalise_t(acc_ref[2 * pr]), _normalise_t(acc_ref[2 * pr + 1])], axis=0)
        o_ref[:, pr * LANES:(pr + 1) * LANES] = pair.T


def _fox_attention(fq_t, cum_t, fk_p, fv_t, tq, tk):
    b, t, _ = fk_p.shape
    g = FOX_HEADS_PER_STEP
    qblk = pl.BlockSpec((None, g * LANES, tq), lambda i, hg, qi: (i, hg, qi))
    cblk = pl.BlockSpec((None, H_FOX, tq), lambda i, hg, qi: (i, 0, qi))
    kblk = pl.BlockSpec((None, t, g * LANES), lambda i, hg, qi: (i, 0, hg))
    vblk = pl.BlockSpec((None, g * LANES, t), lambda i, hg, qi: (i, hg, 0))
    oblk = pl.BlockSpec((None, tq, g * HEAD_DIM), lambda i, hg, qi: (i, qi, hg))
    return pl.pallas_call(
        functools.partial(_fox_kernel, tq=tq, tk=tk, heads=g),
        grid=(b, H_FOX // g, t // tq),
        in_specs=[qblk, cblk, kblk, vblk],
        out_specs=oblk,
        out_shape=jax.ShapeDtypeStruct((b, t, W_FOX), F32),
        scratch_shapes=[pltpu.VMEM((g, LANES, tq), BF16), pltpu.VMEM((g, 1, tq), F32),
                        pltpu.VMEM((g, LANES, tq), F32)],
        compiler_params=_params(("parallel", "parallel", "arbitrary")),
        name="fox_prompt",
    )(fq_t, cum_t, fk_p, fv_t)


def _compress_hidden(load_rows, n_ch, wj_ref, pe_ref, wpe_ref):
    acc_k = jnp.zeros((n_ch, 2 * W_KV), F32)
    acc_v = jnp.zeros((n_ch, 2 * W_KV), F32)
    for j in range(CMP_STRIDE):
        xk, xv = load_rows(j)
        acc_k = acc_k + _dot(xk.astype(BF16), wj_ref[j, 0:W_KV, :])
        acc_v = acc_v + _dot(xv.astype(BF16), wj_ref[j, W_KV:2 * W_KV, :])
    first = jnp.concatenate([acc_k[:, 0:W_KV], acc_v[:, 0:W_KV]], axis=1)
    second = jnp.concatenate([acc_k[:, W_KV:], acc_v[:, W_KV:]], axis=1)
    second = pltpu.roll(second, n_ch - 1, axis=0)
    pe_term = _dot(pe_ref[...], wpe_ref[...])[0:1, :]
    return jax.nn.gelu(first + second + pe_term)


def _compress_kernel(xk_ref, xv_ref, wj_ref, pe_ref, wpe_ref, w2p_ref, kc_ref, vc_ref, *, n_ch):
    rows = lambda j: (xk_ref[pl.ds(j, n_ch, stride=CMP_STRIDE), :], xv_ref[pl.ds(j, n_ch, stride=CMP_STRIDE), :])
    hid = _compress_hidden(rows, n_ch, wj_ref, pe_ref, wpe_ref).astype(BF16)
    out = _dot(hid, w2p_ref[...])
    kc_ref[...] = out[:, 0:H_KV * LANES].astype(BF16)
    for h in range(H_KV):
        vc = out[:, (H_KV + h) * LANES:(H_KV + h + 1) * LANES]
        vc_ref[h * LANES:(h + 1) * LANES, :] = jnp.where(_lo_half(n_ch), vc, 1.0).T.astype(BF16)


def _compress(nkv4, cw):
    b, t, _ = nkv4.shape
    n_ch = t // CMP_STRIDE
    out = pl.BlockSpec((None, n_ch, H_KV * LANES), lambda i: (i, 0, 0))
    return pl.pallas_call(
        functools.partial(_compress_kernel, n_ch=n_ch),
        grid=(b,),
        in_specs=[pl.BlockSpec((None, t, W_KV), lambda i: (i, 0, 0)),
                  pl.BlockSpec((None, t, W_KV), lambda i: (i, 0, 1))] + [_const_spec(w.shape) for w in cw],
        out_specs=[out, pl.BlockSpec((None, H_KV * LANES, n_ch), lambda i: (i, 0, 0))],
        out_shape=[jax.ShapeDtypeStruct((b, n_ch, H_KV * LANES), BF16),
                   jax.ShapeDtypeStruct((b, H_KV * LANES, n_ch), BF16)],
        compiler_params=_params(("parallel",)),
        name="compress_prompt",
    )(nkv4, nkv4, *cw)


def _overlap(n_ch, n_slc):
    n = lax.broadcasted_iota(jnp.int32, (n_ch, LANES), 0)
    j = lax.broadcasted_iota(jnp.int32, (n_ch, LANES), 1)
    hit = ((n * CMP_STRIDE <= j * SLC_LEN + SLC_LEN - 1) & (n * CMP_STRIDE + CMP_LEN - 1 >= j * SLC_LEN)
           & (n < n_ch - 1) & (j < n_slc))
    return hit.astype(BF16)


def _overlap_t(n_ch, n_slc):
    j = lax.broadcasted_iota(jnp.int32, (LANES, n_ch), 0)
    n = lax.broadcasted_iota(jnp.int32, (LANES, n_ch), 1)
    hit = ((n * CMP_STRIDE <= j * SLC_LEN + SLC_LEN - 1) & (n * CMP_STRIDE + CMP_LEN - 1 >= j * SLC_LEN)
           & (n < n_ch - 1) & (j < n_slc))
    return hit.astype(BF16)


def _block_scores(imp, pos, n_slc, axis):
    blk = lax.broadcasted_iota(jnp.int32, imp.shape, axis)
    qblk = _div(pos, SLC_LEN)
    forced = (blk == 0) | (blk == qblk) | (blk == qblk - 1)
    valid = blk * SLC_LEN <= pos
    score = jnp.where(valid, jnp.where(forced, FORCE_SCORE, imp), -1.0)
    return jnp.where(blk < n_slc, score, -2.0)


def _count_beats_lanes(score, n_slc):
    blk = lax.broadcasted_iota(jnp.int32, score.shape, 1)
    cnt = jnp.zeros(score.shape, jnp.int32)
    for i in range(n_slc):
        si = score[:, i:i + 1]
        cnt = cnt + jnp.where((si > score) | ((si == score) & (blk > i)), 1, 0)
    return cnt


def _count_beats_rows(score, lo, hi, n_rows):
    out = []
    for v in range(n_rows // 8):
        s_v = score[8 * v:8 * v + 8, :]
        blk = 8 * v + lax.broadcasted_iota(jnp.int32, s_v.shape, 0)
        cnt = jnp.zeros(s_v.shape, jnp.int32)
        for i in range(lo, hi):
            si = score[i:i + 1, :]
            if 8 * v + 7 < i:
                beats = si > s_v
            elif 8 * v > i:
                beats = si >= s_v
            else:
                beats = (si > s_v) | ((si == s_v) & (blk > i))
            cnt = cnt + jnp.where(beats, 1, 0)
        out.append(cnt)
    return jnp.concatenate(out, axis=0)


def _nsa_t2_kernel(qt_ref, gate_ref, kc_ref, vct_ref, sk_ref, svt_ref, wk_ref, wvt_ref, o_ref,
                   q4_ref, q4s_ref, cnt_ref, m_ref, acc_ref, *, tq, tk, n_ch, n_slc, wlen):
    q0 = pl.program_id(1) * tq
    g = GQA_GROUP
    w = g * tq
    n_sel = min(SLC_TOPN, n_slc)
    hk = range(H_KV)
    blk = lambda h: slice(h * LANES, (h + 1) * LANES)
    pos = q0 + lax.broadcasted_iota(jnp.int32, (1, tq), 1)
    heads = lambda x: jnp.concatenate([x] * g, axis=1)
    for h in hk:
        for i in range(g):
            q4_ref[h, :, i * tq:(i + 1) * tq] = qt_ref[(g * h + i) * LANES:(g * h + i + 1) * LANES, :]
    q4 = [q4_ref[h] for h in hk]

    n_idx = lax.broadcasted_iota(jnp.int32, (n_ch, tq), 0)
    ok = (n_idx * CMP_STRIDE + CMP_LEN - 1 <= pos) & (n_idx < n_ch - 1)
    ok_bias = heads(jnp.where(ok, 0.0, NEG))
    ok_keep = heads(jnp.where(ok, 1.0, 0.0))
    ov_t = _overlap_t(n_ch, n_slc)
    o_cmp, score = [], []
    for h in hk:
        sc = _dot(kc_ref[:, blk(h)], q4[h]) + ok_bias
        e = jnp.exp(sc - jnp.max(sc, axis=0, keepdims=True)) * ok_keep
        lc = jnp.sum(e, axis=0, keepdims=True)
        a = e * (1.0 / jnp.where(lc > 0.0, lc, 1.0))
        a_hi = a.astype(BF16)
        a_lo = (a - a_hi.astype(F32)).astype(BF16)
        o_cmp.append(_dot(vct_ref[blk(h), :], a_hi))
        imp4 = _dot(ov_t, a_hi) + _dot(ov_t, a_lo)
        imp_t = sum(imp4[:, i * tq:(i + 1) * tq] for i in range(g))
        score.append(_block_scores(imp_t, pos, n_slc, 0))

    n_valid = _div(q0 + tq - 1, SLC_LEN) + 1
    n_rows = cnt_ref.shape[1]
    cnt_ref[...] = jnp.zeros(cnt_ref.shape, jnp.int32)
    for b0 in range(0, n_slc, 8):
        @pl.when((b0 < n_valid) & (n_valid > n_sel))
        def _():
            for h in hk:
                cnt_ref[h] += _count_beats_rows(score[h], b0, min(b0 + 8, n_slc), n_rows)
    for h in hk:
        sel_t = (cnt_ref[h] < n_sel) & (lax.broadcasted_iota(jnp.int32, (n_rows, tq), 0) < n_slc)
        bias = jnp.where(sel_t, 0.0, NEG)
        if n_rows < HEAD_DIM:
            bias = jnp.concatenate([bias, jnp.full((HEAD_DIM - n_rows, tq), NEG, F32)], axis=0)
        q4s_ref[h, 0:HEAD_DIM, :] = q4[h][0:HEAD_DIM]
        for i in range(g):
            q4s_ref[h, HEAD_DIM:2 * HEAD_DIM, i * tq:(i + 1) * tq] = bias.astype(BF16)

    m_ref[...] = jnp.full(m_ref.shape, NEG, F32)
    acc_ref[...] = jnp.zeros(acc_ref.shape, F32)
    krow = lax.broadcasted_iota(jnp.int32, (tk, tq), 0)

    def scores(kt):
        k0 = pl.multiple_of(kt * tk, tk)
        return tuple(_dot(sk_ref[pl.ds(k0, tk), blk(h)], q4s_ref[h]) for h in hk)

    def update(kt, s, masked):
        k0 = pl.multiple_of(kt * tk, tk)
        bias = heads(jnp.where(k0 + krow <= pos, 0.0, NEG)) if masked else None
        for h in hk:
            _flash_update_t(s[h], svt_ref[blk(h), pl.ds(k0, tk)], m_ref, acc_ref, h, bias=bias)

    def body(kt, s):
        s_next = scores(kt + 1)
        update(kt, s, False)
        return s_next

    n_full = _div(q0, tk)
    update(n_full, lax.fori_loop(0, n_full, body, scores(0)), True)

    ws = pl.multiple_of(jnp.maximum(q0 + tq - wlen, 0), tq)
    dist = pos - (ws + lax.broadcasted_iota(jnp.int32, (wlen, tq), 0))
    win_bias = heads(jnp.where((dist >= 0) & (dist < WINDOW), 0.0, NEG))
    gates = gate_ref[...]
    grow = lax.broadcasted_iota(jnp.int32, (LANES, tq), 0)
    for h in hk:
        o_slc = _normalise_t(acc_ref[h])
        sw = _dot(wk_ref[pl.ds(ws, wlen), blk(h)], q4[h]) + win_bias
        pw = jnp.exp(sw - jnp.max(sw, axis=0, keepdims=True))
        o_win = _normalise_t(_dot(wvt_ref[blk(h), pl.ds(ws, wlen)], pw.astype(BF16)))
        vals = []
        for i in range(g):
            r0 = GATE0 + 3 * (g * h + i)
            gc = [jnp.sum(jnp.where(grow == r0 + c, gates, 0.0), axis=0, keepdims=True) for c in range(3)]
            sl = slice(i * tq, (i + 1) * tq)
            vals.append(gc[0] * o_cmp[h][0:HEAD_DIM, sl] + gc[1] * o_slc[:, sl] + gc[2] * o_win[:, sl])
        for pr in range(g // 2):
            lanes = slice((h * (g // 2) + pr) * LANES, (h * (g // 2) + pr + 1) * LANES)
            o_ref[:, lanes] = jnp.concatenate([vals[2 * pr], vals[2 * pr + 1]], axis=0).T


def _nsa_attention_t2(nq_t, small_t, kc_p, vc_t, nk_p, nv_t, tq, tk):
    b, _, t = nq_t.shape
    n_ch = kc_p.shape[1]
    n_slc = -(-t // SLC_LEN)
    assert n_slc <= HEAD_DIM, "the block mask rides in the 64 spare query rows"
    wlen = min(t, WINDOW + tq)
    g = GQA_GROUP
    wide = H_KV * LANES
    kblk = lambda c: pl.BlockSpec((None, t, wide), lambda i, qi: (i, 0, c))
    vblk = lambda c: pl.BlockSpec((None, wide, t), lambda i, qi: (i, c, 0))
    return pl.pallas_call(
        functools.partial(_nsa_t2_kernel, tq=tq, tk=tk, n_ch=n_ch, n_slc=n_slc, wlen=wlen),
        grid=(b, t // tq),
        in_specs=[pl.BlockSpec((None, H_NSA * LANES, tq), lambda i, qi: (i, 0, qi)),
                  pl.BlockSpec((None, LANES, tq), lambda i, qi: (i, 0, qi)),
                  pl.BlockSpec((None, n_ch, wide), lambda i, qi: (i, 0, 0)),
                  pl.BlockSpec((None, wide, n_ch), lambda i, qi: (i, 0, 0)),
                  kblk(0), vblk(0), kblk(1), vblk(1)],
        out_specs=pl.BlockSpec((None, tq, W_NSA), lambda i, qi: (i, qi, 0)),
        out_shape=jax.ShapeDtypeStruct((b, t, W_NSA), F32),
        scratch_shapes=[pltpu.VMEM((H_KV, LANES, g * tq), BF16), pltpu.VMEM((H_KV, LANES, g * tq), BF16),
                        pltpu.VMEM((H_KV, -(-n_slc // 8) * 8, tq), jnp.int32),
                        pltpu.VMEM((H_KV, 1, g * tq), F32), pltpu.VMEM((H_KV, LANES, g * tq), F32)],
        compiler_params=_params(("parallel", "arbitrary")),
        name="nsa_prompt",
    )(nq_t, small_t, kc_p, vc_t, nk_p, nv_t, nk_p, nv_t)


def _page_copies(pt_ref, step, slot, n_pages, page, streams, group):
    out = []
    for u in range(group):
        for pg in range(n_pages):
            idx = pt_ref[(step * group + u) * n_pages + pg]
            for hbm, buf, sem, on_lanes in streams:
                dst = buf.at[slot, u, :, pl.ds(pg * page, page)] if on_lanes else buf.at[slot, u, pg]
                out.append(pltpu.make_async_copy(hbm.at[idx], dst, sem.at[slot]))
    return out


def _gather_pages(pt_ref, n_pages, page, streams, group=1):
    i = pl.program_id(0)
    slot = lax.rem(i, 2)

    @pl.when(i == 0)
    def _():
        for n, c in enumerate(_page_copies(pt_ref, 0, 0, n_pages, page, streams, group)):
            c.start(priority=(n // len(streams)) % 2)

    @pl.when(i + 1 < pl.num_programs(0))
    def _():
        for n, c in enumerate(_page_copies(pt_ref, i + 1, 1 - slot, n_pages, page, streams, group)):
            c.start(priority=(n // len(streams)) % 2)

    for c in _page_copies(pt_ref, i, slot, n_pages, page, streams, group):
        c.wait()
    return slot


def _head_scores(q_col, k_view, s_ref, head, rows0, n_tiles, tile):
    qb = jnp.broadcast_to(q_col, (HEAD_DIM, tile))
    for pg in range(n_tiles):
        kt = k_view[rows0:rows0 + HEAD_DIM, pg * tile:(pg + 1) * tile]
        s_ref[pg, head:head + 1, :] = jnp.sum(kt * qb, axis=0, keepdims=True)


def _head_values(p_tile, v_view, rows0, n_tiles, tile):
    acc = jnp.zeros((HEAD_DIM, tile), F32)
    for pg in range(n_tiles):
        vt = v_view[rows0:rows0 + HEAD_DIM, pg * tile:(pg + 1) * tile]
        acc = acc + vt * jnp.broadcast_to(p_tile(pg), (HEAD_DIM, tile))
    return jnp.sum(acc, axis=1, keepdims=True)


def _softmax_tiles(s, s_new):
    m = jnp.maximum(jnp.max(jnp.max(s, axis=0), axis=1, keepdims=True), s_new)
    p = jnp.exp(s - m[None])
    p_new = jnp.exp(s_new - m)
    l = jnp.sum(jnp.sum(p, axis=0), axis=1, keepdims=True) + p_new
    return p, p_new, 1.0 / l


def _softmax_rows(s, s_new, allow):
    s = jnp.where(allow, s, NEG)
    m = jnp.maximum(jnp.max(s, axis=1, keepdims=True), s_new)
    p = jnp.where(allow, jnp.exp(s - m), 0.0)
    p_new = jnp.exp(s_new - m)
    return p, p_new, 1.0 / (jnp.sum(p, axis=1, keepdims=True) + p_new)


def _col_dot(a_col, b_col, n_heads):
    return jnp.sum((a_col * b_col).reshape(n_heads, HEAD_DIM, 1), axis=1)


def _as_column(row):
    return jnp.broadcast_to(row, (LANES, row.shape[1])).T[:, 0:1]


def _as_row(col):
    return jnp.broadcast_to(col, (col.shape[0], LANES)).T[0:1, :]


def _fox_dec_kernel(pt_ref, q_ref, kvnew_ref, small_ref, kv_hbm, lf_hbm, o_ref,
                    kvbuf, lfbuf, s_ref, p_ref, sem_kv, sem_lf, *, n_pages, page, group):
    slot = _gather_pages(pt_ref, n_pages, page,
                         [(kv_hbm, kvbuf, sem_kv, True), (lf_hbm, lfbuf, sem_lf, False)], group)
    for u in range(group):
        o_ref[u] = _fox_dec_one(kvbuf.at[slot, u], lfbuf[slot, u], q_ref[u], kvnew_ref[u], small_ref[u],
                                s_ref, p_ref, n_pages, page)


def _fox_dec_one(kv, lf, q_row, kvnew_row, small_row, s_ref, p_ref, n_pages, page):
    q = _as_column(q_row)
    kv_new = _as_column(kvnew_row)
    lf_new = jnp.concatenate([_as_column(small_row)[0:H_FOX]] * n_pages, axis=0)
    for h in range(H_FOX):
        _head_scores(q[h * HEAD_DIM:(h + 1) * HEAD_DIM], kv, s_ref, h, h * HEAD_DIM, n_pages, page)

    rows = n_pages * H_FOX
    lf = lf.reshape(rows, page)
    r = lax.broadcasted_iota(jnp.int32, (page, page), 0)
    c = lax.broadcasted_iota(jnp.int32, (page, page), 1)
    later = (r > c).astype(BF16)
    hi, mid, lo = _split3(lf)
    within = _dot(hi, later) + _dot(mid, later) + _dot(lo, later)
    r = lax.broadcasted_iota(jnp.int32, (rows, rows), 0)
    c = lax.broadcasted_iota(jnp.int32, (rows, rows), 1)
    later_pages = ((c > r) & (((c - r) & (H_FOX - 1)) == 0)).astype(BF16)
    tot = jnp.broadcast_to(jnp.sum(lf, axis=1, keepdims=True), (rows, page))
    hi, mid, lo = _split3(tot)
    beyond = _dot(later_pages, hi) + _dot(later_pages, mid) + _dot(later_pages, lo)
    bias = (within + beyond + lf_new).reshape(n_pages, H_FOX, page)

    s_new = _col_dot(q, kv_new[0:W_FOX], H_FOX)
    p, p_new, inv_l = _softmax_tiles(s_ref[...] + bias, s_new)
    p_ref[...] = p
    outs = []
    for h in range(H_FOX):
        o = _head_values(lambda pg, h=h: p_ref[pg, h:h + 1, :], kv, W_FOX + h * HEAD_DIM, n_pages, page)
        v_new = kv_new[W_FOX + h * HEAD_DIM: W_FOX + (h + 1) * HEAD_DIM]
        outs.append((o + p_new[h:h + 1] * v_new) * inv_l[h:h + 1])
    return _as_row(jnp.concatenate(outs, axis=0))


def _fox_decode(page_table, q_row, kv_row, small_row, cache_kvt, cache_lft):
    s, n_pages = page_table.shape
    rows, page = cache_kvt.shape[1], cache_kvt.shape[2]
    group = 2 if s % 2 == 0 else 1
    col = lambda n: pl.BlockSpec((group, 1, n), lambda i, pt: (i, 0, 0))
    anyspec = pl.BlockSpec(memory_space=pl.ANY)
    return pl.pallas_call(
        functools.partial(_fox_dec_kernel, n_pages=n_pages, page=page, group=group),
        grid_spec=pltpu.PrefetchScalarGridSpec(
            num_scalar_prefetch=1,
            grid=(s // group,),
            in_specs=[col(W_FOX), col(2 * W_FOX), col(LANES), anyspec, anyspec],
            out_specs=col(W_FOX),
            scratch_shapes=[pltpu.VMEM((2, group, rows, n_pages * page), F32),
                            pltpu.VMEM((2, group, n_pages, H_FOX, page), F32),
                            pltpu.VMEM((n_pages, H_FOX, page), F32), pltpu.VMEM((n_pages, H_FOX, page), F32),
                            pltpu.SemaphoreType.DMA((2,)), pltpu.SemaphoreType.DMA((2,))],
        ),
        out_shape=jax.ShapeDtypeStruct((s, 1, W_FOX), F32),
        compiler_params=_params(("arbitrary",)),
        name="fox_decode",
    )(page_table.reshape(-1), q_row, kv_row, small_row, cache_kvt, cache_lft)


def _nsa_dec_kernel(pt_ref, qrow_ref, nkv4_ref, wnew_ref, gate_ref, win_ref, cache_hbm,
                    wj_ref, pe_ref, wpe_ref, w2_ref, o_ref,
                    xbuf, xk_buf, xv_buf, sem, *, n_pages, page, group):
    slot = _gather_pages(pt_ref, n_pages, page, [(cache_hbm, xbuf, sem, True)], group)
    n_ch = n_pages * page // CMP_STRIDE

    r = lax.broadcasted_iota(jnp.int32, (page, page), 0)
    t = lax.broadcasted_iota(jnp.int32, (page, page), 1)
    per = page // CMP_STRIDE
    perm = (t == CMP_STRIDE * (r & (per - 1)) + _div(r, per)).astype(BF16)
    for u in range(group):
        for pg in range(n_pages):
            xt = _dot_nt(perm, xbuf[slot, u, 0:2 * W_KV, pg * page:(pg + 1) * page].astype(BF16))
            c0 = u * n_ch + pg * per
            for j in range(CMP_STRIDE):
                xk_buf[j, c0:c0 + per, :] = xt[j * per:(j + 1) * per, 0:W_KV]
                xv_buf[j, c0:c0 + per, :] = xt[j * per:(j + 1) * per, W_KV:2 * W_KV]
    hid = _compress_hidden(lambda jj: (xk_buf[jj], xv_buf[jj]), group * n_ch, wj_ref, pe_ref, wpe_ref).astype(BF16)
    for u in range(group):
        o_ref[u] = _nsa_dec_one(hid[u * n_ch:(u + 1) * n_ch], xbuf.at[slot, u], qrow_ref[u], nkv4_ref[u],
                                wnew_ref[u], gate_ref[u], win_ref.at[u], w2_ref, n_pages, page)


def _nsa_dec_one(hid, x, q_row, nkv4_new, win_new, gates, wv, w2_ref, n_pages, page):
    past_len = n_pages * page
    n_ch = past_len // CMP_STRIDE
    n_slc = past_len // SLC_LEN + 1
    n_sel = min(SLC_TOPN, n_slc)
    win_buf = wv.shape[1]
    g = GQA_GROUP
    row = lax.broadcasted_iota(jnp.int32, (H_NSA, LANES), 0)
    lane = lax.broadcasted_iota(jnp.int32, (H_NSA, LANES), 1)
    kc = _dot(hid, w2_ref[:, 0:W_KV]).astype(BF16)
    vc = _dot(hid, w2_ref[:, W_KV:2 * W_KV]).astype(BF16)

    qbd = jnp.zeros((H_NSA, LANES), F32)
    for i in range(H_NSA):
        piece = q_row[:, (i // 2) * LANES:(i // 2 + 1) * LANES]
        if (i % 2) != (i // g):
            piece = pltpu.roll(piece, HEAD_DIM, axis=1)
        qbd = jnp.where(row == i, jnp.broadcast_to(piece, (H_NSA, LANES)), qbd)
    qbd = jnp.where(_div(lane, HEAD_DIM) == _div(row, g), qbd, 0.0).astype(BF16)
    n_idx = lax.broadcasted_iota(jnp.int32, (H_NSA, n_ch), 1)
    ok = (n_idx * CMP_STRIDE + CMP_LEN - 1 <= past_len) & (n_idx < n_ch - 1)
    sc = jnp.where(ok, _dot_nt(qbd, kc), NEG)
    e = jnp.where(ok, jnp.exp(sc - jnp.max(sc, axis=1, keepdims=True)), 0.0)
    lc = jnp.sum(e, axis=1, keepdims=True)
    a = e * (1.0 / jnp.where(lc > 0.0, lc, 1.0))
    a_hi = a.astype(BF16)
    a_lo = (a - a_hi.astype(F32)).astype(BF16)
    o_cmp = _dot(a_hi, vc)
    ov = _overlap(n_ch, n_slc)
    imp8 = _dot(a_hi, ov) + _dot(a_lo, ov)
    imp = jnp.zeros((H_NSA, LANES), F32)
    for h in range(H_KV):
        tot = jnp.sum(imp8[h * g:(h + 1) * g], axis=0, keepdims=True)
        imp = jnp.where(_div(row, g) == h, jnp.broadcast_to(tot, (H_NSA, LANES)), imp)
    score = _block_scores(imp, jnp.full((H_NSA, 1), past_len, jnp.int32), n_slc, 1)
    sel = jnp.where((_count_beats_lanes(score, n_slc) < n_sel) & (lane < n_slc), 1.0, 0.0)

    per_page = page // SLC_LEN
    allow = []
    for pg in range(n_pages):
        m = jnp.zeros((H_NSA, page), F32)
        for b in range(per_page):
            blk = pg * per_page + b
            lanes_b = _div(lax.broadcasted_iota(jnp.int32, (H_NSA, page), 1), SLC_LEN) == b
            m = jnp.where(lanes_b, jnp.broadcast_to(sel[:, blk:blk + 1], (H_NSA, page)), m)
        allow.append(m > 0.5)
    allow = jnp.concatenate(allow, axis=1)
    qf = qbd.astype(F32)
    s_new = jnp.sum(qf * nkv4_new[:, 2 * W_KV:3 * W_KV], axis=1, keepdims=True)
    p, p_new, inv_l = _softmax_rows(_dot(qbd, x[2 * W_KV:3 * W_KV, :].astype(BF16)), s_new, allow)
    o_slc = (_dot_nt(p.astype(BF16), x[3 * W_KV:4 * W_KV, :].astype(BF16))
             + p_new.astype(BF16).astype(F32) * nkv4_new[:, 3 * W_KV:4 * W_KV].astype(BF16).astype(F32)) * inv_l

    slot_idx = lax.broadcasted_iota(jnp.int32, (H_NSA, win_buf), 1)
    sw_new = jnp.sum(qf * win_new[:, 0:W_KV], axis=1, keepdims=True)
    pw, pw_new, inv_lw = _softmax_rows(_dot(qbd, wv[0:W_KV, :].astype(BF16)), sw_new, (win_buf - slot_idx) < WINDOW)
    o_win = (_dot_nt(pw.astype(BF16), wv[W_KV:2 * W_KV, :].astype(BF16))
             + pw_new.astype(BF16).astype(F32) * win_new[:, W_KV:2 * W_KV].astype(BF16).astype(F32)) * inv_lw

    gate_rows = jnp.broadcast_to(gates, (H_NSA, LANES))
    gc = [jnp.sum(jnp.where(lane == GATE0 + 3 * row + c, gate_rows, 0.0), axis=1, keepdims=True) for c in range(3)]
    val = gc[0] * o_cmp + gc[1] * o_slc + gc[2] * o_win
    pieces = []
    for i in range(H_NSA):
        piece = val[i:i + 1]
        if (i % 2) != (i // g):
            piece = pltpu.roll(piece, HEAD_DIM, axis=1)
        pieces.append(piece)
    lo = lax.broadcasted_iota(jnp.int32, (1, LANES), 1) < HEAD_DIM
    return jnp.concatenate([jnp.where(lo, pieces[2 * pr], pieces[2 * pr + 1]) for pr in range(H_NSA // 2)], axis=1)


def _nsa_decode(page_table, q_row, nkv4_row, win_row, small, win_t, cache_t, cw):
    s, n_pages = page_table.shape
    rows, page = cache_t.shape[1], cache_t.shape[2]
    past_len = n_pages * page
    n_ch = past_len // CMP_STRIDE
    win_buf = win_t.shape[2]
    group = next(g for g in (4, 2, 1) if s % g == 0)
    rowspec = lambda n: pl.BlockSpec((group, 1, n), lambda i, pt: (i, 0, 0))
    const = lambda shape: pl.BlockSpec(shape, lambda i, pt: (0,) * len(shape), pipeline_mode=pl.Buffered(1))
    return pl.pallas_call(
        functools.partial(_nsa_dec_kernel, n_pages=n_pages, page=page, group=group),
        grid_spec=pltpu.PrefetchScalarGridSpec(
            num_scalar_prefetch=1,
            grid=(s // group,),
            in_specs=[rowspec(W_NSA), rowspec(4 * W_KV), rowspec(2 * W_KV), rowspec(LANES),
                      pl.BlockSpec((group, 2 * W_KV, win_buf), lambda i, pt: (i, 0, 0)),
                      pl.BlockSpec(memory_space=pl.ANY)] + [const(w.shape) for w in cw],
            out_specs=rowspec(W_NSA),
            scratch_shapes=[pltpu.VMEM((2, group, rows, past_len), F32),
                            pltpu.VMEM((CMP_STRIDE, group * n_ch, W_KV), F32),
                            pltpu.VMEM((CMP_STRIDE, group * n_ch, W_KV), F32),
                            pltpu.SemaphoreType.DMA((2,))],
        ),
        out_shape=jax.ShapeDtypeStruct((s, 1, W_NSA), F32),
        compiler_params=_params(("arbitrary",)),
        name="nsa_decode",
    )(page_table.reshape(-1), q_row, nkv4_row, win_row, small, win_t, cache_t, *cw)


def _rot_cols(w):
    d, n = w.shape
    w = w.reshape(d, n // HEAD_DIM, 2, HEAD_DIM // 2)
    return jnp.stack([-w[:, :, 1], w[:, :, 0]], axis=2).reshape(d, n)


def _prep_projection(w_in, b_f):
    c = [0, W_FOX, 2 * W_FOX, 3 * W_FOX, 3 * W_FOX + H_FOX, 3 * W_FOX + H_FOX + W_NSA,
         3 * W_FOX + H_FOX + W_NSA + 6 * W_KV]
    fq, fk, fv, ff, nq, kv = (w_in[:, c[i]:c[i + 1]] for i in range(6))
    gt = w_in[:, c[6]:]
    ks = jnp.concatenate([kv[:, br * 2 * W_KV: br * 2 * W_KV + W_KV] for br in range(3)], axis=1)
    w_big = jnp.concatenate([fq, fk, fv, nq, kv, _rot_cols(nq), _rot_cols(ks)], axis=1).astype(BF16)
    d = w_in.shape[0]
    w_small = jnp.concatenate([ff, gt, jnp.zeros((d, LANES - N_SMALL), w_in.dtype)], axis=1).astype(BF16)
    b_small = jnp.concatenate([b_f.astype(F32), jnp.zeros((LANES - H_FOX,), F32)])[None, :]
    return w_big, w_small, b_small


def _prep_compress(wk1, wk2, pek, wv1, wv2, pev):
    ratio = CMP_LEN // CMP_STRIDE
    eye = jnp.eye(4, dtype=F32)

    def blocks(w):
        return w.reshape(ratio, CMP_STRIDE, HEAD_DIM, wk1.shape[1])

    per_head = lambda w: jnp.einsum("rjde,hk->jhdrke", blocks(w), jnp.eye(H_KV, dtype=F32)).reshape(
        CMP_STRIDE, W_KV, ratio * W_KV)
    wj = jnp.concatenate([per_head(wk1), per_head(wv1)], axis=1).astype(BF16)
    pe = jnp.concatenate([pek.reshape(-1), pev.reshape(-1)])
    pe = jnp.broadcast_to(pe[None, :], (8, pe.shape[0])).astype(BF16)
    zero = jnp.zeros_like(wk1)
    wpe = jnp.concatenate([jnp.concatenate([wk1, wk1, zero, zero], axis=1),
                           jnp.concatenate([zero, zero, wv1, wv1], axis=1)], axis=0).astype(BF16)
    w2 = jnp.einsum("gde,gh->gdhe", jnp.stack([wk2, wk2, wv2, wv2]), eye)
    w2_pad = jnp.concatenate([w2, jnp.zeros_like(w2)], axis=3)
    w2 = w2.reshape(4 * HEAD_DIM, 4 * HEAD_DIM).astype(BF16)
    w2_pad = w2_pad.reshape(4 * HEAD_DIM, 4 * LANES).astype(BF16)
    return (wj, pe, wpe), w2, w2_pad


def _rope_tables(pos):
    half = HEAD_DIM // 2
    inv = ROPE_THETA ** (-jnp.arange(half, dtype=F32) / half)
    ang = pos.astype(F32)[:, None] * inv[None, :]
    reps = LANES // half
    return jnp.tile(jnp.cos(ang), (1, reps)), jnp.tile(jnp.sin(ang), (1, reps))


def _row_tile(n, cap):
    t = min(n, cap)
    while n % t:
        t //= 2
    return t


def kernel(x_prompt, x_sample, cache_fox_kv, cache_fox_logf, cache_nsa_kv, state_nsa_win_kv, page_table,
           g_ffn1_pre, w_ffn1_gate, w_ffn1_up, w_ffn1_down, g_ffn1_post, g_mix_pre, w_in, b_fox_f,
           w_cmpk_1, w_cmpk_2, pe_cmpk, w_cmpv_1, w_cmpv_2, pe_cmpv, g_fox_out, g_nsa_out, w_out,
           g_mix_post, g_ffn2_pre, w_ffn2_gate, w_ffn2_up, w_ffn2_down, g_ffn2_post):
    depth = w_in.shape[0]
    b, t, d = x_prompt.shape
    s, dec_seq, _ = x_sample.shape
    assert dec_seq == 1, "the sample group decodes one token per sequence"
    page = cache_fox_kv.shape[2]
    n_pages = page_table.shape[1]
    past_len = n_pages * page
    assert t % LANES == 0 and page % SLC_LEN == 0
    page_table = page_table.astype(jnp.int32)

    tm_p = _row_tile(t, 512)
    cos_p, sin_p = _rope_tables(jnp.arange(t, dtype=jnp.int32))
    cos_s, sin_s = _rope_tables(jnp.full((s,), past_len, jnp.int32))
    row = lambda v: v.astype(F32)[None, :]
    to_rows = lambda c: jnp.transpose(c, (0, 2, 3, 4, 1)).reshape(c.shape[0], -1, c.shape[1])

    yp = x_prompt.reshape(b * t, d)
    ys = x_sample.reshape(s, d)
    outs = [[] for _ in range(8)]
    for l in range(depth):
        ffn1 = (row(g_ffn1_pre[l]), w_ffn1_gate[l].astype(BF16), w_ffn1_up[l].astype(BF16),
                w_ffn1_down[l].astype(BF16), row(g_ffn1_post[l]))
        ffn2 = (row(g_ffn2_pre[l]), w_ffn2_gate[l].astype(BF16), w_ffn2_up[l].astype(BF16),
                w_ffn2_down[l].astype(BF16), row(g_ffn2_post[l]))
        w_big, w_small, b_small = _prep_projection(w_in[l], b_fox_f[l])
        cw, w2, w2_pad = _prep_compress(w_cmpk_1[l], w_cmpk_2[l], pe_cmpk[l], w_cmpv_1[l], w_cmpv_2[l], pe_cmpv[l])
        merge = (row(g_fox_out[l]), row(g_nsa_out[l]), w_out[l].astype(BF16), row(g_mix_post[l]))

        hp = _half_ffn(yp, *ffn1, tm_p)
        small, fkv_t, nkv4_t, win_t, fq_t, fv_t, nq_t, nv_t, cmp, fk_p, nk_p = _project(
            hp, row(g_mix_pre[l]), w_big, w_small, b_small, cos_p, sin_p, tm_p, t // tm_p, True)
        small3 = small.reshape(b, t, LANES)
        per_head = lambda x: x.reshape(b, t, x.shape[1])
        cum, fk_b = _forget_bias(small3, per_head(fk_p), _row_tile(t, 512))
        o_fox = _fox_attention(fq_t, jnp.swapaxes(cum[:, :, :H_FOX], 1, 2), fk_b, fv_t,
                               _row_tile(t, 512), _row_tile(t, 512))
        kc_p, vc_t = _compress(cmp.reshape(b, t, 2 * W_KV), cw + (w2_pad,))
        o_nsa = _nsa_attention_t2(nq_t, jnp.swapaxes(small3, 1, 2), kc_p, vc_t, per_head(nk_p), nv_t,
                                 _row_tile(t, 256), _row_tile(t, 512))
        yp = _merge_ffn(hp, o_fox.reshape(b * t, W_FOX), o_nsa.reshape(b * t, W_NSA), *merge, *ffn2, tm_p)
        keep = min(WINDOW, t)
        tokens_first = lambda x, *dims: jnp.transpose(x.reshape(b, *dims, x.shape[2]), (0, 4, 1, 2, 3))
        outs[0].append(tokens_first(fkv_t, 2, H_FOX, HEAD_DIM))
        outs[1].append(small3[:, :, :H_FOX])
        outs[2].append(tokens_first(nkv4_t, 4, H_KV, HEAD_DIM))
        outs[3].append(tokens_first(win_t[:, :, t - keep:], 2, H_KV, HEAD_DIM))

        hs = _half_ffn(ys, *ffn1, s)
        small, fkv, nkv4, win, fq, nq = _project(
            hs, row(g_mix_pre[l]), w_big, w_small, b_small, cos_s, sin_s, s, 1, False)
        as_rows = lambda x: x.reshape(s, 1, x.shape[1])
        o_fox = _fox_decode(page_table, as_rows(fq), as_rows(fkv), as_rows(small),
                            to_rows(cache_fox_kv[l]), jnp.swapaxes(cache_fox_logf[l], 1, 2))
        o_nsa = _nsa_decode(page_table, as_rows(nq), as_rows(nkv4), as_rows(win), as_rows(small),
                            to_rows(state_nsa_win_kv[l]), to_rows(cache_nsa_kv[l]), cw + (w2,))
        ys = _merge_ffn(hs, o_fox.reshape(s, W_FOX), o_nsa.reshape(s, W_NSA), *merge, *ffn2, s)
        keep = min(WINDOW, past_len + 1)
        kw_all = jnp.concatenate([state_nsa_win_kv[l], win.reshape(s, 1, 2, H_KV, HEAD_DIM)], axis=1)
        outs[4].append(fkv.reshape(s, 1, 2, H_FOX, HEAD_DIM))
        outs[5].append(small[:, :H_FOX].reshape(s, 1, H_FOX))
        outs[6].append(nkv4.reshape(s, 1, 4, H_KV, HEAD_DIM))
        outs[7].append(kw_all[:, kw_all.shape[1] - keep:])

    stacked = [jnp.stack(o, axis=0) for o in outs]
    return (yp.reshape(b, t, d), ys.reshape(s, 1, d), *stacked)
```
